```python
import jax, jax.numpy as jnp
from jax import lax
import numpy as np

D_MODEL = 2048
BATCH = 4
SEQ = 4096
DEPTH = 2

N_META = 16
CHUNK = 128
PAD = CHUNK - N_META
RMS_EPS = 1e-6
GN_EPS = 1e-6
NEG_INF = -1e30
RET_HEADS = 8
RET_DK = D_MODEL // RET_HEADS
RET_DV = 2 * D_MODEL // RET_HEADS
RET_QK = RET_HEADS * RET_DK
RET_V = RET_HEADS * RET_DV
ROPE_BASE = 10000.0
FOX_HEADS = 16
FOX_DH = D_MODEL // FOX_HEADS
N_GROUPS = 4
EXPERTS_PER_GROUP = 8
EXPERT_FF = D_MODEL // 4
TOP_K_INNER = 2
N_RET = (DEPTH + 1) // 2
N_FOX = DEPTH // 2

kernel_name = 'hybrid_retention_fox_hmoe_meta'


def rms_norm(h, g):
    hf = h.astype(jnp.float32)
    y = hf * lax.rsqrt(jnp.mean(hf * hf, axis=-1, keepdims=True) + RMS_EPS)
    return (y * g.astype(jnp.float32)).astype(h.dtype)


def rotary(u, pos):
    half = u.shape[-1] // 2
    inv = ROPE_BASE ** (-jnp.arange(half, dtype=jnp.float32) / half)
    ang = pos.astype(jnp.float32)[:, None] * inv[None, :]
    cos = jnp.cos(ang)[None, :, None, :]
    sin = jnp.sin(ang)[None, :, None, :]
    u1, u2 = u[..., :half], u[..., half:]
    return jnp.concatenate([u1 * cos - u2 * sin, u1 * sin + u2 * cos], axis=-1)


def retention(h, w_in, gn_gain, w_out):
    bsz, seq_len, _ = h.shape
    lp = seq_len + PAD
    nc = lp // CHUNK
    hp = jnp.pad(h, ((0, 0), (PAD, 0), (0, 0)))
    proj = hp @ w_in
    q, k, v, g = jnp.split(proj, [RET_QK, 2 * RET_QK, 2 * RET_QK + RET_V], axis=-1)
    pos = jnp.arange(lp) - PAD
    valid = (jnp.arange(lp) >= PAD)[None, :, None, None]
    q = rotary(q.reshape(bsz, lp, RET_HEADS, RET_DK).astype(jnp.float32), pos)
    k = rotary(k.reshape(bsz, lp, RET_HEADS, RET_DK).astype(jnp.float32), pos) * (RET_DK ** -0.5)
    k = jnp.where(valid, k, 0.0)
    v = jnp.where(valid, v.reshape(bsz, lp, RET_HEADS, RET_DV).astype(jnp.float32), 0.0)
    log_g = jnp.log1p(-jnp.exp2(-5.0 - jnp.arange(RET_HEADS, dtype=jnp.float32)))
    idx = jnp.arange(CHUNK, dtype=jnp.float32)
    diff = idx[:, None] - idx[None, :]
    dmat = jnp.where(diff[None] >= 0, jnp.exp(jnp.maximum(diff, 0.0)[None] * log_g[:, None, None]), 0.0)
    xi = jnp.exp((idx + 1.0)[None, :] * log_g[:, None])
    zeta = jnp.exp((CHUNK - 1.0 - idx)[None, :] * log_g[:, None])
    g_chunk = jnp.exp(CHUNK * log_g)

    def to_chunks(u):
        return u.reshape(bsz, nc, CHUNK, RET_HEADS, u.shape[-1]).transpose(1, 0, 3, 2, 4)

    def step(state, inp):
        qc, kc, vc = inp
        scores = jnp.einsum('bhid,bhjd->bhij', qc, kc) * dmat[None]
        inner = jnp.einsum('bhij,bhje->bhie', scores, vc)
        cross = jnp.einsum('bhid,bhde->bhie', qc, state) * xi[None, :, :, None]
        new_state = g_chunk[None, :, None, None] * state + jnp.einsum(
            'bhjd,bhje->bhde', kc, vc * zeta[None, :, :, None])
        return new_state, inner + cross

    s0 = jnp.zeros((bsz, RET_HEADS, RET_DK, RET_DV), jnp.float32)
    _, ys = lax.scan(step, s0, (to_chunks(q), to_chunks(k), to_chunks(v)))
    y = ys.transpose(1, 0, 3, 2, 4).reshape(bsz, lp, RET_HEADS, RET_DV)
    mu = jnp.mean(y, axis=-1, keepdims=True)
    var = jnp.mean(jnp.square(y - mu), axis=-1, keepdims=True)
    yn = (y - mu) * lax.rsqrt(var + GN_EPS) * gn_gain.astype(jnp.float32)[None, None]
    gated = jax.nn.silu(g.astype(jnp.float32)) * yn.reshape(bsz, lp, RET_V)
    out = gated.astype(h.dtype) @ w_out
    return out[:, PAD:]


def forgetting_attention(h, w_in, b_f, w_out):
    bsz, seq_len, dm = h.shape
    lp = seq_len + PAD
    nb = lp // CHUNK
    hp = jnp.pad(h, ((0, 0), (PAD, 0), (0, 0)))
    proj = hp @ w_in
    q, k, v, f_logit = jnp.split(proj, [dm, 2 * dm, 3 * dm], axis=-1)
    key_pos = jnp.arange(lp)
    key_valid = key_pos >= PAD
    log_f = jax.nn.log_sigmoid(f_logit.astype(jnp.float32) + b_f.astype(jnp.float32))
    log_f = jnp.where(key_valid[None, :, None], log_f, 0.0)
    c = jnp.cumsum(log_f, axis=1)
    c_keys = c.transpose(0, 2, 1)
    k_all = k.reshape(bsz, lp, FOX_HEADS, FOX_DH).transpose(0, 2, 1, 3)
    v_all = v.reshape(bsz, lp, FOX_HEADS, FOX_DH).transpose(0, 2, 1, 3)
    q_blk = q.reshape(bsz, nb, CHUNK, FOX_HEADS, FOX_DH).transpose(1, 0, 3, 2, 4)
    c_blk = c.reshape(bsz, nb, CHUNK, FOX_HEADS).transpose(1, 0, 3, 2)
    pos_blk = key_pos.reshape(nb, CHUNK)
    scale = FOX_DH ** -0.5

    def block(args):
        qb, cqb, qpos = args
        s = jnp.einsum('bhqd,bhkd->bhqk', qb, k_all, preferred_element_type=jnp.float32) * scale
        s = s + cqb[..., None] - c_keys[:, :, None, :]
        mask = (key_pos[None, :] <= qpos[:, None]) & key_valid[None, :]
        s = jnp.where(mask[None, None], s, NEG_INF)
        p = jax.nn.softmax(s, axis=-1)
        return jnp.einsum('bhqk,bhkd->bhqd', p.astype(v_all.dtype), v_all)

    o = lax.map(block, (q_blk, c_blk, pos_blk))
    o = o.transpose(1, 0, 3, 2, 4).reshape(bsz, lp, dm)
    return (o @ w_out)[:, PAD:]


def hierarchical_moe(h, w_rg, b_rg, w_re, b_re, w_gate, w_up, w_down):
    bsz, seq_len, dm = h.shape
    t = h.reshape(-1, dm)
    lg = (t @ w_rg).astype(jnp.float32) + b_rg.astype(jnp.float32)
    pg = jax.nn.softmax(lg, axis=-1)
    g_sel = jnp.argmax(lg, axis=-1)
    p_gsel = jnp.take_along_axis(pg, g_sel[:, None], axis=1)[:, 0]
    le = jnp.einsum('td,dge->tge', t, w_re).astype(jnp.float32) + b_re.astype(jnp.float32)[None]
    le_sel = jnp.take_along_axis(le, g_sel[:, None, None], axis=1)[:, 0]
    pe = jax.nn.softmax(le_sel, axis=-1)
    top_w, top_i = lax.top_k(pe, TOP_K_INNER)
    top_w = top_w / jnp.sum(top_w, axis=-1, keepdims=True)
    ew = jnp.sum(jax.nn.one_hot(top_i, EXPERTS_PER_GROUP, dtype=jnp.float32) * top_w[..., None], axis=1)
    combine = (jax.nn.one_hot(g_sel, N_GROUPS, dtype=jnp.float32) * p_gsel[:, None])[:, :, None] * ew[:, None, :]
    combine = combine.astype(h.dtype)
    out = jnp.zeros_like(t)
    for g in range(N_GROUPS):
        a = jnp.einsum('td,edf->tef', t, w_gate[g])
        u = jnp.einsum('td,edf->tef', t, w_up[g])
        hid = jax.nn.silu(a) * u * combine[:, g, :, None]
        out = out + jnp.einsum('tef,efd->td', hid, w_down[g])
    return out.reshape(bsz, seq_len, dm)


def setup_inputs(seed: int = 0) -> dict:
    key = jax.random.key(seed)
    ks = jax.random.split(key, 20)
    f32 = jnp.float32

    def nrm(k, shape, fan_in):
        return jax.random.normal(k, shape, f32) * (fan_in ** -0.5)

    def gain(k, shape):
        return 1.0 + 0.02 * jax.random.normal(k, shape, f32)

    G, E, F, D = N_GROUPS, EXPERTS_PER_GROUP, EXPERT_FF, D_MODEL
    return {
        'x': jax.random.normal(ks[0], (BATCH, SEQ, D), f32),
        'meta_tokens': jax.random.normal(ks[1], (N_META, D), f32),
        'norm_mixer': gain(ks[2], (DEPTH, D)),
        'norm_ffn': gain(ks[3], (DEPTH, D)),
        'norm_final': gain(ks[4], (D,)),
        'ret_w_in': nrm(ks[5], (N_RET, D, 2 * RET_QK + 2 * RET_V), D),
        'ret_gn': gain(ks[6], (N_RET, RET_HEADS, RET_DV)),
        'ret_w_out': nrm(ks[7], (N_RET, RET_V, D), RET_V),
        'fox_w_in': nrm(ks[8], (N_FOX, D, 3 * D + FOX_HEADS), D),
        'fox_b_f': jax.random.uniform(ks[9], (N_FOX, FOX_HEADS), f32, 1.0, 6.0),
        'fox_w_out': nrm(ks[10], (N_FOX, D, D), D),
        'moe_w_rg': nrm(ks[11], (DEPTH, D, G), D),
        'moe_b_rg': 0.01 * jax.random.normal(ks[12], (DEPTH, G), f32),
        'moe_w_re': nrm(ks[13], (DEPTH, D, G, E), D),
        'moe_b_re': 0.01 * jax.random.normal(ks[14], (DEPTH, G, E), f32),
        'moe_w_gate': nrm(ks[15], (DEPTH, G, E, D, F), D),
        'moe_w_up': nrm(ks[16], (DEPTH, G, E, D, F), D),
        'moe_w_down': nrm(ks[17], (DEPTH, G, E, F, D), F),
    }


def reference(x, meta_tokens, norm_mixer, norm_ffn, norm_final, ret_w_in, ret_gn, ret_w_out,
              fox_w_in, fox_b_f, fox_w_out, moe_w_rg, moe_b_rg, moe_w_re, moe_b_re,
              moe_w_gate, moe_w_up, moe_w_down):
    bsz = x.shape[0]
    meta = jnp.broadcast_to(meta_tokens.astype(x.dtype)[None], (bsz, N_META, x.shape[-1]))
    h = jnp.concatenate([meta, x], axis=1)
    for i in range(DEPTH):
        hn = rms_norm(h, norm_mixer[i])
        j = i // 2
        if i % 2 == 0:
            h = h + retention(hn, ret_w_in[j], ret_gn[j], ret_w_out[j])
        else:
            h = h + forgetting_attention(hn, fox_w_in[j], fox_b_f[j], fox_w_out[j])
        hn = rms_norm(h, norm_ffn[i])
        h = h + hierarchical_moe(hn, moe_w_rg[i], moe_b_rg[i], moe_w_re[i], moe_b_re[i],
                                 moe_w_gate[i], moe_w_up[i], moe_w_down[i])
    return rms_norm(h, norm_final)[:, N_META:]
```

```python
import functools

import jax
import jax.numpy as jnp
from jax import lax
from jax.experimental import pallas as pl
from jax.experimental.pallas import tpu as pltpu

N_META = 16
CHUNK = 128
PAD = CHUNK - N_META
RMS_EPS = 1e-6
GN_EPS = 1e-6
NEG_INF = -1e30
RET_HEADS = 8
FOX_HEADS = 16
N_GROUPS = 4
EXPERTS_PER_GROUP = 8
N_EXPERTS = N_GROUPS * EXPERTS_PER_GROUP
ROPE_BASE = 10000.0

LANES = 128
ROUTER_ROWS = 48
EXPERT_ROW0 = N_GROUPS
TILE_M = 256
VMEM_LIMIT = 48 * 1024 * 1024

F32 = jnp.float32
BF16 = jnp.bfloat16


def _params(sem, vmem=VMEM_LIMIT):
    return pltpu.CompilerParams(dimension_semantics=sem, vmem_limit_bytes=vmem)


def _rms(h, g):
    return h * lax.rsqrt(jnp.mean(h * h, axis=-1, keepdims=True) + RMS_EPS) * g


def _sigmoid(x):
    return 1.0 / (1.0 + jnp.exp(-x))


def _embed_norm_kernel(x_ref, meta_ref, g_ref, h_ref, hn_ref):
    i = pl.program_id(1)

    @pl.when(i == 0)
    def _():
        h_ref[...] = jnp.zeros_like(h_ref)
        h_ref[PAD:, :] = meta_ref[...]

    @pl.when(i > 0)
    def _():
        h_ref[...] = x_ref[...]

    hn_ref[...] = _rms(h_ref[...], g_ref[...]).astype(hn_ref.dtype)


def embed_norm(x, meta, gain, hn_dtype):
    bsz, seq, d = x.shape
    nc = (seq + CHUNK) // CHUNK
    tp = bsz * nc * CHUNK
    return pl.pallas_call(
        _embed_norm_kernel,
        out_shape=(jax.ShapeDtypeStruct((tp, d), F32), jax.ShapeDtypeStruct((tp, d), hn_dtype)),
        grid=(bsz, nc),
        in_specs=[
            pl.BlockSpec((None, CHUNK, d), lambda b, i: (b, jnp.maximum(i - 1, 0), 0)),
            pl.BlockSpec((N_META, d), lambda b, i: (0, 0)),
            pl.BlockSpec((1, d), lambda b, i: (0, 0)),
        ],
        out_specs=(
            pl.BlockSpec((CHUNK, d), lambda b, i: (b * nc + i, 0)),
            pl.BlockSpec((CHUNK, d), lambda b, i: (b * nc + i, 0)),
        ),
        compiler_params=_params(("parallel", "parallel")),
        name="embed_norm",
    )(x, meta, gain.reshape(1, d))


def _norm_kernel(h_ref, g_ref, o_ref):
    o_ref[...] = _rms(h_ref[...], g_ref[...]).astype(o_ref.dtype)


def rms_norm(h, gain, out_dtype, tm=512):
    tp, d = h.shape
    return pl.pallas_call(
        _norm_kernel,
        out_shape=jax.ShapeDtypeStruct((tp, d), out_dtype),
        grid=(tp // tm,),
        in_specs=[pl.BlockSpec((tm, d), lambda i: (i, 0)), pl.BlockSpec((1, d), lambda i: (0, 0))],
        out_specs=pl.BlockSpec((tm, d), lambda i: (i, 0)),
        compiler_params=_params(("parallel",)),
        name="rms_norm",
    )(h, gain.reshape(1, d))


def _mm_kernel(*refs, nk, has_res):
    if has_res:
        x_ref, w_ref, r_ref, o_ref, acc_ref = refs
    else:
        x_ref, w_ref, o_ref, acc_ref = refs
    k = pl.program_id(2)

    @pl.when(k == 0)
    def _():
        acc_ref[...] = jnp.zeros_like(acc_ref)

    acc_ref[...] += jnp.dot(x_ref[...].astype(BF16), w_ref[...], preferred_element_type=F32)

    @pl.when(k == nk - 1)
    def _():
        acc = acc_ref[...]
        if has_res:
            acc = acc + r_ref[...]
        o_ref[...] = acc.astype(o_ref.dtype)


def matmul(x, w, out_dtype, residual=None, tm=512, tn=1024, tk=2048):
    m, kdim = x.shape
    n = w.shape[1]
    tn = min(tn, n)
    tk = min(tk, kdim)
    nk = kdim // tk
    in_specs = [
        pl.BlockSpec((tm, tk), lambda j, i, k: (i, k)),
        pl.BlockSpec((tk, tn), lambda j, i, k: (k, j)),
    ]
    args = [x, w]
    if residual is not None:
        in_specs.append(pl.BlockSpec((tm, tn), lambda j, i, k: (i, j)))
        args.append(residual)
    return pl.pallas_call(
        functools.partial(_mm_kernel, nk=nk, has_res=residual is not None),
        out_shape=jax.ShapeDtypeStruct((m, n), out_dtype),
        grid=(n // tn, m // tm, nk),
        in_specs=in_specs,
        out_specs=pl.BlockSpec((tm, tn), lambda j, i, k: (i, j)),
        scratch_shapes=[pltpu.VMEM((tm, tn), F32)],
        compiler_params=_params(("parallel", "parallel", "arbitrary")),
        name="matmul",
    )(*args)


def _retention_kernel(q_ref, k_ref, v_ref, g_ref, cos_ref, sin_ref, dmat_ref, xi_ref, zeta_ref,
                      gch_ref, gn_ref, o_ref, state_ref, *, dk):
    c = pl.program_id(2)

    @pl.when(c == 0)
    def _():
        state_ref[...] = jnp.zeros_like(state_ref)

    cos = cos_ref[...]
    sin = sin_ref[...]
    half = dk // 2

    def rot(u):
        u1, u2 = u[:, :half], u[:, half:]
        return jnp.concatenate([u1 * cos - u2 * sin, u1 * sin + u2 * cos], axis=1)

    row = lax.broadcasted_iota(jnp.int32, (CHUNK, 1), 0)
    valid = jnp.logical_or(row >= PAD, c > 0)
    q = rot(q_ref[...].astype(F32))
    k = rot(k_ref[...].astype(F32)) * (dk ** -0.5)
    k = jnp.where(valid, k, 0.0)
    v = jnp.where(valid, v_ref[...].astype(F32), 0.0)
    qb = q.astype(BF16)
    kb = k.astype(BF16)
    vb = v.astype(BF16)

    scores = lax.dot_general(qb, kb, (((1,), (1,)), ((), ())), preferred_element_type=F32)
    scores = scores * dmat_ref[...]
    inner = jnp.dot(scores.astype(BF16), vb, preferred_element_type=F32)
    state = state_ref[...]
    cross = jnp.dot(qb, state.astype(BF16), preferred_element_type=F32) * xi_ref[...]
    y = inner + cross
    vz = (v * zeta_ref[...]).astype(BF16)
    state_ref[...] = gch_ref[...] * state + lax.dot_general(
        kb, vz, (((0,), (0,)), ((), ())), preferred_element_type=F32)

    mu = jnp.mean(y, axis=-1, keepdims=True)
    dlt = y - mu
    var = jnp.mean(dlt * dlt, axis=-1, keepdims=True)
    yn = dlt * lax.rsqrt(var + GN_EPS) * gn_ref[...]
    g = g_ref[...].astype(F32)
    o_ref[...] = (g * _sigmoid(g) * yn).astype(o_ref.dtype)


def retention_core(proj, gn_gain, bsz, nc):
    tp = proj.shape[0]
    d = proj.shape[1] // 6
    nh = RET_HEADS
    dk = d // nh
    dv = 2 * d // nh
    lp = nc * CHUNK
    half = dk // 2
    pos = (jnp.arange(lp) - PAD).astype(F32)
    inv = ROPE_BASE ** (-jnp.arange(half, dtype=F32) / half)
    ang = pos[:, None] * inv[None, :]
    cos, sin = jnp.cos(ang), jnp.sin(ang)
    log_g = jnp.log1p(-jnp.exp2(-5.0 - jnp.arange(nh, dtype=F32)))
    idx = jnp.arange(CHUNK, dtype=F32)
    diff = idx[:, None] - idx[None, :]
    dmat = jnp.where(diff[None] >= 0, jnp.exp(jnp.maximum(diff, 0.0)[None] * log_g[:, None, None]), 0.0)
    xi = jnp.exp((idx + 1.0)[None, :] * log_g[:, None])[:, :, None]
    zeta = jnp.exp((CHUNK - 1.0 - idx)[None, :] * log_g[:, None])[:, :, None]
    g_chunk = jnp.exp(CHUNK * log_g)[:, None, None]
    kq = d // dk
    kv = 2 * d // dv
    kg = kv + nh
    return pl.pallas_call(
        functools.partial(_retention_kernel, dk=dk),
        out_shape=jax.ShapeDtypeStruct((tp, nh * dv), BF16),
        grid=(bsz, nh, nc),
        in_specs=[
            pl.BlockSpec((CHUNK, dk), lambda b, h, c: (b * nc + c, h)),
            pl.BlockSpec((CHUNK, dk), lambda b, h, c: (b * nc + c, kq + h)),
            pl.BlockSpec((CHUNK, dv), lambda b, h, c: (b * nc + c, kv + h)),
            pl.BlockSpec((CHUNK, dv), lambda b, h, c: (b * nc + c, kg + h)),
            pl.BlockSpec((CHUNK, half), lambda b, h, c: (c, 0)),
            pl.BlockSpec((CHUNK, half), lambda b, h, c: (c, 0)),
            pl.BlockSpec((None, CHUNK, CHUNK), lambda b, h, c: (h, 0, 0)),
            pl.BlockSpec((None, CHUNK, 1), lambda b, h, c: (h, 0, 0)),
            pl.BlockSpec((None, CHUNK, 1), lambda b, h, c: (h, 0, 0)),
            pl.BlockSpec((None, 1, 1), lambda b, h, c: (h, 0, 0)),
            pl.BlockSpec((None, 1, dv), lambda b, h, c: (h, 0, 0)),
        ],
        out_specs=pl.BlockSpec((CHUNK, dv), lambda b, h, c: (b * nc + c, h)),
        scratch_shapes=[pltpu.VMEM((dk, dv), F32)],
        compiler_params=_params(("parallel", "parallel", "arbitrary")),
        name="retention_core",
    )(proj, proj, proj, proj, cos, sin, dmat, xi, zeta, g_chunk, gn_gain.reshape(nh, 1, dv))


def _forget_kernel(hn_ref, wf_ref, bf_ref, c_ref, carry_ref):
    i = pl.program_id(1)

    @pl.when(i == 0)
    def _():
        carry_ref[...] = jnp.zeros_like(carry_ref)

    z = jnp.dot(hn_ref[...].astype(BF16), wf_ref[...], preferred_element_type=F32) + bf_ref[...]
    lf = jnp.minimum(z, 0.0) - jnp.log1p(jnp.exp(-jnp.abs(z)))
    row = lax.broadcasted_iota(jnp.int32, (CHUNK, 1), 0)
    lf = jnp.where(jnp.logical_or(row >= PAD, i > 0), lf, 0.0)
    r_i = lax.broadcasted_iota(jnp.int32, (CHUNK, CHUNK), 0)
    c_i = lax.broadcasted_iota(jnp.int32, (CHUNK, CHUNK), 1)
    tri = (r_i >= c_i).astype(BF16)
    hi = lf.astype(BF16)
    r1 = lf - hi.astype(F32)
    mid = r1.astype(BF16)
    lo = (r1 - mid.astype(F32)).astype(BF16)
    cs = (jnp.dot(tri, hi, preferred_element_type=F32)
          + jnp.dot(tri, mid, preferred_element_type=F32)
          + jnp.dot(tri, lo, preferred_element_type=F32))
    cs = cs + carry_ref[...]
    c_ref[...] = cs
    carry_ref[...] = cs[CHUNK - 1:CHUNK, :]


def forget_cumsum(hn, w_f, b_f, bsz, nc):
    tp, d = hn.shape
    nh = w_f.shape[1]
    wf = jnp.zeros((d, LANES), BF16).at[:, :nh].set(w_f.astype(BF16))
    bf = jnp.zeros((1, LANES), F32).at[0, :nh].set(b_f.astype(F32))
    return pl.pallas_call(
        _forget_kernel,
        out_shape=jax.ShapeDtypeStruct((tp, LANES), F32),
        grid=(bsz, nc),
        in_specs=[
            pl.BlockSpec((CHUNK, d), lambda b, i: (b * nc + i, 0)),
            pl.BlockSpec((d, LANES), lambda b, i: (0, 0)),
            pl.BlockSpec((1, LANES), lambda b, i: (0, 0)),
        ],
        out_specs=pl.BlockSpec((CHUNK, LANES), lambda b, i: (b * nc + i, 0)),
        scratch_shapes=[pltpu.VMEM((1, LANES), F32)],
        compiler_params=_params(("parallel", "arbitrary")),
        name="forget_cumsum",
    )(hn, wf, bf)


def _fox_kernel(q_ref, k_ref, v_ref, cq_ref, ck_ref, o_ref, m_ref, l_ref, acc_ref, *, tq, scale):
    h = pl.program_id(1)
    i = pl.program_id(2)
    q = q_ref[...]
    lane = lax.broadcasted_iota(jnp.int32, (tq, LANES), 1)
    cq = jnp.sum(jnp.where(lane == h, cq_ref[...], 0.0), axis=1, keepdims=True)
    qpos = i * tq + lax.broadcasted_iota(jnp.int32, (tq, 1), 0)
    m_ref[...] = jnp.full_like(m_ref, 10.0 * NEG_INF)
    l_ref[...] = jnp.zeros_like(l_ref)
    acc_ref[...] = jnp.zeros_like(acc_ref)

    def body(kb, carry):
        start = pl.multiple_of(kb * tq, tq)
        k = k_ref[pl.ds(start, tq), :]
        v = v_ref[pl.ds(start, tq), :]
        s = lax.dot_general(q, k, (((1,), (1,)), ((), ())), preferred_element_type=F32) * scale
        s = s + cq - ck_ref[kb]
        kpos = kb * tq + lax.broadcasted_iota(jnp.int32, (1, tq), 1)
        mask = jnp.logical_and(kpos <= qpos, kpos >= PAD)
        s = jnp.where(mask, s, NEG_INF)
        m_old = m_ref[...]
        m_new = jnp.maximum(m_old, jnp.max(s, axis=1, keepdims=True))
        alpha = jnp.exp(m_old - m_new)
        p = jnp.exp(s - m_new)
        l_ref[...] = alpha * l_ref[...] + jnp.sum(p, axis=1, keepdims=True)
        acc_ref[...] = alpha * acc_ref[...] + jnp.dot(p.astype(BF16), v, preferred_element_type=F32)
        m_ref[...] = m_new
        return carry

    lax.fori_loop(0, i + 1, body, 0)
    o_ref[...] = (acc_ref[...] / l_ref[...]).astype(o_ref.dtype)


def fox_attention(qkv, c_std, bsz, lp, tq=384):
    tp = qkv.shape[0]
    d = qkv.shape[1] // 3
    nh = FOX_HEADS
    dh = d // nh
    nq = lp // tq
    qkv3 = qkv.reshape(bsz, lp, 3 * d)
    ck = c_std[:, :nh].reshape(bsz, lp, nh).transpose(0, 2, 1).reshape(bsz * nh, nq, 1, tq)
    return pl.pallas_call(
        functools.partial(_fox_kernel, tq=tq, scale=dh ** -0.5),
        out_shape=jax.ShapeDtypeStruct((tp, d), BF16),
        grid=(bsz, nh, nq),
        in_specs=[
            pl.BlockSpec((tq, dh), lambda b, h, i: (b * nq + i, h)),
            pl.BlockSpec((None, lp, dh), lambda b, h, i: (b, 0, nh + h)),
            pl.BlockSpec((None, lp, dh), lambda b, h, i: (b, 0, 2 * nh + h)),
            pl.BlockSpec((tq, LANES), lambda b, h, i: (b * nq + i, 0)),
            pl.BlockSpec((None, nq, 1, tq), lambda b, h, i: (b * nh + h, 0, 0, 0)),
        ],
        out_specs=pl.BlockSpec((tq, dh), lambda b, h, i: (b * nq + i, h)),
        scratch_shapes=[pltpu.VMEM((tq, 1), F32), pltpu.VMEM((tq, 1), F32), pltpu.VMEM((tq, dh), F32)],
        compiler_params=_params(("parallel", "parallel", "arbitrary")),
        name="fox_attention",
    )(qkv, qkv3, qkv3, c_std, ck)


def _router_kernel(hn_ref, wr_ref, br_ref, idx_ref, w_ref, cnt_ref, carry_ref, *, tr):
    i = pl.program_id(0)

    @pl.when(i == 0)
    def _():
        carry_ref[...] = jnp.zeros_like(carry_ref)

    lg = lax.dot_general(wr_ref[...], hn_ref[...].astype(BF16), (((1,), (1,)), ((), ())),
                         preferred_element_type=F32) + br_ref[...]
    row = lax.broadcasted_iota(jnp.int32, (ROUTER_ROWS, tr), 0)
    big = jnp.int32(1 << 20)
    is_g = row < N_GROUPS
    mg = jnp.max(jnp.where(is_g, lg, -jnp.inf), axis=0, keepdims=True)
    g_sel = jnp.min(jnp.where(jnp.logical_and(is_g, lg == mg), row, big), axis=0, keepdims=True)
    sg = jnp.sum(jnp.where(is_g, jnp.exp(lg - mg), 0.0), axis=0, keepdims=True)
    p_g = 1.0 / sg
    lo = EXPERT_ROW0 + EXPERTS_PER_GROUP * g_sel
    is_e = jnp.logical_and(row >= lo, row < lo + EXPERTS_PER_GROUP)
    me = jnp.max(jnp.where(is_e, lg, -jnp.inf), axis=0, keepdims=True)
    ee = jnp.where(is_e, jnp.exp(lg - me), 0.0)
    pe = ee / jnp.sum(ee, axis=0, keepdims=True)
    pe1 = jnp.where(is_e, pe, -1.0)
    m1 = jnp.max(pe1, axis=0, keepdims=True)
    i1 = jnp.min(jnp.where(pe1 == m1, row, big), axis=0, keepdims=True)
    pe2 = jnp.where(row == i1, -1.0, pe1)
    m2 = jnp.max(pe2, axis=0, keepdims=True)
    i2 = jnp.min(jnp.where(pe2 == m2, row, big), axis=0, keepdims=True)
    den = m1 + m2
    w1 = p_g * (m1 / den)
    w2 = p_g * (m2 / den)

    oh1 = row == i1
    oh2 = row == i2
    ohs = jnp.logical_or(oh1, oh2).astype(F32)
    r_i = lax.broadcasted_iota(jnp.int32, (tr, tr), 0)
    c_i = lax.broadcasted_iota(jnp.int32, (tr, tr), 1)
    tri = (r_i < c_i).astype(BF16)
    cnt = jnp.dot(ohs.astype(BF16), tri, preferred_element_type=F32) + carry_ref[...]
    rank1 = jnp.sum(jnp.where(oh1, cnt, 0.0), axis=0, keepdims=True)
    rank2 = jnp.sum(jnp.where(oh2, cnt, 0.0), axis=0, keepdims=True)
    carry_ref[...] += jnp.sum(ohs, axis=1, keepdims=True)

    r8 = lax.broadcasted_iota(jnp.int32, (8, tr), 0)
    e1 = i1 - EXPERT_ROW0
    e2 = i2 - EXPERT_ROW0
    idx_ref[...] = jnp.where(r8 == 0, e1, jnp.where(r8 == 1, e2, jnp.where(
        r8 == 2, rank1.astype(jnp.int32), jnp.where(r8 == 3, rank2.astype(jnp.int32), 0))))
    w_ref[...] = jnp.where(r8 == 0, w1, jnp.where(r8 == 1, w2, 0.0))
    cnt_ref[...] = jnp.broadcast_to(carry_ref[...], cnt_ref.shape)


def moe_router(hn, w_rg, b_rg, w_re, b_re, tr=512):
    tp, d = hn.shape
    wr = jnp.zeros((ROUTER_ROWS, d), BF16)
    wr = wr.at[:N_GROUPS].set(w_rg.T.astype(BF16))
    wr = wr.at[EXPERT_ROW0:EXPERT_ROW0 + N_EXPERTS].set(w_re.reshape(d, N_EXPERTS).T.astype(BF16))
    br = jnp.zeros((ROUTER_ROWS, 1), F32)
    br = br.at[:N_GROUPS, 0].set(b_rg.astype(F32))
    br = br.at[EXPERT_ROW0:EXPERT_ROW0 + N_EXPERTS, 0].set(b_re.reshape(N_EXPERTS).astype(F32))
    return pl.pallas_call(
        functools.partial(_router_kernel, tr=tr),
        out_shape=(
            jax.ShapeDtypeStruct((8, tp), jnp.int32),
            jax.ShapeDtypeStruct((8, tp), F32),
            jax.ShapeDtypeStruct((ROUTER_ROWS, LANES), F32),
        ),
        grid=(tp // tr,),
        in_specs=[
            pl.BlockSpec((tr, d), lambda i: (i, 0)),
            pl.BlockSpec((ROUTER_ROWS, d), lambda i: (0, 0)),
            pl.BlockSpec((ROUTER_ROWS, 1), lambda i: (0, 0)),
        ],
        out_specs=(
            pl.BlockSpec((8, tr), lambda i: (0, i)),
            pl.BlockSpec((8, tr), lambda i: (0, i)),
            pl.BlockSpec((ROUTER_ROWS, LANES), lambda i: (0, 0)),
        ),
        scratch_shapes=[pltpu.VMEM((ROUTER_ROWS, 1), F32)],
        compiler_params=_params(("arbitrary",)),
        name="moe_router",
    )(hn, wr, br)


def _slots_kernel(cnt_ref, idx_ref, pos_ref, meta_ref):
    e1 = idx_ref[0:1, :]
    e2 = idx_ref[1:2, :]
    off1 = jnp.zeros_like(e1)
    off2 = jnp.zeros_like(e2)
    tile = lax.broadcasted_iota(jnp.int32, (1, meta_ref.shape[1]), 1)
    tile_expert = jnp.zeros_like(tile)
    start = jnp.int32(0)
    for e in range(N_EXPERTS):
        off1 = jnp.where(e1 == e, start, off1)
        off2 = jnp.where(e2 == e, start, off2)
        start = start + ((cnt_ref[e] + (TILE_M - 1)) // TILE_M) * TILE_M
        tile_expert = tile_expert + (tile * TILE_M >= start).astype(jnp.int32)
    r8 = lax.broadcasted_iota(jnp.int32, pos_ref.shape, 0)
    pos_ref[...] = jnp.where(r8 == 0, off1 + idx_ref[2:3, :], jnp.where(r8 == 1, off2 + idx_ref[3:4, :], 0))
    m8 = lax.broadcasted_iota(jnp.int32, meta_ref.shape, 0)
    meta_ref[...] = jnp.where(m8 == 0, jnp.minimum(tile_expert, N_EXPERTS - 1), start // TILE_M)


def moe_slots(cnt, idx, tr=512):
    tp = idx.shape[1]
    counts = cnt[EXPERT_ROW0:EXPERT_ROW0 + N_EXPERTS, 0].astype(jnp.int32)
    nt_lanes = 2 * LANES
    return pl.pallas_call(
        _slots_kernel,
        out_shape=(jax.ShapeDtypeStruct((8, tp), jnp.int32), jax.ShapeDtypeStruct((8, nt_lanes), jnp.int32)),
        grid_spec=pltpu.PrefetchScalarGridSpec(
            num_scalar_prefetch=1,
            grid=(tp // tr,),
            in_specs=[pl.BlockSpec((8, tr), lambda i, c: (0, i))],
            out_specs=(
                pl.BlockSpec((8, tr), lambda i, c: (0, i)),
                pl.BlockSpec((8, nt_lanes), lambda i, c: (0, 0)),
            ),
        ),
        compiler_params=_params(("arbitrary",)),
        name="moe_slots",
    )(counts, idx)


def _dispatch_kernel(pos_ref, hn_ref, xs_in_ref, xs_ref, sem, *, td, tp):
    del xs_in_ref
    base = pl.program_id(0) * td

    def row_copy(r, p):
        return pltpu.make_async_copy(hn_ref.at[pl.ds(r, 1)], xs_ref.at[pl.ds(p, 1)], sem)

    def issue(r, carry):
        row_copy(r, pos_ref[base + r]).start()
        row_copy(r, pos_ref[tp + base + r]).start()
        return carry

    def drain(r, carry):
        row_copy(r, pos_ref[base + r]).wait()
        row_copy(r, pos_ref[tp + base + r]).wait()
        return carry

    lax.fori_loop(0, td, issue, 0)
    lax.fori_loop(0, td, drain, 0)


def moe_dispatch(pos_flat, hn, n_rows, td=256):
    tp, d = hn.shape
    xs0 = jnp.zeros((n_rows, d), hn.dtype)
    return pl.pallas_call(
        functools.partial(_dispatch_kernel, td=td, tp=tp),
        out_shape=jax.ShapeDtypeStruct((n_rows, d), hn.dtype),
        grid_spec=pltpu.PrefetchScalarGridSpec(
            num_scalar_prefetch=1,
            grid=(tp // td,),
            in_specs=[
                pl.BlockSpec((td, d), lambda i, p: (i, 0)),
                pl.BlockSpec(memory_space=pl.ANY),
            ],
            out_specs=pl.BlockSpec(memory_space=pl.ANY),
            scratch_shapes=[pltpu.SemaphoreType.DMA],
        ),
        input_output_aliases={2: 0},
        compiler_params=_params(("arbitrary",)),
        name="moe_dispatch",
    )(pos_flat, hn, xs0)


def _expert_kernel(te_ref, nu_ref, xs_ref, wg_ref, wu_ref, wd_ref, y_ref):
    del te_ref
    used = pl.program_id(0) < nu_ref[0]

    @pl.when(used)
    def _():
        x = xs_ref[...].astype(BF16)
        a = jnp.dot(x, wg_ref[...], preferred_element_type=F32)
        u = jnp.dot(x, wu_ref[...], preferred_element_type=F32)
        hid = (a * _sigmoid(a) * u).astype(BF16)
        y_ref[...] = jnp.dot(hid, wd_ref[...], preferred_element_type=F32)

    @pl.when(jnp.logical_not(used))
    def _():
        y_ref[...] = jnp.zeros_like(y_ref)


def moe_experts(tile_expert, n_used, xs, w_gate, w_up, w_down):
    n_rows, d = xs.shape
    f = w_gate.shape[-1]
    nt = n_rows // TILE_M

    def row_map(j, te, nu):
        return (jnp.minimum(j, nu[0] - 1), 0)

    def w_map(j, te, nu):
        return (te[j], 0, 0)

    return pl.pallas_call(
        _expert_kernel,
        out_shape=jax.ShapeDtypeStruct((n_rows, d), F32),
        grid_spec=pltpu.PrefetchScalarGridSpec(
            num_scalar_prefetch=2,
            grid=(nt,),
            in_specs=[
                pl.BlockSpec((TILE_M, d), row_map),
                pl.BlockSpec((None, d, f), w_map),
                pl.BlockSpec((None, d, f), w_map),
                pl.BlockSpec((None, f, d), w_map),
            ],
            out_specs=pl.BlockSpec((TILE_M, d), lambda j, te, nu: (j, 0)),
        ),
        compiler_params=_params(("arbitrary",)),
        name="moe_experts",
    )(tile_expert, n_used, xs, w_gate, w_up, w_down)


def _combine_kernel(pos_ref, h_ref, w_ref, g_ref, y_ref, ho_ref, hn_ref, ybuf, sem, *, tc, tp):
    base = pl.program_id(0) * tc

    def row_copy(slot, r, p):
        return pltpu.make_async_copy(y_ref.at[pl.ds(p, 1)], ybuf.at[slot, pl.ds(r, 1)], sem)

    def issue(r, carry):
        row_copy(0, r, pos_ref[base + r]).start()
        row_copy(1, r, pos_ref[tp + base + r]).start()
        return carry

    def drain(r, carry):
        row_copy(0, r, pos_ref[base + r]).wait()
        row_copy(1, r, pos_ref[tp + base + r]).wait()
        return carry

    lax.fori_loop(0, tc, issue, 0)
    lax.fori_loop(0, tc, drain, 0)
    w = w_ref[...]
    h = h_ref[...] + w[:, 0:1] * ybuf[0] + w[:, 1:2] * ybuf[1]
    ho_ref[...] = h
    hn_ref[...] = _rms(h, g_ref[...]).astype(hn_ref.dtype)


def moe_combine(pos_flat, h, w_col, y, gain, hn_dtype, tc=256):
    tp, d = h.shape
    return pl.pallas_call(
        functools.partial(_combine_kernel, tc=tc, tp=tp),
        out_shape=(jax.ShapeDtypeStruct((tp, d), F32), jax.ShapeDtypeStruct((tp, d), hn_dtype)),
        grid_spec=pltpu.PrefetchScalarGridSpec(
            num_scalar_prefetch=1,
            grid=(tp // tc,),
            in_specs=[
                pl.BlockSpec((tc, d), lambda i, p: (i, 0)),
                pl.BlockSpec((tc, 2), lambda i, p: (i, 0)),
                pl.BlockSpec((1, d), lambda i, p: (0, 0)),
                pl.BlockSpec(memory_space=pl.ANY),
            ],
            out_specs=(
                pl.BlockSpec((tc, d), lambda i, p: (i, 0)),
                pl.BlockSpec((tc, d), lambda i, p: (i, 0)),
            ),
            scratch_shapes=[pltpu.VMEM((2, tc, d), F32), pltpu.SemaphoreType.DMA],
        ),
        compiler_params=_params(("arbitrary",)),
        name="moe_combine",
    )(pos_flat, h, w_col, gain.reshape(1, d), y)


def hierarchical_moe(h, hn, w_rg, b_rg, w_re, b_re, w_gate, w_up, w_down, next_gain, hn_dtype):
    tp, d = h.shape
    f = w_gate.shape[-1]
    idx, w_rows, cnt = moe_router(hn, w_rg, b_rg, w_re, b_re)
    pos, meta = moe_slots(cnt, idx)
    n_tiles = (2 * tp) // TILE_M + N_EXPERTS
    pos_flat = pos[:2].reshape(2 * tp)
    tile_expert = meta[0, :n_tiles]
    n_used = meta[1, :1]
    xs = moe_dispatch(pos_flat, hn, n_tiles * TILE_M)
    y = moe_experts(tile_expert, n_used, xs,
                    w_gate.reshape(N_EXPERTS, d, f).astype(BF16),
                    w_up.reshape(N_EXPERTS, d, f).astype(BF16),
                    w_down.reshape(N_EXPERTS, f, d).astype(BF16))
    return moe_combine(pos_flat, h, w_rows[:2].T, y, next_gain, hn_dtype)


def kernel(x, meta_tokens, norm_mixer, norm_ffn, norm_final, ret_w_in, ret_gn, ret_w_out,
           fox_w_in, fox_b_f, fox_w_out, moe_w_rg, moe_b_rg, moe_w_re, moe_b_re,
           moe_w_gate, moe_w_up, moe_w_down):
    bsz, seq, d = x.shape
    depth = norm_mixer.shape[0]
    nc = (seq + CHUNK) // CHUNK
    lp = nc * CHUNK
    h, hn = embed_norm(x, meta_tokens.astype(x.dtype), norm_mixer[0], BF16)
    for i in range(depth):
        j = i // 2
        if i % 2 == 0:
            proj = matmul(hn, ret_w_in[j].astype(BF16), BF16)
            gated = retention_core(proj, ret_gn[j], bsz, nc)
            h = matmul(gated, ret_w_out[j].astype(BF16), F32, residual=h)
        else:
            qkv = matmul(hn, fox_w_in[j][:, :3 * d].astype(BF16), BF16)
            c_std = forget_cumsum(hn, fox_w_in[j][:, 3 * d:], fox_b_f[j], bsz, nc)
            o = fox_attention(qkv, c_std, bsz, lp)
            h = matmul(o, fox_w_out[j].astype(BF16), F32, residual=h)
        hn_f = rms_norm(h, norm_ffn[i], F32)
        last = i == depth - 1
        next_gain = norm_final if last else norm_mixer[i + 1]
        h, hn = hierarchical_moe(h, hn_f, moe_w_rg[i], moe_b_rg[i], moe_w_re[i], moe_b_re[i],
                                 moe_w_gate[i], moe_w_up[i], moe_w_down[i], next_gain,
                                 F32 if last else BF16)
    return hn.reshape(bsz, lp, d)[:, CHUNK:]
```

```python
import functools

import jax
import jax.numpy as jnp
from jax import lax
from jax.experimental import pallas as pl
from jax.experimental.pallas import tpu as pltpu

N_META = 16
CHUNK = 128
PAD = CHUNK - N_META
RMS_EPS = 1e-6
GN_EPS = 1e-6
NEG_INF = -1e30
RET_HEADS = 8
FOX_HEADS = 16
N_GROUPS = 4
EXPERTS_PER_GROUP = 8
N_EXPERTS = N_GROUPS * EXPERTS_PER_GROUP
ROPE_BASE = 10000.0

LANES = 128
ROUTER_ROWS = 48
EXPERT_ROW0 = N_GROUPS
TILE_M = 256
VMEM_LIMIT = 48 * 1024 * 1024

F32 = jnp.float32
BF16 = jnp.bfloat16


def _params(sem, vmem=VMEM_LIMIT):
    return pltpu.CompilerParams(dimension_semantics=sem, vmem_limit_bytes=vmem)


def _rms(h, g):
    return h * lax.rsqrt(jnp.mean(h * h, axis=-1, keepdims=True) + RMS_EPS) * g


def _sigmoid(x):
    return 1.0 / (1.0 + jnp.exp(-x))


def _embed_norm_kernel(x_ref, meta_ref, g_ref, h_ref, hn_ref):
    i = pl.program_id(1)

    @pl.when(i == 0)
    def _():
        h_ref[...] = jnp.zeros_like(h_ref)
        h_ref[PAD:, :] = meta_ref[...]

    @pl.when(i > 0)
    def _():
        h_ref[...] = x_ref[...]

    hn_ref[...] = _rms(h_ref[...], g_ref[...]).astype(hn_ref.dtype)


def embed_norm(x, meta, gain, hn_dtype):
    bsz, seq, d = x.shape
    nc = (seq + CHUNK) // CHUNK
    tp = bsz * nc * CHUNK
    return pl.pallas_call(
        _embed_norm_kernel,
        out_shape=(jax.ShapeDtypeStruct((tp, d), F32), jax.ShapeDtypeStruct((tp, d), hn_dtype)),
        grid=(bsz, nc),
        in_specs=[
            pl.BlockSpec((None, CHUNK, d), lambda b, i: (b, jnp.maximum(i - 1, 0), 0)),
            pl.BlockSpec((N_META, d), lambda b, i: (0, 0)),
            pl.BlockSpec((1, d), lambda b, i: (0, 0)),
        ],
        out_specs=(
            pl.BlockSpec((CHUNK, d), lambda b, i: (b * nc + i, 0)),
            pl.BlockSpec((CHUNK, d), lambda b, i: (b * nc + i, 0)),
        ),
        compiler_params=_params(("parallel", "parallel")),
        name="embed_norm",
    )(x, meta, gain.reshape(1, d))


def _norm_kernel(h_ref, g_ref, o_ref):
    o_ref[...] = _rms(h_ref[...], g_ref[...]).astype(o_ref.dtype)


def rms_norm(h, gain, out_dtype, tm=512):
    tp, d = h.shape
    return pl.pallas_call(
        _norm_kernel,
        out_shape=jax.ShapeDtypeStruct((tp, d), out_dtype),
        grid=(tp // tm,),
        in_specs=[pl.BlockSpec((tm, d), lambda i: (i, 0)), pl.BlockSpec((1, d), lambda i: (0, 0))],
        out_specs=pl.BlockSpec((tm, d), lambda i: (i, 0)),
        compiler_params=_params(("parallel",)),
        name="rms_norm",
    )(h, gain.reshape(1, d))


def _mm_kernel(*refs, has_res, scale_tiles, scale):
    if has_res:
        x_ref, w_ref, r_ref, o_ref, wb_ref = refs
    else:
        x_ref, w_ref, o_ref, wb_ref = refs

    @pl.when(pl.program_id(1) == 0)
    def _():
        wb_ref[...] = w_ref[...].astype(BF16)

    acc = jnp.dot(x_ref[...], wb_ref[...], preferred_element_type=F32)
    if scale_tiles:
        acc = acc * jnp.where(pl.program_id(0) < scale_tiles, scale, 1.0)
    if has_res:
        acc = acc + r_ref[...]
    o_ref[...] = acc.astype(o_ref.dtype)


def matmul(x, w, out_dtype, n=None, residual=None, tm=512, tn=1024, scale_cols=0, scale=1.0):
    m, kdim = x.shape
    n = w.shape[1] if n is None else n
    in_specs = [
        pl.BlockSpec((tm, kdim), lambda j, i: (i, 0)),
        pl.BlockSpec((kdim, tn), lambda j, i: (0, j)),
    ]
    args = [x, w]
    if residual is not None:
        in_specs.append(pl.BlockSpec((tm, tn), lambda j, i: (i, j)))
        args.append(residual)
    return pl.pallas_call(
        functools.partial(_mm_kernel, has_res=residual is not None, scale_tiles=scale_cols // tn, scale=scale),
        out_shape=jax.ShapeDtypeStruct((m, n), out_dtype),
        grid=(n // tn, m // tm),
        in_specs=in_specs,
        out_specs=pl.BlockSpec((tm, tn), lambda j, i: (i, j)),
        scratch_shapes=[pltpu.VMEM((kdim, tn), BF16)],
        compiler_params=_params(("parallel", "arbitrary")),
        name="matmul",
    )(*args)


def _retention_kernel(q_ref, k_ref, v_ref, g_ref, cos_ref, sin_ref, dmat_ref, xi_ref, zeta_ref,
                      gch_ref, gn_ref, o_ref, state_ref, *, dk):
    c = pl.program_id(2)

    @pl.when(c == 0)
    def _():
        state_ref[...] = jnp.zeros_like(state_ref)

    cos = cos_ref[...]
    sin = sin_ref[...]
    half = dk // 2

    def rot(u):
        u1, u2 = u[:, :half], u[:, half:]
        return jnp.concatenate([u1 * cos - u2 * sin, u1 * sin + u2 * cos], axis=1)

    row = lax.broadcasted_iota(jnp.int32, (CHUNK, 1), 0)
    valid = jnp.logical_or(row >= PAD, c > 0)
    q = rot(q_ref[...].astype(F32))
    k = rot(k_ref[...].astype(F32)) * (dk ** -0.5)
    k = jnp.where(valid, k, 0.0)
    v = jnp.where(valid, v_ref[...].astype(F32), 0.0)
    qb = q.astype(BF16)
    kb = k.astype(BF16)
    vb = v.astype(BF16)

    scores = lax.dot_general(qb, kb, (((1,), (1,)), ((), ())), preferred_element_type=F32)
    scores = scores * dmat_ref[...]
    inner = jnp.dot(scores.astype(BF16), vb, preferred_element_type=F32)
    state = state_ref[...]
    cross = jnp.dot(qb, state.astype(BF16), preferred_element_type=F32) * xi_ref[...]
    y = inner + cross
    vz = (v * zeta_ref[...]).astype(BF16)
    state_ref[...] = gch_ref[...] * state + lax.dot_general(
        kb, vz, (((0,), (0,)), ((), ())), preferred_element_type=F32)

    mu = jnp.mean(y, axis=-1, keepdims=True)
    dlt = y - mu
    var = jnp.mean(dlt * dlt, axis=-1, keepdims=True)
    yn = dlt * lax.rsqrt(var + GN_EPS) * gn_ref[...]
    g = g_ref[...].astype(F32)
    o_ref[...] = (g * _sigmoid(g) * yn).astype(o_ref.dtype)


def retention_core(proj, gn_gain, bsz, nc):
    tp = proj.shape[0]
    d = proj.shape[1] // 6
    nh = RET_HEADS
    dk = d // nh
    dv = 2 * d // nh
    lp = nc * CHUNK
    half = dk // 2
    pos = (jnp.arange(lp) - PAD).astype(F32)
    inv = ROPE_BASE ** (-jnp.arange(half, dtype=F32) / half)
    ang = pos[:, None] * inv[None, :]
    cos, sin = jnp.cos(ang), jnp.sin(ang)
    log_g = jnp.log1p(-jnp.exp2(-5.0 - jnp.arange(nh, dtype=F32)))
    idx = jnp.arange(CHUNK, dtype=F32)
    diff = idx[:, None] - idx[None, :]
    dmat = jnp.where(diff[None] >= 0, jnp.exp(jnp.maximum(diff, 0.0)[None] * log_g[:, None, None]), 0.0)
    xi = jnp.exp((idx + 1.0)[None, :] * log_g[:, None])[:, :, None]
    zeta = jnp.exp((CHUNK - 1.0 - idx)[None, :] * log_g[:, None])[:, :, None]
    g_chunk = jnp.exp(CHUNK * log_g)[:, None, None]
    kq = d // dk
    kv = 2 * d // dv
    kg = kv + nh
    return pl.pallas_call(
        functools.partial(_retention_kernel, dk=dk),
        out_shape=jax.ShapeDtypeStruct((tp, nh * dv), BF16),
        grid=(bsz, nh, nc),
        in_specs=[
            pl.BlockSpec((CHUNK, dk), lambda b, h, c: (b * nc + c, h)),
            pl.BlockSpec((CHUNK, dk), lambda b, h, c: (b * nc + c, kq + h)),
            pl.BlockSpec((CHUNK, dv), lambda b, h, c: (b * nc + c, kv + h)),
            pl.BlockSpec((CHUNK, dv), lambda b, h, c: (b * nc + c, kg + h)),
            pl.BlockSpec((CHUNK, half), lambda b, h, c: (c, 0)),
            pl.BlockSpec((CHUNK, half), lambda b, h, c: (c, 0)),
            pl.BlockSpec((None, CHUNK, CHUNK), lambda b, h, c: (h, 0, 0)),
            pl.BlockSpec((None, CHUNK, 1), lambda b, h, c: (h, 0, 0)),
            pl.BlockSpec((None, CHUNK, 1), lambda b, h, c: (h, 0, 0)),
            pl.BlockSpec((None, 1, 1), lambda b, h, c: (h, 0, 0)),
            pl.BlockSpec((None, 1, dv), lambda b, h, c: (h, 0, 0)),
        ],
        out_specs=pl.BlockSpec((CHUNK, dv), lambda b, h, c: (b * nc + c, h)),
        scratch_shapes=[pltpu.VMEM((dk, dv), F32)],
        compiler_params=_params(("parallel", "parallel", "arbitrary")),
        name="retention_core",
    )(proj, proj, proj, proj, cos, sin, dmat, xi, zeta, g_chunk, gn_gain.reshape(nh, 1, dv))


LOG2E = 1.4426950408889634
MASK_BIG = 1e30
ONES_LANE = LANES - 1


def _bias_selectors(nh):
    h = jnp.arange(nh)
    selq = jnp.zeros((nh, 3 * LANES, LANES), F32)
    selk = jnp.zeros((nh, 3 * LANES, LANES), F32)
    for part in range(3):
        selq = selq.at[h, part * LANES + h, part].set(1.0)
        selk = selk.at[h, part * LANES + h, 3 + part].set(-1.0)
        selq = selq.at[h, ONES_LANE, 3 + part].set(1.0)
        selk = selk.at[h, ONES_LANE, part].set(1.0)
    return selq.astype(BF16), selk.astype(BF16)


def _forget_kernel(hn_ref, wf_ref, bf_ref, selq_ref, selk_ref, cq_ref, ck_ref, carry_ref, *, nh):
    i = pl.program_id(1)

    @pl.when(i == 0)
    def _():
        carry_ref[...] = jnp.zeros_like(carry_ref)

    z = jnp.dot(hn_ref[...], wf_ref[...], preferred_element_type=F32) + bf_ref[...]
    lf = jnp.minimum(z, 0.0) - jnp.log1p(jnp.exp(-jnp.abs(z)))
    row = lax.broadcasted_iota(jnp.int32, (CHUNK, 1), 0)
    valid = jnp.logical_or(row >= PAD, i > 0)
    lf = jnp.where(valid, lf, 0.0)

    def split3(a):
        hi = a.astype(BF16)
        r1 = a - hi.astype(F32)
        mid = r1.astype(BF16)
        lo = (r1 - mid.astype(F32)).astype(BF16)
        return hi, mid, lo

    r_i = lax.broadcasted_iota(jnp.int32, (CHUNK, CHUNK), 0)
    c_i = lax.broadcasted_iota(jnp.int32, (CHUNK, CHUNK), 1)
    tri = (r_i >= c_i).astype(BF16)
    cs = sum(jnp.dot(tri, part, preferred_element_type=F32) for part in split3(lf))
    cs = cs + carry_ref[...]
    carry_ref[...] = cs[CHUNK - 1:CHUNK, :]

    hi, mid, lo = (p.astype(F32) for p in split3(cs * LOG2E))
    lane = lax.broadcasted_iota(jnp.int32, (CHUNK, LANES), 1)
    ones = lane == ONES_LANE
    live = jnp.logical_and(valid, jnp.logical_not(ones))
    pq = jnp.concatenate([jnp.where(ones, 1.0, hi), jnp.where(ones, 0.0, mid), jnp.where(ones, 0.0, lo)],
                         axis=1).astype(BF16)
    pk = jnp.concatenate([jnp.where(ones, 1.0, jnp.where(valid, hi, MASK_BIG)),
                          jnp.where(live, mid, 0.0), jnp.where(live, lo, 0.0)], axis=1).astype(BF16)
    for h in range(nh):
        cq_ref[:, h * LANES:(h + 1) * LANES] = jnp.dot(
            pq, selq_ref[h], preferred_element_type=F32).astype(BF16)
        ck_ref[:, h * LANES:(h + 1) * LANES] = jnp.dot(
            pk, selk_ref[h], preferred_element_type=F32).astype(BF16)


def forget_bias(hn, w_f, b_f, bsz, nc):
    tp, d = hn.shape
    nh = w_f.shape[1]
    wf = jnp.zeros((d, LANES), BF16).at[:, :nh].set(w_f.astype(BF16))
    bf = jnp.zeros((1, LANES), F32).at[0, :nh].set(b_f.astype(F32))
    selq, selk = _bias_selectors(nh)
    return pl.pallas_call(
        functools.partial(_forget_kernel, nh=nh),
        out_shape=(jax.ShapeDtypeStruct((tp, nh * LANES), BF16), jax.ShapeDtypeStruct((tp, nh * LANES), BF16)),
        grid=(bsz, nc),
        in_specs=[
            pl.BlockSpec((CHUNK, d), lambda b, i: (b * nc + i, 0)),
            pl.BlockSpec((d, LANES), lambda b, i: (0, 0)),
            pl.BlockSpec((1, LANES), lambda b, i: (0, 0)),
            pl.BlockSpec((nh, 3 * LANES, LANES), lambda b, i: (0, 0, 0)),
            pl.BlockSpec((nh, 3 * LANES, LANES), lambda b, i: (0, 0, 0)),
        ],
        out_specs=(
            pl.BlockSpec((CHUNK, nh * LANES), lambda b, i: (b * nc + i, 0)),
            pl.BlockSpec((CHUNK, nh * LANES), lambda b, i: (b * nc + i, 0)),
        ),
        scratch_shapes=[pltpu.VMEM((1, LANES), F32)],
        compiler_params=_params(("parallel", "arbitrary")),
        name="forget_bias",
    )(hn, wf, bf, selq, selk)


FOX_HEADS_PER_STEP = 4


def _fox_kernel(q_ref, cq_ref, k_ref, ck_ref, v_ref, o_ref, m_ref, l_ref, acc_ref, *, tq, dh):
    i = pl.program_id(2)
    hs = FOX_HEADS_PER_STEP
    reps = tq // LANES
    m_ref[...] = jnp.full_like(m_ref, 10.0 * NEG_INF)
    l_ref[...] = jnp.zeros_like(l_ref)
    acc_ref[...] = jnp.zeros_like(acc_ref)
    qa = [jnp.concatenate([q_ref[:, j * dh:(j + 1) * dh], cq_ref[:, j * LANES:(j + 1) * LANES]], axis=1)
          for j in range(hs)]

    def block(kb, causal):
        start = pl.multiple_of(kb * tq, tq)
        for j in range(hs):
            ka = jnp.concatenate([k_ref[pl.ds(start, tq), j * dh:(j + 1) * dh],
                                  ck_ref[pl.ds(start, tq), j * LANES:(j + 1) * LANES]], axis=1)
            s = lax.dot_general(qa[j], ka, (((1,), (1,)), ((), ())), preferred_element_type=F32)
            if causal:
                r_i = lax.broadcasted_iota(jnp.int32, (tq, tq), 0)
                c_i = lax.broadcasted_iota(jnp.int32, (tq, tq), 1)
                s = jnp.where(c_i <= r_i, s, NEG_INF)
            m_old = m_ref[j]
            m_new = jnp.maximum(m_old, jnp.max(s, axis=1, keepdims=True))
            alpha = jnp.exp2(m_old - m_new)
            p = jnp.exp2(s - jnp.concatenate([m_new] * reps, axis=1))
            l_ref[j] = alpha * l_ref[j] + jnp.sum(p, axis=1, keepdims=True)
            acc_ref[j] = alpha * acc_ref[j] + jnp.dot(
                p.astype(BF16), v_ref[pl.ds(start, tq), j * dh:(j + 1) * dh], preferred_element_type=F32)
            m_ref[j] = m_new

    def body(kb, carry):
        block(kb, False)
        return carry

    lax.fori_loop(0, i, body, 0)
    block(i, True)
    for j in range(hs):
        o_ref[:, j * dh:(j + 1) * dh] = (acc_ref[j] / l_ref[j]).astype(o_ref.dtype)


def fox_attention(qkv, cq, ck, bsz, lp, tq=384):
    tp = qkv.shape[0]
    d = qkv.shape[1] // 3
    nh = FOX_HEADS
    dh = d // nh
    assert dh == LANES
    nq = lp // tq
    hs = FOX_HEADS_PER_STEP
    ng = nh // hs
    qkv3 = qkv.reshape(bsz, lp, 3 * d)
    ck3 = ck.reshape(bsz, lp, nh * LANES)
    return pl.pallas_call(
        functools.partial(_fox_kernel, tq=tq, dh=dh),
        out_shape=jax.ShapeDtypeStruct((tp, d), BF16),
        grid=(bsz, ng, nq),
        in_specs=[
            pl.BlockSpec((tq, hs * dh), lambda b, g, i: (b * nq + i, g)),
            pl.BlockSpec((tq, hs * LANES), lambda b, g, i: (b * nq + i, g)),
            pl.BlockSpec((None, lp, hs * dh), lambda b, g, i: (b, 0, ng + g)),
            pl.BlockSpec((None, lp, hs * LANES), lambda b, g, i: (b, 0, g)),
            pl.BlockSpec((None, lp, hs * dh), lambda b, g, i: (b, 0, 2 * ng + g)),
        ],
        out_specs=pl.BlockSpec((tq, hs * dh), lambda b, g, i: (b * nq + i, g)),
        scratch_shapes=[pltpu.VMEM((hs, tq, LANES), F32), pltpu.VMEM((hs, tq, LANES), F32),
                        pltpu.VMEM((hs, tq, dh), F32)],
        compiler_params=_params(("parallel", "parallel", "arbitrary")),
        name="fox_attention",
    )(qkv, cq, qkv3, ck3, qkv3)


def _router_kernel(hn_ref, wr_ref, br_ref, idx_ref, w_ref, cnt_ref, carry_ref, *, tr):
    i = pl.program_id(0)

    @pl.when(i == 0)
    def _():
        carry_ref[...] = jnp.zeros_like(carry_ref)

    lg = lax.dot_general(wr_ref[...], hn_ref[...].astype(BF16), (((1,), (1,)), ((), ())),
                         preferred_element_type=F32) + br_ref[...]
    row = lax.broadcasted_iota(jnp.int32, (ROUTER_ROWS, tr), 0)
    big = jnp.int32(1 << 20)
    is_g = row < N_GROUPS
    mg = jnp.max(jnp.where(is_g, lg, -jnp.inf), axis=0, keepdims=True)
    g_sel = jnp.min(jnp.where(jnp.logical_and(is_g, lg == mg), row, big), axis=0, keepdims=True)
    sg = jnp.sum(jnp.where(is_g, jnp.exp(lg - mg), 0.0), axis=0, keepdims=True)
    p_g = 1.0 / sg
    lo = EXPERT_ROW0 + EXPERTS_PER_GROUP * g_sel
    is_e = jnp.logical_and(row >= lo, row < lo + EXPERTS_PER_GROUP)
    me = jnp.max(jnp.where(is_e, lg, -jnp.inf), axis=0, keepdims=True)
    ee = jnp.where(is_e, jnp.exp(lg - me), 0.0)
    pe = ee / jnp.sum(ee, axis=0, keepdims=True)
    pe1 = jnp.where(is_e, pe, -1.0)
    m1 = jnp.max(pe1, axis=0, keepdims=True)
    i1 = jnp.min(jnp.where(pe1 == m1, row, big), axis=0, keepdims=True)
    pe2 = jnp.where(row == i1, -1.0, pe1)
    m2 = jnp.max(pe2, axis=0, keepdims=True)
    i2 = jnp.min(jnp.where(pe2 == m2, row, big), axis=0, keepdims=True)
    den = m1 + m2
    w1 = p_g * (m1 / den)
    w2 = p_g * (m2 / den)

    oh1 = row == i1
    oh2 = row == i2
    ohs = jnp.logical_or(oh1, oh2).astype(F32)
    r_i = lax.broadcasted_iota(jnp.int32, (tr, tr), 0)
    c_i = lax.broadcasted_iota(jnp.int32, (tr, tr), 1)
    tri = (r_i < c_i).astype(BF16)
    cnt = jnp.dot(ohs.astype(BF16), tri, preferred_element_type=F32) + carry_ref[...]
    rank1 = jnp.sum(jnp.where(oh1, cnt, 0.0), axis=0, keepdims=True)
    rank2 = jnp.sum(jnp.where(oh2, cnt, 0.0), axis=0, keepdims=True)
    carry_ref[...] += jnp.sum(ohs, axis=1, keepdims=True)

    r8 = lax.broadcasted_iota(jnp.int32, (8, tr), 0)
    e1 = i1 - EXPERT_ROW0
    e2 = i2 - EXPERT_ROW0
    idx_ref[...] = jnp.where(r8 == 0, e1, jnp.where(r8 == 1, e2, jnp.where(
        r8 == 2, rank1.astype(jnp.int32), jnp.where(r8 == 3, rank2.astype(jnp.int32), 0))))
    w_ref[...] = jnp.where(r8 == 0, w1, jnp.where(r8 == 1, w2, 0.0))
    cnt_ref[...] = jnp.broadcast_to(carry_ref[...], cnt_ref.shape)


def moe_router(hn, w_rg, b_rg, w_re, b_re, tr=512):
    tp, d = hn.shape
    wr = jnp.zeros((ROUTER_ROWS, d), BF16)
    wr = wr.at[:N_GROUPS].set(w_rg.T.astype(BF16))
    wr = wr.at[EXPERT_ROW0:EXPERT_ROW0 + N_EXPERTS].set(w_re.reshape(d, N_EXPERTS).T.astype(BF16))
    br = jnp.zeros((ROUTER_ROWS, 1), F32)
    br = br.at[:N_GROUPS, 0].set(b_rg.astype(F32))
    br = br.at[EXPERT_ROW0:EXPERT_ROW0 + N_EXPERTS, 0].set(b_re.reshape(N_EXPERTS).astype(F32))
    return pl.pallas_call(
        functools.partial(_router_kernel, tr=tr),
        out_shape=(
            jax.ShapeDtypeStruct((8, tp), jnp.int32),
            jax.ShapeDtypeStruct((8, tp), F32),
            jax.ShapeDtypeStruct((ROUTER_ROWS, LANES), F32),
        ),
        grid=(tp // tr,),
        in_specs=[
            pl.BlockSpec((tr, d), lambda i: (i, 0)),
            pl.BlockSpec((ROUTER_ROWS, d), lambda i: (0, 0)),
            pl.BlockSpec((ROUTER_ROWS, 1), lambda i: (0, 0)),
        ],
        out_specs=(
            pl.BlockSpec((8, tr), lambda i: (0, i)),
            pl.BlockSpec((8, tr), lambda i: (0, i)),
            pl.BlockSpec((ROUTER_ROWS, LANES), lambda i: (0, 0)),
        ),
        scratch_shapes=[pltpu.VMEM((ROUTER_ROWS, 1), F32)],
        compiler_params=_params(("arbitrary",)),
        name="moe_router",
    )(hn, wr, br)


def _slots_kernel(cnt_ref, idx_ref, pos_ref, meta_ref):
    e1 = idx_ref[0:1, :]
    e2 = idx_ref[1:2, :]
    off1 = jnp.zeros_like(e1)
    off2 = jnp.zeros_like(e2)
    tile = lax.broadcasted_iota(jnp.int32, (1, meta_ref.shape[1]), 1)
    tile_expert = jnp.zeros_like(tile)
    tile_rows = jnp.zeros_like(tile)
    start = jnp.int32(0)
    for e in range(N_EXPERTS):
        off1 = jnp.where(e1 == e, start, off1)
        off2 = jnp.where(e2 == e, start, off2)
        live_end = start + cnt_ref[e]
        end = start + ((cnt_ref[e] + (TILE_M - 1)) // TILE_M) * TILE_M
        mine = jnp.logical_and(tile * TILE_M >= start, tile * TILE_M < end)
        tile_rows = jnp.where(mine, jnp.minimum(live_end - tile * TILE_M, TILE_M), tile_rows)
        tile_expert = tile_expert + (tile * TILE_M >= end).astype(jnp.int32)
        start = end
    r8 = lax.broadcasted_iota(jnp.int32, pos_ref.shape, 0)
    pos_ref[...] = jnp.where(r8 == 0, off1 + idx_ref[2:3, :], jnp.where(r8 == 1, off2 + idx_ref[3:4, :], 0))
    m8 = lax.broadcasted_iota(jnp.int32, meta_ref.shape, 0)
    meta_ref[...] = jnp.where(m8 == 0, jnp.minimum(tile_expert, N_EXPERTS - 1),
                              jnp.where(m8 == 1, tile_rows, start // TILE_M))


def moe_slots(counts, idx, tr=512):
    tp = idx.shape[1]
    nt_lanes = 2 * LANES
    return pl.pallas_call(
        _slots_kernel,
        out_shape=(jax.ShapeDtypeStruct((8, tp), jnp.int32), jax.ShapeDtypeStruct((8, nt_lanes), jnp.int32)),
        grid_spec=pltpu.PrefetchScalarGridSpec(
            num_scalar_prefetch=1,
            grid=(tp // tr,),
            in_specs=[pl.BlockSpec((8, tr), lambda i, c: (0, i))],
            out_specs=(
                pl.BlockSpec((8, tr), lambda i, c: (0, i)),
                pl.BlockSpec((8, nt_lanes), lambda i, c: (0, 0)),
            ),
        ),
        compiler_params=_params(("arbitrary",)),
        name="moe_slots",
    )(counts, idx)


def _invert_kernel(pos_ref, cnt_ref, src_ref, dst_ref, *, tp, n_rows):
    def fill(lo, hi):
        def body(r, carry):
            src_ref[r] = 0
            dst_ref[r] = 0
            return carry
        lax.fori_loop(lo, hi, body, 0)

    def per_expert(e, start):
        n = cnt_ref[e]
        end = start + ((n + (TILE_M - 1)) // TILE_M) * TILE_M
        fill(start + n, end)
        return end

    used = lax.fori_loop(0, N_EXPERTS, per_expert, jnp.int32(0))
    fill(used, n_rows)

    def per_token(t, carry):
        p1 = pos_ref[t]
        p2 = pos_ref[tp + t]
        src_ref[p1] = t
        dst_ref[p1] = t
        src_ref[p2] = t
        dst_ref[p2] = tp + t
        return carry

    lax.fori_loop(0, tp, per_token, 0, unroll=8)


def moe_invert(pos_flat, counts, n_rows):
    tp = pos_flat.shape[0] // 2
    return pl.pallas_call(
        functools.partial(_invert_kernel, tp=tp, n_rows=n_rows),
        out_shape=(jax.ShapeDtypeStruct((n_rows,), jnp.int32), jax.ShapeDtypeStruct((n_rows,), jnp.int32)),
        grid_spec=pltpu.PrefetchScalarGridSpec(
            num_scalar_prefetch=2,
            grid=(1,),
            in_specs=[],
            out_specs=(pl.BlockSpec(memory_space=pltpu.SMEM), pl.BlockSpec(memory_space=pltpu.SMEM)),
        ),
        compiler_params=_params(("arbitrary",)),
        name="moe_invert",
    )(pos_flat, counts)


def _expert_kernel(te_ref, rows_ref, nu_ref, src_ref, dst_ref, hn_ref, wg_ref, wu_ref, wd_ref, yt_ref,
                   xbuf, ybuf, wgb, wub, wdb, gsem, ssem):
    j = pl.program_id(0)
    nu = nu_ref[0]
    slot = lax.rem(j, 2)

    def gather_start(tile, buf):
        base = tile * TILE_M

        def body(r, carry):
            pltpu.make_async_copy(hn_ref.at[pl.ds(src_ref[base + r], 1)], xbuf.at[buf, pl.ds(r, 1)],
                                  gsem.at[buf]).start()
            return carry
        lax.fori_loop(0, TILE_M, body, 0, unroll=8)

    def gather_wait(buf):
        pltpu.make_async_copy(hn_ref.at[pl.ds(0, TILE_M)], xbuf.at[buf], gsem.at[buf]).wait()

    def scatter_start(tile, buf, n):
        base = tile * TILE_M

        def body(r, carry):
            pltpu.make_async_copy(ybuf.at[buf, pl.ds(r, 1)], yt_ref.at[pl.ds(dst_ref[base + r], 1)],
                                  ssem.at[buf]).start()
            return carry
        lax.fori_loop(0, n, body, 0)

    def scatter_wait(buf, n):
        n8 = pl.multiple_of((n // 8) * 8, 8)

        @pl.when(n8 > 0)
        def _():
            pltpu.make_async_copy(ybuf.at[buf, pl.ds(0, n8)], yt_ref.at[pl.ds(0, n8)], ssem.at[buf]).wait()

        def body(r, carry):
            pltpu.make_async_copy(ybuf.at[buf, pl.ds(0, 1)], yt_ref.at[pl.ds(0, 1)], ssem.at[buf]).wait()
            return carry
        lax.fori_loop(0, n - n8, body, 0)

    @pl.when(j == 0)
    def _():
        gather_start(0, 0)

    @pl.when(j < nu)
    def _():
        gather_wait(slot)

        @pl.when(j + 1 < nu)
        def _():
            gather_start(j + 1, 1 - slot)

        @pl.when(j >= 2)
        def _():
            scatter_wait(slot, rows_ref[jnp.maximum(j - 2, 0)])

        first = jnp.logical_or(j == 0, te_ref[j] != te_ref[jnp.maximum(j - 1, 0)])

        @pl.when(first)
        def _():
            wgb[...] = wg_ref[...].astype(BF16)
            wub[...] = wu_ref[...].astype(BF16)
            wdb[...] = wd_ref[...].astype(BF16)

        x = xbuf[slot].astype(BF16)
        a = jnp.dot(x, wgb[...], preferred_element_type=F32)
        u = jnp.dot(x, wub[...], preferred_element_type=F32)
        hid = (a * _sigmoid(a) * u).astype(BF16)
        ybuf[slot] = jnp.dot(hid, wdb[...], preferred_element_type=F32)
        scatter_start(j, slot, rows_ref[j])

        @pl.when(j == nu - 1)
        def _():
            scatter_wait(slot, rows_ref[j])

            @pl.when(j >= 1)
            def _():
                scatter_wait(1 - slot, rows_ref[jnp.maximum(j - 1, 0)])


def moe_experts(tile_expert, tile_rows, n_used, src, dst, hn, w_gate, w_up, w_down):
    tp, d = hn.shape
    f = w_gate.shape[-1]
    nt = tile_expert.shape[0]

    def w_map(j, te, rows, nu, src, dst):
        return (te[j], 0, 0)

    return pl.pallas_call(
        _expert_kernel,
        out_shape=jax.ShapeDtypeStruct((2 * tp, d), F32),
        grid_spec=pltpu.PrefetchScalarGridSpec(
            num_scalar_prefetch=5,
            grid=(nt,),
            in_specs=[
                pl.BlockSpec(memory_space=pl.ANY),
                pl.BlockSpec((None, d, f), w_map),
                pl.BlockSpec((None, d, f), w_map),
                pl.BlockSpec((None, f, d), w_map),
            ],
            out_specs=pl.BlockSpec(memory_space=pl.ANY),
            scratch_shapes=[
                pltpu.VMEM((2, TILE_M, d), F32), pltpu.VMEM((2, TILE_M, d), F32),
                pltpu.VMEM((d, f), BF16), pltpu.VMEM((d, f), BF16), pltpu.VMEM((f, d), BF16),
                pltpu.SemaphoreType.DMA((2,)), pltpu.SemaphoreType.DMA((2,)),
            ],
        ),
        compiler_params=_params(("arbitrary",), vmem=56 * 1024 * 1024),
        name="moe_experts",
    )(tile_expert, tile_rows, n_used, src, dst, hn, w_gate, w_up, w_down)


def _combine_kernel(h_ref, w_ref, g_ref, y0_ref, y1_ref, *out_refs, write_h):
    w = w_ref[...]
    h = h_ref[...] + w[:, 0:1] * y0_ref[...] + w[:, 1:2] * y1_ref[...]
    if write_h:
        out_refs[0][...] = h
    out_refs[-1][...] = _rms(h, g_ref[...]).astype(out_refs[-1].dtype)


def moe_combine(h, w_col, yt, gain, hn_dtype, final_shape=None):
    tp, d = h.shape
    yt3 = yt.reshape(2, tp, d)
    if final_shape is None:
        tc = 256
        out_shape = (jax.ShapeDtypeStruct((tp, d), F32), jax.ShapeDtypeStruct((tp, d), hn_dtype))
        out_specs = (pl.BlockSpec((tc, d), lambda i: (i, 0)), pl.BlockSpec((tc, d), lambda i: (i, 0)))
    else:
        tc = CHUNK
        bsz, seq, _ = final_shape
        nc = tp // bsz // tc
        out_shape = (jax.ShapeDtypeStruct(final_shape, hn_dtype),)
        out_specs = (pl.BlockSpec((None, tc, d), lambda i: (i // nc, jnp.maximum(i % nc - 1, 0), 0)),)
    return pl.pallas_call(
        functools.partial(_combine_kernel, write_h=final_shape is None),
        out_shape=out_shape,
        grid=(tp // tc,),
        in_specs=[
            pl.BlockSpec((tc, d), lambda i: (i, 0)),
            pl.BlockSpec((tc, 2), lambda i: (i, 0)),
            pl.BlockSpec((1, d), lambda i: (0, 0)),
            pl.BlockSpec((None, tc, d), lambda i: (0, i, 0)),
            pl.BlockSpec((None, tc, d), lambda i: (1, i, 0)),
        ],
        out_specs=out_specs,
        compiler_params=_params(("arbitrary",)),
        name="moe_combine",
    )(h, w_col, gain.reshape(1, d), yt3, yt3)


def hierarchical_moe(h, hn, w_rg, b_rg, w_re, b_re, w_gate, w_up, w_down, next_gain, hn_dtype,
                     final_shape=None):
    tp, d = h.shape
    f = w_gate.shape[-1]
    idx, w_rows, cnt = moe_router(hn, w_rg, b_rg, w_re, b_re)
    counts = cnt[EXPERT_ROW0:EXPERT_ROW0 + N_EXPERTS, 0].astype(jnp.int32)
    pos, meta = moe_slots(counts, idx)
    n_tiles = (2 * tp) // TILE_M + N_EXPERTS
    pos_flat = pos[:2].reshape(2 * tp)
    src, dst = moe_invert(pos_flat, counts, n_tiles * TILE_M)
    yt = moe_experts(meta[0, :n_tiles], meta[1, :n_tiles], meta[2, :1], src, dst, hn,
                     w_gate.reshape(N_EXPERTS, d, f), w_up.reshape(N_EXPERTS, d, f),
                     w_down.reshape(N_EXPERTS, f, d))
    return moe_combine(h, w_rows[:2].T, yt, next_gain, hn_dtype, final_shape)


def kernel(x, meta_tokens, norm_mixer, norm_ffn, norm_final, ret_w_in, ret_gn, ret_w_out,
           fox_w_in, fox_b_f, fox_w_out, moe_w_rg, moe_b_rg, moe_w_re, moe_b_re,
           moe_w_gate, moe_w_up, moe_w_down):
    bsz, seq, d = x.shape
    depth = norm_mixer.shape[0]
    nc = (seq + CHUNK) // CHUNK
    lp = nc * CHUNK
    h, hn = embed_norm(x, meta_tokens.astype(x.dtype), norm_mixer[0], BF16)
    for i in range(depth):
        j = i // 2
        if i % 2 == 0:
            proj = matmul(hn, ret_w_in[j], BF16)
            gated = retention_core(proj, ret_gn[j], bsz, nc)
            h = matmul(gated, ret_w_out[j], F32, residual=h, tn=512)
        else:
            qkv = matmul(hn, fox_w_in[j], BF16, n=3 * d, scale_cols=d, scale=(d // FOX_HEADS) ** -0.5 * LOG2E)
            cq, ck = forget_bias(hn, fox_w_in[j][:, 3 * d:], fox_b_f[j], bsz, nc)
            o = fox_attention(qkv, cq, ck, bsz, lp)
            h = matmul(o, fox_w_out[j], F32, residual=h)
        hn_f = rms_norm(h, norm_ffn[i], F32)
        last = i == depth - 1
        outs = hierarchical_moe(h, hn_f, moe_w_rg[i], moe_b_rg[i], moe_w_re[i], moe_b_re[i],
                                moe_w_gate[i], moe_w_up[i], moe_w_down[i],
                                norm_final if last else norm_mixer[i + 1],
                                F32 if last else BF16, (bsz, seq, d) if last else None)
        if last:
            return outs[0]
        h, hn = outs
```

```python
import functools

import jax
import jax.numpy as jnp
from jax import lax
from jax.experimental import pallas as pl
from jax.experimental.pallas import tpu as pltpu

N_META = 16
CHUNK = 128
PAD = CHUNK - N_META
RMS_EPS = 1e-6
GN_EPS = 1e-6
NEG_INF = -1e30
RET_HEADS = 8
FOX_HEADS = 16
N_GROUPS = 4
EXPERTS_PER_GROUP = 8
N_EXPERTS = N_GROUPS * EXPERTS_PER_GROUP
ROPE_BASE = 10000.0

LANES = 128
ROUTER_ROWS = 48
EXPERT_ROW0 = N_GROUPS
TILE_M = 256
VMEM_LIMIT = 48 * 1024 * 1024

F32 = jnp.float32
BF16 = jnp.bfloat16


def _params(sem, vmem=VMEM_LIMIT):
    return pltpu.CompilerParams(dimension_semantics=sem, vmem_limit_bytes=vmem)


def _rms(h, g):
    return h * lax.rsqrt(jnp.mean(h * h, axis=-1, keepdims=True) + RMS_EPS) * g


def _sigmoid(x):
    return 1.0 / (1.0 + jnp.exp(-x))


def _embed_norm_kernel(x_ref, meta_ref, g_ref, h_ref, hn_ref):
    i = pl.program_id(1)

    @pl.when(i == 0)
    def _():
        h_ref[...] = jnp.zeros_like(h_ref)
        h_ref[PAD:, :] = meta_ref[...]

    @pl.when(i > 0)
    def _():
        h_ref[...] = x_ref[...]

    hn_ref[...] = _rms(h_ref[...], g_ref[...]).astype(hn_ref.dtype)


def embed_norm(x, meta, gain, hn_dtype):
    bsz, seq, d = x.shape
    nc = (seq + CHUNK) // CHUNK
    tp = bsz * nc * CHUNK
    return pl.pallas_call(
        _embed_norm_kernel,
        out_shape=(jax.ShapeDtypeStruct((tp, d), F32), jax.ShapeDtypeStruct((tp, d), hn_dtype)),
        grid=(bsz, nc),
        in_specs=[
            pl.BlockSpec((None, CHUNK, d), lambda b, i: (b, jnp.maximum(i - 1, 0), 0)),
            pl.BlockSpec((N_META, d), lambda b, i: (0, 0)),
            pl.BlockSpec((1, d), lambda b, i: (0, 0)),
        ],
        out_specs=(
            pl.BlockSpec((CHUNK, d), lambda b, i: (b * nc + i, 0)),
            pl.BlockSpec((CHUNK, d), lambda b, i: (b * nc + i, 0)),
        ),
        compiler_params=_params(("parallel", "parallel")),
        name="embed_norm",
    )(x, meta, gain.reshape(1, d))


def _norm_kernel(h_ref, g_ref, o_ref):
    o_ref[...] = _rms(h_ref[...], g_ref[...]).astype(o_ref.dtype)


def rms_norm(h, gain, out_dtype, tm=512):
    tp, d = h.shape
    return pl.pallas_call(
        _norm_kernel,
        out_shape=jax.ShapeDtypeStruct((tp, d), out_dtype),
        grid=(tp // tm,),
        in_specs=[pl.BlockSpec((tm, d), lambda i: (i, 0)), pl.BlockSpec((1, d), lambda i: (0, 0))],
        out_specs=pl.BlockSpec((tm, d), lambda i: (i, 0)),
        compiler_params=_params(("parallel",)),
        name="rms_norm",
    )(h, gain.reshape(1, d))


def _mm_kernel(*refs, has_res, scale_tiles, scale):
    if has_res:
        x_ref, w_ref, r_ref, o_ref, wb_ref = refs
    else:
        x_ref, w_ref, o_ref, wb_ref = refs

    @pl.when(pl.program_id(1) == 0)
    def _():
        wb_ref[...] = w_ref[...].astype(BF16)

    acc = jnp.dot(x_ref[...], wb_ref[...], preferred_element_type=F32)
    if scale_tiles:
        acc = acc * jnp.where(pl.program_id(0) < scale_tiles, scale, 1.0)
    if has_res:
        acc = acc + r_ref[...]
    o_ref[...] = acc.astype(o_ref.dtype)


def matmul(x, w, out_dtype, n=None, residual=None, tm=512, tn=1024, scale_cols=0, scale=1.0):
    m, kdim = x.shape
    n = w.shape[1] if n is None else n
    in_specs = [
        pl.BlockSpec((tm, kdim), lambda j, i: (i, 0)),
        pl.BlockSpec((kdim, tn), lambda j, i: (0, j)),
    ]
    args = [x, w]
    if residual is not None:
        in_specs.append(pl.BlockSpec((tm, tn), lambda j, i: (i, j)))
        args.append(residual)
    return pl.pallas_call(
        functools.partial(_mm_kernel, has_res=residual is not None, scale_tiles=scale_cols // tn, scale=scale),
        out_shape=jax.ShapeDtypeStruct((m, n), out_dtype),
        grid=(n // tn, m // tm),
        in_specs=in_specs,
        out_specs=pl.BlockSpec((tm, tn), lambda j, i: (i, j)),
        scratch_shapes=[pltpu.VMEM((kdim, tn), BF16)],
        compiler_params=_params(("parallel", "arbitrary")),
        name="matmul",
    )(*args)


def _retention_kernel(q_ref, k_ref, v_ref, g_ref, cos_ref, sin_ref, dmat_ref, xi_ref, zeta_ref,
                      gch_ref, gn_ref, o_ref, state_ref, *, nh, dk, dv):
    c = pl.program_id(1)

    @pl.when(c == 0)
    def _():
        state_ref[...] = jnp.zeros_like(state_ref)

    cos = cos_ref[...]
    sin = sin_ref[...]
    half = dk // 2

    def rot(u):
        u1, u2 = u[:, :half], u[:, half:]
        return jnp.concatenate([u1 * cos - u2 * sin, u1 * sin + u2 * cos], axis=1)

    row = lax.broadcasted_iota(jnp.int32, (CHUNK, 1), 0)
    valid = jnp.logical_or(row >= PAD, c > 0)
    for h in range(nh):
        q = rot(q_ref[:, h * dk:(h + 1) * dk].astype(F32))
        k = rot(k_ref[:, h * dk:(h + 1) * dk].astype(F32)) * (dk ** -0.5)
        k = jnp.where(valid, k, 0.0)
        v = jnp.where(valid, v_ref[:, h * dv:(h + 1) * dv].astype(F32), 0.0)
        qb = q.astype(BF16)
        kb = k.astype(BF16)
        vb = v.astype(BF16)

        scores = lax.dot_general(qb, kb, (((1,), (1,)), ((), ())), preferred_element_type=F32)
        scores = scores * dmat_ref[h]
        inner = jnp.dot(scores.astype(BF16), vb, preferred_element_type=F32)
        state = state_ref[h]
        cross = jnp.dot(qb, state.astype(BF16), preferred_element_type=F32) * xi_ref[h]
        y = inner + cross
        vz = (v * zeta_ref[h]).astype(BF16)
        state_ref[h] = gch_ref[h] * state + lax.dot_general(
            kb, vz, (((0,), (0,)), ((), ())), preferred_element_type=F32)

        mu = jnp.mean(y, axis=-1, keepdims=True)
        dlt = y - mu
        var = jnp.mean(dlt * dlt, axis=-1, keepdims=True)
        yn = dlt * lax.rsqrt(var + GN_EPS) * gn_ref[h]
        g = g_ref[:, h * dv:(h + 1) * dv].astype(F32)
        o_ref[:, h * dv:(h + 1) * dv] = (g * _sigmoid(g) * yn).astype(o_ref.dtype)


def retention_core(proj, gn_gain, bsz, nc):
    tp = proj.shape[0]
    d = proj.shape[1] // 6
    nh = RET_HEADS
    dk = d // nh
    dv = 2 * d // nh
    lp = nc * CHUNK
    half = dk // 2
    pos = (jnp.arange(lp) - PAD).astype(F32)
    inv = ROPE_BASE ** (-jnp.arange(half, dtype=F32) / half)
    ang = pos[:, None] * inv[None, :]
    cos, sin = jnp.cos(ang), jnp.sin(ang)
    log_g = jnp.log1p(-jnp.exp2(-5.0 - jnp.arange(nh, dtype=F32)))
    idx = jnp.arange(CHUNK, dtype=F32)
    diff = idx[:, None] - idx[None, :]
    dmat = jnp.where(diff[None] >= 0, jnp.exp(jnp.maximum(diff, 0.0)[None] * log_g[:, None, None]), 0.0)
    xi = jnp.exp((idx + 1.0)[None, :] * log_g[:, None])[:, :, None]
    zeta = jnp.exp((CHUNK - 1.0 - idx)[None, :] * log_g[:, None])[:, :, None]
    g_chunk = jnp.exp(CHUNK * log_g)[:, None, None]
    return pl.pallas_call(
        functools.partial(_retention_kernel, nh=nh, dk=dk, dv=dv),
        out_shape=jax.ShapeDtypeStruct((tp, 2 * d), BF16),
        grid=(bsz, nc),
        in_specs=[
            pl.BlockSpec((CHUNK, d), lambda b, c: (b * nc + c, 0)),
            pl.BlockSpec((CHUNK, d), lambda b, c: (b * nc + c, 1)),
            pl.BlockSpec((CHUNK, 2 * d), lambda b, c: (b * nc + c, 1)),
            pl.BlockSpec((CHUNK, 2 * d), lambda b, c: (b * nc + c, 2)),
            pl.BlockSpec((CHUNK, half), lambda b, c: (c, 0)),
            pl.BlockSpec((CHUNK, half), lambda b, c: (c, 0)),
            pl.BlockSpec((nh, CHUNK, CHUNK), lambda b, c: (0, 0, 0)),
            pl.BlockSpec((nh, CHUNK, 1), lambda b, c: (0, 0, 0)),
            pl.BlockSpec((nh, CHUNK, 1), lambda b, c: (0, 0, 0)),
            pl.BlockSpec((nh, 1, 1), lambda b, c: (0, 0, 0)),
            pl.BlockSpec((nh, 1, dv), lambda b, c: (0, 0, 0)),
        ],
        out_specs=pl.BlockSpec((CHUNK, 2 * d), lambda b, c: (b * nc + c, 0)),
        scratch_shapes=[pltpu.VMEM((nh, dk, dv), F32)],
        compiler_params=_params(("parallel", "arbitrary")),
        name="retention_core",
    )(proj, proj, proj, proj, cos, sin, dmat, xi, zeta, g_chunk, gn_gain.reshape(nh, 1, dv))


LOG2E = 1.4426950408889634
MASK_BIG = 1e30
ONES_LANE = LANES - 1


def _bias_selectors(nh):
    h = jnp.arange(nh)
    selq = jnp.zeros((nh, 3 * LANES, LANES), F32)
    selk = jnp.zeros((nh, 3 * LANES, LANES), F32)
    for part in range(3):
        selq = selq.at[h, part * LANES + h, part].set(1.0)
        selk = selk.at[h, part * LANES + h, 3 + part].set(-1.0)
        selq = selq.at[h, ONES_LANE, 3 + part].set(1.0)
        selk = selk.at[h, ONES_LANE, part].set(1.0)
    return selq.astype(BF16), selk.astype(BF16)


def _forget_kernel(hn_ref, wf_ref, bf_ref, selq_ref, selk_ref, cq_ref, ck_ref, carry_ref, *, nh):
    i = pl.program_id(1)

    @pl.when(i == 0)
    def _():
        carry_ref[...] = jnp.zeros_like(carry_ref)

    z = jnp.dot(hn_ref[...], wf_ref[...], preferred_element_type=F32) + bf_ref[...]
    lf = jnp.minimum(z, 0.0) - jnp.log1p(jnp.exp(-jnp.abs(z)))
    row = lax.broadcasted_iota(jnp.int32, (CHUNK, 1), 0)
    valid = jnp.logical_or(row >= PAD, i > 0)
    lf = jnp.where(valid, lf, 0.0)

    def split3(a):
        hi = a.astype(BF16)
        r1 = a - hi.astype(F32)
        mid = r1.astype(BF16)
        lo = (r1 - mid.astype(F32)).astype(BF16)
        return hi, mid, lo

    r_i = lax.broadcasted_iota(jnp.int32, (CHUNK, CHUNK), 0)
    c_i = lax.broadcasted_iota(jnp.int32, (CHUNK, CHUNK), 1)
    tri = (r_i >= c_i).astype(BF16)
    cs = sum(jnp.dot(tri, part, preferred_element_type=F32) for part in split3(lf))
    cs = cs + carry_ref[...]
    carry_ref[...] = cs[CHUNK - 1:CHUNK, :]

    hi, mid, lo = (p.astype(F32) for p in split3(cs * LOG2E))
    lane = lax.broadcasted_iota(jnp.int32, (CHUNK, LANES), 1)
    ones = lane == ONES_LANE
    live = jnp.logical_and(valid, jnp.logical_not(ones))
    pq = jnp.concatenate([jnp.where(ones, 1.0, hi), jnp.where(ones, 0.0, mid), jnp.where(ones, 0.0, lo)],
                         axis=1).astype(BF16)
    pk = jnp.concatenate([jnp.where(ones, 1.0, jnp.where(valid, hi, MASK_BIG)),
                          jnp.where(live, mid, 0.0), jnp.where(live, lo, 0.0)], axis=1).astype(BF16)
    for h in range(nh):
        cq_ref[:, h * LANES:(h + 1) * LANES] = jnp.dot(
            pq, selq_ref[h], preferred_element_type=F32).astype(BF16)
        ck_ref[:, h * LANES:(h + 1) * LANES] = jnp.dot(
            pk, selk_ref[h], preferred_element_type=F32).astype(BF16)


def forget_bias(hn, w_f, b_f, bsz, nc):
    tp, d = hn.shape
    nh = w_f.shape[1]
    wf = jnp.zeros((d, LANES), BF16).at[:, :nh].set(w_f.astype(BF16))
    bf = jnp.zeros((1, LANES), F32).at[0, :nh].set(b_f.astype(F32))
    selq, selk = _bias_selectors(nh)
    return pl.pallas_call(
        functools.partial(_forget_kernel, nh=nh),
        out_shape=(jax.ShapeDtypeStruct((tp, nh * LANES), BF16), jax.ShapeDtypeStruct((tp, nh * LANES), BF16)),
        grid=(bsz, nc),
        in_specs=[
            pl.BlockSpec((CHUNK, d), lambda b, i: (b * nc + i, 0)),
            pl.BlockSpec((d, LANES), lambda b, i: (0, 0)),
            pl.BlockSpec((1, LANES), lambda b, i: (0, 0)),
            pl.BlockSpec((nh, 3 * LANES, LANES), lambda b, i: (0, 0, 0)),
            pl.BlockSpec((nh, 3 * LANES, LANES), lambda b, i: (0, 0, 0)),
        ],
        out_specs=(
            pl.BlockSpec((CHUNK, nh * LANES), lambda b, i: (b * nc + i, 0)),
            pl.BlockSpec((CHUNK, nh * LANES), lambda b, i: (b * nc + i, 0)),
        ),
        scratch_shapes=[pltpu.VMEM((1, LANES), F32)],
        compiler_params=_params(("parallel", "arbitrary")),
        name="forget_bias",
    )(hn, wf, bf, selq, selk)


FOX_HEADS_PER_STEP = 4


def _fox_kernel(q_ref, cq_ref, k_ref, ck_ref, v_ref, o_ref, m_ref, l_ref, acc_ref, *, tq, dh):
    i = pl.program_id(2)
    hs = FOX_HEADS_PER_STEP
    reps = tq // LANES
    m_ref[...] = jnp.full_like(m_ref, 10.0 * NEG_INF)
    l_ref[...] = jnp.zeros_like(l_ref)
    acc_ref[...] = jnp.zeros_like(acc_ref)
    qa = [jnp.concatenate([q_ref[:, j * dh:(j + 1) * dh], cq_ref[:, j * LANES:(j + 1) * LANES]], axis=1)
          for j in range(hs)]

    def block(kb, causal):
        start = pl.multiple_of(kb * tq, tq)
        for j in range(hs):
            ka = jnp.concatenate([k_ref[pl.ds(start, tq), j * dh:(j + 1) * dh],
                                  ck_ref[pl.ds(start, tq), j * LANES:(j + 1) * LANES]], axis=1)
            s = lax.dot_general(qa[j], ka, (((1,), (1,)), ((), ())), preferred_element_type=F32)
            if causal:
                r_i = lax.broadcasted_iota(jnp.int32, (tq, tq), 0)
                c_i = lax.broadcasted_iota(jnp.int32, (tq, tq), 1)
                s = jnp.where(c_i <= r_i, s, NEG_INF)
            m_old = m_ref[j]
            m_new = jnp.maximum(m_old, jnp.max(s, axis=1, keepdims=True))
            alpha = jnp.exp2(m_old - m_new)
            p = jnp.exp2(s - jnp.concatenate([m_new] * reps, axis=1))
            l_ref[j] = alpha * l_ref[j] + jnp.sum(p, axis=1, keepdims=True)
            acc_ref[j] = alpha * acc_ref[j] + jnp.dot(
                p.astype(BF16), v_ref[pl.ds(start, tq), j * dh:(j + 1) * dh], preferred_element_type=F32)
            m_ref[j] = m_new

    def body(kb, carry):
        block(kb, False)
        return carry

    lax.fori_loop(0, i, body, 0)
    block(i, True)
    for j in range(hs):
        o_ref[:, j * dh:(j + 1) * dh] = (acc_ref[j] / l_ref[j]).astype(o_ref.dtype)


def fox_attention(qkv, cq, ck, bsz, lp, tq=384):
    tp = qkv.shape[0]
    d = qkv.shape[1] // 3
    nh = FOX_HEADS
    dh = d // nh
    assert dh == LANES
    nq = lp // tq
    hs = FOX_HEADS_PER_STEP
    ng = nh // hs
    qkv3 = qkv.reshape(bsz, lp, 3 * d)
    ck3 = ck.reshape(bsz, lp, nh * LANES)
    return pl.pallas_call(
        functools.partial(_fox_kernel, tq=tq, dh=dh),
        out_shape=jax.ShapeDtypeStruct((tp, d), BF16),
        grid=(bsz, ng, nq),
        in_specs=[
            pl.BlockSpec((tq, hs * dh), lambda b, g, i: (b * nq + i, g)),
            pl.BlockSpec((tq, hs * LANES), lambda b, g, i: (b * nq + i, g)),
            pl.BlockSpec((None, lp, hs * dh), lambda b, g, i: (b, 0, ng + g)),
            pl.BlockSpec((None, lp, hs * LANES), lambda b, g, i: (b, 0, g)),
            pl.BlockSpec((None, lp, hs * dh), lambda b, g, i: (b, 0, 2 * ng + g)),
        ],
        out_specs=pl.BlockSpec((tq, hs * dh), lambda b, g, i: (b * nq + i, g)),
        scratch_shapes=[pltpu.VMEM((hs, tq, LANES), F32), pltpu.VMEM((hs, tq, LANES), F32),
                        pltpu.VMEM((hs, tq, dh), F32)],
        compiler_params=_params(("parallel", "parallel", "arbitrary")),
        name="fox_attention",
    )(qkv, cq, qkv3, ck3, qkv3)


def _router_kernel(hn_ref, wr_ref, br_ref, idx_ref, w_ref, cnt_ref, carry_ref, *, tr):
    i = pl.program_id(0)

    @pl.when(i == 0)
    def _():
        carry_ref[...] = jnp.zeros_like(carry_ref)

    lg = lax.dot_general(wr_ref[...], hn_ref[...].astype(BF16), (((1,), (1,)), ((), ())),
                         preferred_element_type=F32) + br_ref[...]
    row = lax.broadcasted_iota(jnp.int32, (ROUTER_ROWS, tr), 0)
    big = jnp.int32(1 << 20)
    is_g = row < N_GROUPS
    mg = jnp.max(jnp.where(is_g, lg, -jnp.inf), axis=0, keepdims=True)
    g_sel = jnp.min(jnp.where(jnp.logical_and(is_g, lg == mg), row, big), axis=0, keepdims=True)
    sg = jnp.sum(jnp.where(is_g, jnp.exp(lg - mg), 0.0), axis=0, keepdims=True)
    p_g = 1.0 / sg
    lo = EXPERT_ROW0 + EXPERTS_PER_GROUP * g_sel
    is_e = jnp.logical_and(row >= lo, row < lo + EXPERTS_PER_GROUP)
    me = jnp.max(jnp.where(is_e, lg, -jnp.inf), axis=0, keepdims=True)
    ee = jnp.where(is_e, jnp.exp(lg - me), 0.0)
    pe = ee / jnp.sum(ee, axis=0, keepdims=True)
    pe1 = jnp.where(is_e, pe, -1.0)
    m1 = jnp.max(pe1, axis=0, keepdims=True)
    i1 = jnp.min(jnp.where(pe1 == m1, row, big), axis=0, keepdims=True)
    pe2 = jnp.where(row == i1, -1.0, pe1)
    m2 = jnp.max(pe2, axis=0, keepdims=True)
    i2 = jnp.min(jnp.where(pe2 == m2, row, big), axis=0, keepdims=True)
    den = m1 + m2
    w1 = p_g * (m1 / den)
    w2 = p_g * (m2 / den)

    oh1 = row == i1
    oh2 = row == i2
    ohs = jnp.logical_or(oh1, oh2).astype(F32)
    r_i = lax.broadcasted_iota(jnp.int32, (tr, tr), 0)
    c_i = lax.broadcasted_iota(jnp.int32, (tr, tr), 1)
    tri = (r_i < c_i).astype(BF16)
    cnt = jnp.dot(ohs.astype(BF16), tri, preferred_element_type=F32) + carry_ref[...]
    rank1 = jnp.sum(jnp.where(oh1, cnt, 0.0), axis=0, keepdims=True)
    rank2 = jnp.sum(jnp.where(oh2, cnt, 0.0), axis=0, keepdims=True)
    carry_ref[...] += jnp.sum(ohs, axis=1, keepdims=True)

    r8 = lax.broadcasted_iota(jnp.int32, (8, tr), 0)
    e1 = i1 - EXPERT_ROW0
    e2 = i2 - EXPERT_ROW0
    idx_ref[...] = jnp.where(r8 == 0, e1, jnp.where(r8 == 1, e2, jnp.where(
        r8 == 2, rank1.astype(jnp.int32), jnp.where(r8 == 3, rank2.astype(jnp.int32), 0))))
    w_ref[...] = jnp.where(r8 == 0, w1, jnp.where(r8 == 1, w2, 0.0))
    cnt_ref[...] = jnp.broadcast_to(carry_ref[...], cnt_ref.shape)


def moe_router(hn, w_rg, b_rg, w_re, b_re, tr=512):
    tp, d = hn.shape
    wr = jnp.zeros((ROUTER_ROWS, d), BF16)
    wr = wr.at[:N_GROUPS].set(w_rg.T.astype(BF16))
    wr = wr.at[EXPERT_ROW0:EXPERT_ROW0 + N_EXPERTS].set(w_re.reshape(d, N_EXPERTS).T.astype(BF16))
    br = jnp.zeros((ROUTER_ROWS, 1), F32)
    br = br.at[:N_GROUPS, 0].set(b_rg.astype(F32))
    br = br.at[EXPERT_ROW0:EXPERT_ROW0 + N_EXPERTS, 0].set(b_re.reshape(N_EXPERTS).astype(F32))
    return pl.pallas_call(
        functools.partial(_router_kernel, tr=tr),
        out_shape=(
            jax.ShapeDtypeStruct((8, tp), jnp.int32),
            jax.ShapeDtypeStruct((8, tp), F32),
            jax.ShapeDtypeStruct((ROUTER_ROWS, LANES), F32),
        ),
        grid=(tp // tr,),
        in_specs=[
            pl.BlockSpec((tr, d), lambda i: (i, 0)),
            pl.BlockSpec((ROUTER_ROWS, d), lambda i: (0, 0)),
            pl.BlockSpec((ROUTER_ROWS, 1), lambda i: (0, 0)),
        ],
        out_specs=(
            pl.BlockSpec((8, tr), lambda i: (0, i)),
            pl.BlockSpec((8, tr), lambda i: (0, i)),
            pl.BlockSpec((ROUTER_ROWS, LANES), lambda i: (0, 0)),
        ),
        scratch_shapes=[pltpu.VMEM((ROUTER_ROWS, 1), F32)],
        compiler_params=_params(("arbitrary",)),
        name="moe_router",
    )(hn, wr, br)


def _slots_kernel(cnt_ref, idx_ref, pos_ref, meta_ref):
    e1 = idx_ref[0:1, :]
    e2 = idx_ref[1:2, :]
    off1 = jnp.zeros_like(e1)
    off2 = jnp.zeros_like(e2)
    visit = lax.broadcasted_iota(jnp.int32, (1, meta_ref.shape[1]), 1)
    v_tile = jnp.zeros_like(visit)
    v_expert = jnp.zeros_like(visit)
    v_lo = jnp.zeros_like(visit)
    v_hi = jnp.zeros_like(visit)
    start = jnp.int32(0)
    v_base = jnp.int32(0)
    for e in range(N_EXPERTS):
        off1 = jnp.where(e1 == e, start, off1)
        off2 = jnp.where(e2 == e, start, off2)
        n = cnt_ref[e]
        end = start + n
        first_tile = start // TILE_M
        n_visits = jnp.where(n > 0, (jnp.maximum(end, 1) - 1) // TILE_M - first_tile + 1, 0)
        mine = jnp.logical_and(visit >= v_base, visit < v_base + n_visits)
        row0 = (first_tile + visit - v_base) * TILE_M
        v_tile = jnp.where(mine, first_tile + visit - v_base, v_tile)
        v_expert = jnp.where(mine, e, v_expert)
        v_lo = jnp.where(mine, jnp.maximum(start - row0, 0), v_lo)
        v_hi = jnp.where(mine, jnp.minimum(end - row0, TILE_M), v_hi)
        start = end
        v_base = v_base + n_visits
    r8 = lax.broadcasted_iota(jnp.int32, pos_ref.shape, 0)
    pos_ref[...] = jnp.where(r8 == 0, off1 + idx_ref[2:3, :], jnp.where(r8 == 1, off2 + idx_ref[3:4, :], 0))
    m8 = lax.broadcasted_iota(jnp.int32, meta_ref.shape, 0)
    meta_ref[...] = jnp.where(m8 == 0, v_tile, jnp.where(m8 == 1, v_expert, jnp.where(
        m8 == 2, v_lo, jnp.where(m8 == 3, v_hi, v_base))))


def moe_slots(counts, idx, tr=512):
    tp = idx.shape[1]
    nt_lanes = 2 * LANES
    return pl.pallas_call(
        _slots_kernel,
        out_shape=(jax.ShapeDtypeStruct((8, tp), jnp.int32), jax.ShapeDtypeStruct((8, nt_lanes), jnp.int32)),
        grid_spec=pltpu.PrefetchScalarGridSpec(
            num_scalar_prefetch=1,
            grid=(tp // tr,),
            in_specs=[pl.BlockSpec((8, tr), lambda i, c: (0, i))],
            out_specs=(
                pl.BlockSpec((8, tr), lambda i, c: (0, i)),
                pl.BlockSpec((8, nt_lanes), lambda i, c: (0, 0)),
            ),
        ),
        compiler_params=_params(("arbitrary",)),
        name="moe_slots",
    )(counts, idx)


def _invert_kernel(pos_ref, code_ref, *, tp):
    def per_token(t, carry):
        code_ref[pos_ref[t]] = 2 * t
        code_ref[pos_ref[tp + t]] = 2 * t + 1
        return carry

    lax.fori_loop(0, tp, per_token, 0, unroll=8)


def moe_invert(pos_flat):
    tp = pos_flat.shape[0] // 2
    return pl.pallas_call(
        functools.partial(_invert_kernel, tp=tp),
        out_shape=jax.ShapeDtypeStruct((2 * tp,), jnp.int32),
        grid_spec=pltpu.PrefetchScalarGridSpec(
            num_scalar_prefetch=1,
            grid=(1,),
            in_specs=[],
            out_specs=pl.BlockSpec(memory_space=pltpu.SMEM),
        ),
        compiler_params=_params(("arbitrary",)),
        name="moe_invert",
    )(pos_flat)


def _expert_kernel(vt_ref, ve_ref, vlo_ref, vhi_ref, nv_ref, code_ref, hn_ref, wg_ref, wu_ref, wd_ref, y_ref,
                   xbuf, xcur, wgb, wub, wdb, gsem, *, n_tiles):
    v = pl.program_id(0)
    nv = nv_ref[0]
    t = vt_ref[v]
    prev_v = jnp.maximum(v - 1, 0)
    first = jnp.logical_or(v == 0, vt_ref[prev_v] != t)
    new_expert = jnp.logical_or(v == 0, ve_ref[prev_v] != ve_ref[v])

    def row_copy(tile, r):
        tok = lax.shift_right_logical(code_ref[tile * TILE_M + r], 1)
        return pltpu.make_async_copy(hn_ref.at[pl.ds(tok, 1)], xbuf.at[pl.ds(r, 1)], gsem)

    def gather_wait():
        pltpu.make_async_copy(hn_ref.at[pl.ds(0, TILE_M)], xbuf, gsem).wait()

    def compute(accumulate):
        x = xcur[...]
        a = jnp.dot(x, wgb[...], preferred_element_type=F32)
        u = jnp.dot(x, wub[...], preferred_element_type=F32)
        row = lax.broadcasted_iota(jnp.int32, (TILE_M, 1), 0)
        mine = jnp.logical_and(row >= vlo_ref[v], row < vhi_ref[v])
        hid = jnp.where(mine, a * _sigmoid(a) * u, 0.0).astype(BF16)
        y = jnp.dot(hid, wdb[...], preferred_element_type=F32)
        if accumulate:
            y_ref[...] += y
        else:
            y_ref[...] = y

    @pl.when(v == 0)
    def _():
        def step(r, carry):
            row_copy(0, r).start()
            return carry
        lax.fori_loop(0, TILE_M, step, 0, unroll=8)

    @pl.when(v < nv)
    def _():
        @pl.when(new_expert)
        def _():
            wgb[...] = wg_ref[...].astype(BF16)
            wub[...] = wu_ref[...].astype(BF16)
            wdb[...] = wd_ref[...].astype(BF16)

        @pl.when(first)
        def _():
            gather_wait()
            xcur[...] = xbuf[...].astype(BF16)
            ahead = jnp.minimum(t + 1, n_tiles - 1)
            for r in range(TILE_M):
                row_copy(ahead, r).start()
            compute(False)

        @pl.when(jnp.logical_not(first))
        def _():
            compute(True)

        @pl.when(v == nv - 1)
        def _():
            gather_wait()


def moe_experts(meta, code, hn, w_gate, w_up, w_down, layer):
    tp, d = hn.shape
    f = w_gate.shape[-1]
    assert (2 * tp) % TILE_M == 0
    n_tiles = (2 * tp) // TILE_M
    max_visits = n_tiles + N_EXPERTS - 1

    def w_map(v, vt, ve, vlo, vhi, nv, code):
        return (layer * N_EXPERTS + ve[jnp.minimum(v, nv[0] - 1)], 0, 0)

    def y_map(v, vt, ve, vlo, vhi, nv, code):
        return (vt[jnp.minimum(v, nv[0] - 1)], 0)

    return pl.pallas_call(
        functools.partial(_expert_kernel, n_tiles=n_tiles),
        out_shape=jax.ShapeDtypeStruct((2 * tp, d), F32),
        grid_spec=pltpu.PrefetchScalarGridSpec(
            num_scalar_prefetch=6,
            grid=(max_visits,),
            in_specs=[
                pl.BlockSpec(memory_space=pl.ANY),
                pl.BlockSpec((None, d, f), w_map),
                pl.BlockSpec((None, d, f), w_map),
                pl.BlockSpec((None, f, d), w_map),
            ],
            out_specs=pl.BlockSpec((TILE_M, d), y_map),
            scratch_shapes=[
                pltpu.VMEM((TILE_M, d), F32), pltpu.VMEM((TILE_M, d), BF16),
                pltpu.VMEM((d, f), BF16), pltpu.VMEM((d, f), BF16), pltpu.VMEM((f, d), BF16),
                pltpu.SemaphoreType.DMA,
            ],
        ),
        compiler_params=_params(("arbitrary",), vmem=56 * 1024 * 1024),
        name="moe_experts",
    )(meta[0, :max_visits], meta[1, :max_visits], meta[2, :max_visits], meta[3, :max_visits], meta[4, :1],
      code, hn, w_gate, w_up, w_down)


def _combine_kernel(pos_ref, h_ref, w_ref, g_ref, y_ref, *rest, tc, tp, n_steps, write_h):
    out_refs, (ybuf, sem) = rest[:-2], rest[-2:]
    i = pl.program_id(0)
    slot = lax.rem(i, 2)

    def row_copy(tile, buf, s, r):
        p = pos_ref[s * tp + tile * tc + r]
        return pltpu.make_async_copy(y_ref.at[pl.ds(p, 1)], ybuf.at[buf, pl.ds(s * tc + r, 1)], sem.at[buf])

    @pl.when(i == 0)
    def _():
        def step(r, carry):
            row_copy(0, 0, 0, r).start()
            row_copy(0, 0, 1, r).start()
            return carry
        lax.fori_loop(0, tc, step, 0, unroll=8)

    pltpu.make_async_copy(y_ref.at[pl.ds(0, 2 * tc)], ybuf.at[slot], sem.at[slot]).wait()

    @pl.when(i + 1 < n_steps)
    def _():
        for r in range(tc):
            row_copy(i + 1, 1 - slot, 0, r).start()
            row_copy(i + 1, 1 - slot, 1, r).start()

    w = w_ref[...]
    h = h_ref[...] + w[:, 0:1] * ybuf[slot, :tc, :] + w[:, 1:2] * ybuf[slot, tc:, :]
    if write_h:
        out_refs[0][...] = h
    out_refs[-1][...] = _rms(h, g_ref[...]).astype(out_refs[-1].dtype)


def moe_combine(pos_flat, h, w_col, y, gain, hn_dtype, final_shape=None):
    tp, d = h.shape
    if final_shape is None:
        tc = 256
        out_shape = (jax.ShapeDtypeStruct((tp, d), F32), jax.ShapeDtypeStruct((tp, d), hn_dtype))
        out_specs = (pl.BlockSpec((tc, d), lambda i, p: (i, 0)), pl.BlockSpec((tc, d), lambda i, p: (i, 0)))
    else:
        tc = CHUNK
        bsz, seq, _ = final_shape
        nc = tp // bsz // tc
        out_shape = (jax.ShapeDtypeStruct(final_shape, hn_dtype),)
        out_specs = (pl.BlockSpec((None, tc, d), lambda i, p: (i // nc, jnp.maximum(i % nc - 1, 0), 0)),)
    n_steps = tp // tc
    return pl.pallas_call(
        functools.partial(_combine_kernel, tc=tc, tp=tp, n_steps=n_steps, write_h=final_shape is None),
        out_shape=out_shape,
        grid_spec=pltpu.PrefetchScalarGridSpec(
            num_scalar_prefetch=1,
            grid=(n_steps,),
            in_specs=[
                pl.BlockSpec((tc, d), lambda i, p: (i, 0)),
                pl.BlockSpec((tc, 2), lambda i, p: (i, 0)),
                pl.BlockSpec((1, d), lambda i, p: (0, 0)),
                pl.BlockSpec(memory_space=pl.ANY),
            ],
            out_specs=out_specs,
            scratch_shapes=[pltpu.VMEM((2, 2 * tc, d), F32), pltpu.SemaphoreType.DMA((2,))],
        ),
        compiler_params=_params(("arbitrary",)),
        name="moe_combine",
    )(pos_flat, h, w_col, gain.reshape(1, d), y)


def hierarchical_moe(h, hn, w_rg, b_rg, w_re, b_re, w_gate, w_up, w_down, layer, next_gain, hn_dtype,
                     final_shape=None):
    tp, d = h.shape
    f = w_gate.shape[-1]
    idx, w_rows, cnt = moe_router(hn, w_rg, b_rg, w_re, b_re)
    counts = cnt[EXPERT_ROW0:EXPERT_ROW0 + N_EXPERTS, 0].astype(jnp.int32)
    pos, meta = moe_slots(counts, idx)
    pos_flat = pos[:2].reshape(2 * tp)
    code = moe_invert(pos_flat)
    y = moe_experts(meta, code, hn, w_gate.reshape(-1, d, f), w_up.reshape(-1, d, f),
                    w_down.reshape(-1, f, d), layer)
    return moe_combine(pos_flat, h, w_rows[:2].T, y, next_gain, hn_dtype, final_shape)


def kernel(x, meta_tokens, norm_mixer, norm_ffn, norm_final, ret_w_in, ret_gn, ret_w_out,
           fox_w_in, fox_b_f, fox_w_out, moe_w_rg, moe_b_rg, moe_w_re, moe_b_re,
           moe_w_gate, moe_w_up, moe_w_down):
    bsz, seq, d = x.shape
    depth = norm_mixer.shape[0]
    nc = (seq + CHUNK) // CHUNK
    lp = nc * CHUNK
    h, hn = embed_norm(x, meta_tokens.astype(x.dtype), norm_mixer[0], BF16)
    for i in range(depth):
        j = i // 2
        if i % 2 == 0:
            proj = matmul(hn, ret_w_in[j], BF16)
            gated = retention_core(proj, ret_gn[j], bsz, nc)
            h = matmul(gated, ret_w_out[j], F32, residual=h, tn=512)
        else:
            qkv = matmul(hn, fox_w_in[j], BF16, n=3 * d, scale_cols=d, scale=(d // FOX_HEADS) ** -0.5 * LOG2E)
            cq, ck = forget_bias(hn, fox_w_in[j][:, 3 * d:], fox_b_f[j], bsz, nc)
            o = fox_attention(qkv, cq, ck, bsz, lp)
            h = matmul(o, fox_w_out[j], F32, residual=h)
        hn_f = rms_norm(h, norm_ffn[i], F32)
        last = i == depth - 1
        outs = hierarchical_moe(h, hn_f, moe_w_rg[i], moe_b_rg[i], moe_w_re[i], moe_b_re[i],
                                moe_w_gate, moe_w_up, moe_w_down, i,
                                norm_final if last else norm_mixer[i + 1],
                                F32 if last else BF16, (bsz, seq, d) if last else None)
        if last:
            return outs[0]
        h, hn = outs
```

```python
import functools

import jax
import jax.numpy as jnp
from jax import lax
from jax.experimental import pallas as pl
from jax.experimental.pallas import tpu as pltpu

N_META = 16
CHUNK = 128
PAD = CHUNK - N_META
RMS_EPS = 1e-6
GN_EPS = 1e-6
NEG_INF = -1e30
RET_HEADS = 8
FOX_HEADS = 16
N_GROUPS = 4
EXPERTS_PER_GROUP = 8
N_EXPERTS = N_GROUPS * EXPERTS_PER_GROUP
ROPE_BASE = 10000.0

LANES = 128
ROUTER_ROWS = 48
EXPERT_ROW0 = N_GROUPS
TILE_M = 256
VMEM_LIMIT = 48 * 1024 * 1024

F32 = jnp.float32
BF16 = jnp.bfloat16


def _params(sem, vmem=VMEM_LIMIT):
    return pltpu.CompilerParams(dimension_semantics=sem, vmem_limit_bytes=vmem)


def _rms(h, g):
    return h * lax.rsqrt(jnp.mean(h * h, axis=-1, keepdims=True) + RMS_EPS) * g


def _sigmoid(x):
    return 1.0 / (1.0 + jnp.exp(-x))


def _embed_norm_kernel(x_ref, meta_ref, g_ref, h_ref, hn_ref):
    i = pl.program_id(1)

    @pl.when(i == 0)
    def _():
        h_ref[...] = jnp.zeros_like(h_ref)
        h_ref[PAD:, :] = meta_ref[...]

    @pl.when(i > 0)
    def _():
        h_ref[...] = x_ref[...]

    hn_ref[...] = _rms(h_ref[...], g_ref[...]).astype(hn_ref.dtype)


def embed_norm(x, meta, gain, hn_dtype):
    bsz, seq, d = x.shape
    nc = (seq + CHUNK) // CHUNK
    tp = bsz * nc * CHUNK
    return pl.pallas_call(
        _embed_norm_kernel,
        out_shape=(jax.ShapeDtypeStruct((tp, d), F32), jax.ShapeDtypeStruct((tp, d), hn_dtype)),
        grid=(bsz, nc),
        in_specs=[
            pl.BlockSpec((None, CHUNK, d), lambda b, i: (b, jnp.maximum(i - 1, 0), 0)),
            pl.BlockSpec((N_META, d), lambda b, i: (0, 0)),
            pl.BlockSpec((1, d), lambda b, i: (0, 0)),
        ],
        out_specs=(
            pl.BlockSpec((CHUNK, d), lambda b, i: (b * nc + i, 0)),
            pl.BlockSpec((CHUNK, d), lambda b, i: (b * nc + i, 0)),
        ),
        compiler_params=_params(("parallel", "parallel")),
        name="embed_norm",
    )(x, meta, gain.reshape(1, d))


def _mm_kernel(*refs, has_res, scale_tiles, scale):
    if has_res:
        x_ref, w_ref, r_ref, o_ref, wb_ref = refs
    else:
        x_ref, w_ref, o_ref, wb_ref = refs

    @pl.when(pl.program_id(1) == 0)
    def _():
        wb_ref[...] = w_ref[...].astype(BF16)

    acc = jnp.dot(x_ref[...], wb_ref[...], preferred_element_type=F32)
    if scale_tiles:
        acc = acc * jnp.where(pl.program_id(0) < scale_tiles, scale, 1.0)
    if has_res:
        acc = acc + r_ref[...]
    o_ref[...] = acc.astype(o_ref.dtype)


def matmul(x, w, out_dtype, n=None, residual=None, tm=512, tn=1024, scale_cols=0, scale=1.0):
    m, kdim = x.shape
    n = w.shape[1] if n is None else n
    in_specs = [
        pl.BlockSpec((tm, kdim), lambda j, i: (i, 0)),
        pl.BlockSpec((kdim, tn), lambda j, i: (0, j)),
    ]
    args = [x, w]
    if residual is not None:
        in_specs.append(pl.BlockSpec((tm, tn), lambda j, i: (i, j)))
        args.append(residual)
    return pl.pallas_call(
        functools.partial(_mm_kernel, has_res=residual is not None, scale_tiles=scale_cols // tn, scale=scale),
        out_shape=jax.ShapeDtypeStruct((m, n), out_dtype),
        grid=(n // tn, m // tm),
        in_specs=in_specs,
        out_specs=pl.BlockSpec((tm, tn), lambda j, i: (i, j)),
        scratch_shapes=[pltpu.VMEM((kdim, tn), BF16)],
        compiler_params=_params(("parallel", "arbitrary")),
        name="matmul",
    )(*args)


def _retention_kernel(q_ref, k_ref, v_ref, g_ref, cos_ref, sin_ref, dmat_ref, xi_ref, zeta_ref,
                      gch_ref, gn_ref, o_ref, state_ref, *, nh, dk, dv):
    c = pl.program_id(1)

    @pl.when(c == 0)
    def _():
        state_ref[...] = jnp.zeros_like(state_ref)

    cos = cos_ref[...]
    sin = sin_ref[...]
    half = dk // 2

    def rot(u):
        u1, u2 = u[:, :half], u[:, half:]
        return jnp.concatenate([u1 * cos - u2 * sin, u1 * sin + u2 * cos], axis=1)

    row = lax.broadcasted_iota(jnp.int32, (CHUNK, 1), 0)
    valid = jnp.logical_or(row >= PAD, c > 0)
    for h in range(nh):
        q = rot(q_ref[:, h * dk:(h + 1) * dk].astype(F32))
        k = rot(k_ref[:, h * dk:(h + 1) * dk].astype(F32)) * (dk ** -0.5)
        k = jnp.where(valid, k, 0.0)
        v = jnp.where(valid, v_ref[:, h * dv:(h + 1) * dv].astype(F32), 0.0)
        qb = q.astype(BF16)
        kb = k.astype(BF16)
        vb = v.astype(BF16)

        scores = lax.dot_general(qb, kb, (((1,), (1,)), ((), ())), preferred_element_type=F32)
        scores = scores * dmat_ref[h]
        inner = jnp.dot(scores.astype(BF16), vb, preferred_element_type=F32)
        state = state_ref[h]
        cross = jnp.dot(qb, state.astype(BF16), preferred_element_type=F32) * xi_ref[h]
        y = inner + cross
        vz = (v * zeta_ref[h]).astype(BF16)
        state_ref[h] = gch_ref[h] * state + lax.dot_general(
            kb, vz, (((0,), (0,)), ((), ())), preferred_element_type=F32)

        mu = jnp.mean(y, axis=-1, keepdims=True)
        dlt = y - mu
        var = jnp.mean(dlt * dlt, axis=-1, keepdims=True)
        yn = dlt * lax.rsqrt(var + GN_EPS) * gn_ref[h]
        g = g_ref[:, h * dv:(h + 1) * dv].astype(F32)
        o_ref[:, h * dv:(h + 1) * dv] = (g * _sigmoid(g) * yn).astype(o_ref.dtype)


def retention_core(proj, gn_gain, bsz, nc):
    tp = proj.shape[0]
    d = proj.shape[1] // 6
    nh = RET_HEADS
    dk = d // nh
    dv = 2 * d // nh
    lp = nc * CHUNK
    half = dk // 2
    pos = (jnp.arange(lp) - PAD).astype(F32)
    inv = ROPE_BASE ** (-jnp.arange(half, dtype=F32) / half)
    ang = pos[:, None] * inv[None, :]
    cos, sin = jnp.cos(ang), jnp.sin(ang)
    log_g = jnp.log1p(-jnp.exp2(-5.0 - jnp.arange(nh, dtype=F32)))
    idx = jnp.arange(CHUNK, dtype=F32)
    diff = idx[:, None] - idx[None, :]
    dmat = jnp.where(diff[None] >= 0, jnp.exp(jnp.maximum(diff, 0.0)[None] * log_g[:, None, None]), 0.0)
    xi = jnp.exp((idx + 1.0)[None, :] * log_g[:, None])[:, :, None]
    zeta = jnp.exp((CHUNK - 1.0 - idx)[None, :] * log_g[:, None])[:, :, None]
    g_chunk = jnp.exp(CHUNK * log_g)[:, None, None]
    return pl.pallas_call(
        functools.partial(_retention_kernel, nh=nh, dk=dk, dv=dv),
        out_shape=jax.ShapeDtypeStruct((tp, 2 * d), BF16),
        grid=(bsz, nc),
        in_specs=[
            pl.BlockSpec((CHUNK, d), lambda b, c: (b * nc + c, 0)),
            pl.BlockSpec((CHUNK, d), lambda b, c: (b * nc + c, 1)),
            pl.BlockSpec((CHUNK, 2 * d), lambda b, c: (b * nc + c, 1)),
            pl.BlockSpec((CHUNK, 2 * d), lambda b, c: (b * nc + c, 2)),
            pl.BlockSpec((CHUNK, half), lambda b, c: (c, 0)),
            pl.BlockSpec((CHUNK, half), lambda b, c: (c, 0)),
            pl.BlockSpec((nh, CHUNK, CHUNK), lambda b, c: (0, 0, 0)),
            pl.BlockSpec((nh, CHUNK, 1), lambda b, c: (0, 0, 0)),
            pl.BlockSpec((nh, CHUNK, 1), lambda b, c: (0, 0, 0)),
            pl.BlockSpec((nh, 1, 1), lambda b, c: (0, 0, 0)),
            pl.BlockSpec((nh, 1, dv), lambda b, c: (0, 0, 0)),
        ],
        out_specs=pl.BlockSpec((CHUNK, 2 * d), lambda b, c: (b * nc + c, 0)),
        scratch_shapes=[pltpu.VMEM((nh, dk, dv), F32)],
        compiler_params=_params(("parallel", "arbitrary")),
        name="retention_core",
    )(proj, proj, proj, proj, cos, sin, dmat, xi, zeta, g_chunk, gn_gain.reshape(nh, 1, dv))


LOG2E = 1.4426950408889634
MASK_BIG = 1e30
ONES_LANE = LANES - 1


def _bias_selectors(nh):
    h = jnp.arange(nh)
    selq = jnp.zeros((nh, 3 * LANES, LANES), F32)
    selk = jnp.zeros((nh, 3 * LANES, LANES), F32)
    for part in range(3):
        selq = selq.at[h, part * LANES + h, part].set(1.0)
        selk = selk.at[h, part * LANES + h, 3 + part].set(-1.0)
        selq = selq.at[h, ONES_LANE, 3 + part].set(1.0)
        selk = selk.at[h, ONES_LANE, part].set(1.0)
    return selq.astype(BF16), selk.astype(BF16)


def _forget_kernel(hn_ref, wf_ref, bf_ref, selq_ref, selk_ref, cq_ref, ck_ref, carry_ref, *, nh):
    i = pl.program_id(1)

    @pl.when(i == 0)
    def _():
        carry_ref[...] = jnp.zeros_like(carry_ref)

    z = jnp.dot(hn_ref[...], wf_ref[...], preferred_element_type=F32) + bf_ref[...]
    lf = jnp.minimum(z, 0.0) - jnp.log1p(jnp.exp(-jnp.abs(z)))
    row = lax.broadcasted_iota(jnp.int32, (CHUNK, 1), 0)
    valid = jnp.logical_or(row >= PAD, i > 0)
    lf = jnp.where(valid, lf, 0.0)

    def split3(a):
        hi = a.astype(BF16)
        r1 = a - hi.astype(F32)
        mid = r1.astype(BF16)
        lo = (r1 - mid.astype(F32)).astype(BF16)
        return hi, mid, lo

    r_i = lax.broadcasted_iota(jnp.int32, (CHUNK, CHUNK), 0)
    c_i = lax.broadcasted_iota(jnp.int32, (CHUNK, CHUNK), 1)
    tri = (r_i >= c_i).astype(BF16)
    cs = sum(jnp.dot(tri, part, preferred_element_type=F32) for part in split3(lf))
    cs = cs + carry_ref[...]
    carry_ref[...] = cs[CHUNK - 1:CHUNK, :]

    hi, mid, lo = (p.astype(F32) for p in split3(cs * LOG2E))
    lane = lax.broadcasted_iota(jnp.int32, (CHUNK, LANES), 1)
    ones = lane == ONES_LANE
    live = jnp.logical_and(valid, jnp.logical_not(ones))
    pq = jnp.concatenate([jnp.where(ones, 1.0, hi), jnp.where(ones, 0.0, mid), jnp.where(ones, 0.0, lo)],
                         axis=1).astype(BF16)
    pk = jnp.concatenate([jnp.where(ones, 1.0, jnp.where(valid, hi, MASK_BIG)),
                          jnp.where(live, mid, 0.0), jnp.where(live, lo, 0.0)], axis=1).astype(BF16)
    for h in range(nh):
        cq_ref[:, h * LANES:(h + 1) * LANES] = jnp.dot(
            pq, selq_ref[h], preferred_element_type=F32).astype(BF16)
        ck_ref[:, h * LANES:(h + 1) * LANES] = jnp.dot(
            pk, selk_ref[h], preferred_element_type=F32).astype(BF16)


def forget_bias(hn, w_f, b_f, bsz, nc):
    tp, d = hn.shape
    nh = w_f.shape[1]
    wf = jnp.zeros((d, LANES), BF16).at[:, :nh].set(w_f.astype(BF16))
    bf = jnp.zeros((1, LANES), F32).at[0, :nh].set(b_f.astype(F32))
    selq, selk = _bias_selectors(nh)
    return pl.pallas_call(
        functools.partial(_forget_kernel, nh=nh),
        out_shape=(jax.ShapeDtypeStruct((tp, nh * LANES), BF16), jax.ShapeDtypeStruct((tp, nh * LANES), BF16)),
        grid=(bsz, nc),
        in_specs=[
            pl.BlockSpec((CHUNK, d), lambda b, i: (b * nc + i, 0)),
            pl.BlockSpec((d, LANES), lambda b, i: (0, 0)),
            pl.BlockSpec((1, LANES), lambda b, i: (0, 0)),
            pl.BlockSpec((nh, 3 * LANES, LANES), lambda b, i: (0, 0, 0)),
            pl.BlockSpec((nh, 3 * LANES, LANES), lambda b, i: (0, 0, 0)),
        ],
        out_specs=(
            pl.BlockSpec((CHUNK, nh * LANES), lambda b, i: (b * nc + i, 0)),
            pl.BlockSpec((CHUNK, nh * LANES), lambda b, i: (b * nc + i, 0)),
        ),
        scratch_shapes=[pltpu.VMEM((1, LANES), F32)],
        compiler_params=_params(("parallel", "arbitrary")),
        name="forget_bias",
    )(hn, wf, bf, selq, selk)


FOX_HEADS_PER_STEP = 4


FOX_STRIP = 32


def _fox_kernel(q_ref, cq_ref, k_ref, ck_ref, v_ref, o_ref, m_ref, l_ref, acc_ref, s_ref, p_ref, a_ref,
                *, tq, dh):
    i = pl.program_id(2)
    hs = FOX_HEADS_PER_STEP
    reps = tq // LANES
    m_ref[...] = jnp.full_like(m_ref, 10.0 * NEG_INF)
    l_ref[...] = jnp.zeros_like(l_ref)
    acc_ref[...] = jnp.zeros_like(acc_ref)
    qa = [jnp.concatenate([q_ref[:, j * dh:(j + 1) * dh], cq_ref[:, j * LANES:(j + 1) * LANES]], axis=1)
          for j in range(hs)]

    def scores(kb, buf):
        start = pl.multiple_of(kb * tq, tq)
        for j in range(hs):
            ka = jnp.concatenate([k_ref[pl.ds(start, tq), j * dh:(j + 1) * dh],
                                  ck_ref[pl.ds(start, tq), j * LANES:(j + 1) * LANES]], axis=1)
            s_ref[buf, j] = lax.dot_general(qa[j], ka, (((1,), (1,)), ((), ())), preferred_element_type=F32)

    def absorb(kb, buf, causal):
        start = pl.multiple_of(kb * tq, tq)
        for j in range(hs):
            for r0 in range(0, tq, FOX_STRIP):
                rows = slice(r0, r0 + FOX_STRIP)
                s = s_ref[buf, j, rows, :]
                if causal:
                    r_i = r0 + lax.broadcasted_iota(jnp.int32, (FOX_STRIP, tq), 0)
                    c_i = lax.broadcasted_iota(jnp.int32, (FOX_STRIP, tq), 1)
                    s = jnp.where(c_i <= r_i, s, NEG_INF)
                m_old = m_ref[j, rows, :]
                m_new = jnp.maximum(m_old, jnp.max(s, axis=1, keepdims=True))
                alpha = jnp.exp2(m_old - m_new)
                p = jnp.exp2(s - jnp.concatenate([m_new] * reps, axis=1))
                l_ref[j, rows, :] = alpha * l_ref[j, rows, :] + jnp.sum(p, axis=1, keepdims=True)
                m_ref[j, rows, :] = m_new
                a_ref[j, rows, :] = alpha
                p_ref[j, rows, :] = p.astype(BF16)
        for j in range(hs):
            acc_ref[j] = a_ref[j] * acc_ref[j] + jnp.dot(
                p_ref[j], v_ref[pl.ds(start, tq), j * dh:(j + 1) * dh], preferred_element_type=F32)

    def body(kb, carry):
        scores(kb, 0)
        absorb(kb, 0, False)
        return carry

    lax.fori_loop(0, i, body, 0)
    scores(i, 0)
    absorb(i, 0, True)

    for j in range(hs):
        o_ref[:, j * dh:(j + 1) * dh] = (acc_ref[j] / l_ref[j]).astype(o_ref.dtype)


def fox_attention(qkv, cq, ck, bsz, lp, tq=384):
    tp = qkv.shape[0]
    d = qkv.shape[1] // 3
    nh = FOX_HEADS
    dh = d // nh
    assert dh == LANES
    nq = lp // tq
    hs = FOX_HEADS_PER_STEP
    ng = nh // hs
    qkv3 = qkv.reshape(bsz, lp, 3 * d)
    ck3 = ck.reshape(bsz, lp, nh * LANES)
    return pl.pallas_call(
        functools.partial(_fox_kernel, tq=tq, dh=dh),
        out_shape=jax.ShapeDtypeStruct((tp, d), BF16),
        grid=(bsz, ng, nq),
        in_specs=[
            pl.BlockSpec((tq, hs * dh), lambda b, g, i: (b * nq + i, g)),
            pl.BlockSpec((tq, hs * LANES), lambda b, g, i: (b * nq + i, g)),
            pl.BlockSpec((None, lp, hs * dh), lambda b, g, i: (b, 0, ng + g)),
            pl.BlockSpec((None, lp, hs * LANES), lambda b, g, i: (b, 0, g)),
            pl.BlockSpec((None, lp, hs * dh), lambda b, g, i: (b, 0, 2 * ng + g)),
        ],
        out_specs=pl.BlockSpec((tq, hs * dh), lambda b, g, i: (b * nq + i, g)),
        scratch_shapes=[pltpu.VMEM((hs, tq, LANES), F32), pltpu.VMEM((hs, tq, LANES), F32),
                        pltpu.VMEM((hs, tq, dh), F32), pltpu.VMEM((1, hs, tq, tq), F32),
                        pltpu.VMEM((hs, tq, tq), BF16), pltpu.VMEM((hs, tq, LANES), F32)],
        compiler_params=_params(("parallel", "parallel", "arbitrary")),
        name="fox_attention",
    )(qkv, cq, qkv3, ck3, qkv3)


def _router_kernel(h_ref, g_ref, wr_ref, br_ref, hn_ref, idx_ref, w_ref, cnt_ref, carry_ref, *, tr):
    i = pl.program_id(0)

    @pl.when(i == 0)
    def _():
        carry_ref[...] = jnp.zeros_like(carry_ref)

    hn = _rms(h_ref[...], g_ref[...])
    hn_ref[...] = hn
    lg = lax.dot_general(wr_ref[...], hn.astype(BF16), (((1,), (1,)), ((), ())),
                         preferred_element_type=F32) + br_ref[...]
    row = lax.broadcasted_iota(jnp.int32, (ROUTER_ROWS, tr), 0)
    big = jnp.int32(1 << 20)
    is_g = row < N_GROUPS
    mg = jnp.max(jnp.where(is_g, lg, -jnp.inf), axis=0, keepdims=True)
    g_sel = jnp.min(jnp.where(jnp.logical_and(is_g, lg == mg), row, big), axis=0, keepdims=True)
    sg = jnp.sum(jnp.where(is_g, jnp.exp(lg - mg), 0.0), axis=0, keepdims=True)
    p_g = 1.0 / sg
    lo = EXPERT_ROW0 + EXPERTS_PER_GROUP * g_sel
    is_e = jnp.logical_and(row >= lo, row < lo + EXPERTS_PER_GROUP)
    me = jnp.max(jnp.where(is_e, lg, -jnp.inf), axis=0, keepdims=True)
    ee = jnp.where(is_e, jnp.exp(lg - me), 0.0)
    pe = ee / jnp.sum(ee, axis=0, keepdims=True)
    pe1 = jnp.where(is_e, pe, -1.0)
    m1 = jnp.max(pe1, axis=0, keepdims=True)
    i1 = jnp.min(jnp.where(pe1 == m1, row, big), axis=0, keepdims=True)
    pe2 = jnp.where(row == i1, -1.0, pe1)
    m2 = jnp.max(pe2, axis=0, keepdims=True)
    i2 = jnp.min(jnp.where(pe2 == m2, row, big), axis=0, keepdims=True)
    den = m1 + m2
    w1 = p_g * (m1 / den)
    w2 = p_g * (m2 / den)

    oh1 = row == i1
    oh2 = row == i2
    ohs = jnp.logical_or(oh1, oh2).astype(F32)
    r_i = lax.broadcasted_iota(jnp.int32, (tr, tr), 0)
    c_i = lax.broadcasted_iota(jnp.int32, (tr, tr), 1)
    tri = (r_i < c_i).astype(BF16)
    cnt = jnp.dot(ohs.astype(BF16), tri, preferred_element_type=F32) + carry_ref[...]
    rank1 = jnp.sum(jnp.where(oh1, cnt, 0.0), axis=0, keepdims=True)
    rank2 = jnp.sum(jnp.where(oh2, cnt, 0.0), axis=0, keepdims=True)
    carry_ref[...] += jnp.sum(ohs, axis=1, keepdims=True)

    r8 = lax.broadcasted_iota(jnp.int32, (8, tr), 0)
    e1 = i1 - EXPERT_ROW0
    e2 = i2 - EXPERT_ROW0
    idx_ref[...] = jnp.where(r8 == 0, e1, jnp.where(r8 == 1, e2, jnp.where(
        r8 == 2, rank1.astype(jnp.int32), jnp.where(r8 == 3, rank2.astype(jnp.int32), 0))))
    w_ref[...] = jnp.where(r8 == 0, w1, jnp.where(r8 == 1, w2, 0.0))
    cnt_ref[...] = jnp.broadcast_to(carry_ref[...], cnt_ref.shape)


def moe_router(h, gain, w_rg, b_rg, w_re, b_re, tr=512):
    tp, d = h.shape
    wr = jnp.zeros((ROUTER_ROWS, d), BF16)
    wr = wr.at[:N_GROUPS].set(w_rg.T.astype(BF16))
    wr = wr.at[EXPERT_ROW0:EXPERT_ROW0 + N_EXPERTS].set(w_re.reshape(d, N_EXPERTS).T.astype(BF16))
    br = jnp.zeros((ROUTER_ROWS, 1), F32)
    br = br.at[:N_GROUPS, 0].set(b_rg.astype(F32))
    br = br.at[EXPERT_ROW0:EXPERT_ROW0 + N_EXPERTS, 0].set(b_re.reshape(N_EXPERTS).astype(F32))
    return pl.pallas_call(
        functools.partial(_router_kernel, tr=tr),
        out_shape=(
            jax.ShapeDtypeStruct((tp, d), F32),
            jax.ShapeDtypeStruct((8, tp), jnp.int32),
            jax.ShapeDtypeStruct((8, tp), F32),
            jax.ShapeDtypeStruct((ROUTER_ROWS, LANES), F32),
        ),
        grid=(tp // tr,),
        in_specs=[
            pl.BlockSpec((tr, d), lambda i: (i, 0)),
            pl.BlockSpec((1, d), lambda i: (0, 0)),
            pl.BlockSpec((ROUTER_ROWS, d), lambda i: (0, 0)),
            pl.BlockSpec((ROUTER_ROWS, 1), lambda i: (0, 0)),
        ],
        out_specs=(
            pl.BlockSpec((tr, d), lambda i: (i, 0)),
            pl.BlockSpec((8, tr), lambda i: (0, i)),
            pl.BlockSpec((8, tr), lambda i: (0, i)),
            pl.BlockSpec((ROUTER_ROWS, LANES), lambda i: (0, 0)),
        ),
        scratch_shapes=[pltpu.VMEM((ROUTER_ROWS, 1), F32)],
        compiler_params=_params(("arbitrary",)),
        name="moe_router",
    )(h, gain.reshape(1, d), wr, br)


def _slots_kernel(cnt_ref, idx_ref, pos_ref, meta_ref):
    e1 = idx_ref[0:1, :]
    e2 = idx_ref[1:2, :]
    off1 = jnp.zeros_like(e1)
    off2 = jnp.zeros_like(e2)
    visit = lax.broadcasted_iota(jnp.int32, (1, meta_ref.shape[1]), 1)
    v_tile = jnp.zeros_like(visit)
    v_expert = jnp.zeros_like(visit)
    v_lo = jnp.zeros_like(visit)
    v_hi = jnp.zeros_like(visit)
    start = jnp.int32(0)
    v_base = jnp.int32(0)
    for e in range(N_EXPERTS):
        off1 = jnp.where(e1 == e, start, off1)
        off2 = jnp.where(e2 == e, start, off2)
        n = cnt_ref[e]
        end = start + n
        first_tile = start // TILE_M
        n_visits = jnp.where(n > 0, (jnp.maximum(end, 1) - 1) // TILE_M - first_tile + 1, 0)
        mine = jnp.logical_and(visit >= v_base, visit < v_base + n_visits)
        row0 = (first_tile + visit - v_base) * TILE_M
        v_tile = jnp.where(mine, first_tile + visit - v_base, v_tile)
        v_expert = jnp.where(mine, e, v_expert)
        v_lo = jnp.where(mine, jnp.maximum(start - row0, 0), v_lo)
        v_hi = jnp.where(mine, jnp.minimum(end - row0, TILE_M), v_hi)
        start = end
        v_base = v_base + n_visits
    r8 = lax.broadcasted_iota(jnp.int32, pos_ref.shape, 0)
    pos_ref[...] = jnp.where(r8 == 0, off1 + idx_ref[2:3, :], jnp.where(r8 == 1, off2 + idx_ref[3:4, :], 0))
    m8 = lax.broadcasted_iota(jnp.int32, meta_ref.shape, 0)
    meta_ref[...] = jnp.where(m8 == 0, v_tile, jnp.where(m8 == 1, v_expert, jnp.where(
        m8 == 2, v_lo, jnp.where(m8 == 3, v_hi, v_base))))


def moe_slots(counts, idx, tr=512):
    tp = idx.shape[1]
    nt_lanes = 2 * LANES
    return pl.pallas_call(
        _slots_kernel,
        out_shape=(jax.ShapeDtypeStruct((8, tp), jnp.int32), jax.ShapeDtypeStruct((8, nt_lanes), jnp.int32)),
        grid_spec=pltpu.PrefetchScalarGridSpec(
            num_scalar_prefetch=1,
            grid=(tp // tr,),
            in_specs=[pl.BlockSpec((8, tr), lambda i, c: (0, i))],
            out_specs=(
                pl.BlockSpec((8, tr), lambda i, c: (0, i)),
                pl.BlockSpec((8, nt_lanes), lambda i, c: (0, 0)),
            ),
        ),
        compiler_params=_params(("arbitrary",)),
        name="moe_slots",
    )(counts, idx)


def _invert_kernel(pos_ref, code_ref, *, tp):
    def per_token(t, carry):
        code_ref[pos_ref[t]] = 2 * t
        code_ref[pos_ref[tp + t]] = 2 * t + 1
        return carry

    lax.fori_loop(0, tp, per_token, 0, unroll=8)


def moe_invert(pos_flat):
    tp = pos_flat.shape[0] // 2
    return pl.pallas_call(
        functools.partial(_invert_kernel, tp=tp),
        out_shape=jax.ShapeDtypeStruct((2 * tp,), jnp.int32),
        grid_spec=pltpu.PrefetchScalarGridSpec(
            num_scalar_prefetch=1,
            grid=(1,),
            in_specs=[],
            out_specs=pl.BlockSpec(memory_space=pltpu.SMEM),
        ),
        compiler_params=_params(("arbitrary",)),
        name="moe_invert",
    )(pos_flat)


def _expert_kernel(vt_ref, ve_ref, vlo_ref, vhi_ref, nv_ref, code_ref, hn_ref, wg_ref, wu_ref, wd_ref, y_ref,
                   xbuf, xcur, wgb, wub, wdb, gsem, *, n_tiles):
    v = pl.program_id(0)
    nv = nv_ref[0]
    t = vt_ref[v]
    prev_v = jnp.maximum(v - 1, 0)
    first = jnp.logical_or(v == 0, vt_ref[prev_v] != t)
    new_expert = jnp.logical_or(v == 0, ve_ref[prev_v] != ve_ref[v])

    def row_copy(tile, r):
        tok = lax.shift_right_logical(code_ref[tile * TILE_M + r], 1)
        return pltpu.make_async_copy(hn_ref.at[pl.ds(tok, 1)], xbuf.at[pl.ds(r, 1)], gsem)

    def gather_wait():
        pltpu.make_async_copy(hn_ref.at[pl.ds(0, TILE_M)], xbuf, gsem).wait()

    def compute(accumulate):
        x = xcur[...]
        a = jnp.dot(x, wgb[...], preferred_element_type=F32)
        u = jnp.dot(x, wub[...], preferred_element_type=F32)
        row = lax.broadcasted_iota(jnp.int32, (TILE_M, 1), 0)
        mine = jnp.logical_and(row >= vlo_ref[v], row < vhi_ref[v])
        hid = jnp.where(mine, a * _sigmoid(a) * u, 0.0).astype(BF16)
        y = jnp.dot(hid, wdb[...], preferred_element_type=F32)
        if accumulate:
            y_ref[...] += y
        else:
            y_ref[...] = y

    @pl.when(v == 0)
    def _():
        def step(r, carry):
            row_copy(0, r).start()
            return carry
        lax.fori_loop(0, TILE_M, step, 0, unroll=8)

    @pl.when(v < nv)
    def _():
        @pl.when(new_expert)
        def _():
            wgb[...] = wg_ref[...].astype(BF16)
            wub[...] = wu_ref[...].astype(BF16)
            wdb[...] = wd_ref[...].astype(BF16)

        @pl.when(first)
        def _():
            gather_wait()
            xcur[...] = xbuf[...].astype(BF16)
            ahead = jnp.minimum(t + 1, n_tiles - 1)
            for r in range(TILE_M):
                row_copy(ahead, r).start()
            compute(False)

        @pl.when(jnp.logical_not(first))
        def _():
            compute(True)

        @pl.when(v == nv - 1)
        def _():
            gather_wait()


def moe_experts(meta, code, hn, w_gate, w_up, w_down, layer):
    tp, d = hn.shape
    f = w_gate.shape[-1]
    assert (2 * tp) % TILE_M == 0
    n_tiles = (2 * tp) // TILE_M
    max_visits = n_tiles + N_EXPERTS - 1

    def w_map(v, vt, ve, vlo, vhi, nv, code):
        return (layer * N_EXPERTS + ve[jnp.minimum(v, nv[0] - 1)], 0, 0)

    def y_map(v, vt, ve, vlo, vhi, nv, code):
        return (vt[jnp.minimum(v, nv[0] - 1)], 0)

    return pl.pallas_call(
        functools.partial(_expert_kernel, n_tiles=n_tiles),
        out_shape=jax.ShapeDtypeStruct((2 * tp, d), F32),
        grid_spec=pltpu.PrefetchScalarGridSpec(
            num_scalar_prefetch=6,
            grid=(max_visits,),
            in_specs=[
                pl.BlockSpec(memory_space=pl.ANY),
                pl.BlockSpec((None, d, f), w_map),
                pl.BlockSpec((None, d, f), w_map),
                pl.BlockSpec((None, f, d), w_map),
            ],
            out_specs=pl.BlockSpec((TILE_M, d), y_map),
            scratch_shapes=[
                pltpu.VMEM((TILE_M, d), F32), pltpu.VMEM((TILE_M, d), BF16),
                pltpu.VMEM((d, f), BF16), pltpu.VMEM((d, f), BF16), pltpu.VMEM((f, d), BF16),
                pltpu.SemaphoreType.DMA,
            ],
        ),
        compiler_params=_params(("arbitrary",), vmem=56 * 1024 * 1024),
        name="moe_experts",
    )(meta[0, :max_visits], meta[1, :max_visits], meta[2, :max_visits], meta[3, :max_visits], meta[4, :1],
      code, hn, w_gate, w_up, w_down)


def _combine_kernel(pos_ref, h_ref, w_ref, g_ref, y_ref, *rest, tc, tp, n_steps, write_h):
    out_refs, (ybuf, sem) = rest[:-2], rest[-2:]
    i = pl.program_id(0)
    slot = lax.rem(i, 2)

    def row_copy(tile, buf, s, r):
        p = pos_ref[s * tp + tile * tc + r]
        return pltpu.make_async_copy(y_ref.at[pl.ds(p, 1)], ybuf.at[buf, pl.ds(s * tc + r, 1)], sem.at[buf])

    @pl.when(i == 0)
    def _():
        def step(r, carry):
            row_copy(0, 0, 0, r).start()
            row_copy(0, 0, 1, r).start()
            return carry
        lax.fori_loop(0, tc, step, 0, unroll=8)

    pltpu.make_async_copy(y_ref.at[pl.ds(0, 2 * tc)], ybuf.at[slot], sem.at[slot]).wait()

    @pl.when(i + 1 < n_steps)
    def _():
        for r in range(tc):
            row_copy(i + 1, 1 - slot, 0, r).start()
            row_copy(i + 1, 1 - slot, 1, r).start()

    w = w_ref[...]
    h = h_ref[...] + w[:, 0:1] * ybuf[slot, :tc, :] + w[:, 1:2] * ybuf[slot, tc:, :]
    if write_h:
        out_refs[0][...] = h
    out_refs[-1][...] = _rms(h, g_ref[...]).astype(out_refs[-1].dtype)


def moe_combine(pos_flat, h, w_col, y, gain, hn_dtype, final_shape=None):
    tp, d = h.shape
    if final_shape is None:
        tc = 256
        out_shape = (jax.ShapeDtypeStruct((tp, d), F32), jax.ShapeDtypeStruct((tp, d), hn_dtype))
        out_specs = (pl.BlockSpec((tc, d), lambda i, p: (i, 0)), pl.BlockSpec((tc, d), lambda i, p: (i, 0)))
    else:
        tc = CHUNK
        bsz, seq, _ = final_shape
        nc = tp // bsz // tc
        out_shape = (jax.ShapeDtypeStruct(final_shape, hn_dtype),)
        out_specs = (pl.BlockSpec((None, tc, d), lambda i, p: (i // nc, jnp.maximum(i % nc - 1, 0), 0)),)
    n_steps = tp // tc
    return pl.pallas_call(
        functools.partial(_combine_kernel, tc=tc, tp=tp, n_steps=n_steps, write_h=final_shape is None),
        out_shape=out_shape,
        grid_spec=pltpu.PrefetchScalarGridSpec(
            num_scalar_prefetch=1,
            grid=(n_steps,),
            in_specs=[
                pl.BlockSpec((tc, d), lambda i, p: (i, 0)),
                pl.BlockSpec((tc, 2), lambda i, p: (i, 0)),
                pl.BlockSpec((1, d), lambda i, p: (0, 0)),
                pl.BlockSpec(memory_space=pl.ANY),
            ],
            out_specs=out_specs,
            scratch_shapes=[pltpu.VMEM((2, 2 * tc, d), F32), pltpu.SemaphoreType.DMA((2,))],
        ),
        compiler_params=_params(("arbitrary",)),
        name="moe_combine",
    )(pos_flat, h, w_col, gain.reshape(1, d), y)


def hierarchical_moe(h, gain, w_rg, b_rg, w_re, b_re, w_gate, w_up, w_down, layer, next_gain, hn_dtype,
                     final_shape=None):
    tp, d = h.shape
    f = w_gate.shape[-1]
    hn, idx, w_rows, cnt = moe_router(h, gain, w_rg, b_rg, w_re, b_re)
    counts = cnt[EXPERT_ROW0:EXPERT_ROW0 + N_EXPERTS, 0].astype(jnp.int32)
    pos, meta = moe_slots(counts, idx)
    pos_flat = pos[:2].reshape(2 * tp)
    code = moe_invert(pos_flat)
    y = moe_experts(meta, code, hn, w_gate.reshape(-1, d, f), w_up.reshape(-1, d, f),
                    w_down.reshape(-1, f, d), layer)
    return moe_combine(pos_flat, h, w_rows[:2].T, y, next_gain, hn_dtype, final_shape)


def kernel(x, meta_tokens, norm_mixer, norm_ffn, norm_final, ret_w_in, ret_gn, ret_w_out,
           fox_w_in, fox_b_f, fox_w_out, moe_w_rg, moe_b_rg, moe_w_re, moe_b_re,
           moe_w_gate, moe_w_up, moe_w_down):
    bsz, seq, d = x.shape
    depth = norm_mixer.shape[0]
    nc = (seq + CHUNK) // CHUNK
    lp = nc * CHUNK
    h, hn = embed_norm(x, meta_tokens.astype(x.dtype), norm_mixer[0], BF16)
    for i in range(depth):
        j = i // 2
        if i % 2 == 0:
            proj = matmul(hn, ret_w_in[j], BF16)
            gated = retention_core(proj, ret_gn[j], bsz, nc)
            h = matmul(gated, ret_w_out[j], F32, residual=h, tn=512)
        else:
            qkv = matmul(hn, fox_w_in[j], BF16, n=3 * d, scale_cols=d, scale=(d // FOX_HEADS) ** -0.5 * LOG2E)
            cq, ck = forget_bias(hn, fox_w_in[j][:, 3 * d:], fox_b_f[j], bsz, nc)
            o = fox_attention(qkv, cq, ck, bsz, lp)
            h = matmul(o, fox_w_out[j], F32, residual=h)
        last = i == depth - 1
        outs = hierarchical_moe(h, norm_ffn[i], moe_w_rg[i], moe_b_rg[i], moe_w_re[i], moe_b_re[i],
                                moe_w_gate, moe_w_up, moe_w_down, i,
                                norm_final if last else norm_mixer[i + 1],
                                F32 if last else BF16, (bsz, seq, d) if last else None)
        if last:
            return outs[0]
        h, hn = outs
```

```python
import functools

import jax
import jax.numpy as jnp
from jax import lax
from jax.experimental import pallas as pl
from jax.experimental.pallas import tpu as pltpu

N_META = 16
CHUNK = 128
PAD = CHUNK - N_META
RMS_EPS = 1e-6
GN_EPS = 1e-6
NEG_INF = -1e30
RET_HEADS = 8
FOX_HEADS = 16
N_GROUPS = 4
EXPERTS_PER_GROUP = 8
N_EXPERTS = N_GROUPS * EXPERTS_PER_GROUP
ROPE_BASE = 10000.0

LANES = 128
ROUTER_ROWS = 48
EXPERT_ROW0 = N_GROUPS
TILE_M = 256
VMEM_LIMIT = 48 * 1024 * 1024

F32 = jnp.float32
BF16 = jnp.bfloat16


def _params(sem, vmem=VMEM_LIMIT):
    return pltpu.CompilerParams(dimension_semantics=sem, vmem_limit_bytes=vmem)


def _rms(h, g):
    return h * lax.rsqrt(jnp.mean(h * h, axis=-1, keepdims=True) + RMS_EPS) * g


def _sigmoid(x):
    return 1.0 / (1.0 + jnp.exp(-x))


def _embed_norm_kernel(x_ref, meta_ref, g_ref, h_ref, hn_ref):
    i = pl.program_id(1)

    @pl.when(i == 0)
    def _():
        h_ref[...] = jnp.zeros_like(h_ref)
        h_ref[PAD:, :] = meta_ref[...]

    @pl.when(i > 0)
    def _():
        h_ref[...] = x_ref[...]

    hn_ref[...] = _rms(h_ref[...], g_ref[...]).astype(hn_ref.dtype)


def embed_norm(x, meta, gain, hn_dtype):
    bsz, seq, d = x.shape
    nc = (seq + CHUNK) // CHUNK
    tp = bsz * nc * CHUNK
    return pl.pallas_call(
        _embed_norm_kernel,
        out_shape=(jax.ShapeDtypeStruct((tp, d), F32), jax.ShapeDtypeStruct((tp, d), hn_dtype)),
        grid=(bsz, nc),
        in_specs=[
            pl.BlockSpec((None, CHUNK, d), lambda b, i: (b, jnp.maximum(i - 1, 0), 0)),
            pl.BlockSpec((N_META, d), lambda b, i: (0, 0)),
            pl.BlockSpec((1, d), lambda b, i: (0, 0)),
        ],
        out_specs=(
            pl.BlockSpec((CHUNK, d), lambda b, i: (b * nc + i, 0)),
            pl.BlockSpec((CHUNK, d), lambda b, i: (b * nc + i, 0)),
        ),
        compiler_params=_params(("parallel", "parallel")),
        name="embed_norm",
    )(x, meta, gain.reshape(1, d))


def _mm_kernel(*refs, has_res, scale_tiles, scale):
    if has_res:
        x_ref, w_ref, r_ref, o_ref, wb_ref = refs
    else:
        x_ref, w_ref, o_ref, wb_ref = refs

    @pl.when(pl.program_id(1) == 0)
    def _():
        wb_ref[...] = w_ref[...].astype(BF16)

    acc = jnp.dot(x_ref[...], wb_ref[...], preferred_element_type=F32)
    if scale_tiles:
        acc = acc * jnp.where(pl.program_id(0) < scale_tiles, scale, 1.0)
    if has_res:
        acc = acc + r_ref[...]
    o_ref[...] = acc.astype(o_ref.dtype)


def matmul(x, w, out_dtype, n=None, residual=None, tm=512, tn=1024, scale_cols=0, scale=1.0):
    m, kdim = x.shape
    n = w.shape[1] if n is None else n
    in_specs = [
        pl.BlockSpec((tm, kdim), lambda j, i: (i, 0)),
        pl.BlockSpec((kdim, tn), lambda j, i: (0, j)),
    ]
    args = [x, w]
    if residual is not None:
        in_specs.append(pl.BlockSpec((tm, tn), lambda j, i: (i, j)))
        args.append(residual)
    return pl.pallas_call(
        functools.partial(_mm_kernel, has_res=residual is not None, scale_tiles=scale_cols // tn, scale=scale),
        out_shape=jax.ShapeDtypeStruct((m, n), out_dtype),
        grid=(n // tn, m // tm),
        in_specs=in_specs,
        out_specs=pl.BlockSpec((tm, tn), lambda j, i: (i, j)),
        scratch_shapes=[pltpu.VMEM((kdim, tn), BF16)],
        compiler_params=_params(("parallel", "arbitrary")),
        name="matmul",
    )(*args)


def _retention_kernel(q_ref, k_ref, v_ref, g_ref, cos_ref, sin_ref, dmat_ref, xi_ref, zeta_ref,
                      gch_ref, gn_ref, o_ref, state_ref, *, nh, dk, dv):
    c = pl.program_id(1)

    @pl.when(c == 0)
    def _():
        state_ref[...] = jnp.zeros_like(state_ref)

    cos = cos_ref[...]
    sin = sin_ref[...]
    half = dk // 2

    def rot(u):
        u1, u2 = u[:, :half], u[:, half:]
        return jnp.concatenate([u1 * cos - u2 * sin, u1 * sin + u2 * cos], axis=1)

    row = lax.broadcasted_iota(jnp.int32, (CHUNK, 1), 0)
    valid = jnp.logical_or(row >= PAD, c > 0)
    for h in range(nh):
        q = rot(q_ref[:, h * dk:(h + 1) * dk].astype(F32))
        k = rot(k_ref[:, h * dk:(h + 1) * dk].astype(F32)) * (dk ** -0.5)
        k = jnp.where(valid, k, 0.0)
        v = jnp.where(valid, v_ref[:, h * dv:(h + 1) * dv].astype(F32), 0.0)
        qb = q.astype(BF16)
        kb = k.astype(BF16)
        vb = v.astype(BF16)

        scores = lax.dot_general(qb, kb, (((1,), (1,)), ((), ())), preferred_element_type=F32)
        scores = scores * dmat_ref[h]
        inner = jnp.dot(scores.astype(BF16), vb, preferred_element_type=F32)
        state = state_ref[h]
        cross = jnp.dot(qb, state.astype(BF16), preferred_element_type=F32) * xi_ref[h]
        y = inner + cross
        vz = (v * zeta_ref[h]).astype(BF16)
        state_ref[h] = gch_ref[h] * state + lax.dot_general(
            kb, vz, (((0,), (0,)), ((), ())), preferred_element_type=F32)

        mu = jnp.mean(y, axis=-1, keepdims=True)
        dlt = y - mu
        var = jnp.mean(dlt * dlt, axis=-1, keepdims=True)
        yn = dlt * lax.rsqrt(var + GN_EPS) * gn_ref[h]
        g = g_ref[:, h * dv:(h + 1) * dv].astype(F32)
        o_ref[:, h * dv:(h + 1) * dv] = (g * _sigmoid(g) * yn).astype(o_ref.dtype)


def retention_core(proj, gn_gain, bsz, nc):
    tp = proj.shape[0]
    d = proj.shape[1] // 6
    nh = RET_HEADS
    dk = d // nh
    dv = 2 * d // nh
    lp = nc * CHUNK
    half = dk // 2
    pos = (jnp.arange(lp) - PAD).astype(F32)
    inv = ROPE_BASE ** (-jnp.arange(half, dtype=F32) / half)
    ang = pos[:, None] * inv[None, :]
    cos, sin = jnp.cos(ang), jnp.sin(ang)
    log_g = jnp.log1p(-jnp.exp2(-5.0 - jnp.arange(nh, dtype=F32)))
    idx = jnp.arange(CHUNK, dtype=F32)
    diff = idx[:, None] - idx[None, :]
    dmat = jnp.where(diff[None] >= 0, jnp.exp(jnp.maximum(diff, 0.0)[None] * log_g[:, None, None]), 0.0)
    xi = jnp.exp((idx + 1.0)[None, :] * log_g[:, None])[:, :, None]
    zeta = jnp.exp((CHUNK - 1.0 - idx)[None, :] * log_g[:, None])[:, :, None]
    g_chunk = jnp.exp(CHUNK * log_g)[:, None, None]
    return pl.pallas_call(
        functools.partial(_retention_kernel, nh=nh, dk=dk, dv=dv),
        out_shape=jax.ShapeDtypeStruct((tp, 2 * d), BF16),
        grid=(bsz, nc),
        in_specs=[
            pl.BlockSpec((CHUNK, d), lambda b, c: (b * nc + c, 0)),
            pl.BlockSpec((CHUNK, d), lambda b, c: (b * nc + c, 1)),
            pl.BlockSpec((CHUNK, 2 * d), lambda b, c: (b * nc + c, 1)),
            pl.BlockSpec((CHUNK, 2 * d), lambda b, c: (b * nc + c, 2)),
            pl.BlockSpec((CHUNK, half), lambda b, c: (c, 0)),
            pl.BlockSpec((CHUNK, half), lambda b, c: (c, 0)),
            pl.BlockSpec((nh, CHUNK, CHUNK), lambda b, c: (0, 0, 0)),
            pl.BlockSpec((nh, CHUNK, 1), lambda b, c: (0, 0, 0)),
            pl.BlockSpec((nh, CHUNK, 1), lambda b, c: (0, 0, 0)),
            pl.BlockSpec((nh, 1, 1), lambda b, c: (0, 0, 0)),
            pl.BlockSpec((nh, 1, dv), lambda b, c: (0, 0, 0)),
        ],
        out_specs=pl.BlockSpec((CHUNK, 2 * d), lambda b, c: (b * nc + c, 0)),
        scratch_shapes=[pltpu.VMEM((nh, dk, dv), F32)],
        compiler_params=_params(("parallel", "arbitrary")),
        name="retention_core",
    )(proj, proj, proj, proj, cos, sin, dmat, xi, zeta, g_chunk, gn_gain.reshape(nh, 1, dv))


LOG2E = 1.4426950408889634
MASK_BIG = 1e30
ONES_LANE = LANES - 1


def _bias_selectors(nh):
    h = jnp.arange(nh)
    selq = jnp.zeros((nh, 3 * LANES, LANES), F32)
    selk = jnp.zeros((nh, 3 * LANES, LANES), F32)
    for part in range(3):
        selq = selq.at[h, part * LANES + h, part].set(1.0)
        selk = selk.at[h, part * LANES + h, 3 + part].set(-1.0)
        selq = selq.at[h, ONES_LANE, 3 + part].set(1.0)
        selk = selk.at[h, ONES_LANE, part].set(1.0)
    return selq.astype(BF16), selk.astype(BF16)


def _forget_kernel(hn_ref, wf_ref, bf_ref, pq_ref, pk_ref, carry_ref, *, rows):
    i = pl.program_id(1)

    @pl.when(i == 0)
    def _():
        carry_ref[...] = jnp.zeros_like(carry_ref)

    z = jnp.dot(hn_ref[...], wf_ref[...], preferred_element_type=F32) + bf_ref[...]
    lf = jnp.minimum(z, 0.0) - jnp.log1p(jnp.exp(-jnp.abs(z)))
    row = lax.broadcasted_iota(jnp.int32, (rows, 1), 0)
    valid = i * rows + row >= PAD
    lf = jnp.where(valid, lf, 0.0)

    def split3(a):
        hi = a.astype(BF16)
        r1 = a - hi.astype(F32)
        mid = r1.astype(BF16)
        lo = (r1 - mid.astype(F32)).astype(BF16)
        return hi, mid, lo

    r_i = lax.broadcasted_iota(jnp.int32, (rows, rows), 0)
    c_i = lax.broadcasted_iota(jnp.int32, (rows, rows), 1)
    tri = (r_i >= c_i).astype(BF16)
    cs = sum(jnp.dot(tri, part, preferred_element_type=F32) for part in split3(lf))
    cs = cs + carry_ref[...]
    carry_ref[...] = cs[rows - 1:rows, :]

    hi, mid, lo = (p.astype(F32) for p in split3(cs * LOG2E))
    lane = lax.broadcasted_iota(jnp.int32, (rows, LANES), 1)
    ones = lane == ONES_LANE
    live = jnp.logical_and(valid, jnp.logical_not(ones))
    pq_ref[...] = jnp.concatenate(
        [jnp.where(ones, 1.0, hi), jnp.where(ones, 0.0, mid), jnp.where(ones, 0.0, lo)], axis=1).astype(BF16)
    pk_ref[...] = jnp.concatenate(
        [jnp.where(ones, 1.0, jnp.where(valid, hi, MASK_BIG)), jnp.where(live, mid, 0.0),
         jnp.where(live, lo, 0.0)], axis=1).astype(BF16)


def forget_gates(hn, w_f, b_f, bsz, lp, rows=384):
    tp, d = hn.shape
    nh = w_f.shape[1]
    steps = lp // rows
    wf = jnp.zeros((d, LANES), BF16).at[:, :nh].set(w_f.astype(BF16))
    bf = jnp.zeros((1, LANES), F32).at[0, :nh].set(b_f.astype(F32))
    return pl.pallas_call(
        functools.partial(_forget_kernel, rows=rows),
        out_shape=(jax.ShapeDtypeStruct((tp, 3 * LANES), BF16), jax.ShapeDtypeStruct((tp, 3 * LANES), BF16)),
        grid=(bsz, steps),
        in_specs=[
            pl.BlockSpec((rows, d), lambda b, i: (b * steps + i, 0)),
            pl.BlockSpec((d, LANES), lambda b, i: (0, 0)),
            pl.BlockSpec((1, LANES), lambda b, i: (0, 0)),
        ],
        out_specs=(
            pl.BlockSpec((rows, 3 * LANES), lambda b, i: (b * steps + i, 0)),
            pl.BlockSpec((rows, 3 * LANES), lambda b, i: (b * steps + i, 0)),
        ),
        scratch_shapes=[pltpu.VMEM((1, LANES), F32)],
        compiler_params=_params(("parallel", "arbitrary")),
        name="forget_gates",
    )(hn, wf, bf)


FOX_HEADS_PER_STEP = 4


FOX_STRIP = 32


def _fox_kernel(q_ref, pq_ref, k_ref, pk_ref, v_ref, selq_ref, selk_ref, o_ref,
                m_ref, l_ref, acc_ref, s_ref, p_ref, a_ref, ck_ref, *, tq, dh):
    i = pl.program_id(2)
    hs = FOX_HEADS_PER_STEP
    reps = tq // LANES
    m_ref[...] = jnp.full_like(m_ref, 10.0 * NEG_INF)
    l_ref[...] = jnp.zeros_like(l_ref)
    acc_ref[...] = jnp.zeros_like(acc_ref)

    @pl.when(i == 0)
    def _():
        for j in range(hs):
            ck_ref[:, j * LANES:(j + 1) * LANES] = jnp.dot(
                pk_ref[...], selk_ref[j], preferred_element_type=F32).astype(BF16)

    qa = [jnp.concatenate([q_ref[:, j * dh:(j + 1) * dh],
                           jnp.dot(pq_ref[...], selq_ref[j], preferred_element_type=F32).astype(BF16)], axis=1)
          for j in range(hs)]

    def scores(kb, buf):
        start = pl.multiple_of(kb * tq, tq)
        for j in range(hs):
            ka = jnp.concatenate([k_ref[pl.ds(start, tq), j * dh:(j + 1) * dh],
                                  ck_ref[pl.ds(start, tq), j * LANES:(j + 1) * LANES]], axis=1)
            s_ref[buf, j] = lax.dot_general(qa[j], ka, (((1,), (1,)), ((), ())), preferred_element_type=F32)

    def absorb(kb, buf, causal):
        start = pl.multiple_of(kb * tq, tq)
        for j in range(hs):
            for r0 in range(0, tq, FOX_STRIP):
                rows = slice(r0, r0 + FOX_STRIP)
                s = s_ref[buf, j, rows, :]
                if causal:
                    r_i = r0 + lax.broadcasted_iota(jnp.int32, (FOX_STRIP, tq), 0)
                    c_i = lax.broadcasted_iota(jnp.int32, (FOX_STRIP, tq), 1)
                    s = jnp.where(c_i <= r_i, s, NEG_INF)
                m_old = m_ref[j, rows, :]
                m_new = jnp.maximum(m_old, jnp.max(s, axis=1, keepdims=True))
                alpha = jnp.exp2(m_old - m_new)
                p = jnp.exp2(s - jnp.concatenate([m_new] * reps, axis=1))
                l_ref[j, rows, :] = alpha * l_ref[j, rows, :] + jnp.sum(p, axis=1, keepdims=True)
                m_ref[j, rows, :] = m_new
                a_ref[j, rows, :] = alpha
                p_ref[j, rows, :] = p.astype(BF16)
        for j in range(hs):
            acc_ref[j] = a_ref[j] * acc_ref[j] + jnp.dot(
                p_ref[j], v_ref[pl.ds(start, tq), j * dh:(j + 1) * dh], preferred_element_type=F32)

    def body(kb, carry):
        scores(kb, 0)
        absorb(kb, 0, False)
        return carry

    lax.fori_loop(0, i, body, 0)
    scores(i, 0)
    absorb(i, 0, True)

    for j in range(hs):
        o_ref[:, j * dh:(j + 1) * dh] = (acc_ref[j] / l_ref[j]).astype(o_ref.dtype)


def fox_attention(qkv, pq, pk, bsz, lp, tq=384):
    tp = qkv.shape[0]
    d = qkv.shape[1] // 3
    nh = FOX_HEADS
    dh = d // nh
    assert dh == LANES
    nq = lp // tq
    hs = FOX_HEADS_PER_STEP
    ng = nh // hs
    qkv3 = qkv.reshape(bsz, lp, 3 * d)
    pk3 = pk.reshape(bsz, lp, 3 * LANES)
    selq, selk = _bias_selectors(nh)
    return pl.pallas_call(
        functools.partial(_fox_kernel, tq=tq, dh=dh),
        out_shape=jax.ShapeDtypeStruct((tp, d), BF16),
        grid=(bsz, ng, nq),
        in_specs=[
            pl.BlockSpec((tq, hs * dh), lambda b, g, i: (b * nq + i, g)),
            pl.BlockSpec((tq, 3 * LANES), lambda b, g, i: (b * nq + i, 0)),
            pl.BlockSpec((None, lp, hs * dh), lambda b, g, i: (b, 0, ng + g)),
            pl.BlockSpec((None, lp, 3 * LANES), lambda b, g, i: (b, 0, 0)),
            pl.BlockSpec((None, lp, hs * dh), lambda b, g, i: (b, 0, 2 * ng + g)),
            pl.BlockSpec((hs, 3 * LANES, LANES), lambda b, g, i: (g, 0, 0)),
            pl.BlockSpec((hs, 3 * LANES, LANES), lambda b, g, i: (g, 0, 0)),
        ],
        out_specs=pl.BlockSpec((tq, hs * dh), lambda b, g, i: (b * nq + i, g)),
        scratch_shapes=[pltpu.VMEM((hs, tq, LANES), F32), pltpu.VMEM((hs, tq, LANES), F32),
                        pltpu.VMEM((hs, tq, dh), F32), pltpu.VMEM((1, hs, tq, tq), F32),
                        pltpu.VMEM((hs, tq, tq), BF16), pltpu.VMEM((hs, tq, LANES), F32),
                        pltpu.VMEM((lp, hs * LANES), BF16)],
        compiler_params=_params(("parallel", "parallel", "arbitrary")),
        name="fox_attention",
    )(qkv, pq, qkv3, pk3, qkv3, selq, selk)


def _router_kernel(h_ref, g_ref, wr_ref, br_ref, hn_ref, idx_ref, w_ref, cnt_ref, carry_ref, *, tr):
    i = pl.program_id(0)

    @pl.when(i == 0)
    def _():
        carry_ref[...] = jnp.zeros_like(carry_ref)

    hn = _rms(h_ref[...], g_ref[...])
    hn_ref[...] = hn
    lg = lax.dot_general(wr_ref[...], hn.astype(BF16), (((1,), (1,)), ((), ())),
                         preferred_element_type=F32) + br_ref[...]
    row = lax.broadcasted_iota(jnp.int32, (ROUTER_ROWS, tr), 0)
    big = jnp.int32(1 << 20)
    is_g = row < N_GROUPS
    mg = jnp.max(jnp.where(is_g, lg, -jnp.inf), axis=0, keepdims=True)
    g_sel = jnp.min(jnp.where(jnp.logical_and(is_g, lg == mg), row, big), axis=0, keepdims=True)
    sg = jnp.sum(jnp.where(is_g, jnp.exp(lg - mg), 0.0), axis=0, keepdims=True)
    p_g = 1.0 / sg
    lo = EXPERT_ROW0 + EXPERTS_PER_GROUP * g_sel
    is_e = jnp.logical_and(row >= lo, row < lo + EXPERTS_PER_GROUP)
    me = jnp.max(jnp.where(is_e, lg, -jnp.inf), axis=0, keepdims=True)
    ee = jnp.where(is_e, jnp.exp(lg - me), 0.0)
    pe = ee / jnp.sum(ee, axis=0, keepdims=True)
    pe1 = jnp.where(is_e, pe, -1.0)
    m1 = jnp.max(pe1, axis=0, keepdims=True)
    i1 = jnp.min(jnp.where(pe1 == m1, row, big), axis=0, keepdims=True)
    pe2 = jnp.where(row == i1, -1.0, pe1)
    m2 = jnp.max(pe2, axis=0, keepdims=True)
    i2 = jnp.min(jnp.where(pe2 == m2, row, big), axis=0, keepdims=True)
    den = m1 + m2
    w1 = p_g * (m1 / den)
    w2 = p_g * (m2 / den)

    oh1 = row == i1
    oh2 = row == i2
    ohs = jnp.logical_or(oh1, oh2).astype(F32)
    r_i = lax.broadcasted_iota(jnp.int32, (tr, tr), 0)
    c_i = lax.broadcasted_iota(jnp.int32, (tr, tr), 1)
    tri = (r_i < c_i).astype(BF16)
    cnt = jnp.dot(ohs.astype(BF16), tri, preferred_element_type=F32) + carry_ref[...]
    rank1 = jnp.sum(jnp.where(oh1, cnt, 0.0), axis=0, keepdims=True)
    rank2 = jnp.sum(jnp.where(oh2, cnt, 0.0), axis=0, keepdims=True)
    carry_ref[...] += jnp.sum(ohs, axis=1, keepdims=True)

    r8 = lax.broadcasted_iota(jnp.int32, (8, tr), 0)
    e1 = i1 - EXPERT_ROW0
    e2 = i2 - EXPERT_ROW0
    idx_ref[...] = jnp.where(r8 == 0, e1, jnp.where(r8 == 1, e2, jnp.where(
        r8 == 2, rank1.astype(jnp.int32), jnp.where(r8 == 3, rank2.astype(jnp.int32), 0))))
    w_ref[...] = jnp.where(r8 == 0, w1, jnp.where(r8 == 1, w2, 0.0))
    cnt_ref[...] = jnp.broadcast_to(carry_ref[...], cnt_ref.shape)


def moe_router(h, gain, w_rg, b_rg, w_re, b_re, tr=512):
    tp, d = h.shape
    wr = jnp.zeros((ROUTER_ROWS, d), BF16)
    wr = wr.at[:N_GROUPS].set(w_rg.T.astype(BF16))
    wr = wr.at[EXPERT_ROW0:EXPERT_ROW0 + N_EXPERTS].set(w_re.reshape(d, N_EXPERTS).T.astype(BF16))
    br = jnp.zeros((ROUTER_ROWS, 1), F32)
    br = br.at[:N_GROUPS, 0].set(b_rg.astype(F32))
    br = br.at[EXPERT_ROW0:EXPERT_ROW0 + N_EXPERTS, 0].set(b_re.reshape(N_EXPERTS).astype(F32))
    return pl.pallas_call(
        functools.partial(_router_kernel, tr=tr),
        out_shape=(
            jax.ShapeDtypeStruct((tp, d), F32),
            jax.ShapeDtypeStruct((8, tp), jnp.int32),
            jax.ShapeDtypeStruct((8, tp), F32),
            jax.ShapeDtypeStruct((ROUTER_ROWS, LANES), F32),
        ),
        grid=(tp // tr,),
        in_specs=[
            pl.BlockSpec((tr, d), lambda i: (i, 0)),
            pl.BlockSpec((1, d), lambda i: (0, 0)),
            pl.BlockSpec((ROUTER_ROWS, d), lambda i: (0, 0)),
            pl.BlockSpec((ROUTER_ROWS, 1), lambda i: (0, 0)),
        ],
        out_specs=(
            pl.BlockSpec((tr, d), lambda i: (i, 0)),
            pl.BlockSpec((8, tr), lambda i: (0, i)),
            pl.BlockSpec((8, tr), lambda i: (0, i)),
            pl.BlockSpec((ROUTER_ROWS, LANES), lambda i: (0, 0)),
        ),
        scratch_shapes=[pltpu.VMEM((ROUTER_ROWS, 1), F32)],
        compiler_params=_params(("arbitrary",)),
        name="moe_router",
    )(h, gain.reshape(1, d), wr, br)


def _slots_kernel(cnt_ref, idx_ref, pos_ref, meta_ref):
    e1 = idx_ref[0:1, :]
    e2 = idx_ref[1:2, :]
    off1 = jnp.zeros_like(e1)
    off2 = jnp.zeros_like(e2)
    visit = lax.broadcasted_iota(jnp.int32, (1, meta_ref.shape[1]), 1)
    v_tile = jnp.zeros_like(visit)
    v_expert = jnp.zeros_like(visit)
    v_lo = jnp.zeros_like(visit)
    v_hi = jnp.zeros_like(visit)
    v_next = jnp.zeros_like(visit)
    v_start = jnp.zeros_like(visit)
    v_count = jnp.zeros_like(visit)
    next_live = [None] * N_EXPERTS
    nxt = jnp.int32(-1)
    for e in reversed(range(N_EXPERTS)):
        next_live[e] = jnp.where(nxt >= 0, nxt, e)
        nxt = jnp.where(cnt_ref[e] > 0, e, nxt)
    start = jnp.int32(0)
    v_base = jnp.int32(0)
    for e in range(N_EXPERTS):
        off1 = jnp.where(e1 == e, start, off1)
        off2 = jnp.where(e2 == e, start, off2)
        n = cnt_ref[e]
        end = start + n
        first_tile = start // TILE_M
        n_visits = jnp.where(n > 0, (jnp.maximum(end, 1) - 1) // TILE_M - first_tile + 1, 0)
        mine = jnp.logical_and(visit >= v_base, visit < v_base + n_visits)
        row0 = (first_tile + visit - v_base) * TILE_M
        v_tile = jnp.where(mine, first_tile + visit - v_base, v_tile)
        v_expert = jnp.where(mine, e, v_expert)
        v_next = jnp.where(mine, next_live[e], v_next)
        v_start = jnp.where(mine, v_base, v_start)
        v_count = jnp.where(mine, n_visits, v_count)
        v_lo = jnp.where(mine, jnp.maximum(start - row0, 0), v_lo)
        v_hi = jnp.where(mine, jnp.minimum(end - row0, TILE_M), v_hi)
        start = end
        v_base = v_base + n_visits
    r8 = lax.broadcasted_iota(jnp.int32, pos_ref.shape, 0)
    pos_ref[...] = jnp.where(r8 == 0, off1 + idx_ref[2:3, :], jnp.where(r8 == 1, off2 + idx_ref[3:4, :], 0))
    m8 = lax.broadcasted_iota(jnp.int32, meta_ref.shape, 0)
    rows = (v_tile, v_expert, v_lo, v_hi, v_base, v_next, v_start, v_count)
    meta = rows[-1]
    for r in reversed(range(len(rows) - 1)):
        meta = jnp.where(m8 == r, rows[r], meta)
    meta_ref[...] = meta


def moe_slots(counts, idx, tr=512):
    tp = idx.shape[1]
    nt_lanes = 2 * LANES
    return pl.pallas_call(
        _slots_kernel,
        out_shape=(jax.ShapeDtypeStruct((8, tp), jnp.int32), jax.ShapeDtypeStruct((8, nt_lanes), jnp.int32)),
        grid_spec=pltpu.PrefetchScalarGridSpec(
            num_scalar_prefetch=1,
            grid=(tp // tr,),
            in_specs=[pl.BlockSpec((8, tr), lambda i, c: (0, i))],
            out_specs=(
                pl.BlockSpec((8, tr), lambda i, c: (0, i)),
                pl.BlockSpec((8, nt_lanes), lambda i, c: (0, 0)),
            ),
        ),
        compiler_params=_params(("arbitrary",)),
        name="moe_slots",
    )(counts, idx)


def _invert_kernel(pos_ref, code_ref, *, tp):
    def per_token(t, carry):
        code_ref[pos_ref[t]] = 2 * t
        code_ref[pos_ref[tp + t]] = 2 * t + 1
        return carry

    lax.fori_loop(0, tp, per_token, 0, unroll=8)


def moe_invert(pos_flat):
    tp = pos_flat.shape[0] // 2
    return pl.pallas_call(
        functools.partial(_invert_kernel, tp=tp),
        out_shape=jax.ShapeDtypeStruct((2 * tp,), jnp.int32),
        grid_spec=pltpu.PrefetchScalarGridSpec(
            num_scalar_prefetch=1,
            grid=(1,),
            in_specs=[],
            out_specs=pl.BlockSpec(memory_space=pltpu.SMEM),
        ),
        compiler_params=_params(("arbitrary",)),
        name="moe_invert",
    )(pos_flat)


def _expert_kernel(vt_ref, ve_ref, vlo_ref, vhi_ref, nv_ref, vnext_ref, vstart_ref, vcount_ref, code_ref,
                   hn_ref, wg_hbm, wu_hbm, wd_hbm, y_ref,
                   xbuf, xcur, wgs, wus, wds, wgb, wub, wdb, gsem, wsem, *, n_tiles, layer):
    v = pl.program_id(0)
    nv = nv_ref[0]
    t = vt_ref[v]
    prev_v = jnp.maximum(v - 1, 0)
    first = jnp.logical_or(v == 0, vt_ref[prev_v] != t)
    new_expert = jnp.logical_or(v == 0, ve_ref[prev_v] != ve_ref[v])

    def weight_copies(e):
        w = layer * N_EXPERTS + e
        return (pltpu.make_async_copy(wg_hbm.at[w], wgs, wsem.at[0]),
                pltpu.make_async_copy(wu_hbm.at[w], wus, wsem.at[1]),
                pltpu.make_async_copy(wd_hbm.at[w], wds, wsem.at[2]))

    def row_copy(tile, r):
        tok = lax.shift_right_logical(code_ref[tile * TILE_M + r], 1)
        return pltpu.make_async_copy(hn_ref.at[pl.ds(tok, 1)], xbuf.at[pl.ds(r, 1)], gsem)

    def gather_wait():
        pltpu.make_async_copy(hn_ref.at[pl.ds(0, TILE_M)], xbuf, gsem).wait()

    def compute(accumulate):
        x = xcur[...]
        a = jnp.dot(x, wgb[...], preferred_element_type=F32)
        u = jnp.dot(x, wub[...], preferred_element_type=F32)
        row = lax.broadcasted_iota(jnp.int32, (TILE_M, 1), 0)
        mine = jnp.logical_and(row >= vlo_ref[v], row < vhi_ref[v])
        hid = jnp.where(mine, a * _sigmoid(a) * u, 0.0).astype(BF16)
        y = jnp.dot(hid, wdb[...], preferred_element_type=F32)
        if accumulate:
            y_ref[...] += y
        else:
            y_ref[...] = y

    @pl.when(v == 0)
    def _():
        def step(r, carry):
            row_copy(0, r).start()
            return carry
        lax.fori_loop(0, TILE_M, step, 0, unroll=8)
        for copy in weight_copies(ve_ref[0]):
            copy.start()

    @pl.when(v < nv)
    def _():
        @pl.when(new_expert)
        def _():
            for copy in weight_copies(ve_ref[v]):
                copy.wait()
            wgb[...] = wgs[...].astype(BF16)
            wub[...] = wus[...].astype(BF16)
            wdb[...] = wds[...].astype(BF16)

        @pl.when(first)
        def _():
            gather_wait()
            xcur[...] = xbuf[...].astype(BF16)
            ahead = jnp.minimum(t + 1, n_tiles - 1)
            for r in range(TILE_M):
                row_copy(ahead, r).start()
            compute(False)

        @pl.when(jnp.logical_not(first))
        def _():
            compute(True)

        k = v - vstart_ref[v]
        last_of_expert = k == vcount_ref[v] - 1
        has_next = vnext_ref[v] != ve_ref[v]
        for c, copy in enumerate(weight_copies(vnext_ref[v])):
            @pl.when(jnp.logical_and(has_next, jnp.logical_or(k == c, jnp.logical_and(last_of_expert, k < c))))
            def _():
                copy.start()

        @pl.when(v == nv - 1)
        def _():
            gather_wait()


def moe_experts(meta, code, hn, w_gate, w_up, w_down, layer):
    tp, d = hn.shape
    f = w_gate.shape[-1]
    assert (2 * tp) % TILE_M == 0
    n_tiles = (2 * tp) // TILE_M
    max_visits = n_tiles + N_EXPERTS - 1

    def y_map(v, vt, ve, vlo, vhi, nv, *_):
        return (vt[jnp.minimum(v, nv[0] - 1)], 0)

    any_spec = pl.BlockSpec(memory_space=pl.ANY)
    return pl.pallas_call(
        functools.partial(_expert_kernel, n_tiles=n_tiles, layer=layer),
        out_shape=jax.ShapeDtypeStruct((2 * tp, d), F32),
        grid_spec=pltpu.PrefetchScalarGridSpec(
            num_scalar_prefetch=9,
            grid=(max_visits,),
            in_specs=[any_spec, any_spec, any_spec, any_spec],
            out_specs=pl.BlockSpec((TILE_M, d), y_map),
            scratch_shapes=[
                pltpu.VMEM((TILE_M, d), F32), pltpu.VMEM((TILE_M, d), BF16),
                pltpu.VMEM((d, f), F32), pltpu.VMEM((d, f), F32), pltpu.VMEM((f, d), F32),
                pltpu.VMEM((d, f), BF16), pltpu.VMEM((d, f), BF16), pltpu.VMEM((f, d), BF16),
                pltpu.SemaphoreType.DMA, pltpu.SemaphoreType.DMA((3,)),
            ],
        ),
        compiler_params=_params(("arbitrary",), vmem=56 * 1024 * 1024),
        name="moe_experts",
    )(meta[0, :max_visits], meta[1, :max_visits], meta[2, :max_visits], meta[3, :max_visits], meta[4, :1],
      meta[5, :max_visits], meta[6, :max_visits], meta[7, :max_visits], code, hn, w_gate, w_up, w_down)


def _combine_kernel(pos_ref, h_ref, w_ref, g_ref, y_ref, *rest, tc, tp, n_steps, write_h):
    out_refs, (ybuf, sem) = rest[:-2], rest[-2:]
    i = pl.program_id(0)
    slot = lax.rem(i, 2)

    def row_copy(tile, buf, s, r):
        p = pos_ref[s * tp + tile * tc + r]
        return pltpu.make_async_copy(y_ref.at[pl.ds(p, 1)], ybuf.at[buf, pl.ds(s * tc + r, 1)], sem.at[buf])

    @pl.when(i == 0)
    def _():
        def step(r, carry):
            row_copy(0, 0, 0, r).start()
            row_copy(0, 0, 1, r).start()
            return carry
        lax.fori_loop(0, tc, step, 0, unroll=8)

    pltpu.make_async_copy(y_ref.at[pl.ds(0, 2 * tc)], ybuf.at[slot], sem.at[slot]).wait()

    @pl.when(i + 1 < n_steps)
    def _():
        for r in range(tc):
            row_copy(i + 1, 1 - slot, 0, r).start()
            row_copy(i + 1, 1 - slot, 1, r).start()

    w = w_ref[...]
    h = h_ref[...] + w[:, 0:1] * ybuf[slot, :tc, :] + w[:, 1:2] * ybuf[slot, tc:, :]
    if write_h:
        out_refs[0][...] = h
    out_refs[-1][...] = _rms(h, g_ref[...]).astype(out_refs[-1].dtype)


def moe_combine(pos_flat, h, w_col, y, gain, hn_dtype, final_shape=None):
    tp, d = h.shape
    if final_shape is None:
        tc = 256
        out_shape = (jax.ShapeDtypeStruct((tp, d), F32), jax.ShapeDtypeStruct((tp, d), hn_dtype))
        out_specs = (pl.BlockSpec((tc, d), lambda i, p: (i, 0)), pl.BlockSpec((tc, d), lambda i, p: (i, 0)))
    else:
        tc = CHUNK
        bsz, seq, _ = final_shape
        nc = tp // bsz // tc
        out_shape = (jax.ShapeDtypeStruct(final_shape, hn_dtype),)
        out_specs = (pl.BlockSpec((None, tc, d), lambda i, p: (i // nc, jnp.maximum(i % nc - 1, 0), 0)),)
    n_steps = tp // tc
    return pl.pallas_call(
        functools.partial(_combine_kernel, tc=tc, tp=tp, n_steps=n_steps, write_h=final_shape is None),
        out_shape=out_shape,
        grid_spec=pltpu.PrefetchScalarGridSpec(
            num_scalar_prefetch=1,
            grid=(n_steps,),
            in_specs=[
                pl.BlockSpec((tc, d), lambda i, p: (i, 0)),
                pl.BlockSpec((tc, 2), lambda i, p: (i, 0)),
                pl.BlockSpec((1, d), lambda i, p: (0, 0)),
                pl.BlockSpec(memory_space=pl.ANY),
            ],
            out_specs=out_specs,
            scratch_shapes=[pltpu.VMEM((2, 2 * tc, d), F32), pltpu.SemaphoreType.DMA((2,))],
        ),
        compiler_params=_params(("arbitrary",)),
        name="moe_combine",
    )(pos_flat, h, w_col, gain.reshape(1, d), y)


def hierarchical_moe(h, gain, w_rg, b_rg, w_re, b_re, w_gate, w_up, w_down, layer, next_gain, hn_dtype,
                     final_shape=None):
    tp, d = h.shape
    f = w_gate.shape[-1]
    hn, idx, w_rows, cnt = moe_router(h, gain, w_rg, b_rg, w_re, b_re)
    counts = cnt[EXPERT_ROW0:EXPERT_ROW0 + N_EXPERTS, 0].astype(jnp.int32)
    pos, meta = moe_slots(counts, idx)
    pos_flat = pos[:2].reshape(2 * tp)
    code = moe_invert(pos_flat)
    y = moe_experts(meta, code, hn, w_gate.reshape(-1, d, f), w_up.reshape(-1, d, f),
                    w_down.reshape(-1, f, d), layer)
    return moe_combine(pos_flat, h, w_rows[:2].T, y, next_gain, hn_dtype, final_shape)


def kernel(x, meta_tokens, norm_mixer, norm_ffn, norm_final, ret_w_in, ret_gn, ret_w_out,
           fox_w_in, fox_b_f, fox_w_out, moe_w_rg, moe_b_rg, moe_w_re, moe_b_re,
           moe_w_gate, moe_w_up, moe_w_down):
    bsz, seq, d = x.shape
    depth = norm_mixer.shape[0]
    nc = (seq + CHUNK) // CHUNK
    lp = nc * CHUNK
    h, hn = embed_norm(x, meta_tokens.astype(x.dtype), norm_mixer[0], BF16)
    for i in range(depth):
        j = i // 2
        if i % 2 == 0:
            proj = matmul(hn, ret_w_in[j], BF16)
            gated = retention_core(proj, ret_gn[j], bsz, nc)
            h = matmul(gated, ret_w_out[j], F32, residual=h, tn=512)
        else:
            qkv = matmul(hn, fox_w_in[j], BF16, n=3 * d, scale_cols=d, scale=(d // FOX_HEADS) ** -0.5 * LOG2E)
            pq, pk = forget_gates(hn, fox_w_in[j][:, 3 * d:], fox_b_f[j], bsz, lp)
            o = fox_attention(qkv, pq, pk, bsz, lp)
            h = matmul(o, fox_w_out[j], F32, residual=h)
        last = i == depth - 1
        outs = hierarchical_moe(h, norm_ffn[i], moe_w_rg[i], moe_b_rg[i], moe_w_re[i], moe_b_re[i],
                                moe_w_gate, moe_w_up, moe_w_down, i,
                                norm_final if last else norm_mixer[i + 1],
                                F32 if last else BF16, (bsz, seq, d) if last else None)
        if last:
            return outs[0]
        h, hn = outs
```

```python
import functools

import jax
import jax.numpy as jnp
from jax import lax
from jax.experimental import pallas as pl
from jax.experimental.pallas import tpu as pltpu

N_META = 16
CHUNK = 128
PAD = CHUNK - N_META
RMS_EPS = 1e-6
GN_EPS = 1e-6
NEG_INF = -1e30
RET_HEADS = 8
FOX_HEADS = 16
N_GROUPS = 4
EXPERTS_PER_GROUP = 8
N_EXPERTS = N_GROUPS * EXPERTS_PER_GROUP
ROPE_BASE = 10000.0

LANES = 128
ROUTER_ROWS = 48
EXPERT_ROW0 = N_GROUPS
TILE_M = 256
VMEM_LIMIT = 48 * 1024 * 1024

F32 = jnp.float32
BF16 = jnp.bfloat16


def _params(sem, vmem=VMEM_LIMIT):
    return pltpu.CompilerParams(dimension_semantics=sem, vmem_limit_bytes=vmem)


def _rms(h, g):
    return h * lax.rsqrt(jnp.mean(h * h, axis=-1, keepdims=True) + RMS_EPS) * g


def _sigmoid(x):
    return 1.0 / (1.0 + jnp.exp(-x))


def _pack_halves(x):
    half = x.shape[1] // 2
    return pltpu.pack_elementwise([x[:, :half], x[:, half:]], packed_dtype=BF16)


def _unpack_halves(p):
    lo = pltpu.unpack_elementwise(p, index=0, packed_dtype=BF16, unpacked_dtype=F32)
    hi = pltpu.unpack_elementwise(p, index=1, packed_dtype=BF16, unpacked_dtype=F32)
    return jnp.concatenate([lo, hi], axis=1)


def _embed_norm_kernel(x_ref, meta_ref, g_ref, h_ref, hn_ref):
    i = pl.program_id(1)

    @pl.when(i == 0)
    def _():
        h_ref[...] = jnp.zeros_like(h_ref)
        h_ref[PAD:, :] = meta_ref[...]

    @pl.when(i > 0)
    def _():
        h_ref[...] = x_ref[...]

    hn_ref[...] = _rms(h_ref[...], g_ref[...]).astype(hn_ref.dtype)


def embed_norm(x, meta, gain, hn_dtype):
    bsz, seq, d = x.shape
    nc = (seq + CHUNK) // CHUNK
    tp = bsz * nc * CHUNK
    return pl.pallas_call(
        _embed_norm_kernel,
        out_shape=(jax.ShapeDtypeStruct((tp, d), F32), jax.ShapeDtypeStruct((tp, d), hn_dtype)),
        grid=(bsz, nc),
        in_specs=[
            pl.BlockSpec((None, CHUNK, d), lambda b, i: (b, jnp.maximum(i - 1, 0), 0)),
            pl.BlockSpec((N_META, d), lambda b, i: (0, 0)),
            pl.BlockSpec((1, d), lambda b, i: (0, 0)),
        ],
        out_specs=(
            pl.BlockSpec((CHUNK, d), lambda b, i: (b * nc + i, 0)),
            pl.BlockSpec((CHUNK, d), lambda b, i: (b * nc + i, 0)),
        ),
        compiler_params=_params(("parallel", "parallel")),
        name="embed_norm",
    )(x, meta, gain.reshape(1, d))


def _mm_kernel(*refs, has_res, scale_tiles, scale):
    if has_res:
        x_ref, w_ref, r_ref, o_ref, wb_ref = refs
    else:
        x_ref, w_ref, o_ref, wb_ref = refs

    @pl.when(pl.program_id(1) == 0)
    def _():
        wb_ref[...] = w_ref[...].astype(BF16)

    acc = jnp.dot(x_ref[...], wb_ref[...], preferred_element_type=F32)
    if scale_tiles:
        acc = acc * jnp.where(pl.program_id(0) < scale_tiles, scale, 1.0)
    if has_res:
        acc = acc + r_ref[...]
    o_ref[...] = acc.astype(o_ref.dtype)


def matmul(x, w, out_dtype, n=None, residual=None, tm=512, tn=1024, scale_cols=0, scale=1.0):
    m, kdim = x.shape
    n = w.shape[1] if n is None else n
    in_specs = [
        pl.BlockSpec((tm, kdim), lambda j, i: (i, 0)),
        pl.BlockSpec((kdim, tn), lambda j, i: (0, j)),
    ]
    args = [x, w]
    if residual is not None:
        in_specs.append(pl.BlockSpec((tm, tn), lambda j, i: (i, j)))
        args.append(residual)
    return pl.pallas_call(
        functools.partial(_mm_kernel, has_res=residual is not None, scale_tiles=scale_cols // tn, scale=scale),
        out_shape=jax.ShapeDtypeStruct((m, n), out_dtype),
        grid=(n // tn, m // tm),
        in_specs=in_specs,
        out_specs=pl.BlockSpec((tm, tn), lambda j, i: (i, j)),
        scratch_shapes=[pltpu.VMEM((kdim, tn), BF16)],
        compiler_params=_params(("parallel", "arbitrary")),
        name="matmul",
    )(*args)


def _retention_kernel(q_ref, k_ref, v_ref, g_ref, cos_ref, sin_ref, dmat_ref, xi_ref, zeta_ref,
                      gch_ref, gn_ref, o_ref, state_ref, *, nh, dk, dv):
    c = pl.program_id(1)

    @pl.when(c == 0)
    def _():
        state_ref[...] = jnp.zeros_like(state_ref)

    cos = cos_ref[...]
    sin = sin_ref[...]
    half = dk // 2

    def rot(u):
        u1, u2 = u[:, :half], u[:, half:]
        return jnp.concatenate([u1 * cos - u2 * sin, u1 * sin + u2 * cos], axis=1)

    row = lax.broadcasted_iota(jnp.int32, (CHUNK, 1), 0)
    valid = jnp.logical_or(row >= PAD, c > 0)
    for h in range(nh):
        q = rot(q_ref[:, h * dk:(h + 1) * dk].astype(F32))
        k = rot(k_ref[:, h * dk:(h + 1) * dk].astype(F32)) * (dk ** -0.5)
        k = jnp.where(valid, k, 0.0)
        v = v_ref[:, h * dv:(h + 1) * dv]
        vb = jnp.where(valid, v, jnp.zeros_like(v))
        qb = q.astype(BF16)
        kb = k.astype(BF16)

        scores = lax.dot_general(qb, kb, (((1,), (1,)), ((), ())), preferred_element_type=F32)
        scores = scores * dmat_ref[h]
        inner = jnp.dot(scores.astype(BF16), vb, preferred_element_type=F32)
        state = state_ref[h]
        cross = jnp.dot(qb, state.astype(BF16), preferred_element_type=F32) * xi_ref[h]
        y = inner + cross
        kz = (k * zeta_ref[h]).astype(BF16)
        state_ref[h] = gch_ref[h] * state + lax.dot_general(
            kz, vb, (((0,), (0,)), ((), ())), preferred_element_type=F32)

        mu = jnp.mean(y, axis=-1, keepdims=True)
        dlt = y - mu
        var = jnp.mean(dlt * dlt, axis=-1, keepdims=True)
        yn = dlt * lax.rsqrt(var + GN_EPS) * gn_ref[h]
        g = g_ref[:, h * dv:(h + 1) * dv].astype(F32)
        o_ref[:, h * dv:(h + 1) * dv] = (g * _sigmoid(g) * yn).astype(o_ref.dtype)


def retention_core(proj, gn_gain, bsz, nc):
    tp = proj.shape[0]
    d = proj.shape[1] // 6
    nh = RET_HEADS
    dk = d // nh
    dv = 2 * d // nh
    lp = nc * CHUNK
    half = dk // 2
    pos = (jnp.arange(lp) - PAD).astype(F32)
    inv = ROPE_BASE ** (-jnp.arange(half, dtype=F32) / half)
    ang = pos[:, None] * inv[None, :]
    cos, sin = jnp.cos(ang), jnp.sin(ang)
    log_g = jnp.log1p(-jnp.exp2(-5.0 - jnp.arange(nh, dtype=F32)))
    idx = jnp.arange(CHUNK, dtype=F32)
    diff = idx[:, None] - idx[None, :]
    dmat = jnp.where(diff[None] >= 0, jnp.exp(jnp.maximum(diff, 0.0)[None] * log_g[:, None, None]), 0.0)
    xi = jnp.exp((idx + 1.0)[None, :] * log_g[:, None])[:, :, None]
    zeta = jnp.exp((CHUNK - 1.0 - idx)[None, :] * log_g[:, None])[:, :, None]
    g_chunk = jnp.exp(CHUNK * log_g)[:, None, None]
    return pl.pallas_call(
        functools.partial(_retention_kernel, nh=nh, dk=dk, dv=dv),
        out_shape=jax.ShapeDtypeStruct((tp, 2 * d), BF16),
        grid=(bsz, nc),
        in_specs=[
            pl.BlockSpec((CHUNK, d), lambda b, c: (b * nc + c, 0)),
            pl.BlockSpec((CHUNK, d), lambda b, c: (b * nc + c, 1)),
            pl.BlockSpec((CHUNK, 2 * d), lambda b, c: (b * nc + c, 1)),
            pl.BlockSpec((CHUNK, 2 * d), lambda b, c: (b * nc + c, 2)),
            pl.BlockSpec((CHUNK, half), lambda b, c: (c, 0)),
            pl.BlockSpec((CHUNK, half), lambda b, c: (c, 0)),
            pl.BlockSpec((nh, CHUNK, CHUNK), lambda b, c: (0, 0, 0)),
            pl.BlockSpec((nh, CHUNK, 1), lambda b, c: (0, 0, 0)),
            pl.BlockSpec((nh, CHUNK, 1), lambda b, c: (0, 0, 0)),
            pl.BlockSpec((nh, 1, 1), lambda b, c: (0, 0, 0)),
            pl.BlockSpec((nh, 1, dv), lambda b, c: (0, 0, 0)),
        ],
        out_specs=pl.BlockSpec((CHUNK, 2 * d), lambda b, c: (b * nc + c, 0)),
        scratch_shapes=[pltpu.VMEM((nh, dk, dv), F32)],
        compiler_params=_params(("parallel", "arbitrary")),
        name="retention_core",
    )(proj, proj, proj, proj, cos, sin, dmat, xi, zeta, g_chunk, gn_gain.reshape(nh, 1, dv))


LOG2E = 1.4426950408889634
MASK_BIG = 1e30
ONES_LANE = LANES - 1


def _bias_selectors(nh):
    h = jnp.arange(nh)
    selq = jnp.zeros((nh, 3 * LANES, LANES), F32)
    selk = jnp.zeros((nh, 3 * LANES, LANES), F32)
    for part in range(3):
        selq = selq.at[h, part * LANES + h, part].set(1.0)
        selk = selk.at[h, part * LANES + h, 3 + part].set(-1.0)
        selq = selq.at[h, ONES_LANE, 3 + part].set(1.0)
        selk = selk.at[h, ONES_LANE, part].set(1.0)
    return selq.astype(BF16), selk.astype(BF16)


def _forget_kernel(hn_ref, wf_ref, bf_ref, pq_ref, pk_ref, carry_ref, *, rows):
    i = pl.program_id(1)

    @pl.when(i == 0)
    def _():
        carry_ref[...] = jnp.zeros_like(carry_ref)

    z = jnp.dot(hn_ref[...], wf_ref[...], preferred_element_type=F32) + bf_ref[...]
    lf = jnp.minimum(z, 0.0) - jnp.log1p(jnp.exp(-jnp.abs(z)))
    row = lax.broadcasted_iota(jnp.int32, (rows, 1), 0)
    valid = i * rows + row >= PAD
    lf = jnp.where(valid, lf, 0.0)

    def split3(a):
        hi = a.astype(BF16)
        r1 = a - hi.astype(F32)
        mid = r1.astype(BF16)
        lo = (r1 - mid.astype(F32)).astype(BF16)
        return hi, mid, lo

    r_i = lax.broadcasted_iota(jnp.int32, (rows, rows), 0)
    c_i = lax.broadcasted_iota(jnp.int32, (rows, rows), 1)
    tri = (r_i >= c_i).astype(BF16)
    cs = sum(jnp.dot(tri, part, preferred_element_type=F32) for part in split3(lf))
    cs = cs + carry_ref[...]
    carry_ref[...] = cs[rows - 1:rows, :]

    hi, mid, lo = (p.astype(F32) for p in split3(cs * LOG2E))
    lane = lax.broadcasted_iota(jnp.int32, (rows, LANES), 1)
    ones = lane == ONES_LANE
    live = jnp.logical_and(valid, jnp.logical_not(ones))
    pq_ref[...] = jnp.concatenate(
        [jnp.where(ones, 1.0, hi), jnp.where(ones, 0.0, mid), jnp.where(ones, 0.0, lo)], axis=1).astype(BF16)
    pk_ref[...] = jnp.concatenate(
        [jnp.where(ones, 1.0, jnp.where(valid, hi, MASK_BIG)), jnp.where(live, mid, 0.0),
         jnp.where(live, lo, 0.0)], axis=1).astype(BF16)


def forget_gates(hn, w_f, b_f, bsz, lp, rows=384):
    tp, d = hn.shape
    nh = w_f.shape[1]
    steps = lp // rows
    wf = jnp.zeros((d, LANES), BF16).at[:, :nh].set(w_f.astype(BF16))
    bf = jnp.zeros((1, LANES), F32).at[0, :nh].set(b_f.astype(F32))
    return pl.pallas_call(
        functools.partial(_forget_kernel, rows=rows),
        out_shape=(jax.ShapeDtypeStruct((tp, 3 * LANES), BF16), jax.ShapeDtypeStruct((tp, 3 * LANES), BF16)),
        grid=(bsz, steps),
        in_specs=[
            pl.BlockSpec((rows, d), lambda b, i: (b * steps + i, 0)),
            pl.BlockSpec((d, LANES), lambda b, i: (0, 0)),
            pl.BlockSpec((1, LANES), lambda b, i: (0, 0)),
        ],
        out_specs=(
            pl.BlockSpec((rows, 3 * LANES), lambda b, i: (b * steps + i, 0)),
            pl.BlockSpec((rows, 3 * LANES), lambda b, i: (b * steps + i, 0)),
        ),
        scratch_shapes=[pltpu.VMEM((1, LANES), F32)],
        compiler_params=_params(("parallel", "arbitrary")),
        name="forget_gates",
    )(hn, wf, bf)


FOX_HEADS_PER_STEP = 4


FOX_STRIP = 32


def _fox_kernel(q_ref, pq_ref, k_ref, pk_ref, v_ref, selq_ref, selk_ref, o_ref,
                m_ref, l_ref, acc_ref, s_ref, p_ref, a_ref, ck_ref, *, tq, dh):
    i = pl.program_id(2)
    hs = FOX_HEADS_PER_STEP
    reps = tq // LANES
    m_ref[...] = jnp.full_like(m_ref, 10.0 * NEG_INF)
    l_ref[...] = jnp.zeros_like(l_ref)
    acc_ref[...] = jnp.zeros_like(acc_ref)

    @pl.when(i == 0)
    def _():
        for j in range(hs):
            ck_ref[:, j * LANES:(j + 1) * LANES] = jnp.dot(
                pk_ref[...], selk_ref[j], preferred_element_type=F32).astype(BF16)

    qa = [jnp.concatenate([q_ref[:, j * dh:(j + 1) * dh],
                           jnp.dot(pq_ref[...], selq_ref[j], preferred_element_type=F32).astype(BF16)], axis=1)
          for j in range(hs)]

    def scores(kb, buf):
        start = pl.multiple_of(kb * tq, tq)
        for j in range(hs):
            ka = jnp.concatenate([k_ref[pl.ds(start, tq), j * dh:(j + 1) * dh],
                                  ck_ref[pl.ds(start, tq), j * LANES:(j + 1) * LANES]], axis=1)
            s_ref[buf, j] = lax.dot_general(qa[j], ka, (((1,), (1,)), ((), ())), preferred_element_type=F32)

    def absorb(kb, buf, causal):
        start = pl.multiple_of(kb * tq, tq)
        for j in range(hs):
            for r0 in range(0, tq, FOX_STRIP):
                rows = slice(r0, r0 + FOX_STRIP)
                s = s_ref[buf, j, rows, :]
                if causal:
                    r_i = r0 + lax.broadcasted_iota(jnp.int32, (FOX_STRIP, tq), 0)
                    c_i = lax.broadcasted_iota(jnp.int32, (FOX_STRIP, tq), 1)
                    s = jnp.where(c_i <= r_i, s, NEG_INF)
                m_old = m_ref[j, rows, :]
                m_new = jnp.maximum(m_old, jnp.max(s, axis=1, keepdims=True))
                alpha = jnp.exp2(m_old - m_new)
                p = jnp.exp2(s - jnp.concatenate([m_new] * reps, axis=1))
                l_ref[j, rows, :] = alpha * l_ref[j, rows, :] + jnp.sum(p, axis=1, keepdims=True)
                m_ref[j, rows, :] = m_new
                a_ref[j, rows, :] = alpha
                p_ref[j, rows, :] = p.astype(BF16)
        for j in range(hs):
            acc_ref[j] = a_ref[j] * acc_ref[j] + jnp.dot(
                p_ref[j], v_ref[pl.ds(start, tq), j * dh:(j + 1) * dh], preferred_element_type=F32)

    def body(kb, carry):
        scores(kb, 0)
        absorb(kb, 0, False)
        return carry

    lax.fori_loop(0, i, body, 0)
    scores(i, 0)
    absorb(i, 0, True)

    for j in range(hs):
        o_ref[:, j * dh:(j + 1) * dh] = (acc_ref[j] / l_ref[j]).astype(o_ref.dtype)


def fox_attention(qkv, pq, pk, bsz, lp, tq=384):
    tp = qkv.shape[0]
    d = qkv.shape[1] // 3
    nh = FOX_HEADS
    dh = d // nh
    assert dh == LANES
    nq = lp // tq
    hs = FOX_HEADS_PER_STEP
    ng = nh // hs
    qkv3 = qkv.reshape(bsz, lp, 3 * d)
    pk3 = pk.reshape(bsz, lp, 3 * LANES)
    selq, selk = _bias_selectors(nh)
    return pl.pallas_call(
        functools.partial(_fox_kernel, tq=tq, dh=dh),
        out_shape=jax.ShapeDtypeStruct((tp, d), BF16),
        grid=(bsz, ng, nq),
        in_specs=[
            pl.BlockSpec((tq, hs * dh), lambda b, g, i: (b * nq + i, g)),
            pl.BlockSpec((tq, 3 * LANES), lambda b, g, i: (b * nq + i, 0)),
            pl.BlockSpec((None, lp, hs * dh), lambda b, g, i: (b, 0, ng + g)),
            pl.BlockSpec((None, lp, 3 * LANES), lambda b, g, i: (b, 0, 0)),
            pl.BlockSpec((None, lp, hs * dh), lambda b, g, i: (b, 0, 2 * ng + g)),
            pl.BlockSpec((hs, 3 * LANES, LANES), lambda b, g, i: (g, 0, 0)),
            pl.BlockSpec((hs, 3 * LANES, LANES), lambda b, g, i: (g, 0, 0)),
        ],
        out_specs=pl.BlockSpec((tq, hs * dh), lambda b, g, i: (b * nq + i, g)),
        scratch_shapes=[pltpu.VMEM((hs, tq, LANES), F32), pltpu.VMEM((hs, tq, LANES), F32),
                        pltpu.VMEM((hs, tq, dh), F32), pltpu.VMEM((1, hs, tq, tq), F32),
                        pltpu.VMEM((hs, tq, tq), BF16), pltpu.VMEM((hs, tq, LANES), F32),
                        pltpu.VMEM((lp, hs * LANES), BF16)],
        compiler_params=_params(("parallel", "parallel", "arbitrary")),
        name="fox_attention",
    )(qkv, pq, qkv3, pk3, qkv3, selq, selk)


def _router_kernel(h_ref, g_ref, wr_ref, br_ref, hn_ref, idx_ref, w_ref, cnt_ref, carry_ref, *, tr):
    i = pl.program_id(0)

    @pl.when(i == 0)
    def _():
        carry_ref[...] = jnp.zeros_like(carry_ref)

    hn = _rms(h_ref[...], g_ref[...])
    hn_ref[...] = _pack_halves(hn)
    lg = lax.dot_general(wr_ref[...], hn.astype(BF16), (((1,), (1,)), ((), ())),
                         preferred_element_type=F32) + br_ref[...]
    row = lax.broadcasted_iota(jnp.int32, (ROUTER_ROWS, tr), 0)
    big = jnp.int32(1 << 20)
    is_g = row < N_GROUPS
    mg = jnp.max(jnp.where(is_g, lg, -jnp.inf), axis=0, keepdims=True)
    g_sel = jnp.min(jnp.where(jnp.logical_and(is_g, lg == mg), row, big), axis=0, keepdims=True)
    sg = jnp.sum(jnp.where(is_g, jnp.exp(lg - mg), 0.0), axis=0, keepdims=True)
    p_g = 1.0 / sg
    lo = EXPERT_ROW0 + EXPERTS_PER_GROUP * g_sel
    is_e = jnp.logical_and(row >= lo, row < lo + EXPERTS_PER_GROUP)
    me = jnp.max(jnp.where(is_e, lg, -jnp.inf), axis=0, keepdims=True)
    ee = jnp.where(is_e, jnp.exp(lg - me), 0.0)
    pe = ee / jnp.sum(ee, axis=0, keepdims=True)
    pe1 = jnp.where(is_e, pe, -1.0)
    m1 = jnp.max(pe1, axis=0, keepdims=True)
    i1 = jnp.min(jnp.where(pe1 == m1, row, big), axis=0, keepdims=True)
    pe2 = jnp.where(row == i1, -1.0, pe1)
    m2 = jnp.max(pe2, axis=0, keepdims=True)
    i2 = jnp.min(jnp.where(pe2 == m2, row, big), axis=0, keepdims=True)
    den = m1 + m2
    w1 = p_g * (m1 / den)
    w2 = p_g * (m2 / den)

    oh1 = row == i1
    oh2 = row == i2
    ohs = jnp.logical_or(oh1, oh2).astype(F32)
    r_i = lax.broadcasted_iota(jnp.int32, (tr, tr), 0)
    c_i = lax.broadcasted_iota(jnp.int32, (tr, tr), 1)
    tri = (r_i < c_i).astype(BF16)
    cnt = jnp.dot(ohs.astype(BF16), tri, preferred_element_type=F32) + carry_ref[...]
    rank1 = jnp.sum(jnp.where(oh1, cnt, 0.0), axis=0, keepdims=True)
    rank2 = jnp.sum(jnp.where(oh2, cnt, 0.0), axis=0, keepdims=True)
    carry_ref[...] += jnp.sum(ohs, axis=1, keepdims=True)

    r8 = lax.broadcasted_iota(jnp.int32, (8, tr), 0)
    e1 = i1 - EXPERT_ROW0
    e2 = i2 - EXPERT_ROW0
    idx_ref[...] = jnp.where(r8 == 0, e1, jnp.where(r8 == 1, e2, jnp.where(
        r8 == 2, rank1.astype(jnp.int32), jnp.where(r8 == 3, rank2.astype(jnp.int32), 0))))
    w_ref[...] = jnp.where(r8 == 0, w1, jnp.where(r8 == 1, w2, 0.0))
    cnt_ref[...] = jnp.broadcast_to(carry_ref[...], cnt_ref.shape)


def moe_router(h, gain, w_rg, b_rg, w_re, b_re, tr=512):
    tp, d = h.shape
    wr = jnp.zeros((ROUTER_ROWS, d), BF16)
    wr = wr.at[:N_GROUPS].set(w_rg.T.astype(BF16))
    wr = wr.at[EXPERT_ROW0:EXPERT_ROW0 + N_EXPERTS].set(w_re.reshape(d, N_EXPERTS).T.astype(BF16))
    br = jnp.zeros((ROUTER_ROWS, 1), F32)
    br = br.at[:N_GROUPS, 0].set(b_rg.astype(F32))
    br = br.at[EXPERT_ROW0:EXPERT_ROW0 + N_EXPERTS, 0].set(b_re.reshape(N_EXPERTS).astype(F32))
    return pl.pallas_call(
        functools.partial(_router_kernel, tr=tr),
        out_shape=(
            jax.ShapeDtypeStruct((tp, d // 2), jnp.uint32),
            jax.ShapeDtypeStruct((8, tp), jnp.int32),
            jax.ShapeDtypeStruct((8, tp), F32),
            jax.ShapeDtypeStruct((ROUTER_ROWS, LANES), F32),
        ),
        grid=(tp // tr,),
        in_specs=[
            pl.BlockSpec((tr, d), lambda i: (i, 0)),
            pl.BlockSpec((1, d), lambda i: (0, 0)),
            pl.BlockSpec((ROUTER_ROWS, d), lambda i: (0, 0)),
            pl.BlockSpec((ROUTER_ROWS, 1), lambda i: (0, 0)),
        ],
        out_specs=(
            pl.BlockSpec((tr, d // 2), lambda i: (i, 0)),
            pl.BlockSpec((8, tr), lambda i: (0, i)),
            pl.BlockSpec((8, tr), lambda i: (0, i)),
            pl.BlockSpec((ROUTER_ROWS, LANES), lambda i: (0, 0)),
        ),
        scratch_shapes=[pltpu.VMEM((ROUTER_ROWS, 1), F32)],
        compiler_params=_params(("arbitrary",)),
        name="moe_router",
    )(h, gain.reshape(1, d), wr, br)


def _slots_kernel(cnt_ref, idx_ref, pos_ref, meta_ref):
    e1 = idx_ref[0:1, :]
    e2 = idx_ref[1:2, :]
    off1 = jnp.zeros_like(e1)
    off2 = jnp.zeros_like(e2)
    visit = lax.broadcasted_iota(jnp.int32, (1, meta_ref.shape[1]), 1)
    v_tile = jnp.zeros_like(visit)
    v_expert = jnp.zeros_like(visit)
    v_lo = jnp.zeros_like(visit)
    v_hi = jnp.zeros_like(visit)
    v_next = jnp.zeros_like(visit)
    v_start = jnp.zeros_like(visit)
    v_count = jnp.zeros_like(visit)
    next_live = [None] * N_EXPERTS
    nxt = jnp.int32(-1)
    for e in reversed(range(N_EXPERTS)):
        next_live[e] = jnp.where(nxt >= 0, nxt, e)
        nxt = jnp.where(cnt_ref[e] > 0, e, nxt)
    start = jnp.int32(0)
    v_base = jnp.int32(0)
    for e in range(N_EXPERTS):
        off1 = jnp.where(e1 == e, start, off1)
        off2 = jnp.where(e2 == e, start, off2)
        n = cnt_ref[e]
        end = start + n
        first_tile = start // TILE_M
        n_visits = jnp.where(n > 0, (jnp.maximum(end, 1) - 1) // TILE_M - first_tile + 1, 0)
        mine = jnp.logical_and(visit >= v_base, visit < v_base + n_visits)
        row0 = (first_tile + visit - v_base) * TILE_M
        v_tile = jnp.where(mine, first_tile + visit - v_base, v_tile)
        v_expert = jnp.where(mine, e, v_expert)
        v_next = jnp.where(mine, next_live[e], v_next)
        v_start = jnp.where(mine, v_base, v_start)
        v_count = jnp.where(mine, n_visits, v_count)
        v_lo = jnp.where(mine, jnp.maximum(start - row0, 0), v_lo)
        v_hi = jnp.where(mine, jnp.minimum(end - row0, TILE_M), v_hi)
        start = end
        v_base = v_base + n_visits
    r8 = lax.broadcasted_iota(jnp.int32, pos_ref.shape, 0)
    pos_ref[...] = jnp.where(r8 == 0, off1 + idx_ref[2:3, :], jnp.where(r8 == 1, off2 + idx_ref[3:4, :], 0))
    m8 = lax.broadcasted_iota(jnp.int32, meta_ref.shape, 0)
    rows = (v_tile, v_expert, v_lo, v_hi, v_base, v_next, v_start, v_count)
    meta = rows[-1]
    for r in reversed(range(len(rows) - 1)):
        meta = jnp.where(m8 == r, rows[r], meta)
    meta_ref[...] = meta


def moe_slots(counts, idx, tr=512):
    tp = idx.shape[1]
    nt_lanes = 2 * LANES
    return pl.pallas_call(
        _slots_kernel,
        out_shape=(jax.ShapeDtypeStruct((8, tp), jnp.int32), jax.ShapeDtypeStruct((8, nt_lanes), jnp.int32)),
        grid_spec=pltpu.PrefetchScalarGridSpec(
            num_scalar_prefetch=1,
            grid=(tp // tr,),
            in_specs=[pl.BlockSpec((8, tr), lambda i, c: (0, i))],
            out_specs=(
                pl.BlockSpec((8, tr), lambda i, c: (0, i)),
                pl.BlockSpec((8, nt_lanes), lambda i, c: (0, 0)),
            ),
        ),
        compiler_params=_params(("arbitrary",)),
        name="moe_slots",
    )(counts, idx)


def _invert_kernel(pos_ref, code_ref, *, tp):
    def per_token(t, carry):
        code_ref[pos_ref[t]] = 2 * t
        code_ref[pos_ref[tp + t]] = 2 * t + 1
        return carry

    lax.fori_loop(0, tp, per_token, 0, unroll=8)


def moe_invert(pos_flat):
    tp = pos_flat.shape[0] // 2
    return pl.pallas_call(
        functools.partial(_invert_kernel, tp=tp),
        out_shape=jax.ShapeDtypeStruct((2 * tp,), jnp.int32),
        grid_spec=pltpu.PrefetchScalarGridSpec(
            num_scalar_prefetch=1,
            grid=(1,),
            in_specs=[],
            out_specs=pl.BlockSpec(memory_space=pltpu.SMEM),
        ),
        compiler_params=_params(("arbitrary",)),
        name="moe_invert",
    )(pos_flat)


def _expert_kernel(vt_ref, ve_ref, vlo_ref, vhi_ref, nv_ref, vnext_ref, vstart_ref, vcount_ref, code_ref,
                   hn_ref, wg_hbm, wu_hbm, wd_hbm, y_ref,
                   xbuf, xcur, yacc, wgs, wus, wds, wgb, wub, wdb, gsem, wsem, *, n_tiles, layer):
    v = pl.program_id(0)
    nv = nv_ref[0]
    t = vt_ref[v]
    prev_v = jnp.maximum(v - 1, 0)
    first = jnp.logical_or(v == 0, vt_ref[prev_v] != t)
    new_expert = jnp.logical_or(v == 0, ve_ref[prev_v] != ve_ref[v])

    def weight_copies(e):
        w = layer * N_EXPERTS + e
        return (pltpu.make_async_copy(wg_hbm.at[w], wgs, wsem.at[0]),
                pltpu.make_async_copy(wu_hbm.at[w], wus, wsem.at[1]),
                pltpu.make_async_copy(wd_hbm.at[w], wds, wsem.at[2]))

    def row_copy(tile, r):
        tok = lax.shift_right_logical(code_ref[tile * TILE_M + r], 1)
        return pltpu.make_async_copy(hn_ref.at[pl.ds(tok, 1)], xbuf.at[pl.ds(r, 1)], gsem)

    def gather_wait():
        pltpu.make_async_copy(hn_ref.at[pl.ds(0, TILE_M)], xbuf, gsem).wait()

    def compute(accumulate):
        x = xcur[...]
        a = jnp.dot(x, wgb[...], preferred_element_type=F32)
        u = jnp.dot(x, wub[...], preferred_element_type=F32)
        row = lax.broadcasted_iota(jnp.int32, (TILE_M, 1), 0)
        mine = jnp.logical_and(row >= vlo_ref[v], row < vhi_ref[v])
        hid = jnp.where(mine, a * _sigmoid(a) * u, 0.0).astype(BF16)
        y = jnp.dot(hid, wdb[...], preferred_element_type=F32)
        if accumulate:
            y = y + yacc[...]
        yacc[...] = y
        y_ref[...] = _pack_halves(y)

    @pl.when(v == 0)
    def _():
        def step(r, carry):
            row_copy(0, r).start()
            return carry
        lax.fori_loop(0, TILE_M, step, 0, unroll=8)
        for copy in weight_copies(ve_ref[0]):
            copy.start()

    @pl.when(v < nv)
    def _():
        @pl.when(new_expert)
        def _():
            for copy in weight_copies(ve_ref[v]):
                copy.wait()
            wgb[...] = wgs[...].astype(BF16)
            wub[...] = wus[...].astype(BF16)
            wdb[...] = wds[...].astype(BF16)

        @pl.when(first)
        def _():
            gather_wait()
            xcur[...] = _unpack_halves(xbuf[...]).astype(BF16)
            ahead = jnp.minimum(t + 1, n_tiles - 1)
            for r in range(TILE_M):
                row_copy(ahead, r).start()
            compute(False)

        @pl.when(jnp.logical_not(first))
        def _():
            compute(True)

        k = v - vstart_ref[v]
        last_of_expert = k == vcount_ref[v] - 1
        has_next = vnext_ref[v] != ve_ref[v]
        for c, copy in enumerate(weight_copies(vnext_ref[v])):
            @pl.when(jnp.logical_and(has_next, jnp.logical_or(k == c, jnp.logical_and(last_of_expert, k < c))))
            def _():
                copy.start()

        @pl.when(v == nv - 1)
        def _():
            gather_wait()


def moe_experts(meta, code, hn, w_gate, w_up, w_down, layer):
    tp = hn.shape[0]
    d = 2 * hn.shape[1]
    f = w_gate.shape[-1]
    assert (2 * tp) % TILE_M == 0
    n_tiles = (2 * tp) // TILE_M
    max_visits = n_tiles + N_EXPERTS - 1

    def y_map(v, vt, ve, vlo, vhi, nv, *_):
        return (vt[jnp.minimum(v, nv[0] - 1)], 0)

    any_spec = pl.BlockSpec(memory_space=pl.ANY)
    return pl.pallas_call(
        functools.partial(_expert_kernel, n_tiles=n_tiles, layer=layer),
        out_shape=jax.ShapeDtypeStruct((2 * tp, d // 2), jnp.uint32),
        grid_spec=pltpu.PrefetchScalarGridSpec(
            num_scalar_prefetch=9,
            grid=(max_visits,),
            in_specs=[any_spec, any_spec, any_spec, any_spec],
            out_specs=pl.BlockSpec((TILE_M, d // 2), y_map),
            scratch_shapes=[
                pltpu.VMEM((TILE_M, d // 2), jnp.uint32), pltpu.VMEM((TILE_M, d), BF16),
                pltpu.VMEM((TILE_M, d), F32),
                pltpu.VMEM((d, f), F32), pltpu.VMEM((d, f), F32), pltpu.VMEM((f, d), F32),
                pltpu.VMEM((d, f), BF16), pltpu.VMEM((d, f), BF16), pltpu.VMEM((f, d), BF16),
                pltpu.SemaphoreType.DMA, pltpu.SemaphoreType.DMA((3,)),
            ],
        ),
        compiler_params=_params(("arbitrary",), vmem=56 * 1024 * 1024),
        name="moe_experts",
    )(meta[0, :max_visits], meta[1, :max_visits], meta[2, :max_visits], meta[3, :max_visits], meta[4, :1],
      meta[5, :max_visits], meta[6, :max_visits], meta[7, :max_visits], code, hn, w_gate, w_up, w_down)


def _combine_kernel(pos_ref, h_ref, w_ref, g_ref, y_ref, *rest, tc, tp, n_steps, write_h):
    out_refs, (ybuf, sem) = rest[:-2], rest[-2:]
    i = pl.program_id(0)
    slot = lax.rem(i, 2)

    def row_copy(tile, buf, s, r):
        p = pos_ref[s * tp + tile * tc + r]
        return pltpu.make_async_copy(y_ref.at[pl.ds(p, 1)], ybuf.at[buf, pl.ds(s * tc + r, 1)], sem.at[buf])

    @pl.when(i == 0)
    def _():
        def step(r, carry):
            row_copy(0, 0, 0, r).start()
            row_copy(0, 0, 1, r).start()
            return carry
        lax.fori_loop(0, tc, step, 0, unroll=8)

    pltpu.make_async_copy(y_ref.at[pl.ds(0, 2 * tc)], ybuf.at[slot], sem.at[slot]).wait()

    @pl.when(i + 1 < n_steps)
    def _():
        for r in range(tc):
            row_copy(i + 1, 1 - slot, 0, r).start()
            row_copy(i + 1, 1 - slot, 1, r).start()

    w = w_ref[...]
    h = (h_ref[...] + w[:, 0:1] * _unpack_halves(ybuf[slot, :tc, :])
         + w[:, 1:2] * _unpack_halves(ybuf[slot, tc:, :]))
    if write_h:
        out_refs[0][...] = h
    out_refs[-1][...] = _rms(h, g_ref[...]).astype(out_refs[-1].dtype)


def moe_combine(pos_flat, h, w_col, y, gain, hn_dtype, final_shape=None):
    tp, d = h.shape
    if final_shape is None:
        tc = 256
        out_shape = (jax.ShapeDtypeStruct((tp, d), F32), jax.ShapeDtypeStruct((tp, d), hn_dtype))
        out_specs = (pl.BlockSpec((tc, d), lambda i, p: (i, 0)), pl.BlockSpec((tc, d), lambda i, p: (i, 0)))
    else:
        tc = CHUNK
        bsz, seq, _ = final_shape
        nc = tp // bsz // tc
        out_shape = (jax.ShapeDtypeStruct(final_shape, hn_dtype),)
        out_specs = (pl.BlockSpec((None, tc, d), lambda i, p: (i // nc, jnp.maximum(i % nc - 1, 0), 0)),)
    n_steps = tp // tc
    return pl.pallas_call(
        functools.partial(_combine_kernel, tc=tc, tp=tp, n_steps=n_steps, write_h=final_shape is None),
        out_shape=out_shape,
        grid_spec=pltpu.PrefetchScalarGridSpec(
            num_scalar_prefetch=1,
            grid=(n_steps,),
            in_specs=[
                pl.BlockSpec((tc, d), lambda i, p: (i, 0)),
                pl.BlockSpec((tc, 2), lambda i, p: (i, 0)),
                pl.BlockSpec((1, d), lambda i, p: (0, 0)),
                pl.BlockSpec(memory_space=pl.ANY),
            ],
            out_specs=out_specs,
            scratch_shapes=[pltpu.VMEM((2, 2 * tc, d // 2), jnp.uint32), pltpu.SemaphoreType.DMA((2,))],
        ),
        compiler_params=_params(("arbitrary",)),
        name="moe_combine",
    )(pos_flat, h, w_col, gain.reshape(1, d), y)


def hierarchical_moe(h, gain, w_rg, b_rg, w_re, b_re, w_gate, w_up, w_down, layer, next_gain, hn_dtype,
                     final_shape=None):
    tp, d = h.shape
    f = w_gate.shape[-1]
    hn, idx, w_rows, cnt = moe_router(h, gain, w_rg, b_rg, w_re, b_re)
    counts = cnt[EXPERT_ROW0:EXPERT_ROW0 + N_EXPERTS, 0].astype(jnp.int32)
    pos, meta = moe_slots(counts, idx)
    pos_flat = pos[:2].reshape(2 * tp)
    code = moe_invert(pos_flat)
    y = moe_experts(meta, code, hn, w_gate.reshape(-1, d, f), w_up.reshape(-1, d, f),
                    w_down.reshape(-1, f, d), layer)
    return moe_combine(pos_flat, h, w_rows[:2].T, y, next_gain, hn_dtype, final_shape)


def kernel(x, meta_tokens, norm_mixer, norm_ffn, norm_final, ret_w_in, ret_gn, ret_w_out,
           fox_w_in, fox_b_f, fox_w_out, moe_w_rg, moe_b_rg, moe_w_re, moe_b_re,
           moe_w_gate, moe_w_up, moe_w_down):
    bsz, seq, d = x.shape
    depth = norm_mixer.shape[0]
    nc = (seq + CHUNK) // CHUNK
    lp = nc * CHUNK
    h, hn = embed_norm(x, meta_tokens.astype(x.dtype), norm_mixer[0], BF16)
    for i in range(depth):
        j = i // 2
        if i % 2 == 0:
            proj = matmul(hn, ret_w_in[j], BF16, tm=1536)
            gated = retention_core(proj, ret_gn[j], bsz, nc)
            h = matmul(gated, ret_w_out[j], F32, residual=h, tn=512)
        else:
            qkv = matmul(hn, fox_w_in[j], BF16, n=3 * d, tm=1536, scale_cols=d,
                         scale=(d // FOX_HEADS) ** -0.5 * LOG2E)
            pq, pk = forget_gates(hn, fox_w_in[j][:, 3 * d:], fox_b_f[j], bsz, lp)
            o = fox_attention(qkv, pq, pk, bsz, lp)
            h = matmul(o, fox_w_out[j], F32, residual=h, tm=1536, tn=512)
        last = i == depth - 1
        outs = hierarchical_moe(h, norm_ffn[i], moe_w_rg[i], moe_b_rg[i], moe_w_re[i], moe_b_re[i],
                                moe_w_gate, moe_w_up, moe_w_down, i,
                                norm_final if last else norm_mixer[i + 1],
                                F32 if last else BF16, (bsz, seq, d) if last else None)
        if last:
            return outs[0]
        h, hn = outs
```

```python
import functools

import jax
import jax.numpy as jnp
from jax import lax
from jax.experimental import pallas as pl
from jax.experimental.pallas import tpu as pltpu

N_META = 16
CHUNK = 128
PAD = CHUNK - N_META
RMS_EPS = 1e-6
GN_EPS = 1e-6
NEG_INF = -1e30
RET_HEADS = 8
FOX_HEADS = 16
N_GROUPS = 4
EXPERTS_PER_GROUP = 8
N_EXPERTS = N_GROUPS * EXPERTS_PER_GROUP
ROPE_BASE = 10000.0

LANES = 128
ROUTER_ROWS = 48
EXPERT_ROW0 = N_GROUPS
TILE_M = 256
VMEM_LIMIT = 48 * 1024 * 1024

F32 = jnp.float32
BF16 = jnp.bfloat16


def _params(sem, vmem=VMEM_LIMIT):
    return pltpu.CompilerParams(dimension_semantics=sem, vmem_limit_bytes=vmem)


def _rms(h, g):
    return h * lax.rsqrt(jnp.mean(h * h, axis=-1, keepdims=True) + RMS_EPS) * g


def _sigmoid(x):
    return 1.0 / (1.0 + jnp.exp(-x))


def _pack_halves(x):
    half = x.shape[1] // 2
    return pltpu.pack_elementwise([x[:, :half], x[:, half:]], packed_dtype=BF16)


def _unpack_halves(p):
    lo = pltpu.unpack_elementwise(p, index=0, packed_dtype=BF16, unpacked_dtype=F32)
    hi = pltpu.unpack_elementwise(p, index=1, packed_dtype=BF16, unpacked_dtype=F32)
    return jnp.concatenate([lo, hi], axis=1)


EMBED_CHUNKS = 3


def _embed_norm_kernel(*refs):
    x_refs, (meta_ref, g_ref, h_ref, hn_ref) = refs[:EMBED_CHUNKS], refs[EMBED_CHUNKS:]
    i = pl.program_id(1)
    for c, x_ref in enumerate(x_refs):
        h_ref[c * CHUNK:(c + 1) * CHUNK, :] = x_ref[...]

    @pl.when(i == 0)
    def _():
        h_ref[:PAD, :] = jnp.zeros((PAD, h_ref.shape[1]), h_ref.dtype)
        h_ref[PAD:CHUNK, :] = meta_ref[...]

    hn_ref[...] = _rms(h_ref[...], g_ref[...]).astype(hn_ref.dtype)


def embed_norm(x, meta, gain, hn_dtype):
    bsz, seq, d = x.shape
    nc = (seq + CHUNK) // CHUNK
    assert nc % EMBED_CHUNKS == 0
    tp = bsz * nc * CHUNK
    steps = nc // EMBED_CHUNKS
    rows = EMBED_CHUNKS * CHUNK

    def x_spec(c):
        return pl.BlockSpec((None, CHUNK, d), lambda b, i: (b, jnp.maximum(EMBED_CHUNKS * i + c - 1, 0), 0))

    return pl.pallas_call(
        _embed_norm_kernel,
        out_shape=(jax.ShapeDtypeStruct((tp, d), F32), jax.ShapeDtypeStruct((tp, d), hn_dtype)),
        grid=(bsz, steps),
        in_specs=[x_spec(c) for c in range(EMBED_CHUNKS)] + [
            pl.BlockSpec((N_META, d), lambda b, i: (0, 0)),
            pl.BlockSpec((1, d), lambda b, i: (0, 0)),
        ],
        out_specs=(
            pl.BlockSpec((rows, d), lambda b, i: (b * steps + i, 0)),
            pl.BlockSpec((rows, d), lambda b, i: (b * steps + i, 0)),
        ),
        compiler_params=_params(("parallel", "parallel")),
        name="embed_norm",
    )(*([x] * EMBED_CHUNKS), meta, gain.reshape(1, d))


def _mm_kernel(*refs, has_res, scale_tiles, scale):
    if has_res:
        x_ref, w_ref, r_ref, o_ref, wb_ref = refs
    else:
        x_ref, w_ref, o_ref, wb_ref = refs

    @pl.when(pl.program_id(1) == 0)
    def _():
        wb_ref[...] = w_ref[...].astype(BF16)

    acc = jnp.dot(x_ref[...], wb_ref[...], preferred_element_type=F32)
    if scale_tiles:
        acc = acc * jnp.where(pl.program_id(0) < scale_tiles, scale, 1.0)
    if has_res:
        acc = acc + r_ref[...]
    o_ref[...] = acc.astype(o_ref.dtype)


def matmul(x, w, layer, out_dtype, n=None, residual=None, tm=512, tn=1024, scale_cols=0, scale=1.0):
    m, kdim = x.shape
    n = w.shape[2] if n is None else n
    assert m % tm == 0 and n % tn == 0 and scale_cols % tn == 0
    in_specs = [
        pl.BlockSpec((tm, kdim), lambda j, i: (i, 0)),
        pl.BlockSpec((None, kdim, tn), lambda j, i: (layer, 0, j)),
    ]
    args = [x, w]
    if residual is not None:
        in_specs.append(pl.BlockSpec((tm, tn), lambda j, i: (i, j)))
        args.append(residual)
    return pl.pallas_call(
        functools.partial(_mm_kernel, has_res=residual is not None, scale_tiles=scale_cols // tn, scale=scale),
        out_shape=jax.ShapeDtypeStruct((m, n), out_dtype),
        grid=(n // tn, m // tm),
        in_specs=in_specs,
        out_specs=pl.BlockSpec((tm, tn), lambda j, i: (i, j)),
        scratch_shapes=[pltpu.VMEM((kdim, tn), BF16)],
        compiler_params=_params(("parallel", "arbitrary")),
        name="matmul",
    )(*args)


def _retention_kernel(q_ref, k_ref, v_ref, g_ref, cos_ref, sin_ref, dmat_ref, xi_ref, zeta_ref,
                      gch_ref, gn_ref, o_ref, state_ref, *, nh, dk, dv):
    c = pl.program_id(1)

    @pl.when(c == 0)
    def _():
        state_ref[...] = jnp.zeros_like(state_ref)

    cos = cos_ref[...]
    sin = sin_ref[...]
    half = dk // 2

    def rot(u):
        u1, u2 = u[:, :half], u[:, half:]
        return jnp.concatenate([u1 * cos - u2 * sin, u1 * sin + u2 * cos], axis=1)

    row = lax.broadcasted_iota(jnp.int32, (CHUNK, 1), 0)
    valid = jnp.logical_or(row >= PAD, c > 0)
    for h in range(nh):
        q = rot(q_ref[:, h * dk:(h + 1) * dk].astype(F32))
        k = rot(k_ref[:, h * dk:(h + 1) * dk].astype(F32)) * (dk ** -0.5)
        k = jnp.where(valid, k, 0.0)
        v = v_ref[:, h * dv:(h + 1) * dv]
        vb = jnp.where(valid, v, jnp.zeros_like(v))
        qb = q.astype(BF16)
        kb = k.astype(BF16)

        scores = lax.dot_general(qb, kb, (((1,), (1,)), ((), ())), preferred_element_type=F32)
        scores = scores * dmat_ref[h]
        inner = jnp.dot(scores.astype(BF16), vb, preferred_element_type=F32)
        state = state_ref[h]
        cross = jnp.dot(qb, state.astype(BF16), preferred_element_type=F32) * xi_ref[h]
        y = inner + cross
        kz = (k * zeta_ref[h]).astype(BF16)
        state_ref[h] = gch_ref[h] * state + lax.dot_general(
            kz, vb, (((0,), (0,)), ((), ())), preferred_element_type=F32)

        mu = jnp.mean(y, axis=-1, keepdims=True)
        dlt = y - mu
        var = jnp.mean(dlt * dlt, axis=-1, keepdims=True)
        yn = dlt * lax.rsqrt(var + GN_EPS) * gn_ref[h]
        hg = 0.5 * g_ref[:, h * dv:(h + 1) * dv].astype(F32)
        o_ref[:, h * dv:(h + 1) * dv] = ((hg + hg * jnp.tanh(hg)) * yn).astype(o_ref.dtype)


def retention_core(proj, gn_gain, bsz, nc):
    tp = proj.shape[0]
    d = proj.shape[1] // 6
    nh = RET_HEADS
    dk = d // nh
    dv = 2 * d // nh
    lp = nc * CHUNK
    half = dk // 2
    pos = (jnp.arange(lp) - PAD).astype(F32)
    inv = ROPE_BASE ** (-jnp.arange(half, dtype=F32) / half)
    ang = pos[:, None] * inv[None, :]
    cos, sin = jnp.cos(ang), jnp.sin(ang)
    log_g = jnp.log1p(-jnp.exp2(-5.0 - jnp.arange(nh, dtype=F32)))
    idx = jnp.arange(CHUNK, dtype=F32)
    diff = idx[:, None] - idx[None, :]
    dmat = jnp.where(diff[None] >= 0, jnp.exp(jnp.maximum(diff, 0.0)[None] * log_g[:, None, None]), 0.0)
    xi = jnp.exp((idx + 1.0)[None, :] * log_g[:, None])[:, :, None]
    zeta = jnp.exp((CHUNK - 1.0 - idx)[None, :] * log_g[:, None])[:, :, None]
    g_chunk = jnp.exp(CHUNK * log_g)[:, None, None]
    return pl.pallas_call(
        functools.partial(_retention_kernel, nh=nh, dk=dk, dv=dv),
        out_shape=jax.ShapeDtypeStruct((tp, 2 * d), BF16),
        grid=(bsz, nc),
        in_specs=[
            pl.BlockSpec((CHUNK, d), lambda b, c: (b * nc + c, 0)),
            pl.BlockSpec((CHUNK, d), lambda b, c: (b * nc + c, 1)),
            pl.BlockSpec((CHUNK, 2 * d), lambda b, c: (b * nc + c, 1)),
            pl.BlockSpec((CHUNK, 2 * d), lambda b, c: (b * nc + c, 2)),
            pl.BlockSpec((CHUNK, half), lambda b, c: (c, 0)),
            pl.BlockSpec((CHUNK, half), lambda b, c: (c, 0)),
            pl.BlockSpec((nh, CHUNK, CHUNK), lambda b, c: (0, 0, 0)),
            pl.BlockSpec((nh, CHUNK, 1), lambda b, c: (0, 0, 0)),
            pl.BlockSpec((nh, CHUNK, 1), lambda b, c: (0, 0, 0)),
            pl.BlockSpec((nh, 1, 1), lambda b, c: (0, 0, 0)),
            pl.BlockSpec((nh, 1, dv), lambda b, c: (0, 0, 0)),
        ],
        out_specs=pl.BlockSpec((CHUNK, 2 * d), lambda b, c: (b * nc + c, 0)),
        scratch_shapes=[pltpu.VMEM((nh, dk, dv), F32)],
        compiler_params=_params(("parallel", "arbitrary")),
        name="retention_core",
    )(proj, proj, proj, proj, cos, sin, dmat, xi, zeta, g_chunk, gn_gain.reshape(nh, 1, dv))


LOG2E = 1.4426950408889634
MASK_BIG = 1e30
ONES_LANE = LANES - 1


def _bias_selectors(nh):
    h = jnp.arange(nh)
    selq = jnp.zeros((nh, 3 * LANES, LANES), F32)
    selk = jnp.zeros((nh, 3 * LANES, LANES), F32)
    for part in range(3):
        selq = selq.at[h, part * LANES + h, part].set(1.0)
        selk = selk.at[h, part * LANES + h, 3 + part].set(-1.0)
        selq = selq.at[h, ONES_LANE, 3 + part].set(1.0)
        selk = selk.at[h, ONES_LANE, part].set(1.0)
    return selq.astype(BF16), selk.astype(BF16)


def _forget_kernel(hn_ref, wf_ref, bf_ref, pq_ref, pk_ref, carry_ref, *, rows):
    i = pl.program_id(1)

    @pl.when(i == 0)
    def _():
        carry_ref[...] = jnp.zeros_like(carry_ref)

    z = jnp.dot(hn_ref[...], wf_ref[...], preferred_element_type=F32) + bf_ref[...]
    lf = jnp.minimum(z, 0.0) - jnp.log1p(jnp.exp(-jnp.abs(z)))
    row = lax.broadcasted_iota(jnp.int32, (rows, 1), 0)
    valid = i * rows + row >= PAD
    lf = jnp.where(valid, lf, 0.0)

    def split3(a):
        hi = a.astype(BF16)
        r1 = a - hi.astype(F32)
        mid = r1.astype(BF16)
        lo = (r1 - mid.astype(F32)).astype(BF16)
        return hi, mid, lo

    r_i = lax.broadcasted_iota(jnp.int32, (rows, rows), 0)
    c_i = lax.broadcasted_iota(jnp.int32, (rows, rows), 1)
    tri = (r_i >= c_i).astype(BF16)
    cs = sum(jnp.dot(tri, part, preferred_element_type=F32) for part in split3(lf))
    cs = cs + carry_ref[...]
    carry_ref[...] = cs[rows - 1:rows, :]

    hi, mid, lo = (p.astype(F32) for p in split3(cs * LOG2E))
    lane = lax.broadcasted_iota(jnp.int32, (rows, LANES), 1)
    ones = lane == ONES_LANE
    live = jnp.logical_and(valid, jnp.logical_not(ones))
    pq_ref[...] = jnp.concatenate(
        [jnp.where(ones, 1.0, hi), jnp.where(ones, 0.0, mid), jnp.where(ones, 0.0, lo)], axis=1).astype(BF16)
    pk_ref[...] = jnp.concatenate(
        [jnp.where(ones, 1.0, jnp.where(valid, hi, MASK_BIG)), jnp.where(live, mid, 0.0),
         jnp.where(live, lo, 0.0)], axis=1).astype(BF16)


def forget_gates(hn, w_f, b_f, bsz, lp, rows=384):
    tp, d = hn.shape
    nh = w_f.shape[1]
    steps = lp // rows
    wf = jnp.zeros((d, LANES), BF16).at[:, :nh].set(w_f.astype(BF16))
    bf = jnp.zeros((1, LANES), F32).at[0, :nh].set(b_f.astype(F32))
    return pl.pallas_call(
        functools.partial(_forget_kernel, rows=rows),
        out_shape=(jax.ShapeDtypeStruct((tp, 3 * LANES), BF16), jax.ShapeDtypeStruct((tp, 3 * LANES), BF16)),
        grid=(bsz, steps),
        in_specs=[
            pl.BlockSpec((rows, d), lambda b, i: (b * steps + i, 0)),
            pl.BlockSpec((d, LANES), lambda b, i: (0, 0)),
            pl.BlockSpec((1, LANES), lambda b, i: (0, 0)),
        ],
        out_specs=(
            pl.BlockSpec((rows, 3 * LANES), lambda b, i: (b * steps + i, 0)),
            pl.BlockSpec((rows, 3 * LANES), lambda b, i: (b * steps + i, 0)),
        ),
        scratch_shapes=[pltpu.VMEM((1, LANES), F32)],
        compiler_params=_params(("parallel", "arbitrary")),
        name="forget_gates",
    )(hn, wf, bf)


FOX_HEADS_PER_STEP = 4


FOX_STRIP = 32


def _fox_kernel(q_ref, pq_ref, k_ref, pk_ref, v_ref, selq_ref, selk_ref, o_ref,
                m_ref, l_ref, acc_ref, s_ref, p_ref, a_ref, ck_ref, *, tq, dh):
    i = pl.program_id(2)
    hs = FOX_HEADS_PER_STEP
    reps = tq // LANES
    m_ref[...] = jnp.full_like(m_ref, 10.0 * NEG_INF)
    l_ref[...] = jnp.zeros_like(l_ref)
    acc_ref[...] = jnp.zeros_like(acc_ref)

    @pl.when(i == 0)
    def _():
        for j in range(hs):
            ck_ref[:, j * LANES:(j + 1) * LANES] = jnp.dot(
                pk_ref[...], selk_ref[j], preferred_element_type=F32).astype(BF16)

    qa = [jnp.concatenate([q_ref[:, j * dh:(j + 1) * dh],
                           jnp.dot(pq_ref[...], selq_ref[j], preferred_element_type=F32).astype(BF16)], axis=1)
          for j in range(hs)]

    def scores(kb, buf):
        start = pl.multiple_of(kb * tq, tq)
        for j in range(hs):
            ka = jnp.concatenate([k_ref[pl.ds(start, tq), j * dh:(j + 1) * dh],
                                  ck_ref[pl.ds(start, tq), j * LANES:(j + 1) * LANES]], axis=1)
            s_ref[buf, j] = lax.dot_general(qa[j], ka, (((1,), (1,)), ((), ())), preferred_element_type=F32)

    def absorb(kb, buf, causal):
        start = pl.multiple_of(kb * tq, tq)
        for j in range(hs):
            for r0 in range(0, tq, FOX_STRIP):
                rows = slice(r0, r0 + FOX_STRIP)
                s = s_ref[buf, j, rows, :]
                if causal:
                    r_i = r0 + lax.broadcasted_iota(jnp.int32, (FOX_STRIP, tq), 0)
                    c_i = lax.broadcasted_iota(jnp.int32, (FOX_STRIP, tq), 1)
                    s = jnp.where(c_i <= r_i, s, NEG_INF)
                m_old = m_ref[j, rows, :]
                m_new = jnp.maximum(m_old, jnp.max(s, axis=1, keepdims=True))
                alpha = jnp.exp2(m_old - m_new)
                p = jnp.exp2(s - jnp.concatenate([m_new] * reps, axis=1))
                l_ref[j, rows, :] = alpha * l_ref[j, rows, :] + jnp.sum(p, axis=1, keepdims=True)
                m_ref[j, rows, :] = m_new
                a_ref[j, rows, :] = alpha
                p_ref[j, rows, :] = p.astype(BF16)
        for j in range(hs):
            acc_ref[j] = a_ref[j] * acc_ref[j] + jnp.dot(
                p_ref[j], v_ref[pl.ds(start, tq), j * dh:(j + 1) * dh], preferred_element_type=F32)

    def body(kb, carry):
        scores(kb, 0)
        absorb(kb, 0, False)
        return carry

    lax.fori_loop(0, i, body, 0)
    scores(i, 0)
    absorb(i, 0, True)

    for j in range(hs):
        o_ref[:, j * dh:(j + 1) * dh] = (acc_ref[j] / l_ref[j]).astype(o_ref.dtype)


def fox_attention(qkv, pq, pk, bsz, lp, tq=384):
    tp = qkv.shape[0]
    d = qkv.shape[1] // 3
    nh = FOX_HEADS
    dh = d // nh
    assert dh == LANES
    nq = lp // tq
    hs = FOX_HEADS_PER_STEP
    ng = nh // hs
    qkv3 = qkv.reshape(bsz, lp, 3 * d)
    pk3 = pk.reshape(bsz, lp, 3 * LANES)
    selq, selk = _bias_selectors(nh)
    return pl.pallas_call(
        functools.partial(_fox_kernel, tq=tq, dh=dh),
        out_shape=jax.ShapeDtypeStruct((tp, d), BF16),
        grid=(bsz, ng, nq),
        in_specs=[
            pl.BlockSpec((tq, hs * dh), lambda b, g, i: (b * nq + i, g)),
            pl.BlockSpec((tq, 3 * LANES), lambda b, g, i: (b * nq + i, 0)),
            pl.BlockSpec((None, lp, hs * dh), lambda b, g, i: (b, 0, ng + g)),
            pl.BlockSpec((None, lp, 3 * LANES), lambda b, g, i: (b, 0, 0)),
            pl.BlockSpec((None, lp, hs * dh), lambda b, g, i: (b, 0, 2 * ng + g)),
            pl.BlockSpec((hs, 3 * LANES, LANES), lambda b, g, i: (g, 0, 0)),
            pl.BlockSpec((hs, 3 * LANES, LANES), lambda b, g, i: (g, 0, 0)),
        ],
        out_specs=pl.BlockSpec((tq, hs * dh), lambda b, g, i: (b * nq + i, g)),
        scratch_shapes=[pltpu.VMEM((hs, tq, LANES), F32), pltpu.VMEM((hs, tq, LANES), F32),
                        pltpu.VMEM((hs, tq, dh), F32), pltpu.VMEM((1, hs, tq, tq), F32),
                        pltpu.VMEM((hs, tq, tq), BF16), pltpu.VMEM((hs, tq, LANES), F32),
                        pltpu.VMEM((lp, hs * LANES), BF16)],
        compiler_params=_params(("parallel", "parallel", "arbitrary")),
        name="fox_attention",
    )(qkv, pq, qkv3, pk3, qkv3, selq, selk)


def _router_kernel(h_ref, g_ref, wr_ref, br_ref, hn_ref, idx_ref, w_ref, cnt_ref, carry_ref, *, tr):
    i = pl.program_id(0)

    @pl.when(i == 0)
    def _():
        carry_ref[...] = jnp.zeros_like(carry_ref)

    hn = _rms(h_ref[...], g_ref[...])
    hn_ref[...] = _pack_halves(hn)
    lg = lax.dot_general(wr_ref[...], hn.astype(BF16), (((1,), (1,)), ((), ())),
                         preferred_element_type=F32) + br_ref[...]
    row = lax.broadcasted_iota(jnp.int32, (ROUTER_ROWS, tr), 0)
    big = jnp.int32(1 << 20)
    is_g = row < N_GROUPS
    mg = jnp.max(jnp.where(is_g, lg, -jnp.inf), axis=0, keepdims=True)
    g_sel = jnp.min(jnp.where(jnp.logical_and(is_g, lg == mg), row, big), axis=0, keepdims=True)
    sg = jnp.sum(jnp.where(is_g, jnp.exp(lg - mg), 0.0), axis=0, keepdims=True)
    p_g = 1.0 / sg
    lo = EXPERT_ROW0 + EXPERTS_PER_GROUP * g_sel
    is_e = jnp.logical_and(row >= lo, row < lo + EXPERTS_PER_GROUP)
    me = jnp.max(jnp.where(is_e, lg, -jnp.inf), axis=0, keepdims=True)
    ee = jnp.where(is_e, jnp.exp(lg - me), 0.0)
    pe = ee / jnp.sum(ee, axis=0, keepdims=True)
    pe1 = jnp.where(is_e, pe, -1.0)
    m1 = jnp.max(pe1, axis=0, keepdims=True)
    i1 = jnp.min(jnp.where(pe1 == m1, row, big), axis=0, keepdims=True)
    pe2 = jnp.where(row == i1, -1.0, pe1)
    m2 = jnp.max(pe2, axis=0, keepdims=True)
    i2 = jnp.min(jnp.where(pe2 == m2, row, big), axis=0, keepdims=True)
    den = m1 + m2
    w1 = p_g * (m1 / den)
    w2 = p_g * (m2 / den)

    oh1 = row == i1
    oh2 = row == i2
    ohs = jnp.logical_or(oh1, oh2).astype(F32)
    r_i = lax.broadcasted_iota(jnp.int32, (tr, tr), 0)
    c_i = lax.broadcasted_iota(jnp.int32, (tr, tr), 1)
    tri = (r_i < c_i).astype(BF16)
    cnt = jnp.dot(ohs.astype(BF16), tri, preferred_element_type=F32) + carry_ref[...]
    rank1 = jnp.sum(jnp.where(oh1, cnt, 0.0), axis=0, keepdims=True)
    rank2 = jnp.sum(jnp.where(oh2, cnt, 0.0), axis=0, keepdims=True)
    carry_ref[...] += jnp.sum(ohs, axis=1, keepdims=True)

    r8 = lax.broadcasted_iota(jnp.int32, (8, tr), 0)
    e1 = i1 - EXPERT_ROW0
    e2 = i2 - EXPERT_ROW0
    idx_ref[...] = jnp.where(r8 == 0, e1, jnp.where(r8 == 1, e2, jnp.where(
        r8 == 2, rank1.astype(jnp.int32), jnp.where(r8 == 3, rank2.astype(jnp.int32), 0))))
    w_ref[...] = jnp.where(r8 == 0, w1, jnp.where(r8 == 1, w2, 0.0))
    cnt_ref[...] = jnp.broadcast_to(carry_ref[...], cnt_ref.shape)


def moe_router(h, gain, w_rg, b_rg, w_re, b_re, tr=512):
    tp, d = h.shape
    wr = jnp.zeros((ROUTER_ROWS, d), BF16)
    wr = wr.at[:N_GROUPS].set(w_rg.T.astype(BF16))
    wr = wr.at[EXPERT_ROW0:EXPERT_ROW0 + N_EXPERTS].set(w_re.reshape(d, N_EXPERTS).T.astype(BF16))
    br = jnp.zeros((ROUTER_ROWS, 1), F32)
    br = br.at[:N_GROUPS, 0].set(b_rg.astype(F32))
    br = br.at[EXPERT_ROW0:EXPERT_ROW0 + N_EXPERTS, 0].set(b_re.reshape(N_EXPERTS).astype(F32))
    return pl.pallas_call(
        functools.partial(_router_kernel, tr=tr),
        out_shape=(
            jax.ShapeDtypeStruct((tp, d // 2), jnp.uint32),
            jax.ShapeDtypeStruct((8, tp), jnp.int32),
            jax.ShapeDtypeStruct((8, tp), F32),
            jax.ShapeDtypeStruct((ROUTER_ROWS, LANES), F32),
        ),
        grid=(tp // tr,),
        in_specs=[
            pl.BlockSpec((tr, d), lambda i: (i, 0)),
            pl.BlockSpec((1, d), lambda i: (0, 0)),
            pl.BlockSpec((ROUTER_ROWS, d), lambda i: (0, 0)),
            pl.BlockSpec((ROUTER_ROWS, 1), lambda i: (0, 0)),
        ],
        out_specs=(
            pl.BlockSpec((tr, d // 2), lambda i: (i, 0)),
            pl.BlockSpec((8, tr), lambda i: (0, i)),
            pl.BlockSpec((8, tr), lambda i: (0, i)),
            pl.BlockSpec((ROUTER_ROWS, LANES), lambda i: (0, 0)),
        ),
        scratch_shapes=[pltpu.VMEM((ROUTER_ROWS, 1), F32)],
        compiler_params=_params(("arbitrary",)),
        name="moe_router",
    )(h, gain.reshape(1, d), wr, br)


def _slots_kernel(cnt_ref, idx_ref, pos_ref, meta_ref):
    e1 = idx_ref[0:1, :]
    e2 = idx_ref[1:2, :]
    off1 = jnp.zeros_like(e1)
    off2 = jnp.zeros_like(e2)
    visit = lax.broadcasted_iota(jnp.int32, (1, meta_ref.shape[1]), 1)
    v_tile = jnp.zeros_like(visit)
    v_expert = jnp.zeros_like(visit)
    v_lo = jnp.zeros_like(visit)
    v_hi = jnp.zeros_like(visit)
    v_next = jnp.zeros_like(visit)
    v_start = jnp.zeros_like(visit)
    v_count = jnp.zeros_like(visit)
    next_live = [None] * N_EXPERTS
    nxt = jnp.int32(-1)
    for e in reversed(range(N_EXPERTS)):
        next_live[e] = jnp.where(nxt >= 0, nxt, e)
        nxt = jnp.where(cnt_ref[e] > 0, e, nxt)
    start = jnp.int32(0)
    v_base = jnp.int32(0)
    for e in range(N_EXPERTS):
        off1 = jnp.where(e1 == e, start, off1)
        off2 = jnp.where(e2 == e, start, off2)
        n = cnt_ref[e]
        end = start + n
        first_tile = start // TILE_M
        n_visits = jnp.where(n > 0, (jnp.maximum(end, 1) - 1) // TILE_M - first_tile + 1, 0)
        mine = jnp.logical_and(visit >= v_base, visit < v_base + n_visits)
        row0 = (first_tile + visit - v_base) * TILE_M
        v_tile = jnp.where(mine, first_tile + visit - v_base, v_tile)
        v_expert = jnp.where(mine, e, v_expert)
        v_next = jnp.where(mine, next_live[e], v_next)
        v_start = jnp.where(mine, v_base, v_start)
        v_count = jnp.where(mine, n_visits, v_count)
        v_lo = jnp.where(mine, jnp.maximum(start - row0, 0), v_lo)
        v_hi = jnp.where(mine, jnp.minimum(end - row0, TILE_M), v_hi)
        start = end
        v_base = v_base + n_visits
    r8 = lax.broadcasted_iota(jnp.int32, pos_ref.shape, 0)
    pos_ref[...] = jnp.where(r8 == 0, off1 + idx_ref[2:3, :], jnp.where(r8 == 1, off2 + idx_ref[3:4, :], 0))
    m8 = lax.broadcasted_iota(jnp.int32, meta_ref.shape, 0)
    rows = (v_tile, v_expert, v_lo, v_hi, v_base, v_next, v_start, v_count)
    meta = rows[-1]
    for r in reversed(range(len(rows) - 1)):
        meta = jnp.where(m8 == r, rows[r], meta)
    meta_ref[...] = meta


def moe_slots(counts, idx, tr=512):
    tp = idx.shape[1]
    nt_lanes = 2 * LANES
    return pl.pallas_call(
        _slots_kernel,
        out_shape=(jax.ShapeDtypeStruct((8, tp), jnp.int32), jax.ShapeDtypeStruct((8, nt_lanes), jnp.int32)),
        grid_spec=pltpu.PrefetchScalarGridSpec(
            num_scalar_prefetch=1,
            grid=(tp // tr,),
            in_specs=[pl.BlockSpec((8, tr), lambda i, c: (0, i))],
            out_specs=(
                pl.BlockSpec((8, tr), lambda i, c: (0, i)),
                pl.BlockSpec((8, nt_lanes), lambda i, c: (0, 0)),
            ),
        ),
        compiler_params=_params(("arbitrary",)),
        name="moe_slots",
    )(counts, idx)


def _invert_kernel(pos_ref, code_ref, *, tp):
    def per_token(t, carry):
        code_ref[pos_ref[t]] = 2 * t
        code_ref[pos_ref[tp + t]] = 2 * t + 1
        return carry

    lax.fori_loop(0, tp, per_token, 0, unroll=8)


def moe_invert(pos_flat):
    tp = pos_flat.shape[0] // 2
    return pl.pallas_call(
        functools.partial(_invert_kernel, tp=tp),
        out_shape=jax.ShapeDtypeStruct((2 * tp,), jnp.int32),
        grid_spec=pltpu.PrefetchScalarGridSpec(
            num_scalar_prefetch=1,
            grid=(1,),
            in_specs=[],
            out_specs=pl.BlockSpec(memory_space=pltpu.SMEM),
        ),
        compiler_params=_params(("arbitrary",)),
        name="moe_invert",
    )(pos_flat)


def _expert_kernel(vt_ref, ve_ref, vlo_ref, vhi_ref, nv_ref, vnext_ref, vstart_ref, vcount_ref, code_ref,
                   hn_ref, wg_hbm, wu_hbm, wd_hbm, y_ref,
                   xbuf, xcur, yacc, wgs, wus, wds, wgb, wub, wdb, gsem, wsem, *, n_tiles, layer):
    v = pl.program_id(0)
    nv = nv_ref[0]
    t = vt_ref[v]
    prev_v = jnp.maximum(v - 1, 0)
    first = jnp.logical_or(v == 0, vt_ref[prev_v] != t)
    new_expert = jnp.logical_or(v == 0, ve_ref[prev_v] != ve_ref[v])

    def weight_copies(e):
        w = layer * N_EXPERTS + e
        return (pltpu.make_async_copy(wg_hbm.at[w], wgs, wsem.at[0]),
                pltpu.make_async_copy(wu_hbm.at[w], wus, wsem.at[1]),
                pltpu.make_async_copy(wd_hbm.at[w], wds, wsem.at[2]))

    def row_copy(tile, r):
        tok = lax.shift_right_logical(code_ref[tile * TILE_M + r], 1)
        return pltpu.make_async_copy(hn_ref.at[pl.ds(tok, 1)], xbuf.at[pl.ds(r, 1)], gsem)

    def gather_wait():
        pltpu.make_async_copy(hn_ref.at[pl.ds(0, TILE_M)], xbuf, gsem).wait()

    def compute(accumulate):
        x = xcur[...]
        a = jnp.dot(x, wgb[...], preferred_element_type=F32)
        u = jnp.dot(x, wub[...], preferred_element_type=F32)
        row = lax.broadcasted_iota(jnp.int32, (TILE_M, 1), 0)
        mine = jnp.logical_and(row >= vlo_ref[v], row < vhi_ref[v])
        hid = jnp.where(mine, a * _sigmoid(a) * u, 0.0).astype(BF16)
        y = jnp.dot(hid, wdb[...], preferred_element_type=F32)
        if accumulate:
            y = y + yacc[...]
        yacc[...] = y
        y_ref[...] = _pack_halves(y)

    @pl.when(v == 0)
    def _():
        def step(r, carry):
            row_copy(0, r).start()
            return carry
        lax.fori_loop(0, TILE_M, step, 0, unroll=8)
        for copy in weight_copies(ve_ref[0]):
            copy.start()

    @pl.when(v < nv)
    def _():
        @pl.when(new_expert)
        def _():
            for copy in weight_copies(ve_ref[v]):
                copy.wait()
            wgb[...] = wgs[...].astype(BF16)
            wub[...] = wus[...].astype(BF16)
            wdb[...] = wds[...].astype(BF16)

        @pl.when(first)
        def _():
            gather_wait()
            xcur[...] = _unpack_halves(xbuf[...]).astype(BF16)
            ahead = jnp.minimum(t + 1, n_tiles - 1)
            for r in range(TILE_M):
                row_copy(ahead, r).start()
            compute(False)

        @pl.when(jnp.logical_not(first))
        def _():
            compute(True)

        k = v - vstart_ref[v]
        last_of_expert = k == vcount_ref[v] - 1
        has_next = vnext_ref[v] != ve_ref[v]
        for c, copy in enumerate(weight_copies(vnext_ref[v])):
            @pl.when(jnp.logical_and(has_next, jnp.logical_or(k == c, jnp.logical_and(last_of_expert, k < c))))
            def _():
                copy.start()

        @pl.when(v == nv - 1)
        def _():
            gather_wait()


def moe_experts(meta, code, hn, w_gate, w_up, w_down, layer):
    tp = hn.shape[0]
    d = 2 * hn.shape[1]
    f = w_gate.shape[-1]
    assert (2 * tp) % TILE_M == 0
    n_tiles = (2 * tp) // TILE_M
    max_visits = n_tiles + N_EXPERTS - 1

    def y_map(v, vt, ve, vlo, vhi, nv, *_):
        return (vt[jnp.minimum(v, nv[0] - 1)], 0)

    any_spec = pl.BlockSpec(memory_space=pl.ANY)
    return pl.pallas_call(
        functools.partial(_expert_kernel, n_tiles=n_tiles, layer=layer),
        out_shape=jax.ShapeDtypeStruct((2 * tp, d // 2), jnp.uint32),
        grid_spec=pltpu.PrefetchScalarGridSpec(
            num_scalar_prefetch=9,
            grid=(max_visits,),
            in_specs=[any_spec, any_spec, any_spec, any_spec],
            out_specs=pl.BlockSpec((TILE_M, d // 2), y_map),
            scratch_shapes=[
                pltpu.VMEM((TILE_M, d // 2), jnp.uint32), pltpu.VMEM((TILE_M, d), BF16),
                pltpu.VMEM((TILE_M, d), F32),
                pltpu.VMEM((d, f), F32), pltpu.VMEM((d, f), F32), pltpu.VMEM((f, d), F32),
                pltpu.VMEM((d, f), BF16), pltpu.VMEM((d, f), BF16), pltpu.VMEM((f, d), BF16),
                pltpu.SemaphoreType.DMA, pltpu.SemaphoreType.DMA((3,)),
            ],
        ),
        compiler_params=_params(("arbitrary",), vmem=56 * 1024 * 1024),
        name="moe_experts",
    )(meta[0, :max_visits], meta[1, :max_visits], meta[2, :max_visits], meta[3, :max_visits], meta[4, :1],
      meta[5, :max_visits], meta[6, :max_visits], meta[7, :max_visits], code, hn, w_gate, w_up, w_down)


def _combine_kernel(pos_ref, h_ref, w_ref, g_ref, y_ref, *rest, tc, tp, n_steps, write_h):
    out_refs, (ybuf, sem) = rest[:-2], rest[-2:]
    i = pl.program_id(0)
    slot = lax.rem(i, 2)

    def row_copy(tile, buf, s, r):
        p = pos_ref[s * tp + tile * tc + r]
        return pltpu.make_async_copy(y_ref.at[pl.ds(p, 1)], ybuf.at[buf, pl.ds(s * tc + r, 1)], sem.at[buf])

    @pl.when(i == 0)
    def _():
        def step(r, carry):
            row_copy(0, 0, 0, r).start()
            row_copy(0, 0, 1, r).start()
            return carry
        lax.fori_loop(0, tc, step, 0, unroll=8)

    pltpu.make_async_copy(y_ref.at[pl.ds(0, 2 * tc)], ybuf.at[slot], sem.at[slot]).wait()

    @pl.when(i + 1 < n_steps)
    def _():
        for r in range(tc):
            row_copy(i + 1, 1 - slot, 0, r).start()
            row_copy(i + 1, 1 - slot, 1, r).start()

    w = w_ref[...]
    h = (h_ref[...] + w[:, 0:1] * _unpack_halves(ybuf[slot, :tc, :])
         + w[:, 1:2] * _unpack_halves(ybuf[slot, tc:, :]))
    if write_h:
        out_refs[0][...] = h
    out_refs[-1][...] = _rms(h, g_ref[...]).astype(out_refs[-1].dtype)


def moe_combine(pos_flat, h, w_col, y, gain, hn_dtype, final_shape=None):
    tp, d = h.shape
    if final_shape is None:
        tc = 256
        out_shape = (jax.ShapeDtypeStruct((tp, d), F32), jax.ShapeDtypeStruct((tp, d), hn_dtype))
        out_specs = (pl.BlockSpec((tc, d), lambda i, p: (i, 0)), pl.BlockSpec((tc, d), lambda i, p: (i, 0)))
    else:
        tc = CHUNK
        bsz, seq, _ = final_shape
        nc = tp // bsz // tc
        out_shape = (jax.ShapeDtypeStruct(final_shape, hn_dtype),)
        out_specs = (pl.BlockSpec((None, tc, d), lambda i, p: (i // nc, jnp.maximum(i % nc - 1, 0), 0)),)
    n_steps = tp // tc
    return pl.pallas_call(
        functools.partial(_combine_kernel, tc=tc, tp=tp, n_steps=n_steps, write_h=final_shape is None),
        out_shape=out_shape,
        grid_spec=pltpu.PrefetchScalarGridSpec(
            num_scalar_prefetch=1,
            grid=(n_steps,),
            in_specs=[
                pl.BlockSpec((tc, d), lambda i, p: (i, 0)),
                pl.BlockSpec((tc, 2), lambda i, p: (i, 0)),
                pl.BlockSpec((1, d), lambda i, p: (0, 0)),
                pl.BlockSpec(memory_space=pl.ANY),
            ],
            out_specs=out_specs,
            scratch_shapes=[pltpu.VMEM((2, 2 * tc, d // 2), jnp.uint32), pltpu.SemaphoreType.DMA((2,))],
        ),
        compiler_params=_params(("arbitrary",)),
        name="moe_combine",
    )(pos_flat, h, w_col, gain.reshape(1, d), y)


def hierarchical_moe(h, gain, w_rg, b_rg, w_re, b_re, w_gate, w_up, w_down, layer, next_gain, hn_dtype,
                     final_shape=None):
    tp, d = h.shape
    f = w_gate.shape[-1]
    hn, idx, w_rows, cnt = moe_router(h, gain, w_rg, b_rg, w_re, b_re)
    counts = cnt[EXPERT_ROW0:EXPERT_ROW0 + N_EXPERTS, 0].astype(jnp.int32)
    pos, meta = moe_slots(counts, idx)
    pos_flat = pos[:2].reshape(2 * tp)
    code = moe_invert(pos_flat)
    y = moe_experts(meta, code, hn, w_gate.reshape(-1, d, f), w_up.reshape(-1, d, f),
                    w_down.reshape(-1, f, d), layer)
    return moe_combine(pos_flat, h, w_rows[:2].T, y, next_gain, hn_dtype, final_shape)


def kernel(x, meta_tokens, norm_mixer, norm_ffn, norm_final, ret_w_in, ret_gn, ret_w_out,
           fox_w_in, fox_b_f, fox_w_out, moe_w_rg, moe_b_rg, moe_w_re, moe_b_re,
           moe_w_gate, moe_w_up, moe_w_down):
    bsz, seq, d = x.shape
    depth = norm_mixer.shape[0]
    nc = (seq + CHUNK) // CHUNK
    lp = nc * CHUNK
    h, hn = embed_norm(x, meta_tokens.astype(x.dtype), norm_mixer[0], BF16)
    for i in range(depth):
        j = i // 2
        if i % 2 == 0:
            proj = matmul(hn, ret_w_in, j, BF16, tm=1536)
            gated = retention_core(proj, ret_gn[j], bsz, nc)
            h = matmul(gated, ret_w_out, j, F32, residual=h, tn=512)
        else:
            qkv = matmul(hn, fox_w_in, j, BF16, n=3 * d, tm=1536, scale_cols=d,
                         scale=(d // FOX_HEADS) ** -0.5 * LOG2E)
            pq, pk = forget_gates(hn, fox_w_in[j, :, 3 * d:], fox_b_f[j], bsz, lp)
            o = fox_attention(qkv, pq, pk, bsz, lp)
            h = matmul(o, fox_w_out, j, F32, residual=h, tm=1536, tn=512)
        last = i == depth - 1
        outs = hierarchical_moe(h, norm_ffn[i], moe_w_rg[i], moe_b_rg[i], moe_w_re[i], moe_b_re[i],
                                moe_w_gate, moe_w_up, moe_w_down, i,
                                norm_final if last else norm_mixer[i + 1],
                                F32 if last else BF16, (bsz, seq, d) if last else None)
        if last:
            return outs[0]
        h, hn = outs
```

```python
import functools

import jax
import jax.numpy as jnp
from jax import lax
from jax.experimental import pallas as pl
from jax.experimental.pallas import tpu as pltpu

N_META = 16
CHUNK = 128
PAD = CHUNK - N_META
RMS_EPS = 1e-6
GN_EPS = 1e-6
NEG_INF = -1e30
RET_HEADS = 8
FOX_HEADS = 16
N_GROUPS = 4
EXPERTS_PER_GROUP = 8
N_EXPERTS = N_GROUPS * EXPERTS_PER_GROUP
ROPE_BASE = 10000.0

LANES = 128
ROUTER_ROWS = 48
EXPERT_ROW0 = N_GROUPS
TILE_M = 256
VMEM_LIMIT = 48 * 1024 * 1024

F32 = jnp.float32
BF16 = jnp.bfloat16


def _params(sem, vmem=VMEM_LIMIT):
    return pltpu.CompilerParams(dimension_semantics=sem, vmem_limit_bytes=vmem)


def _rms(h, g):
    return h * lax.rsqrt(jnp.mean(h * h, axis=-1, keepdims=True) + RMS_EPS) * g


def _sigmoid(x):
    return 1.0 / (1.0 + jnp.exp(-x))


def _pack_halves(x):
    half = x.shape[1] // 2
    return pltpu.pack_elementwise([x[:, :half], x[:, half:]], packed_dtype=BF16)


def _unpack_halves(p):
    lo = pltpu.unpack_elementwise(p, index=0, packed_dtype=BF16, unpacked_dtype=F32)
    hi = pltpu.unpack_elementwise(p, index=1, packed_dtype=BF16, unpacked_dtype=F32)
    return jnp.concatenate([lo, hi], axis=1)


SUBLANES = 8


def _token_tile(t):
    start = t * SUBLANES
    return pl.ds(start if isinstance(t, int) else pl.multiple_of(start, SUBLANES), SUBLANES)


def _store_token_tiles(ref, index, packed):
    m = packed.shape[0]
    for s in range(SUBLANES):
        ref[index + (pl.ds(s, m, stride=SUBLANES), slice(None))] = packed[:, s * LANES:(s + 1) * LANES]


def _load_token_tiles(ref, index, first_token, m):
    return jnp.concatenate(
        [ref[index + (pl.ds(first_token * SUBLANES + s, m, stride=SUBLANES), slice(None))]
         for s in range(SUBLANES)], axis=1)


EMBED_CHUNKS = 3


def _embed_norm_kernel(*refs):
    x_refs, (meta_ref, g_ref, h_ref, hn_ref) = refs[:EMBED_CHUNKS], refs[EMBED_CHUNKS:]
    i = pl.program_id(1)
    for c, x_ref in enumerate(x_refs):
        h_ref[c * CHUNK:(c + 1) * CHUNK, :] = x_ref[...]

    @pl.when(i == 0)
    def _():
        h_ref[:PAD, :] = jnp.zeros((PAD, h_ref.shape[1]), h_ref.dtype)
        h_ref[PAD:CHUNK, :] = meta_ref[...]

    hn_ref[...] = _rms(h_ref[...], g_ref[...]).astype(hn_ref.dtype)


def embed_norm(x, meta, gain, hn_dtype):
    bsz, seq, d = x.shape
    nc = (seq + CHUNK) // CHUNK
    assert nc % EMBED_CHUNKS == 0
    tp = bsz * nc * CHUNK
    steps = nc // EMBED_CHUNKS
    rows = EMBED_CHUNKS * CHUNK

    def x_spec(c):
        return pl.BlockSpec((None, CHUNK, d), lambda b, i: (b, jnp.maximum(EMBED_CHUNKS * i + c - 1, 0), 0))

    return pl.pallas_call(
        _embed_norm_kernel,
        out_shape=(jax.ShapeDtypeStruct((tp, d), F32), jax.ShapeDtypeStruct((tp, d), hn_dtype)),
        grid=(bsz, steps),
        in_specs=[x_spec(c) for c in range(EMBED_CHUNKS)] + [
            pl.BlockSpec((N_META, d), lambda b, i: (0, 0)),
            pl.BlockSpec((1, d), lambda b, i: (0, 0)),
        ],
        out_specs=(
            pl.BlockSpec((rows, d), lambda b, i: (b * steps + i, 0)),
            pl.BlockSpec((rows, d), lambda b, i: (b * steps + i, 0)),
        ),
        compiler_params=_params(("parallel", "parallel")),
        name="embed_norm",
    )(*([x] * EMBED_CHUNKS), meta, gain.reshape(1, d))


def _mm_kernel(*refs, has_res, scale_tiles, scale):
    if has_res:
        x_ref, w_ref, r_ref, o_ref, wb_ref = refs
    else:
        x_ref, w_ref, o_ref, wb_ref = refs

    @pl.when(pl.program_id(1) == 0)
    def _():
        wb_ref[...] = w_ref[...].astype(BF16)

    acc = jnp.dot(x_ref[...], wb_ref[...], preferred_element_type=F32)
    if scale_tiles:
        acc = acc * jnp.where(pl.program_id(0) < scale_tiles, scale, 1.0)
    if has_res:
        acc = acc + r_ref[...]
    o_ref[...] = acc.astype(o_ref.dtype)


def matmul(x, w, layer, out_dtype, n=None, residual=None, tm=512, tn=1024, scale_cols=0, scale=1.0):
    m, kdim = x.shape
    n = w.shape[2] if n is None else n
    assert m % tm == 0 and n % tn == 0 and scale_cols % tn == 0
    in_specs = [
        pl.BlockSpec((tm, kdim), lambda j, i: (i, 0)),
        pl.BlockSpec((None, kdim, tn), lambda j, i: (layer, 0, j)),
    ]
    args = [x, w]
    if residual is not None:
        in_specs.append(pl.BlockSpec((tm, tn), lambda j, i: (i, j)))
        args.append(residual)
    return pl.pallas_call(
        functools.partial(_mm_kernel, has_res=residual is not None, scale_tiles=scale_cols // tn, scale=scale),
        out_shape=jax.ShapeDtypeStruct((m, n), out_dtype),
        grid=(n // tn, m // tm),
        in_specs=in_specs,
        out_specs=pl.BlockSpec((tm, tn), lambda j, i: (i, j)),
        scratch_shapes=[pltpu.VMEM((kdim, tn), BF16)],
        compiler_params=_params(("parallel", "arbitrary")),
        name="matmul",
    )(*args)


def _retention_kernel(q_ref, k_ref, v_ref, g_ref, cos_ref, sin_ref, dmat_ref, xi_ref, zeta_ref,
                      gch_ref, gn_ref, o_ref, state_ref, *, nh, dk, dv):
    c = pl.program_id(1)

    @pl.when(c == 0)
    def _():
        state_ref[...] = jnp.zeros_like(state_ref)

    cos = cos_ref[...]
    sin = sin_ref[...]
    half = dk // 2

    def rot(u):
        u1, u2 = u[:, :half], u[:, half:]
        return jnp.concatenate([u1 * cos - u2 * sin, u1 * sin + u2 * cos], axis=1)

    row = lax.broadcasted_iota(jnp.int32, (CHUNK, 1), 0)
    valid = jnp.logical_or(row >= PAD, c > 0)
    for h in range(nh):
        q = rot(q_ref[:, h * dk:(h + 1) * dk].astype(F32))
        k = rot(k_ref[:, h * dk:(h + 1) * dk].astype(F32)) * (dk ** -0.5)
        k = jnp.where(valid, k, 0.0)
        v = v_ref[:, h * dv:(h + 1) * dv]
        vb = jnp.where(valid, v, jnp.zeros_like(v))
        qb = q.astype(BF16)
        kb = k.astype(BF16)

        scores = lax.dot_general(qb, kb, (((1,), (1,)), ((), ())), preferred_element_type=F32)
        scores = scores * dmat_ref[h]
        inner = jnp.dot(scores.astype(BF16), vb, preferred_element_type=F32)
        state = state_ref[h]
        cross = jnp.dot(qb, state.astype(BF16), preferred_element_type=F32) * xi_ref[h]
        y = inner + cross
        kz = (k * zeta_ref[h]).astype(BF16)
        state_ref[h] = gch_ref[h] * state + lax.dot_general(
            kz, vb, (((0,), (0,)), ((), ())), preferred_element_type=F32)

        mu = jnp.mean(y, axis=-1, keepdims=True)
        dlt = y - mu
        var = jnp.mean(dlt * dlt, axis=-1, keepdims=True)
        yn = dlt * lax.rsqrt(var + GN_EPS) * gn_ref[h]
        hg = 0.5 * g_ref[:, h * dv:(h + 1) * dv].astype(F32)
        o_ref[:, h * dv:(h + 1) * dv] = ((hg + hg * jnp.tanh(hg)) * yn).astype(o_ref.dtype)


def retention_core(proj, gn_gain, bsz, nc):
    tp = proj.shape[0]
    d = proj.shape[1] // 6
    nh = RET_HEADS
    dk = d // nh
    dv = 2 * d // nh
    lp = nc * CHUNK
    half = dk // 2
    pos = (jnp.arange(lp) - PAD).astype(F32)
    inv = ROPE_BASE ** (-jnp.arange(half, dtype=F32) / half)
    ang = pos[:, None] * inv[None, :]
    cos, sin = jnp.cos(ang), jnp.sin(ang)
    log_g = jnp.log1p(-jnp.exp2(-5.0 - jnp.arange(nh, dtype=F32)))
    idx = jnp.arange(CHUNK, dtype=F32)
    diff = idx[:, None] - idx[None, :]
    dmat = jnp.where(diff[None] >= 0, jnp.exp(jnp.maximum(diff, 0.0)[None] * log_g[:, None, None]), 0.0)
    xi = jnp.exp((idx + 1.0)[None, :] * log_g[:, None])[:, :, None]
    zeta = jnp.exp((CHUNK - 1.0 - idx)[None, :] * log_g[:, None])[:, :, None]
    g_chunk = jnp.exp(CHUNK * log_g)[:, None, None]
    return pl.pallas_call(
        functools.partial(_retention_kernel, nh=nh, dk=dk, dv=dv),
        out_shape=jax.ShapeDtypeStruct((tp, 2 * d), BF16),
        grid=(bsz, nc),
        in_specs=[
            pl.BlockSpec((CHUNK, d), lambda b, c: (b * nc + c, 0)),
            pl.BlockSpec((CHUNK, d), lambda b, c: (b * nc + c, 1)),
            pl.BlockSpec((CHUNK, 2 * d), lambda b, c: (b * nc + c, 1)),
            pl.BlockSpec((CHUNK, 2 * d), lambda b, c: (b * nc + c, 2)),
            pl.BlockSpec((CHUNK, half), lambda b, c: (c, 0)),
            pl.BlockSpec((CHUNK, half), lambda b, c: (c, 0)),
            pl.BlockSpec((nh, CHUNK, CHUNK), lambda b, c: (0, 0, 0)),
            pl.BlockSpec((nh, CHUNK, 1), lambda b, c: (0, 0, 0)),
            pl.BlockSpec((nh, CHUNK, 1), lambda b, c: (0, 0, 0)),
            pl.BlockSpec((nh, 1, 1), lambda b, c: (0, 0, 0)),
            pl.BlockSpec((nh, 1, dv), lambda b, c: (0, 0, 0)),
        ],
        out_specs=pl.BlockSpec((CHUNK, 2 * d), lambda b, c: (b * nc + c, 0)),
        scratch_shapes=[pltpu.VMEM((nh, dk, dv), F32)],
        compiler_params=_params(("parallel", "arbitrary")),
        name="retention_core",
    )(proj, proj, proj, proj, cos, sin, dmat, xi, zeta, g_chunk, gn_gain.reshape(nh, 1, dv))


LOG2E = 1.4426950408889634
MASK_BIG = 1e30
ONES_LANE = LANES - 1


def _bias_selectors(nh):
    h = jnp.arange(nh)
    selq = jnp.zeros((nh, 3 * LANES, LANES), F32)
    selk = jnp.zeros((nh, 3 * LANES, LANES), F32)
    for part in range(3):
        selq = selq.at[h, part * LANES + h, part].set(1.0)
        selk = selk.at[h, part * LANES + h, 3 + part].set(-1.0)
        selq = selq.at[h, ONES_LANE, 3 + part].set(1.0)
        selk = selk.at[h, ONES_LANE, part].set(1.0)
    return selq.astype(BF16), selk.astype(BF16)


def _forget_kernel(hn_ref, wf_ref, bf_ref, pq_ref, pk_ref, carry_ref, *, rows):
    i = pl.program_id(1)

    @pl.when(i == 0)
    def _():
        carry_ref[...] = jnp.zeros_like(carry_ref)

    z = jnp.dot(hn_ref[...], wf_ref[...], preferred_element_type=F32) + bf_ref[...]
    lf = jnp.minimum(z, 0.0) - jnp.log1p(jnp.exp(-jnp.abs(z)))
    row = lax.broadcasted_iota(jnp.int32, (rows, 1), 0)
    valid = i * rows + row >= PAD
    lf = jnp.where(valid, lf, 0.0)

    def split3(a):
        hi = a.astype(BF16)
        r1 = a - hi.astype(F32)
        mid = r1.astype(BF16)
        lo = (r1 - mid.astype(F32)).astype(BF16)
        return hi, mid, lo

    r_i = lax.broadcasted_iota(jnp.int32, (rows, rows), 0)
    c_i = lax.broadcasted_iota(jnp.int32, (rows, rows), 1)
    tri = (r_i >= c_i).astype(BF16)
    cs = sum(jnp.dot(tri, part, preferred_element_type=F32) for part in split3(lf))
    cs = cs + carry_ref[...]
    carry_ref[...] = cs[rows - 1:rows, :]

    hi, mid, lo = (p.astype(F32) for p in split3(cs * LOG2E))
    lane = lax.broadcasted_iota(jnp.int32, (rows, LANES), 1)
    ones = lane == ONES_LANE
    live = jnp.logical_and(valid, jnp.logical_not(ones))
    pq_ref[...] = jnp.concatenate(
        [jnp.where(ones, 1.0, hi), jnp.where(ones, 0.0, mid), jnp.where(ones, 0.0, lo)], axis=1).astype(BF16)
    pk_ref[...] = jnp.concatenate(
        [jnp.where(ones, 1.0, jnp.where(valid, hi, MASK_BIG)), jnp.where(live, mid, 0.0),
         jnp.where(live, lo, 0.0)], axis=1).astype(BF16)


def forget_gates(hn, w_f, b_f, bsz, lp, rows=384):
    tp, d = hn.shape
    nh = w_f.shape[1]
    steps = lp // rows
    wf = jnp.zeros((d, LANES), BF16).at[:, :nh].set(w_f.astype(BF16))
    bf = jnp.zeros((1, LANES), F32).at[0, :nh].set(b_f.astype(F32))
    return pl.pallas_call(
        functools.partial(_forget_kernel, rows=rows),
        out_shape=(jax.ShapeDtypeStruct((tp, 3 * LANES), BF16), jax.ShapeDtypeStruct((tp, 3 * LANES), BF16)),
        grid=(bsz, steps),
        in_specs=[
            pl.BlockSpec((rows, d), lambda b, i: (b * steps + i, 0)),
            pl.BlockSpec((d, LANES), lambda b, i: (0, 0)),
            pl.BlockSpec((1, LANES), lambda b, i: (0, 0)),
        ],
        out_specs=(
            pl.BlockSpec((rows, 3 * LANES), lambda b, i: (b * steps + i, 0)),
            pl.BlockSpec((rows, 3 * LANES), lambda b, i: (b * steps + i, 0)),
        ),
        scratch_shapes=[pltpu.VMEM((1, LANES), F32)],
        compiler_params=_params(("parallel", "arbitrary")),
        name="forget_gates",
    )(hn, wf, bf)


FOX_HEADS_PER_STEP = 4


FOX_STRIP = 32


def _fox_kernel(q_ref, pq_ref, k_ref, pk_ref, v_ref, selq_ref, selk_ref, o_ref,
                m_ref, l_ref, acc_ref, s_ref, p_ref, a_ref, ck_ref, *, tq, dh):
    i = pl.program_id(2)
    hs = FOX_HEADS_PER_STEP
    reps = tq // LANES
    m_ref[...] = jnp.full_like(m_ref, 10.0 * NEG_INF)
    l_ref[...] = jnp.zeros_like(l_ref)
    acc_ref[...] = jnp.zeros_like(acc_ref)

    @pl.when(i == 0)
    def _():
        for j in range(hs):
            ck_ref[:, j * LANES:(j + 1) * LANES] = jnp.dot(
                pk_ref[...], selk_ref[j], preferred_element_type=F32).astype(BF16)

    qa = [jnp.concatenate([q_ref[:, j * dh:(j + 1) * dh],
                           jnp.dot(pq_ref[...], selq_ref[j], preferred_element_type=F32).astype(BF16)], axis=1)
          for j in range(hs)]

    def scores(kb, buf):
        start = pl.multiple_of(kb * tq, tq)
        for j in range(hs):
            ka = jnp.concatenate([k_ref[pl.ds(start, tq), j * dh:(j + 1) * dh],
                                  ck_ref[pl.ds(start, tq), j * LANES:(j + 1) * LANES]], axis=1)
            s_ref[buf, j] = lax.dot_general(qa[j], ka, (((1,), (1,)), ((), ())), preferred_element_type=F32)

    def absorb(kb, buf, causal):
        start = pl.multiple_of(kb * tq, tq)
        for j in range(hs):
            for r0 in range(0, tq, FOX_STRIP):
                rows = slice(r0, r0 + FOX_STRIP)
                s = s_ref[buf, j, rows, :]
                if causal:
                    r_i = r0 + lax.broadcasted_iota(jnp.int32, (FOX_STRIP, tq), 0)
                    c_i = lax.broadcasted_iota(jnp.int32, (FOX_STRIP, tq), 1)
                    s = jnp.where(c_i <= r_i, s, NEG_INF)
                m_old = m_ref[j, rows, :]
                m_new = jnp.maximum(m_old, jnp.max(s, axis=1, keepdims=True))
                alpha = jnp.exp2(m_old - m_new)
                p = jnp.exp2(s - jnp.concatenate([m_new] * reps, axis=1))
                l_ref[j, rows, :] = alpha * l_ref[j, rows, :] + jnp.sum(p, axis=1, keepdims=True)
                m_ref[j, rows, :] = m_new
                a_ref[j, rows, :] = alpha
                p_ref[j, rows, :] = p.astype(BF16)
        for j in range(hs):
            acc_ref[j] = a_ref[j] * acc_ref[j] + jnp.dot(
                p_ref[j], v_ref[pl.ds(start, tq), j * dh:(j + 1) * dh], preferred_element_type=F32)

    def body(kb, carry):
        scores(kb, 0)
        absorb(kb, 0, False)
        return carry

    lax.fori_loop(0, i, body, 0)
    scores(i, 0)
    absorb(i, 0, True)

    for j in range(hs):
        o_ref[:, j * dh:(j + 1) * dh] = (acc_ref[j] / l_ref[j]).astype(o_ref.dtype)


def fox_attention(qkv, pq, pk, bsz, lp, tq=384):
    tp = qkv.shape[0]
    d = qkv.shape[1] // 3
    nh = FOX_HEADS
    dh = d // nh
    assert dh == LANES
    nq = lp // tq
    hs = FOX_HEADS_PER_STEP
    ng = nh // hs
    qkv3 = qkv.reshape(bsz, lp, 3 * d)
    pk3 = pk.reshape(bsz, lp, 3 * LANES)
    selq, selk = _bias_selectors(nh)
    return pl.pallas_call(
        functools.partial(_fox_kernel, tq=tq, dh=dh),
        out_shape=jax.ShapeDtypeStruct((tp, d), BF16),
        grid=(bsz, ng, nq),
        in_specs=[
            pl.BlockSpec((tq, hs * dh), lambda b, g, i: (b * nq + i, g)),
            pl.BlockSpec((tq, 3 * LANES), lambda b, g, i: (b * nq + i, 0)),
            pl.BlockSpec((None, lp, hs * dh), lambda b, g, i: (b, 0, ng + g)),
            pl.BlockSpec((None, lp, 3 * LANES), lambda b, g, i: (b, 0, 0)),
            pl.BlockSpec((None, lp, hs * dh), lambda b, g, i: (b, 0, 2 * ng + g)),
            pl.BlockSpec((hs, 3 * LANES, LANES), lambda b, g, i: (g, 0, 0)),
            pl.BlockSpec((hs, 3 * LANES, LANES), lambda b, g, i: (g, 0, 0)),
        ],
        out_specs=pl.BlockSpec((tq, hs * dh), lambda b, g, i: (b * nq + i, g)),
        scratch_shapes=[pltpu.VMEM((hs, tq, LANES), F32), pltpu.VMEM((hs, tq, LANES), F32),
                        pltpu.VMEM((hs, tq, dh), F32), pltpu.VMEM((1, hs, tq, tq), F32),
                        pltpu.VMEM((hs, tq, tq), BF16), pltpu.VMEM((hs, tq, LANES), F32),
                        pltpu.VMEM((lp, hs * LANES), BF16)],
        compiler_params=_params(("parallel", "parallel", "arbitrary")),
        name="fox_attention",
    )(qkv, pq, qkv3, pk3, qkv3, selq, selk)


def _router_kernel(h_ref, g_ref, wr_ref, br_ref, hn_ref, idx_ref, w_ref, cnt_ref, carry_ref, *, tr):
    i = pl.program_id(0)

    @pl.when(i == 0)
    def _():
        carry_ref[...] = jnp.zeros_like(carry_ref)

    hn = _rms(h_ref[...], g_ref[...])
    _store_token_tiles(hn_ref, (), _pack_halves(hn))
    lg = lax.dot_general(wr_ref[...], hn.astype(BF16), (((1,), (1,)), ((), ())),
                         preferred_element_type=F32) + br_ref[...]
    row = lax.broadcasted_iota(jnp.int32, (ROUTER_ROWS, tr), 0)
    big = jnp.int32(1 << 20)
    is_g = row < N_GROUPS
    mg = jnp.max(jnp.where(is_g, lg, -jnp.inf), axis=0, keepdims=True)
    g_sel = jnp.min(jnp.where(jnp.logical_and(is_g, lg == mg), row, big), axis=0, keepdims=True)
    sg = jnp.sum(jnp.where(is_g, jnp.exp(lg - mg), 0.0), axis=0, keepdims=True)
    p_g = 1.0 / sg
    lo = EXPERT_ROW0 + EXPERTS_PER_GROUP * g_sel
    is_e = jnp.logical_and(row >= lo, row < lo + EXPERTS_PER_GROUP)
    me = jnp.max(jnp.where(is_e, lg, -jnp.inf), axis=0, keepdims=True)
    ee = jnp.where(is_e, jnp.exp(lg - me), 0.0)
    pe = ee / jnp.sum(ee, axis=0, keepdims=True)
    pe1 = jnp.where(is_e, pe, -1.0)
    m1 = jnp.max(pe1, axis=0, keepdims=True)
    i1 = jnp.min(jnp.where(pe1 == m1, row, big), axis=0, keepdims=True)
    pe2 = jnp.where(row == i1, -1.0, pe1)
    m2 = jnp.max(pe2, axis=0, keepdims=True)
    i2 = jnp.min(jnp.where(pe2 == m2, row, big), axis=0, keepdims=True)
    den = m1 + m2
    w1 = p_g * (m1 / den)
    w2 = p_g * (m2 / den)

    oh1 = row == i1
    oh2 = row == i2
    ohs = jnp.logical_or(oh1, oh2).astype(F32)
    r_i = lax.broadcasted_iota(jnp.int32, (tr, tr), 0)
    c_i = lax.broadcasted_iota(jnp.int32, (tr, tr), 1)
    tri = (r_i < c_i).astype(BF16)
    cnt = jnp.dot(ohs.astype(BF16), tri, preferred_element_type=F32) + carry_ref[...]
    rank1 = jnp.sum(jnp.where(oh1, cnt, 0.0), axis=0, keepdims=True)
    rank2 = jnp.sum(jnp.where(oh2, cnt, 0.0), axis=0, keepdims=True)
    carry_ref[...] += jnp.sum(ohs, axis=1, keepdims=True)

    r8 = lax.broadcasted_iota(jnp.int32, (8, tr), 0)
    e1 = i1 - EXPERT_ROW0
    e2 = i2 - EXPERT_ROW0
    idx_ref[...] = jnp.where(r8 == 0, e1, jnp.where(r8 == 1, e2, jnp.where(
        r8 == 2, rank1.astype(jnp.int32), jnp.where(r8 == 3, rank2.astype(jnp.int32), 0))))
    w_ref[...] = jnp.where(r8 == 0, w1, jnp.where(r8 == 1, w2, 0.0))
    cnt_ref[...] = jnp.broadcast_to(carry_ref[...], cnt_ref.shape)


def moe_router(h, gain, w_rg, b_rg, w_re, b_re, tr=512):
    tp, d = h.shape
    assert d // 2 == SUBLANES * LANES
    wr = jnp.zeros((ROUTER_ROWS, d), BF16)
    wr = wr.at[:N_GROUPS].set(w_rg.T.astype(BF16))
    wr = wr.at[EXPERT_ROW0:EXPERT_ROW0 + N_EXPERTS].set(w_re.reshape(d, N_EXPERTS).T.astype(BF16))
    br = jnp.zeros((ROUTER_ROWS, 1), F32)
    br = br.at[:N_GROUPS, 0].set(b_rg.astype(F32))
    br = br.at[EXPERT_ROW0:EXPERT_ROW0 + N_EXPERTS, 0].set(b_re.reshape(N_EXPERTS).astype(F32))
    return pl.pallas_call(
        functools.partial(_router_kernel, tr=tr),
        out_shape=(
            jax.ShapeDtypeStruct((tp * SUBLANES, LANES), jnp.uint32),
            jax.ShapeDtypeStruct((8, tp), jnp.int32),
            jax.ShapeDtypeStruct((8, tp), F32),
            jax.ShapeDtypeStruct((ROUTER_ROWS, LANES), F32),
        ),
        grid=(tp // tr,),
        in_specs=[
            pl.BlockSpec((tr, d), lambda i: (i, 0)),
            pl.BlockSpec((1, d), lambda i: (0, 0)),
            pl.BlockSpec((ROUTER_ROWS, d), lambda i: (0, 0)),
            pl.BlockSpec((ROUTER_ROWS, 1), lambda i: (0, 0)),
        ],
        out_specs=(
            pl.BlockSpec((tr * SUBLANES, LANES), lambda i: (i, 0)),
            pl.BlockSpec((8, tr), lambda i: (0, i)),
            pl.BlockSpec((8, tr), lambda i: (0, i)),
            pl.BlockSpec((ROUTER_ROWS, LANES), lambda i: (0, 0)),
        ),
        scratch_shapes=[pltpu.VMEM((ROUTER_ROWS, 1), F32)],
        compiler_params=_params(("arbitrary",)),
        name="moe_router",
    )(h, gain.reshape(1, d), wr, br)


def _slots_kernel(cnt_ref, idx_ref, pos_ref, meta_ref):
    e1 = idx_ref[0:1, :]
    e2 = idx_ref[1:2, :]
    off1 = jnp.zeros_like(e1)
    off2 = jnp.zeros_like(e2)
    visit = lax.broadcasted_iota(jnp.int32, (1, meta_ref.shape[1]), 1)
    v_tile = jnp.zeros_like(visit)
    v_expert = jnp.zeros_like(visit)
    v_lo = jnp.zeros_like(visit)
    v_hi = jnp.zeros_like(visit)
    v_next = jnp.zeros_like(visit)
    v_start = jnp.zeros_like(visit)
    v_count = jnp.zeros_like(visit)
    next_live = [None] * N_EXPERTS
    nxt = jnp.int32(-1)
    for e in reversed(range(N_EXPERTS)):
        next_live[e] = jnp.where(nxt >= 0, nxt, e)
        nxt = jnp.where(cnt_ref[e] > 0, e, nxt)
    start = jnp.int32(0)
    v_base = jnp.int32(0)
    for e in range(N_EXPERTS):
        off1 = jnp.where(e1 == e, start, off1)
        off2 = jnp.where(e2 == e, start, off2)
        n = cnt_ref[e]
        end = start + n
        first_tile = start // TILE_M
        n_visits = jnp.where(n > 0, (jnp.maximum(end, 1) - 1) // TILE_M - first_tile + 1, 0)
        mine = jnp.logical_and(visit >= v_base, visit < v_base + n_visits)
        row0 = (first_tile + visit - v_base) * TILE_M
        v_tile = jnp.where(mine, first_tile + visit - v_base, v_tile)
        v_expert = jnp.where(mine, e, v_expert)
        v_next = jnp.where(mine, next_live[e], v_next)
        v_start = jnp.where(mine, v_base, v_start)
        v_count = jnp.where(mine, n_visits, v_count)
        v_lo = jnp.where(mine, jnp.maximum(start - row0, 0), v_lo)
        v_hi = jnp.where(mine, jnp.minimum(end - row0, TILE_M), v_hi)
        start = end
        v_base = v_base + n_visits
    r8 = lax.broadcasted_iota(jnp.int32, pos_ref.shape, 0)
    pos_ref[...] = jnp.where(r8 == 0, off1 + idx_ref[2:3, :], jnp.where(r8 == 1, off2 + idx_ref[3:4, :], 0))
    m8 = lax.broadcasted_iota(jnp.int32, meta_ref.shape, 0)
    rows = (v_tile, v_expert, v_lo, v_hi, v_base, v_next, v_start, v_count)
    meta = rows[-1]
    for r in reversed(range(len(rows) - 1)):
        meta = jnp.where(m8 == r, rows[r], meta)
    meta_ref[...] = meta


def moe_slots(counts, idx, tr=512):
    tp = idx.shape[1]
    nt_lanes = 2 * LANES
    return pl.pallas_call(
        _slots_kernel,
        out_shape=(jax.ShapeDtypeStruct((8, tp), jnp.int32), jax.ShapeDtypeStruct((8, nt_lanes), jnp.int32)),
        grid_spec=pltpu.PrefetchScalarGridSpec(
            num_scalar_prefetch=1,
            grid=(tp // tr,),
            in_specs=[pl.BlockSpec((8, tr), lambda i, c: (0, i))],
            out_specs=(
                pl.BlockSpec((8, tr), lambda i, c: (0, i)),
                pl.BlockSpec((8, nt_lanes), lambda i, c: (0, 0)),
            ),
        ),
        compiler_params=_params(("arbitrary",)),
        name="moe_slots",
    )(counts, idx)


def _invert_kernel(pos_ref, code_ref, *, tp):
    def per_token(t, carry):
        code_ref[pos_ref[t]] = 2 * t
        code_ref[pos_ref[tp + t]] = 2 * t + 1
        return carry

    lax.fori_loop(0, tp, per_token, 0, unroll=8)


def moe_invert(pos_flat):
    tp = pos_flat.shape[0] // 2
    return pl.pallas_call(
        functools.partial(_invert_kernel, tp=tp),
        out_shape=jax.ShapeDtypeStruct((2 * tp,), jnp.int32),
        grid_spec=pltpu.PrefetchScalarGridSpec(
            num_scalar_prefetch=1,
            grid=(1,),
            in_specs=[],
            out_specs=pl.BlockSpec(memory_space=pltpu.SMEM),
        ),
        compiler_params=_params(("arbitrary",)),
        name="moe_invert",
    )(pos_flat)


def _expert_kernel(vt_ref, ve_ref, vlo_ref, vhi_ref, nv_ref, vnext_ref, vstart_ref, vcount_ref, code_ref,
                   hn_ref, wg_hbm, wu_hbm, wd_hbm, y_ref,
                   xbuf, xcur, yacc, wgs, wus, wds, wgb, wub, wdb, gsem, wsem, *, n_tiles, layer):
    v = pl.program_id(0)
    nv = nv_ref[0]
    t = vt_ref[v]
    prev_v = jnp.maximum(v - 1, 0)
    first = jnp.logical_or(v == 0, vt_ref[prev_v] != t)
    new_expert = jnp.logical_or(v == 0, ve_ref[prev_v] != ve_ref[v])

    def weight_copies(e):
        w = layer * N_EXPERTS + e
        return (pltpu.make_async_copy(wg_hbm.at[w], wgs, wsem.at[0]),
                pltpu.make_async_copy(wu_hbm.at[w], wus, wsem.at[1]),
                pltpu.make_async_copy(wd_hbm.at[w], wds, wsem.at[2]))

    def row_copy(tile, r):
        tok = lax.shift_right_logical(code_ref[tile * TILE_M + r], 1)
        return pltpu.make_async_copy(hn_ref.at[_token_tile(tok)], xbuf.at[_token_tile(r)], gsem)

    def gather_wait():
        pltpu.make_async_copy(hn_ref.at[pl.ds(0, TILE_M * SUBLANES)], xbuf, gsem).wait()

    def compute(accumulate):
        x = xcur[...]
        a = jnp.dot(x, wgb[...], preferred_element_type=F32)
        u = jnp.dot(x, wub[...], preferred_element_type=F32)
        row = lax.broadcasted_iota(jnp.int32, (TILE_M, 1), 0)
        mine = jnp.logical_and(row >= vlo_ref[v], row < vhi_ref[v])
        hid = jnp.where(mine, a * _sigmoid(a) * u, 0.0).astype(BF16)
        y = jnp.dot(hid, wdb[...], preferred_element_type=F32)
        if accumulate:
            y = y + yacc[...]
        yacc[...] = y
        _store_token_tiles(y_ref, (), _pack_halves(y))

    @pl.when(v == 0)
    def _():
        def step(r, carry):
            row_copy(0, r).start()
            return carry
        lax.fori_loop(0, TILE_M, step, 0, unroll=8)
        for copy in weight_copies(ve_ref[0]):
            copy.start()

    @pl.when(v < nv)
    def _():
        @pl.when(new_expert)
        def _():
            for copy in weight_copies(ve_ref[v]):
                copy.wait()
            wgb[...] = wgs[...].astype(BF16)
            wub[...] = wus[...].astype(BF16)
            wdb[...] = wds[...].astype(BF16)

        @pl.when(first)
        def _():
            gather_wait()
            xcur[...] = _unpack_halves(_load_token_tiles(xbuf, (), 0, TILE_M)).astype(BF16)
            ahead = jnp.minimum(t + 1, n_tiles - 1)
            for r in range(TILE_M):
                row_copy(ahead, r).start()
            compute(False)

        @pl.when(jnp.logical_not(first))
        def _():
            compute(True)

        k = v - vstart_ref[v]
        last_of_expert = k == vcount_ref[v] - 1
        has_next = vnext_ref[v] != ve_ref[v]
        for c, copy in enumerate(weight_copies(vnext_ref[v])):
            @pl.when(jnp.logical_and(has_next, jnp.logical_or(k == c, jnp.logical_and(last_of_expert, k < c))))
            def _():
                copy.start()

        @pl.when(v == nv - 1)
        def _():
            gather_wait()


def moe_experts(meta, code, hn, w_gate, w_up, w_down, layer):
    tp = hn.shape[0] // SUBLANES
    d = 2 * SUBLANES * LANES
    f = w_gate.shape[-1]
    assert (2 * tp) % TILE_M == 0
    n_tiles = (2 * tp) // TILE_M
    max_visits = n_tiles + N_EXPERTS - 1

    def y_map(v, vt, ve, vlo, vhi, nv, *_):
        return (vt[jnp.minimum(v, nv[0] - 1)], 0)

    any_spec = pl.BlockSpec(memory_space=pl.ANY)
    return pl.pallas_call(
        functools.partial(_expert_kernel, n_tiles=n_tiles, layer=layer),
        out_shape=jax.ShapeDtypeStruct((2 * tp * SUBLANES, LANES), jnp.uint32),
        grid_spec=pltpu.PrefetchScalarGridSpec(
            num_scalar_prefetch=9,
            grid=(max_visits,),
            in_specs=[any_spec, any_spec, any_spec, any_spec],
            out_specs=pl.BlockSpec((TILE_M * SUBLANES, LANES), y_map),
            scratch_shapes=[
                pltpu.VMEM((TILE_M * SUBLANES, LANES), jnp.uint32), pltpu.VMEM((TILE_M, d), BF16),
                pltpu.VMEM((TILE_M, d), F32),
                pltpu.VMEM((d, f), F32), pltpu.VMEM((d, f), F32), pltpu.VMEM((f, d), F32),
                pltpu.VMEM((d, f), BF16), pltpu.VMEM((d, f), BF16), pltpu.VMEM((f, d), BF16),
                pltpu.SemaphoreType.DMA, pltpu.SemaphoreType.DMA((3,)),
            ],
        ),
        compiler_params=_params(("arbitrary",), vmem=56 * 1024 * 1024),
        name="moe_experts",
    )(meta[0, :max_visits], meta[1, :max_visits], meta[2, :max_visits], meta[3, :max_visits], meta[4, :1],
      meta[5, :max_visits], meta[6, :max_visits], meta[7, :max_visits], code, hn, w_gate, w_up, w_down)


def _combine_kernel(pos_ref, h_ref, w_ref, g_ref, y_ref, *rest, tc, tp, n_steps, write_h):
    out_refs, (ybuf, sem) = rest[:-2], rest[-2:]
    i = pl.program_id(0)
    slot = lax.rem(i, 2)

    def row_copy(tile, buf, s, r):
        p = pos_ref[s * tp + tile * tc + r]
        return pltpu.make_async_copy(y_ref.at[_token_tile(p)], ybuf.at[buf, _token_tile(s * tc + r)],
                                     sem.at[buf])

    @pl.when(i == 0)
    def _():
        def step(r, carry):
            row_copy(0, 0, 0, r).start()
            row_copy(0, 0, 1, r).start()
            return carry
        lax.fori_loop(0, tc, step, 0, unroll=8)

    pltpu.make_async_copy(y_ref.at[pl.ds(0, 2 * tc * SUBLANES)], ybuf.at[slot], sem.at[slot]).wait()

    @pl.when(i + 1 < n_steps)
    def _():
        for r in range(tc):
            row_copy(i + 1, 1 - slot, 0, r).start()
            row_copy(i + 1, 1 - slot, 1, r).start()

    w = w_ref[...]
    h = (h_ref[...] + w[:, 0:1] * _unpack_halves(_load_token_tiles(ybuf, (slot,), 0, tc))
         + w[:, 1:2] * _unpack_halves(_load_token_tiles(ybuf, (slot,), tc, tc)))
    if write_h:
        out_refs[0][...] = h
    out_refs[-1][...] = _rms(h, g_ref[...]).astype(out_refs[-1].dtype)


def moe_combine(pos_flat, h, w_col, y, gain, hn_dtype, final_shape=None):
    tp, d = h.shape
    if final_shape is None:
        tc = 256
        out_shape = (jax.ShapeDtypeStruct((tp, d), F32), jax.ShapeDtypeStruct((tp, d), hn_dtype))
        out_specs = (pl.BlockSpec((tc, d), lambda i, p: (i, 0)), pl.BlockSpec((tc, d), lambda i, p: (i, 0)))
    else:
        tc = CHUNK
        bsz, seq, _ = final_shape
        nc = tp // bsz // tc
        out_shape = (jax.ShapeDtypeStruct(final_shape, hn_dtype),)
        out_specs = (pl.BlockSpec((None, tc, d), lambda i, p: (i // nc, jnp.maximum(i % nc - 1, 0), 0)),)
    n_steps = tp // tc
    return pl.pallas_call(
        functools.partial(_combine_kernel, tc=tc, tp=tp, n_steps=n_steps, write_h=final_shape is None),
        out_shape=out_shape,
        grid_spec=pltpu.PrefetchScalarGridSpec(
            num_scalar_prefetch=1,
            grid=(n_steps,),
            in_specs=[
                pl.BlockSpec((tc, d), lambda i, p: (i, 0)),
                pl.BlockSpec((tc, 2), lambda i, p: (i, 0)),
                pl.BlockSpec((1, d), lambda i, p: (0, 0)),
                pl.BlockSpec(memory_space=pl.ANY),
            ],
            out_specs=out_specs,
            scratch_shapes=[pltpu.VMEM((2, 2 * tc * SUBLANES, LANES), jnp.uint32),
                            pltpu.SemaphoreType.DMA((2,))],
        ),
        compiler_params=_params(("arbitrary",)),
        name="moe_combine",
    )(pos_flat, h, w_col, gain.reshape(1, d), y)


def hierarchical_moe(h, gain, w_rg, b_rg, w_re, b_re, w_gate, w_up, w_down, layer, next_gain, hn_dtype,
                     final_shape=None):
    tp, d = h.shape
    f = w_gate.shape[-1]
    hn, idx, w_rows, cnt = moe_router(h, gain, w_rg, b_rg, w_re, b_re)
    counts = cnt[EXPERT_ROW0:EXPERT_ROW0 + N_EXPERTS, 0].astype(jnp.int32)
    pos, meta = moe_slots(counts, idx)
    pos_flat = pos[:2].reshape(2 * tp)
    code = moe_invert(pos_flat)
    y = moe_experts(meta, code, hn, w_gate.reshape(-1, d, f), w_up.reshape(-1, d, f),
                    w_down.reshape(-1, f, d), layer)
    return moe_combine(pos_flat, h, w_rows[:2].T, y, next_gain, hn_dtype, final_shape)


def kernel(x, meta_tokens, norm_mixer, norm_ffn, norm_final, ret_w_in, ret_gn, ret_w_out,
           fox_w_in, fox_b_f, fox_w_out, moe_w_rg, moe_b_rg, moe_w_re, moe_b_re,
           moe_w_gate, moe_w_up, moe_w_down):
    bsz, seq, d = x.shape
    depth = norm_mixer.shape[0]
    nc = (seq + CHUNK) // CHUNK
    lp = nc * CHUNK
    h, hn = embed_norm(x, meta_tokens.astype(x.dtype), norm_mixer[0], BF16)
    for i in range(depth):
        j = i // 2
        if i % 2 == 0:
            proj = matmul(hn, ret_w_in, j, BF16, tm=1536)
            gated = retention_core(proj, ret_gn[j], bsz, nc)
            h = matmul(gated, ret_w_out, j, F32, residual=h, tn=512)
        else:
            qkv = matmul(hn, fox_w_in, j, BF16, n=3 * d, tm=1536, scale_cols=d,
                         scale=(d // FOX_HEADS) ** -0.5 * LOG2E)
            pq, pk = forget_gates(hn, fox_w_in[j, :, 3 * d:], fox_b_f[j], bsz, lp)
            o = fox_attention(qkv, pq, pk, bsz, lp)
            h = matmul(o, fox_w_out, j, F32, residual=h, tm=1536, tn=512)
        last = i == depth - 1
        outs = hierarchical_moe(h, norm_ffn[i], moe_w_rg[i], moe_b_rg[i], moe_w_re[i], moe_b_re[i],
                                moe_w_gate, moe_w_up, moe_w_down, i,
                                norm_final if last else norm_mixer[i + 1],
                                F32 if last else BF16, (bsz, seq, d) if last else None)
        if last:
            return outs[0]
        h, hn = outs
```

```python
import functools

import jax
import jax.numpy as jnp
from jax import lax
from jax.experimental import pallas as pl
from jax.experimental.pallas import tpu as pltpu

N_META = 16
CHUNK = 128
PAD = CHUNK - N_META
RMS_EPS = 1e-6
GN_EPS = 1e-6
NEG_INF = -1e30
RET_HEADS = 8
FOX_HEADS = 16
N_GROUPS = 4
EXPERTS_PER_GROUP = 8
N_EXPERTS = N_GROUPS * EXPERTS_PER_GROUP
ROPE_BASE = 10000.0

LANES = 128
ROUTER_ROWS = 48
EXPERT_ROW0 = N_GROUPS
TILE_M = 256
VMEM_LIMIT = 48 * 1024 * 1024

F32 = jnp.float32
BF16 = jnp.bfloat16


def _params(sem, vmem=VMEM_LIMIT):
    return pltpu.CompilerParams(dimension_semantics=sem, vmem_limit_bytes=vmem)


def _rms(h, g):
    return h * lax.rsqrt(jnp.mean(h * h, axis=-1, keepdims=True) + RMS_EPS) * g


def _sigmoid(x):
    return 1.0 / (1.0 + jnp.exp(-x))


def _pack_halves(x):
    half = x.shape[1] // 2
    return pltpu.pack_elementwise([x[:, :half], x[:, half:]], packed_dtype=BF16)


def _unpack_halves(p):
    lo = pltpu.unpack_elementwise(p, index=0, packed_dtype=BF16, unpacked_dtype=F32)
    hi = pltpu.unpack_elementwise(p, index=1, packed_dtype=BF16, unpacked_dtype=F32)
    return jnp.concatenate([lo, hi], axis=1)


SUBLANES = 8


def _token_tile(t):
    start = t * SUBLANES
    return pl.ds(start if isinstance(t, int) else pl.multiple_of(start, SUBLANES), SUBLANES)


def _store_token_tiles(ref, index, packed):
    m = packed.shape[0]
    for s in range(SUBLANES):
        ref[index + (pl.ds(s, m, stride=SUBLANES), slice(None))] = packed[:, s * LANES:(s + 1) * LANES]


def _load_token_tiles(ref, index, first_token, m):
    return jnp.concatenate(
        [ref[index + (pl.ds(first_token * SUBLANES + s, m, stride=SUBLANES), slice(None))]
         for s in range(SUBLANES)], axis=1)


EMBED_CHUNKS = 3


def _embed_norm_kernel(*refs):
    x_refs, (meta_ref, g_ref, h_ref, hn_ref) = refs[:EMBED_CHUNKS], refs[EMBED_CHUNKS:]
    i = pl.program_id(1)
    for c, x_ref in enumerate(x_refs):
        h_ref[c * CHUNK:(c + 1) * CHUNK, :] = x_ref[...]

    @pl.when(i == 0)
    def _():
        h_ref[:PAD, :] = jnp.zeros((PAD, h_ref.shape[1]), h_ref.dtype)
        h_ref[PAD:CHUNK, :] = meta_ref[...]

    hn_ref[...] = _rms(h_ref[...], g_ref[...]).astype(hn_ref.dtype)


def embed_norm(x, meta, gain, hn_dtype):
    bsz, seq, d = x.shape
    nc = (seq + CHUNK) // CHUNK
    assert nc % EMBED_CHUNKS == 0
    tp = bsz * nc * CHUNK
    steps = nc // EMBED_CHUNKS
    rows = EMBED_CHUNKS * CHUNK

    def x_spec(c):
        return pl.BlockSpec((None, CHUNK, d), lambda b, i: (b, jnp.maximum(EMBED_CHUNKS * i + c - 1, 0), 0))

    return pl.pallas_call(
        _embed_norm_kernel,
        out_shape=(jax.ShapeDtypeStruct((tp, d), F32), jax.ShapeDtypeStruct((tp, d), hn_dtype)),
        grid=(bsz, steps),
        in_specs=[x_spec(c) for c in range(EMBED_CHUNKS)] + [
            pl.BlockSpec((N_META, d), lambda b, i: (0, 0)),
            pl.BlockSpec((1, d), lambda b, i: (0, 0)),
        ],
        out_specs=(
            pl.BlockSpec((rows, d), lambda b, i: (b * steps + i, 0)),
            pl.BlockSpec((rows, d), lambda b, i: (b * steps + i, 0)),
        ),
        compiler_params=_params(("parallel", "parallel")),
        name="embed_norm",
    )(*([x] * EMBED_CHUNKS), meta, gain.reshape(1, d))


def _mm_kernel(*refs, has_res, scale_tiles, scale):
    if has_res:
        x_ref, w_ref, r_ref, o_ref, wb_ref = refs
    else:
        x_ref, w_ref, o_ref, wb_ref = refs

    @pl.when(pl.program_id(1) == 0)
    def _():
        wb_ref[...] = w_ref[...].astype(BF16)

    acc = jnp.dot(x_ref[...], wb_ref[...], preferred_element_type=F32)
    if scale_tiles:
        acc = acc * jnp.where(pl.program_id(0) < scale_tiles, scale, 1.0)
    if has_res:
        acc = acc + r_ref[...]
    o_ref[...] = acc.astype(o_ref.dtype)


def matmul(x, w, layer, out_dtype, n=None, residual=None, tm=512, tn=1024, scale_cols=0, scale=1.0):
    m, kdim = x.shape
    n = w.shape[2] if n is None else n
    assert m % tm == 0 and n % tn == 0 and scale_cols % tn == 0
    in_specs = [
        pl.BlockSpec((tm, kdim), lambda j, i: (i, 0)),
        pl.BlockSpec((None, kdim, tn), lambda j, i: (layer, 0, j)),
    ]
    args = [x, w]
    if residual is not None:
        in_specs.append(pl.BlockSpec((tm, tn), lambda j, i: (i, j)))
        args.append(residual)
    return pl.pallas_call(
        functools.partial(_mm_kernel, has_res=residual is not None, scale_tiles=scale_cols // tn, scale=scale),
        out_shape=jax.ShapeDtypeStruct((m, n), out_dtype),
        grid=(n // tn, m // tm),
        in_specs=in_specs,
        out_specs=pl.BlockSpec((tm, tn), lambda j, i: (i, j)),
        scratch_shapes=[pltpu.VMEM((kdim, tn), BF16)],
        compiler_params=_params(("parallel", "arbitrary")),
        name="matmul",
    )(*args)


def _retention_kernel(q_ref, k_ref, v_ref, g_ref, cos_ref, sin_ref, dmat_ref, xi_ref, zeta_ref,
                      gch_ref, gn_ref, o_ref, state_ref, *, nh, dk, dv):
    c = pl.program_id(1)

    @pl.when(c == 0)
    def _():
        state_ref[...] = jnp.zeros_like(state_ref)

    cos = cos_ref[...]
    sin = sin_ref[...]
    half = dk // 2

    def rot(u):
        u1, u2 = u[:, :half], u[:, half:]
        return jnp.concatenate([u1 * cos - u2 * sin, u1 * sin + u2 * cos], axis=1)

    row = lax.broadcasted_iota(jnp.int32, (CHUNK, 1), 0)
    valid = jnp.logical_or(row >= PAD, c > 0)
    for h in range(nh):
        q = rot(q_ref[:, h * dk:(h + 1) * dk].astype(F32))
        k = rot(k_ref[:, h * dk:(h + 1) * dk].astype(F32)) * (dk ** -0.5)
        k = jnp.where(valid, k, 0.0)
        v = v_ref[:, h * dv:(h + 1) * dv]
        vb = jnp.where(valid, v, jnp.zeros_like(v))
        qb = q.astype(BF16)
        kb = k.astype(BF16)

        scores = lax.dot_general(qb, kb, (((1,), (1,)), ((), ())), preferred_element_type=F32)
        scores = scores * dmat_ref[h]
        inner = jnp.dot(scores.astype(BF16), vb, preferred_element_type=F32)
        state = state_ref[h]
        cross = jnp.dot(qb, state.astype(BF16), preferred_element_type=F32) * xi_ref[h]
        y = inner + cross
        kz = (k * zeta_ref[h]).astype(BF16)
        state_ref[h] = gch_ref[h] * state + lax.dot_general(
            kz, vb, (((0,), (0,)), ((), ())), preferred_element_type=F32)

        mu = jnp.mean(y, axis=-1, keepdims=True)
        dlt = y - mu
        var = jnp.mean(dlt * dlt, axis=-1, keepdims=True)
        yn = dlt * lax.rsqrt(var + GN_EPS) * gn_ref[h]
        hg = 0.5 * g_ref[:, h * dv:(h + 1) * dv].astype(F32)
        o_ref[:, h * dv:(h + 1) * dv] = ((hg + hg * jnp.tanh(hg)) * yn).astype(o_ref.dtype)


def retention_core(proj, gn_gain, bsz, nc):
    tp = proj.shape[0]
    d = proj.shape[1] // 6
    nh = RET_HEADS
    dk = d // nh
    dv = 2 * d // nh
    lp = nc * CHUNK
    half = dk // 2
    pos = (jnp.arange(lp) - PAD).astype(F32)
    inv = ROPE_BASE ** (-jnp.arange(half, dtype=F32) / half)
    ang = pos[:, None] * inv[None, :]
    cos, sin = jnp.cos(ang), jnp.sin(ang)
    log_g = jnp.log1p(-jnp.exp2(-5.0 - jnp.arange(nh, dtype=F32)))
    idx = jnp.arange(CHUNK, dtype=F32)
    diff = idx[:, None] - idx[None, :]
    dmat = jnp.where(diff[None] >= 0, jnp.exp(jnp.maximum(diff, 0.0)[None] * log_g[:, None, None]), 0.0)
    xi = jnp.exp((idx + 1.0)[None, :] * log_g[:, None])[:, :, None]
    zeta = jnp.exp((CHUNK - 1.0 - idx)[None, :] * log_g[:, None])[:, :, None]
    g_chunk = jnp.exp(CHUNK * log_g)[:, None, None]
    return pl.pallas_call(
        functools.partial(_retention_kernel, nh=nh, dk=dk, dv=dv),
        out_shape=jax.ShapeDtypeStruct((tp, 2 * d), BF16),
        grid=(bsz, nc),
        in_specs=[
            pl.BlockSpec((CHUNK, d), lambda b, c: (b * nc + c, 0)),
            pl.BlockSpec((CHUNK, d), lambda b, c: (b * nc + c, 1)),
            pl.BlockSpec((CHUNK, 2 * d), lambda b, c: (b * nc + c, 1)),
            pl.BlockSpec((CHUNK, 2 * d), lambda b, c: (b * nc + c, 2)),
            pl.BlockSpec((CHUNK, half), lambda b, c: (c, 0)),
            pl.BlockSpec((CHUNK, half), lambda b, c: (c, 0)),
            pl.BlockSpec((nh, CHUNK, CHUNK), lambda b, c: (0, 0, 0)),
            pl.BlockSpec((nh, CHUNK, 1), lambda b, c: (0, 0, 0)),
            pl.BlockSpec((nh, CHUNK, 1), lambda b, c: (0, 0, 0)),
            pl.BlockSpec((nh, 1, 1), lambda b, c: (0, 0, 0)),
            pl.BlockSpec((nh, 1, dv), lambda b, c: (0, 0, 0)),
        ],
        out_specs=pl.BlockSpec((CHUNK, 2 * d), lambda b, c: (b * nc + c, 0)),
        scratch_shapes=[pltpu.VMEM((nh, dk, dv), F32)],
        compiler_params=_params(("parallel", "arbitrary")),
        name="retention_core",
    )(proj, proj, proj, proj, cos, sin, dmat, xi, zeta, g_chunk, gn_gain.reshape(nh, 1, dv))


LOG2E = 1.4426950408889634
MASK_BIG = 1e30
ONES_LANE = LANES - 1


def _bias_selectors(nh):
    h = jnp.arange(nh)
    selq = jnp.zeros((nh, 3 * LANES, LANES), F32)
    selk = jnp.zeros((nh, 3 * LANES, LANES), F32)
    for part in range(3):
        selq = selq.at[h, part * LANES + h, part].set(1.0)
        selk = selk.at[h, part * LANES + h, 3 + part].set(-1.0)
        selq = selq.at[h, ONES_LANE, 3 + part].set(1.0)
        selk = selk.at[h, ONES_LANE, part].set(1.0)
    return selq.astype(BF16), selk.astype(BF16)


def _forget_kernel(hn_ref, wf_ref, bf_ref, pq_ref, pk_ref, carry_ref, *, rows):
    i = pl.program_id(1)

    @pl.when(i == 0)
    def _():
        carry_ref[...] = jnp.zeros_like(carry_ref)

    z = jnp.dot(hn_ref[...], wf_ref[...], preferred_element_type=F32) + bf_ref[...]
    lf = jnp.minimum(z, 0.0) - jnp.log1p(jnp.exp(-jnp.abs(z)))
    row = lax.broadcasted_iota(jnp.int32, (rows, 1), 0)
    valid = i * rows + row >= PAD
    lf = jnp.where(valid, lf, 0.0)

    def split3(a):
        hi = a.astype(BF16)
        r1 = a - hi.astype(F32)
        mid = r1.astype(BF16)
        lo = (r1 - mid.astype(F32)).astype(BF16)
        return hi, mid, lo

    r_i = lax.broadcasted_iota(jnp.int32, (rows, rows), 0)
    c_i = lax.broadcasted_iota(jnp.int32, (rows, rows), 1)
    tri = (r_i >= c_i).astype(BF16)
    cs = sum(jnp.dot(tri, part, preferred_element_type=F32) for part in split3(lf))
    cs = cs + carry_ref[...]
    carry_ref[...] = cs[rows - 1:rows, :]

    hi, mid, lo = (p.astype(F32) for p in split3(cs * LOG2E))
    lane = lax.broadcasted_iota(jnp.int32, (rows, LANES), 1)
    ones = lane == ONES_LANE
    live = jnp.logical_and(valid, jnp.logical_not(ones))
    pq_ref[...] = jnp.concatenate(
        [jnp.where(ones, 1.0, hi), jnp.where(ones, 0.0, mid), jnp.where(ones, 0.0, lo)], axis=1).astype(BF16)
    pk_ref[...] = jnp.concatenate(
        [jnp.where(ones, 1.0, jnp.where(valid, hi, MASK_BIG)), jnp.where(live, mid, 0.0),
         jnp.where(live, lo, 0.0)], axis=1).astype(BF16)


def forget_gates(hn, w_f, b_f, bsz, lp, rows=384):
    tp, d = hn.shape
    nh = w_f.shape[1]
    steps = lp // rows
    wf = jnp.zeros((d, LANES), BF16).at[:, :nh].set(w_f.astype(BF16))
    bf = jnp.zeros((1, LANES), F32).at[0, :nh].set(b_f.astype(F32))
    return pl.pallas_call(
        functools.partial(_forget_kernel, rows=rows),
        out_shape=(jax.ShapeDtypeStruct((tp, 3 * LANES), BF16), jax.ShapeDtypeStruct((tp, 3 * LANES), BF16)),
        grid=(bsz, steps),
        in_specs=[
            pl.BlockSpec((rows, d), lambda b, i: (b * steps + i, 0)),
            pl.BlockSpec((d, LANES), lambda b, i: (0, 0)),
            pl.BlockSpec((1, LANES), lambda b, i: (0, 0)),
        ],
        out_specs=(
            pl.BlockSpec((rows, 3 * LANES), lambda b, i: (b * steps + i, 0)),
            pl.BlockSpec((rows, 3 * LANES), lambda b, i: (b * steps + i, 0)),
        ),
        scratch_shapes=[pltpu.VMEM((1, LANES), F32)],
        compiler_params=_params(("parallel", "arbitrary")),
        name="forget_gates",
    )(hn, wf, bf)


FOX_HEADS_PER_STEP = 4


FOX_STRIP = 32


def _fox_kernel(q_ref, pq_ref, k_ref, pk_ref, v_ref, selq_ref, selk_ref, o_ref,
                m_ref, l_ref, acc_ref, s_ref, p_ref, a_ref, ck_ref, *, tq, tk, lp, dh):
    i = pl.program_id(2)
    hs = FOX_HEADS_PER_STEP
    reps = tk // LANES

    @pl.when(i == 0)
    def _():
        for j in range(hs):
            ck_ref[:, j * LANES:(j + 1) * LANES] = jnp.dot(
                pk_ref[...], selk_ref[j], preferred_element_type=F32).astype(BF16)

    def sweep(nr):
        m_ref[:, :nr, :] = jnp.full((hs, nr, LANES), 10.0 * NEG_INF, F32)
        l_ref[:, :nr, :] = jnp.zeros((hs, nr, LANES), F32)
        acc_ref[:, :nr, :] = jnp.zeros((hs, nr, dh), F32)
        qa = [jnp.concatenate(
            [q_ref[:nr, j * dh:(j + 1) * dh],
             jnp.dot(pq_ref[:nr, :], selq_ref[j], preferred_element_type=F32).astype(BF16)], axis=1)
            for j in range(hs)]

        def scores(kb):
            start = pl.multiple_of(kb * tk, tk)
            for j in range(hs):
                ka = jnp.concatenate([k_ref[pl.ds(start, tk), j * dh:(j + 1) * dh],
                                      ck_ref[pl.ds(start, tk), j * LANES:(j + 1) * LANES]], axis=1)
                s_ref[j, :nr, :] = lax.dot_general(qa[j], ka, (((1,), (1,)), ((), ())),
                                                   preferred_element_type=F32)

        def absorb(kb, diag):
            start = pl.multiple_of(kb * tk, tk)
            for j in range(hs):
                for r0 in range(0, nr, FOX_STRIP):
                    rows = slice(r0, r0 + FOX_STRIP)
                    if diag is not None and r0 + FOX_STRIP - 1 < diag * tk:
                        a_ref[j, rows, :] = jnp.ones((FOX_STRIP, LANES), F32)
                        p_ref[j, rows, :] = jnp.zeros((FOX_STRIP, tk), BF16)
                        continue
                    s = s_ref[j, rows, :]
                    if diag is not None and r0 < diag * tk + tk - 1:
                        r_i = r0 + lax.broadcasted_iota(jnp.int32, (FOX_STRIP, tk), 0)
                        c_i = diag * tk + lax.broadcasted_iota(jnp.int32, (FOX_STRIP, tk), 1)
                        s = jnp.where(c_i <= r_i, s, NEG_INF)
                    m_old = m_ref[j, rows, :]
                    m_new = jnp.maximum(m_old, jnp.max(s, axis=1, keepdims=True))
                    alpha = jnp.exp2(m_old - m_new)
                    p = jnp.exp2(s - jnp.concatenate([m_new] * reps, axis=1))
                    l_ref[j, rows, :] = alpha * l_ref[j, rows, :] + jnp.sum(p, axis=1, keepdims=True)
                    m_ref[j, rows, :] = m_new
                    a_ref[j, rows, :] = alpha
                    p_ref[j, rows, :] = p.astype(BF16)
            for j in range(hs):
                acc_ref[j, :nr, :] = a_ref[j, :nr, :] * acc_ref[j, :nr, :] + jnp.dot(
                    p_ref[j, :nr, :], v_ref[pl.ds(start, tk), j * dh:(j + 1) * dh],
                    preferred_element_type=F32)

        def body(kb, carry):
            scores(kb)
            absorb(kb, None)
            return carry

        n_full = i * (tq // tk)
        lax.fori_loop(0, n_full, body, 0)
        for diag in range(nr // tk):
            scores(n_full + diag)
            absorb(n_full + diag, diag)
        for j in range(hs):
            o_ref[:nr, j * dh:(j + 1) * dh] = (acc_ref[j, :nr, :] / l_ref[j, :nr, :]).astype(o_ref.dtype)

    tail = lp % tq
    if tail == 0:
        sweep(tq)
    else:
        @pl.when(i < lp // tq)
        def _():
            sweep(tq)

        @pl.when(i == lp // tq)
        def _():
            sweep(tail)


def fox_attention(qkv, pq, pk, bsz, lp, tq=768, tk=384):
    tp = qkv.shape[0]
    d = qkv.shape[1] // 3
    nh = FOX_HEADS
    dh = d // nh
    assert dh == LANES and tq % tk == 0 and lp % tk == 0 and (lp % tq) % tk == 0
    nq = pl.cdiv(lp, tq)
    hs = FOX_HEADS_PER_STEP
    ng = nh // hs
    qkv3 = qkv.reshape(bsz, lp, 3 * d)
    pq3 = pq.reshape(bsz, lp, 3 * LANES)
    pk3 = pk.reshape(bsz, lp, 3 * LANES)
    selq, selk = _bias_selectors(nh)
    out = pl.pallas_call(
        functools.partial(_fox_kernel, tq=tq, tk=tk, lp=lp, dh=dh),
        out_shape=jax.ShapeDtypeStruct((bsz, lp, d), BF16),
        grid=(bsz, ng, nq),
        in_specs=[
            pl.BlockSpec((None, tq, hs * dh), lambda b, g, i: (b, i, g)),
            pl.BlockSpec((None, tq, 3 * LANES), lambda b, g, i: (b, i, 0)),
            pl.BlockSpec((None, lp, hs * dh), lambda b, g, i: (b, 0, ng + g)),
            pl.BlockSpec((None, lp, 3 * LANES), lambda b, g, i: (b, 0, 0)),
            pl.BlockSpec((None, lp, hs * dh), lambda b, g, i: (b, 0, 2 * ng + g)),
            pl.BlockSpec((hs, 3 * LANES, LANES), lambda b, g, i: (g, 0, 0)),
            pl.BlockSpec((hs, 3 * LANES, LANES), lambda b, g, i: (g, 0, 0)),
        ],
        out_specs=pl.BlockSpec((None, tq, hs * dh), lambda b, g, i: (b, i, g)),
        scratch_shapes=[pltpu.VMEM((hs, tq, LANES), F32), pltpu.VMEM((hs, tq, LANES), F32),
                        pltpu.VMEM((hs, tq, dh), F32), pltpu.VMEM((hs, tq, tk), F32),
                        pltpu.VMEM((hs, tq, tk), BF16), pltpu.VMEM((hs, tq, LANES), F32),
                        pltpu.VMEM((lp, hs * LANES), BF16)],
        compiler_params=_params(("parallel", "parallel", "arbitrary"), vmem=56 * 1024 * 1024),
        name="fox_attention",
    )(qkv3, pq3, qkv3, pk3, qkv3, selq, selk)
    return out.reshape(tp, d)


def _router_kernel(h_ref, g_ref, wr_ref, br_ref, hn_ref, idx_ref, w_ref, cnt_ref, carry_ref, *, tr):
    i = pl.program_id(0)

    @pl.when(i == 0)
    def _():
        carry_ref[...] = jnp.zeros_like(carry_ref)

    hn = _rms(h_ref[...], g_ref[...])
    _store_token_tiles(hn_ref, (), _pack_halves(hn))
    lg = lax.dot_general(wr_ref[...], hn.astype(BF16), (((1,), (1,)), ((), ())),
                         preferred_element_type=F32) + br_ref[...]
    row = lax.broadcasted_iota(jnp.int32, (ROUTER_ROWS, tr), 0)
    big = jnp.int32(1 << 20)
    is_g = row < N_GROUPS
    mg = jnp.max(jnp.where(is_g, lg, -jnp.inf), axis=0, keepdims=True)
    g_sel = jnp.min(jnp.where(jnp.logical_and(is_g, lg == mg), row, big), axis=0, keepdims=True)
    sg = jnp.sum(jnp.where(is_g, jnp.exp(lg - mg), 0.0), axis=0, keepdims=True)
    p_g = 1.0 / sg
    lo = EXPERT_ROW0 + EXPERTS_PER_GROUP * g_sel
    is_e = jnp.logical_and(row >= lo, row < lo + EXPERTS_PER_GROUP)
    me = jnp.max(jnp.where(is_e, lg, -jnp.inf), axis=0, keepdims=True)
    ee = jnp.where(is_e, jnp.exp(lg - me), 0.0)
    pe = ee / jnp.sum(ee, axis=0, keepdims=True)
    pe1 = jnp.where(is_e, pe, -1.0)
    m1 = jnp.max(pe1, axis=0, keepdims=True)
    i1 = jnp.min(jnp.where(pe1 == m1, row, big), axis=0, keepdims=True)
    pe2 = jnp.where(row == i1, -1.0, pe1)
    m2 = jnp.max(pe2, axis=0, keepdims=True)
    i2 = jnp.min(jnp.where(pe2 == m2, row, big), axis=0, keepdims=True)
    den = m1 + m2
    w1 = p_g * (m1 / den)
    w2 = p_g * (m2 / den)

    oh1 = row == i1
    oh2 = row == i2
    ohs = jnp.logical_or(oh1, oh2).astype(F32)
    r_i = lax.broadcasted_iota(jnp.int32, (tr, tr), 0)
    c_i = lax.broadcasted_iota(jnp.int32, (tr, tr), 1)
    tri = (r_i < c_i).astype(BF16)
    cnt = jnp.dot(ohs.astype(BF16), tri, preferred_element_type=F32) + carry_ref[...]
    rank1 = jnp.sum(jnp.where(oh1, cnt, 0.0), axis=0, keepdims=True)
    rank2 = jnp.sum(jnp.where(oh2, cnt, 0.0), axis=0, keepdims=True)
    carry_ref[...] += jnp.sum(ohs, axis=1, keepdims=True)

    r8 = lax.broadcasted_iota(jnp.int32, (8, tr), 0)
    e1 = i1 - EXPERT_ROW0
    e2 = i2 - EXPERT_ROW0
    idx_ref[...] = jnp.where(r8 == 0, e1, jnp.where(r8 == 1, e2, jnp.where(
        r8 == 2, rank1.astype(jnp.int32), jnp.where(r8 == 3, rank2.astype(jnp.int32), 0))))
    w_ref[...] = jnp.where(r8 == 0, w1, jnp.where(r8 == 1, w2, 0.0))
    cnt_ref[...] = jnp.broadcast_to(carry_ref[...], cnt_ref.shape)


def moe_router(h, gain, w_rg, b_rg, w_re, b_re, tr=512):
    tp, d = h.shape
    assert d // 2 == SUBLANES * LANES
    wr = jnp.zeros((ROUTER_ROWS, d), BF16)
    wr = wr.at[:N_GROUPS].set(w_rg.T.astype(BF16))
    wr = wr.at[EXPERT_ROW0:EXPERT_ROW0 + N_EXPERTS].set(w_re.reshape(d, N_EXPERTS).T.astype(BF16))
    br = jnp.zeros((ROUTER_ROWS, 1), F32)
    br = br.at[:N_GROUPS, 0].set(b_rg.astype(F32))
    br = br.at[EXPERT_ROW0:EXPERT_ROW0 + N_EXPERTS, 0].set(b_re.reshape(N_EXPERTS).astype(F32))
    return pl.pallas_call(
        functools.partial(_router_kernel, tr=tr),
        out_shape=(
            jax.ShapeDtypeStruct((tp * SUBLANES, LANES), jnp.uint32),
            jax.ShapeDtypeStruct((8, tp), jnp.int32),
            jax.ShapeDtypeStruct((8, tp), F32),
            jax.ShapeDtypeStruct((ROUTER_ROWS, LANES), F32),
        ),
        grid=(tp // tr,),
        in_specs=[
            pl.BlockSpec((tr, d), lambda i: (i, 0)),
            pl.BlockSpec((1, d), lambda i: (0, 0)),
            pl.BlockSpec((ROUTER_ROWS, d), lambda i: (0, 0)),
            pl.BlockSpec((ROUTER_ROWS, 1), lambda i: (0, 0)),
        ],
        out_specs=(
            pl.BlockSpec((tr * SUBLANES, LANES), lambda i: (i, 0)),
            pl.BlockSpec((8, tr), lambda i: (0, i)),
            pl.BlockSpec((8, tr), lambda i: (0, i)),
            pl.BlockSpec((ROUTER_ROWS, LANES), lambda i: (0, 0)),
        ),
        scratch_shapes=[pltpu.VMEM((ROUTER_ROWS, 1), F32)],
        compiler_params=_params(("arbitrary",)),
        name="moe_router",
    )(h, gain.reshape(1, d), wr, br)


def _slots_kernel(cnt_ref, idx_ref, pos_ref, meta_ref):
    e1 = idx_ref[0:1, :]
    e2 = idx_ref[1:2, :]
    off1 = jnp.zeros_like(e1)
    off2 = jnp.zeros_like(e2)
    visit = lax.broadcasted_iota(jnp.int32, (1, meta_ref.shape[1]), 1)
    v_tile = jnp.zeros_like(visit)
    v_expert = jnp.zeros_like(visit)
    v_lo = jnp.zeros_like(visit)
    v_hi = jnp.zeros_like(visit)
    v_next = jnp.zeros_like(visit)
    v_start = jnp.zeros_like(visit)
    v_count = jnp.zeros_like(visit)
    next_live = [None] * N_EXPERTS
    nxt = jnp.int32(-1)
    for e in reversed(range(N_EXPERTS)):
        next_live[e] = jnp.where(nxt >= 0, nxt, e)
        nxt = jnp.where(cnt_ref[e] > 0, e, nxt)
    start = jnp.int32(0)
    v_base = jnp.int32(0)
    for e in range(N_EXPERTS):
        off1 = jnp.where(e1 == e, start, off1)
        off2 = jnp.where(e2 == e, start, off2)
        n = cnt_ref[e]
        end = start + n
        first_tile = start // TILE_M
        n_visits = jnp.where(n > 0, (jnp.maximum(end, 1) - 1) // TILE_M - first_tile + 1, 0)
        mine = jnp.logical_and(visit >= v_base, visit < v_base + n_visits)
        row0 = (first_tile + visit - v_base) * TILE_M
        v_tile = jnp.where(mine, first_tile + visit - v_base, v_tile)
        v_expert = jnp.where(mine, e, v_expert)
        v_next = jnp.where(mine, next_live[e], v_next)
        v_start = jnp.where(mine, v_base, v_start)
        v_count = jnp.where(mine, n_visits, v_count)
        v_lo = jnp.where(mine, jnp.maximum(start - row0, 0), v_lo)
        v_hi = jnp.where(mine, jnp.minimum(end - row0, TILE_M), v_hi)
        start = end
        v_base = v_base + n_visits
    r8 = lax.broadcasted_iota(jnp.int32, pos_ref.shape, 0)
    pos_ref[...] = jnp.where(r8 == 0, off1 + idx_ref[2:3, :], jnp.where(r8 == 1, off2 + idx_ref[3:4, :], 0))
    m8 = lax.broadcasted_iota(jnp.int32, meta_ref.shape, 0)
    rows = (v_tile, v_expert, v_lo, v_hi, v_base, v_next, v_start, v_count)
    meta = rows[-1]
    for r in reversed(range(len(rows) - 1)):
        meta = jnp.where(m8 == r, rows[r], meta)
    meta_ref[...] = meta


def moe_slots(counts, idx, tr=512):
    tp = idx.shape[1]
    nt_lanes = 2 * LANES
    return pl.pallas_call(
        _slots_kernel,
        out_shape=(jax.ShapeDtypeStruct((8, tp), jnp.int32), jax.ShapeDtypeStruct((8, nt_lanes), jnp.int32)),
        grid_spec=pltpu.PrefetchScalarGridSpec(
            num_scalar_prefetch=1,
            grid=(tp // tr,),
            in_specs=[pl.BlockSpec((8, tr), lambda i, c: (0, i))],
            out_specs=(
                pl.BlockSpec((8, tr), lambda i, c: (0, i)),
                pl.BlockSpec((8, nt_lanes), lambda i, c: (0, 0)),
            ),
        ),
        compiler_params=_params(("arbitrary",)),
        name="moe_slots",
    )(counts, idx)


def _invert_kernel(pos_ref, code_ref, *, tp):
    def per_token(t, carry):
        code_ref[pos_ref[t]] = 2 * t
        code_ref[pos_ref[tp + t]] = 2 * t + 1
        return carry

    lax.fori_loop(0, tp, per_token, 0, unroll=8)


def moe_invert(pos_flat):
    tp = pos_flat.shape[0] // 2
    return pl.pallas_call(
        functools.partial(_invert_kernel, tp=tp),
        out_shape=jax.ShapeDtypeStruct((2 * tp,), jnp.int32),
        grid_spec=pltpu.PrefetchScalarGridSpec(
            num_scalar_prefetch=1,
            grid=(1,),
            in_specs=[],
            out_specs=pl.BlockSpec(memory_space=pltpu.SMEM),
        ),
        compiler_params=_params(("arbitrary",)),
        name="moe_invert",
    )(pos_flat)


def _expert_kernel(vt_ref, ve_ref, vlo_ref, vhi_ref, nv_ref, vnext_ref, vstart_ref, vcount_ref, code_ref,
                   hn_ref, wg_hbm, wu_hbm, wd_hbm, y_ref,
                   xbuf, xcur, yacc, wgs, wus, wds, wgb, wub, wdb, gsem, wsem, *, n_tiles, layer):
    v = pl.program_id(0)
    nv = nv_ref[0]
    t = vt_ref[v]
    prev_v = jnp.maximum(v - 1, 0)
    first = jnp.logical_or(v == 0, vt_ref[prev_v] != t)
    new_expert = jnp.logical_or(v == 0, ve_ref[prev_v] != ve_ref[v])

    def weight_copies(e):
        w = layer * N_EXPERTS + e
        return (pltpu.make_async_copy(wg_hbm.at[w], wgs, wsem.at[0]),
                pltpu.make_async_copy(wu_hbm.at[w], wus, wsem.at[1]),
                pltpu.make_async_copy(wd_hbm.at[w], wds, wsem.at[2]))

    def row_copy(tile, r):
        tok = lax.shift_right_logical(code_ref[tile * TILE_M + r], 1)
        return pltpu.make_async_copy(hn_ref.at[_token_tile(tok)], xbuf.at[_token_tile(r)], gsem)

    def gather_wait():
        pltpu.make_async_copy(hn_ref.at[pl.ds(0, TILE_M * SUBLANES)], xbuf, gsem).wait()

    def compute(accumulate):
        x = xcur[...]
        a = jnp.dot(x, wgb[...], preferred_element_type=F32)
        u = jnp.dot(x, wub[...], preferred_element_type=F32)
        row = lax.broadcasted_iota(jnp.int32, (TILE_M, 1), 0)
        mine = jnp.logical_and(row >= vlo_ref[v], row < vhi_ref[v])
        hid = jnp.where(mine, a * _sigmoid(a) * u, 0.0).astype(BF16)
        y = jnp.dot(hid, wdb[...], preferred_element_type=F32)
        if accumulate:
            y = y + yacc[...]
        yacc[...] = y
        _store_token_tiles(y_ref, (), _pack_halves(y))

    @pl.when(v == 0)
    def _():
        def step(r, carry):
            row_copy(0, r).start()
            return carry
        lax.fori_loop(0, TILE_M, step, 0, unroll=8)
        for copy in weight_copies(ve_ref[0]):
            copy.start()

    @pl.when(v < nv)
    def _():
        @pl.when(new_expert)
        def _():
            for copy in weight_copies(ve_ref[v]):
                copy.wait()
            wgb[...] = wgs[...].astype(BF16)
            wub[...] = wus[...].astype(BF16)
            wdb[...] = wds[...].astype(BF16)

        @pl.when(first)
        def _():
            gather_wait()
            xcur[...] = _unpack_halves(_load_token_tiles(xbuf, (), 0, TILE_M)).astype(BF16)
            ahead = jnp.minimum(t + 1, n_tiles - 1)
            for r in range(TILE_M):
                row_copy(ahead, r).start()
            compute(False)

        @pl.when(jnp.logical_not(first))
        def _():
            compute(True)

        k = v - vstart_ref[v]
        last_of_expert = k == vcount_ref[v] - 1
        has_next = vnext_ref[v] != ve_ref[v]
        for c, copy in enumerate(weight_copies(vnext_ref[v])):
            @pl.when(jnp.logical_and(has_next, jnp.logical_or(k == c, jnp.logical_and(last_of_expert, k < c))))
            def _():
                copy.start()

        @pl.when(v == nv - 1)
        def _():
            gather_wait()


def moe_experts(meta, code, hn, w_gate, w_up, w_down, layer):
    tp = hn.shape[0] // SUBLANES
    d = 2 * SUBLANES * LANES
    f = w_gate.shape[-1]
    assert (2 * tp) % TILE_M == 0
    n_tiles = (2 * tp) // TILE_M
    max_visits = n_tiles + N_EXPERTS - 1

    def y_map(v, vt, ve, vlo, vhi, nv, *_):
        return (vt[jnp.minimum(v, nv[0] - 1)], 0)

    any_spec = pl.BlockSpec(memory_space=pl.ANY)
    return pl.pallas_call(
        functools.partial(_expert_kernel, n_tiles=n_tiles, layer=layer),
        out_shape=jax.ShapeDtypeStruct((2 * tp * SUBLANES, LANES), jnp.uint32),
        grid_spec=pltpu.PrefetchScalarGridSpec(
            num_scalar_prefetch=9,
            grid=(max_visits,),
            in_specs=[any_spec, any_spec, any_spec, any_spec],
            out_specs=pl.BlockSpec((TILE_M * SUBLANES, LANES), y_map),
            scratch_shapes=[
                pltpu.VMEM((TILE_M * SUBLANES, LANES), jnp.uint32), pltpu.VMEM((TILE_M, d), BF16),
                pltpu.VMEM((TILE_M, d), F32),
                pltpu.VMEM((d, f), F32), pltpu.VMEM((d, f), F32), pltpu.VMEM((f, d), F32),
                pltpu.VMEM((d, f), BF16), pltpu.VMEM((d, f), BF16), pltpu.VMEM((f, d), BF16),
                pltpu.SemaphoreType.DMA, pltpu.SemaphoreType.DMA((3,)),
            ],
        ),
        compiler_params=_params(("arbitrary",), vmem=56 * 1024 * 1024),
        name="moe_experts",
    )(meta[0, :max_visits], meta[1, :max_visits], meta[2, :max_visits], meta[3, :max_visits], meta[4, :1],
      meta[5, :max_visits], meta[6, :max_visits], meta[7, :max_visits], code, hn, w_gate, w_up, w_down)


def _combine_kernel(pos_ref, h_ref, w_ref, g_ref, y_ref, *rest, tc, tp, n_steps, write_h):
    out_refs, (ybuf, sem) = rest[:-2], rest[-2:]
    i = pl.program_id(0)
    slot = lax.rem(i, 2)

    def row_copy(tile, buf, s, r):
        p = pos_ref[s * tp + tile * tc + r]
        return pltpu.make_async_copy(y_ref.at[_token_tile(p)], ybuf.at[buf, _token_tile(s * tc + r)],
                                     sem.at[buf])

    @pl.when(i == 0)
    def _():
        def step(r, carry):
            row_copy(0, 0, 0, r).start()
            row_copy(0, 0, 1, r).start()
            return carry
        lax.fori_loop(0, tc, step, 0, unroll=8)

    pltpu.make_async_copy(y_ref.at[pl.ds(0, 2 * tc * SUBLANES)], ybuf.at[slot], sem.at[slot]).wait()

    @pl.when(i + 1 < n_steps)
    def _():
        for r in range(tc):
            row_copy(i + 1, 1 - slot, 0, r).start()
            row_copy(i + 1, 1 - slot, 1, r).start()

    w = w_ref[...]
    h = (h_ref[...] + w[:, 0:1] * _unpack_halves(_load_token_tiles(ybuf, (slot,), 0, tc))
         + w[:, 1:2] * _unpack_halves(_load_token_tiles(ybuf, (slot,), tc, tc)))
    if write_h:
        out_refs[0][...] = h
    out_refs[-1][...] = _rms(h, g_ref[...]).astype(out_refs[-1].dtype)


def moe_combine(pos_flat, h, w_col, y, gain, hn_dtype, final_shape=None):
    tp, d = h.shape
    if final_shape is None:
        tc = 256
        out_shape = (jax.ShapeDtypeStruct((tp, d), F32), jax.ShapeDtypeStruct((tp, d), hn_dtype))
        out_specs = (pl.BlockSpec((tc, d), lambda i, p: (i, 0)), pl.BlockSpec((tc, d), lambda i, p: (i, 0)))
    else:
        tc = CHUNK
        bsz, seq, _ = final_shape
        nc = tp // bsz // tc
        out_shape = (jax.ShapeDtypeStruct(final_shape, hn_dtype),)
        out_specs = (pl.BlockSpec((None, tc, d), lambda i, p: (i // nc, jnp.maximum(i % nc - 1, 0), 0)),)
    n_steps = tp // tc
    return pl.pallas_call(
        functools.partial(_combine_kernel, tc=tc, tp=tp, n_steps=n_steps, write_h=final_shape is None),
        out_shape=out_shape,
        grid_spec=pltpu.PrefetchScalarGridSpec(
            num_scalar_prefetch=1,
            grid=(n_steps,),
            in_specs=[
                pl.BlockSpec((tc, d), lambda i, p: (i, 0)),
                pl.BlockSpec((tc, 2), lambda i, p: (i, 0)),
                pl.BlockSpec((1, d), lambda i, p: (0, 0)),
                pl.BlockSpec(memory_space=pl.ANY),
            ],
            out_specs=out_specs,
            scratch_shapes=[pltpu.VMEM((2, 2 * tc * SUBLANES, LANES), jnp.uint32),
                            pltpu.SemaphoreType.DMA((2,))],
        ),
        compiler_params=_params(("arbitrary",)),
        name="moe_combine",
    )(pos_flat, h, w_col, gain.reshape(1, d), y)


def hierarchical_moe(h, gain, w_rg, b_rg, w_re, b_re, w_gate, w_up, w_down, layer, next_gain, hn_dtype,
                     final_shape=None):
    tp, d = h.shape
    f = w_gate.shape[-1]
    hn, idx, w_rows, cnt = moe_router(h, gain, w_rg, b_rg, w_re, b_re)
    counts = cnt[EXPERT_ROW0:EXPERT_ROW0 + N_EXPERTS, 0].astype(jnp.int32)
    pos, meta = moe_slots(counts, idx)
    pos_flat = pos[:2].reshape(2 * tp)
    code = moe_invert(pos_flat)
    y = moe_experts(meta, code, hn, w_gate.reshape(-1, d, f), w_up.reshape(-1, d, f),
                    w_down.reshape(-1, f, d), layer)
    return moe_combine(pos_flat, h, w_rows[:2].T, y, next_gain, hn_dtype, final_shape)


def kernel(x, meta_tokens, norm_mixer, norm_ffn, norm_final, ret_w_in, ret_gn, ret_w_out,
           fox_w_in, fox_b_f, fox_w_out, moe_w_rg, moe_b_rg, moe_w_re, moe_b_re,
           moe_w_gate, moe_w_up, moe_w_down):
    bsz, seq, d = x.shape
    depth = norm_mixer.shape[0]
    nc = (seq + CHUNK) // CHUNK
    lp = nc * CHUNK
    h, hn = embed_norm(x, meta_tokens.astype(x.dtype), norm_mixer[0], BF16)
    for i in range(depth):
        j = i // 2
        if i % 2 == 0:
            proj = matmul(hn, ret_w_in, j, BF16, tm=1536)
            gated = retention_core(proj, ret_gn[j], bsz, nc)
            h = matmul(gated, ret_w_out, j, F32, residual=h, tn=512)
        else:
            qkv = matmul(hn, fox_w_in, j, BF16, n=3 * d, tm=1536, scale_cols=d,
                         scale=(d // FOX_HEADS) ** -0.5 * LOG2E)
            pq, pk = forget_gates(hn, fox_w_in[j, :, 3 * d:], fox_b_f[j], bsz, lp)
            o = fox_attention(qkv, pq, pk, bsz, lp)
            h = matmul(o, fox_w_out, j, F32, residual=h, tm=1536, tn=512)
        last = i == depth - 1
        outs = hierarchical_moe(h, norm_ffn[i], moe_w_rg[i], moe_b_rg[i], moe_w_re[i], moe_b_re[i],
                                moe_w_gate, moe_w_up, moe_w_down, i,
                                norm_final if last else norm_mixer[i + 1],
                                F32 if last else BF16, (bsz, seq, d) if last else None)
        if last:
            return outs[0]
        h, hn = outs
```

```python
import functools

import jax
import jax.numpy as jnp
from jax import lax
from jax.experimental import pallas as pl
from jax.experimental.pallas import tpu as pltpu

N_META = 16
CHUNK = 128
PAD = CHUNK - N_META
RMS_EPS = 1e-6
GN_EPS = 1e-6
NEG_INF = -1e30
RET_HEADS = 8
FOX_HEADS = 16
N_GROUPS = 4
EXPERTS_PER_GROUP = 8
N_EXPERTS = N_GROUPS * EXPERTS_PER_GROUP
ROPE_BASE = 10000.0

LANES = 128
ROUTER_ROWS = 48
EXPERT_ROW0 = N_GROUPS
TILE_M = 256
VMEM_LIMIT = 48 * 1024 * 1024

F32 = jnp.float32
BF16 = jnp.bfloat16


def _params(sem, vmem=VMEM_LIMIT):
    return pltpu.CompilerParams(dimension_semantics=sem, vmem_limit_bytes=vmem)


def _rms(h, g):
    return h * lax.rsqrt(jnp.mean(h * h, axis=-1, keepdims=True) + RMS_EPS) * g


def _sigmoid(x):
    return 1.0 / (1.0 + jnp.exp(-x))


def _pack_halves(x):
    half = x.shape[1] // 2
    return pltpu.pack_elementwise([x[:, :half], x[:, half:]], packed_dtype=BF16)


def _unpack_halves(p):
    lo = pltpu.unpack_elementwise(p, index=0, packed_dtype=BF16, unpacked_dtype=F32)
    hi = pltpu.unpack_elementwise(p, index=1, packed_dtype=BF16, unpacked_dtype=F32)
    return jnp.concatenate([lo, hi], axis=1)


SUBLANES = 8


def _token_tile(t):
    start = t * SUBLANES
    return pl.ds(start if isinstance(t, int) else pl.multiple_of(start, SUBLANES), SUBLANES)


def _store_token_tiles(ref, index, packed):
    m = packed.shape[0]
    for s in range(SUBLANES):
        ref[index + (pl.ds(s, m, stride=SUBLANES), slice(None))] = packed[:, s * LANES:(s + 1) * LANES]


def _load_token_tiles(ref, index, first_token, m):
    return jnp.concatenate(
        [ref[index + (pl.ds(first_token * SUBLANES + s, m, stride=SUBLANES), slice(None))]
         for s in range(SUBLANES)], axis=1)


EMBED_CHUNKS = 3


def _embed_norm_kernel(*refs):
    x_refs, (meta_ref, g_ref, h_ref, hn_ref) = refs[:EMBED_CHUNKS], refs[EMBED_CHUNKS:]
    i = pl.program_id(1)
    for c, x_ref in enumerate(x_refs):
        h_ref[c * CHUNK:(c + 1) * CHUNK, :] = x_ref[...]

    @pl.when(i == 0)
    def _():
        h_ref[:PAD, :] = jnp.zeros((PAD, h_ref.shape[1]), h_ref.dtype)
        h_ref[PAD:CHUNK, :] = meta_ref[...]

    hn_ref[...] = _rms(h_ref[...], g_ref[...]).astype(hn_ref.dtype)


def embed_norm(x, meta, gain, hn_dtype):
    bsz, seq, d = x.shape
    nc = (seq + CHUNK) // CHUNK
    assert nc % EMBED_CHUNKS == 0
    tp = bsz * nc * CHUNK
    steps = nc // EMBED_CHUNKS
    rows = EMBED_CHUNKS * CHUNK

    def x_spec(c):
        return pl.BlockSpec((None, CHUNK, d), lambda b, i: (b, jnp.maximum(EMBED_CHUNKS * i + c - 1, 0), 0))

    return pl.pallas_call(
        _embed_norm_kernel,
        out_shape=(jax.ShapeDtypeStruct((tp, d), F32), jax.ShapeDtypeStruct((tp, d), hn_dtype)),
        grid=(bsz, steps),
        in_specs=[x_spec(c) for c in range(EMBED_CHUNKS)] + [
            pl.BlockSpec((N_META, d), lambda b, i: (0, 0)),
            pl.BlockSpec((1, d), lambda b, i: (0, 0)),
        ],
        out_specs=(
            pl.BlockSpec((rows, d), lambda b, i: (b * steps + i, 0)),
            pl.BlockSpec((rows, d), lambda b, i: (b * steps + i, 0)),
        ),
        compiler_params=_params(("parallel", "parallel")),
        name="embed_norm",
    )(*([x] * EMBED_CHUNKS), meta, gain.reshape(1, d))


def _mm_kernel(*refs, has_res, scale_tiles, scale):
    if has_res:
        x_ref, w_ref, r_ref, o_ref, wb_ref = refs
    else:
        x_ref, w_ref, o_ref, wb_ref = refs

    @pl.when(pl.program_id(1) == 0)
    def _():
        wb_ref[...] = w_ref[...].astype(BF16)

    acc = jnp.dot(x_ref[...], wb_ref[...], preferred_element_type=F32)
    if scale_tiles:
        acc = acc * jnp.where(pl.program_id(0) < scale_tiles, scale, 1.0)
    if has_res:
        acc = acc + r_ref[...]
    o_ref[...] = acc.astype(o_ref.dtype)


def matmul(x, w, layer, out_dtype, n=None, residual=None, tm=512, tn=1024, scale_cols=0, scale=1.0):
    m, kdim = x.shape
    n = w.shape[2] if n is None else n
    assert m % tm == 0 and n % tn == 0 and scale_cols % tn == 0
    in_specs = [
        pl.BlockSpec((tm, kdim), lambda j, i: (i, 0)),
        pl.BlockSpec((None, kdim, tn), lambda j, i: (layer, 0, j)),
    ]
    args = [x, w]
    if residual is not None:
        in_specs.append(pl.BlockSpec((tm, tn), lambda j, i: (i, j)))
        args.append(residual)
    return pl.pallas_call(
        functools.partial(_mm_kernel, has_res=residual is not None, scale_tiles=scale_cols // tn, scale=scale),
        out_shape=jax.ShapeDtypeStruct((m, n), out_dtype),
        grid=(n // tn, m // tm),
        in_specs=in_specs,
        out_specs=pl.BlockSpec((tm, tn), lambda j, i: (i, j)),
        scratch_shapes=[pltpu.VMEM((kdim, tn), BF16)],
        compiler_params=_params(("parallel", "arbitrary")),
        name="matmul",
    )(*args)


def _retention_kernel(q_ref, k_ref, v_ref, g_ref, cos_ref, sin_ref, dmat_ref, xi_ref, zeta_ref,
                      gch_ref, gn_ref, o_ref, state_ref, *, nh, dk, dv):
    c = pl.program_id(1)

    @pl.when(c == 0)
    def _():
        state_ref[...] = jnp.zeros_like(state_ref)

    cos = cos_ref[...]
    sin = sin_ref[...]
    half = dk // 2

    def rot(u):
        u1, u2 = u[:, :half], u[:, half:]
        return jnp.concatenate([u1 * cos - u2 * sin, u1 * sin + u2 * cos], axis=1)

    row = lax.broadcasted_iota(jnp.int32, (CHUNK, 1), 0)
    valid = jnp.logical_or(row >= PAD, c > 0)
    for h in range(nh):
        q = rot(q_ref[:, h * dk:(h + 1) * dk].astype(F32))
        k = rot(k_ref[:, h * dk:(h + 1) * dk].astype(F32)) * (dk ** -0.5)
        k = jnp.where(valid, k, 0.0)
        v = v_ref[:, h * dv:(h + 1) * dv]
        vb = jnp.where(valid, v, jnp.zeros_like(v))
        qb = q.astype(BF16)
        kb = k.astype(BF16)

        scores = lax.dot_general(qb, kb, (((1,), (1,)), ((), ())), preferred_element_type=F32)
        scores = scores * dmat_ref[h]
        inner = jnp.dot(scores.astype(BF16), vb, preferred_element_type=F32)
        state = state_ref[h]
        cross = jnp.dot(qb, state.astype(BF16), preferred_element_type=F32) * xi_ref[h]
        y = inner + cross
        kz = (k * zeta_ref[h]).astype(BF16)
        state_ref[h] = gch_ref[h] * state + lax.dot_general(
            kz, vb, (((0,), (0,)), ((), ())), preferred_element_type=F32)

        mu = jnp.mean(y, axis=-1, keepdims=True)
        dlt = y - mu
        var = jnp.mean(dlt * dlt, axis=-1, keepdims=True)
        yn = dlt * lax.rsqrt(var + GN_EPS) * gn_ref[h]
        hg = 0.5 * g_ref[:, h * dv:(h + 1) * dv].astype(F32)
        o_ref[:, h * dv:(h + 1) * dv] = ((hg + hg * jnp.tanh(hg)) * yn).astype(o_ref.dtype)


def retention_core(proj, gn_gain, bsz, nc):
    tp = proj.shape[0]
    d = proj.shape[1] // 6
    nh = RET_HEADS
    dk = d // nh
    dv = 2 * d // nh
    lp = nc * CHUNK
    half = dk // 2
    pos = (jnp.arange(lp) - PAD).astype(F32)
    inv = ROPE_BASE ** (-jnp.arange(half, dtype=F32) / half)
    ang = pos[:, None] * inv[None, :]
    cos, sin = jnp.cos(ang), jnp.sin(ang)
    log_g = jnp.log1p(-jnp.exp2(-5.0 - jnp.arange(nh, dtype=F32)))
    idx = jnp.arange(CHUNK, dtype=F32)
    diff = idx[:, None] - idx[None, :]
    dmat = jnp.where(diff[None] >= 0, jnp.exp(jnp.maximum(diff, 0.0)[None] * log_g[:, None, None]), 0.0)
    xi = jnp.exp((idx + 1.0)[None, :] * log_g[:, None])[:, :, None]
    zeta = jnp.exp((CHUNK - 1.0 - idx)[None, :] * log_g[:, None])[:, :, None]
    g_chunk = jnp.exp(CHUNK * log_g)[:, None, None]
    return pl.pallas_call(
        functools.partial(_retention_kernel, nh=nh, dk=dk, dv=dv),
        out_shape=jax.ShapeDtypeStruct((tp, 2 * d), BF16),
        grid=(bsz, nc),
        in_specs=[
            pl.BlockSpec((CHUNK, d), lambda b, c: (b * nc + c, 0)),
            pl.BlockSpec((CHUNK, d), lambda b, c: (b * nc + c, 1)),
            pl.BlockSpec((CHUNK, 2 * d), lambda b, c: (b * nc + c, 1)),
            pl.BlockSpec((CHUNK, 2 * d), lambda b, c: (b * nc + c, 2)),
            pl.BlockSpec((CHUNK, half), lambda b, c: (c, 0)),
            pl.BlockSpec((CHUNK, half), lambda b, c: (c, 0)),
            pl.BlockSpec((nh, CHUNK, CHUNK), lambda b, c: (0, 0, 0)),
            pl.BlockSpec((nh, CHUNK, 1), lambda b, c: (0, 0, 0)),
            pl.BlockSpec((nh, CHUNK, 1), lambda b, c: (0, 0, 0)),
            pl.BlockSpec((nh, 1, 1), lambda b, c: (0, 0, 0)),
            pl.BlockSpec((nh, 1, dv), lambda b, c: (0, 0, 0)),
        ],
        out_specs=pl.BlockSpec((CHUNK, 2 * d), lambda b, c: (b * nc + c, 0)),
        scratch_shapes=[pltpu.VMEM((nh, dk, dv), F32)],
        compiler_params=_params(("parallel", "arbitrary")),
        name="retention_core",
    )(proj, proj, proj, proj, cos, sin, dmat, xi, zeta, g_chunk, gn_gain.reshape(nh, 1, dv))


LOG2E = 1.4426950408889634
MASK_BIG = 1e30
ONES_LANE = LANES - 1


def _bias_selectors(nh):
    h = jnp.arange(nh)
    selq = jnp.zeros((nh, 3 * LANES, LANES), F32)
    selk = jnp.zeros((nh, 3 * LANES, LANES), F32)
    for part in range(3):
        selq = selq.at[h, part * LANES + h, part].set(1.0)
        selk = selk.at[h, part * LANES + h, 3 + part].set(-1.0)
        selq = selq.at[h, ONES_LANE, 3 + part].set(1.0)
        selk = selk.at[h, ONES_LANE, part].set(1.0)
    return selq.astype(BF16), selk.astype(BF16)


def _forget_kernel(hn_ref, wf_ref, bf_ref, pq_ref, pk_ref, carry_ref, *, rows):
    i = pl.program_id(1)

    @pl.when(i == 0)
    def _():
        carry_ref[...] = jnp.zeros_like(carry_ref)

    z = jnp.dot(hn_ref[...], wf_ref[...], preferred_element_type=F32) + bf_ref[...]
    lf = jnp.minimum(z, 0.0) - jnp.log1p(jnp.exp(-jnp.abs(z)))
    row = lax.broadcasted_iota(jnp.int32, (rows, 1), 0)
    valid = i * rows + row >= PAD
    lf = jnp.where(valid, lf, 0.0)

    def split3(a):
        hi = a.astype(BF16)
        r1 = a - hi.astype(F32)
        mid = r1.astype(BF16)
        lo = (r1 - mid.astype(F32)).astype(BF16)
        return hi, mid, lo

    r_i = lax.broadcasted_iota(jnp.int32, (rows, rows), 0)
    c_i = lax.broadcasted_iota(jnp.int32, (rows, rows), 1)
    tri = (r_i >= c_i).astype(BF16)
    cs = sum(jnp.dot(tri, part, preferred_element_type=F32) for part in split3(lf))
    cs = cs + carry_ref[...]
    carry_ref[...] = cs[rows - 1:rows, :]

    hi, mid, lo = (p.astype(F32) for p in split3(cs * LOG2E))
    lane = lax.broadcasted_iota(jnp.int32, (rows, LANES), 1)
    ones = lane == ONES_LANE
    live = jnp.logical_and(valid, jnp.logical_not(ones))
    pq_ref[...] = jnp.concatenate(
        [jnp.where(ones, 1.0, hi), jnp.where(ones, 0.0, mid), jnp.where(ones, 0.0, lo)], axis=1).astype(BF16)
    pk_ref[...] = jnp.concatenate(
        [jnp.where(ones, 1.0, jnp.where(valid, hi, MASK_BIG)), jnp.where(live, mid, 0.0),
         jnp.where(live, lo, 0.0)], axis=1).astype(BF16)


def forget_gates(hn, w_f, b_f, bsz, lp, rows=384):
    tp, d = hn.shape
    nh = w_f.shape[1]
    steps = lp // rows
    wf = jnp.zeros((d, LANES), BF16).at[:, :nh].set(w_f.astype(BF16))
    bf = jnp.zeros((1, LANES), F32).at[0, :nh].set(b_f.astype(F32))
    return pl.pallas_call(
        functools.partial(_forget_kernel, rows=rows),
        out_shape=(jax.ShapeDtypeStruct((tp, 3 * LANES), BF16), jax.ShapeDtypeStruct((tp, 3 * LANES), BF16)),
        grid=(bsz, steps),
        in_specs=[
            pl.BlockSpec((rows, d), lambda b, i: (b * steps + i, 0)),
            pl.BlockSpec((d, LANES), lambda b, i: (0, 0)),
            pl.BlockSpec((1, LANES), lambda b, i: (0, 0)),
        ],
        out_specs=(
            pl.BlockSpec((rows, 3 * LANES), lambda b, i: (b * steps + i, 0)),
            pl.BlockSpec((rows, 3 * LANES), lambda b, i: (b * steps + i, 0)),
        ),
        scratch_shapes=[pltpu.VMEM((1, LANES), F32)],
        compiler_params=_params(("parallel", "arbitrary")),
        name="forget_gates",
    )(hn, wf, bf)


FOX_HEADS_PER_STEP = 4


FOX_STRIP = 32


def _fox_kernel(q_ref, pq_ref, k_ref, pk_ref, v_ref, selq_ref, selk_ref, o_ref,
                m_ref, l_ref, acc_ref, s_ref, p_ref, a_ref, ck_ref, *, tq, tk, lp, dh):
    i = pl.program_id(2)
    hs = FOX_HEADS_PER_STEP
    reps = tk // LANES

    @pl.when(i == 0)
    def _():
        for j in range(hs):
            ck_ref[:, j * LANES:(j + 1) * LANES] = jnp.dot(
                pk_ref[...], selk_ref[j], preferred_element_type=F32).astype(BF16)

    def sweep(nr):
        m_ref[:, :nr, :] = jnp.full((hs, nr, LANES), 10.0 * NEG_INF, F32)
        l_ref[:, :nr, :] = jnp.zeros((hs, nr, LANES), F32)
        acc_ref[:, :nr, :] = jnp.zeros((hs, nr, dh), F32)
        qa = [jnp.concatenate(
            [q_ref[:nr, j * dh:(j + 1) * dh],
             jnp.dot(pq_ref[:nr, :], selq_ref[j], preferred_element_type=F32).astype(BF16)], axis=1)
            for j in range(hs)]

        def scores(kb):
            start = pl.multiple_of(kb * tk, tk)
            for j in range(hs):
                ka = jnp.concatenate([k_ref[pl.ds(start, tk), j * dh:(j + 1) * dh],
                                      ck_ref[pl.ds(start, tk), j * LANES:(j + 1) * LANES]], axis=1)
                s_ref[j, :nr, :] = lax.dot_general(qa[j], ka, (((1,), (1,)), ((), ())),
                                                   preferred_element_type=F32)

        def absorb(kb, diag):
            start = pl.multiple_of(kb * tk, tk)
            for j in range(hs):
                for r0 in range(0, nr, FOX_STRIP):
                    rows = slice(r0, r0 + FOX_STRIP)
                    if diag is not None and r0 + FOX_STRIP - 1 < diag * tk:
                        a_ref[j, rows, :] = jnp.ones((FOX_STRIP, LANES), F32)
                        p_ref[j, rows, :] = jnp.zeros((FOX_STRIP, tk), BF16)
                        continue
                    s = s_ref[j, rows, :]
                    if diag is not None and r0 < diag * tk + tk - 1:
                        r_i = r0 + lax.broadcasted_iota(jnp.int32, (FOX_STRIP, tk), 0)
                        c_i = diag * tk + lax.broadcasted_iota(jnp.int32, (FOX_STRIP, tk), 1)
                        s = jnp.where(c_i <= r_i, s, NEG_INF)
                    m_old = m_ref[j, rows, :]
                    m_new = jnp.maximum(m_old, jnp.max(s, axis=1, keepdims=True))
                    alpha = jnp.exp2(m_old - m_new)
                    p = jnp.exp2(s - jnp.concatenate([m_new] * reps, axis=1))
                    l_ref[j, rows, :] = alpha * l_ref[j, rows, :] + jnp.sum(p, axis=1, keepdims=True)
                    m_ref[j, rows, :] = m_new
                    a_ref[j, rows, :] = alpha
                    p_ref[j, rows, :] = p.astype(BF16)
            for j in range(hs):
                acc_ref[j, :nr, :] = a_ref[j, :nr, :] * acc_ref[j, :nr, :] + jnp.dot(
                    p_ref[j, :nr, :], v_ref[pl.ds(start, tk), j * dh:(j + 1) * dh],
                    preferred_element_type=F32)

        def body(kb, carry):
            scores(kb)
            absorb(kb, None)
            return carry

        n_full = i * (tq // tk)
        lax.fori_loop(0, n_full, body, 0)
        for diag in range(nr // tk):
            scores(n_full + diag)
            absorb(n_full + diag, diag)
        for j in range(hs):
            o_ref[:nr, j * dh:(j + 1) * dh] = (acc_ref[j, :nr, :] / l_ref[j, :nr, :]).astype(o_ref.dtype)

    tail = lp % tq
    if tail == 0:
        sweep(tq)
    else:
        @pl.when(i < lp // tq)
        def _():
            sweep(tq)

        @pl.when(i == lp // tq)
        def _():
            sweep(tail)


def fox_attention(qkv, pq, pk, bsz, lp, tq=768, tk=384):
    tp = qkv.shape[0]
    d = qkv.shape[1] // 3
    nh = FOX_HEADS
    dh = d // nh
    assert dh == LANES and tq % tk == 0 and lp % tk == 0 and (lp % tq) % tk == 0
    nq = pl.cdiv(lp, tq)
    hs = FOX_HEADS_PER_STEP
    ng = nh // hs
    qkv3 = qkv.reshape(bsz, lp, 3 * d)
    pq3 = pq.reshape(bsz, lp, 3 * LANES)
    pk3 = pk.reshape(bsz, lp, 3 * LANES)
    selq, selk = _bias_selectors(nh)
    out = pl.pallas_call(
        functools.partial(_fox_kernel, tq=tq, tk=tk, lp=lp, dh=dh),
        out_shape=jax.ShapeDtypeStruct((bsz, lp, d), BF16),
        grid=(bsz, ng, nq),
        in_specs=[
            pl.BlockSpec((None, tq, hs * dh), lambda b, g, i: (b, i, g)),
            pl.BlockSpec((None, tq, 3 * LANES), lambda b, g, i: (b, i, 0)),
            pl.BlockSpec((None, lp, hs * dh), lambda b, g, i: (b, 0, ng + g)),
            pl.BlockSpec((None, lp, 3 * LANES), lambda b, g, i: (b, 0, 0)),
            pl.BlockSpec((None, lp, hs * dh), lambda b, g, i: (b, 0, 2 * ng + g)),
            pl.BlockSpec((hs, 3 * LANES, LANES), lambda b, g, i: (g, 0, 0)),
            pl.BlockSpec((hs, 3 * LANES, LANES), lambda b, g, i: (g, 0, 0)),
        ],
        out_specs=pl.BlockSpec((None, tq, hs * dh), lambda b, g, i: (b, i, g)),
        scratch_shapes=[pltpu.VMEM((hs, tq, LANES), F32), pltpu.VMEM((hs, tq, LANES), F32),
                        pltpu.VMEM((hs, tq, dh), F32), pltpu.VMEM((hs, tq, tk), F32),
                        pltpu.VMEM((hs, tq, tk), BF16), pltpu.VMEM((hs, tq, LANES), F32),
                        pltpu.VMEM((lp, hs * LANES), BF16)],
        compiler_params=_params(("parallel", "parallel", "arbitrary"), vmem=56 * 1024 * 1024),
        name="fox_attention",
    )(qkv3, pq3, qkv3, pk3, qkv3, selq, selk)
    return out.reshape(tp, d)


def _router_kernel(h_ref, g_ref, wr_ref, br_ref, hn_ref, idx_ref, w_ref, cnt_ref, carry_ref, *, tr):
    i = pl.program_id(0)

    @pl.when(i == 0)
    def _():
        carry_ref[...] = jnp.zeros_like(carry_ref)

    hn = _rms(h_ref[...], g_ref[...])
    _store_token_tiles(hn_ref, (), _pack_halves(hn))
    lg = lax.dot_general(wr_ref[...], hn.astype(BF16), (((1,), (1,)), ((), ())),
                         preferred_element_type=F32) + br_ref[...]
    row = lax.broadcasted_iota(jnp.int32, (ROUTER_ROWS, tr), 0)
    big = jnp.int32(1 << 20)
    is_g = row < N_GROUPS
    mg = jnp.max(jnp.where(is_g, lg, -jnp.inf), axis=0, keepdims=True)
    g_sel = jnp.min(jnp.where(jnp.logical_and(is_g, lg == mg), row, big), axis=0, keepdims=True)
    sg = jnp.sum(jnp.where(is_g, jnp.exp(lg - mg), 0.0), axis=0, keepdims=True)
    p_g = 1.0 / sg
    lo = EXPERT_ROW0 + EXPERTS_PER_GROUP * g_sel
    is_e = jnp.logical_and(row >= lo, row < lo + EXPERTS_PER_GROUP)
    me = jnp.max(jnp.where(is_e, lg, -jnp.inf), axis=0, keepdims=True)
    ee = jnp.where(is_e, jnp.exp(lg - me), 0.0)
    pe = ee / jnp.sum(ee, axis=0, keepdims=True)
    pe1 = jnp.where(is_e, pe, -1.0)
    m1 = jnp.max(pe1, axis=0, keepdims=True)
    i1 = jnp.min(jnp.where(pe1 == m1, row, big), axis=0, keepdims=True)
    pe2 = jnp.where(row == i1, -1.0, pe1)
    m2 = jnp.max(pe2, axis=0, keepdims=True)
    i2 = jnp.min(jnp.where(pe2 == m2, row, big), axis=0, keepdims=True)
    den = m1 + m2
    w1 = p_g * (m1 / den)
    w2 = p_g * (m2 / den)

    oh1 = row == i1
    oh2 = row == i2
    ohs = jnp.logical_or(oh1, oh2).astype(F32)
    r_i = lax.broadcasted_iota(jnp.int32, (tr, tr), 0)
    c_i = lax.broadcasted_iota(jnp.int32, (tr, tr), 1)
    tri = (r_i < c_i).astype(BF16)
    cnt = jnp.dot(ohs.astype(BF16), tri, preferred_element_type=F32) + carry_ref[...]
    rank1 = jnp.sum(jnp.where(oh1, cnt, 0.0), axis=0, keepdims=True)
    rank2 = jnp.sum(jnp.where(oh2, cnt, 0.0), axis=0, keepdims=True)
    carry_ref[...] += jnp.sum(ohs, axis=1, keepdims=True)

    r8 = lax.broadcasted_iota(jnp.int32, (8, tr), 0)
    e1 = i1 - EXPERT_ROW0
    e2 = i2 - EXPERT_ROW0
    idx_ref[...] = jnp.where(r8 == 0, e1, jnp.where(r8 == 1, e2, jnp.where(
        r8 == 2, rank1.astype(jnp.int32), jnp.where(r8 == 3, rank2.astype(jnp.int32), 0))))
    w_ref[...] = jnp.where(r8 == 0, w1, jnp.where(r8 == 1, w2, 0.0))
    cnt_ref[...] = jnp.broadcast_to(carry_ref[...], cnt_ref.shape)


def moe_router(h, gain, w_rg, b_rg, w_re, b_re, tr=512):
    tp, d = h.shape
    assert d // 2 == SUBLANES * LANES
    wr = jnp.zeros((ROUTER_ROWS, d), BF16)
    wr = wr.at[:N_GROUPS].set(w_rg.T.astype(BF16))
    wr = wr.at[EXPERT_ROW0:EXPERT_ROW0 + N_EXPERTS].set(w_re.reshape(d, N_EXPERTS).T.astype(BF16))
    br = jnp.zeros((ROUTER_ROWS, 1), F32)
    br = br.at[:N_GROUPS, 0].set(b_rg.astype(F32))
    br = br.at[EXPERT_ROW0:EXPERT_ROW0 + N_EXPERTS, 0].set(b_re.reshape(N_EXPERTS).astype(F32))
    return pl.pallas_call(
        functools.partial(_router_kernel, tr=tr),
        out_shape=(
            jax.ShapeDtypeStruct((tp * SUBLANES, LANES), jnp.uint32),
            jax.ShapeDtypeStruct((8, tp), jnp.int32),
            jax.ShapeDtypeStruct((8, tp), F32),
            jax.ShapeDtypeStruct((ROUTER_ROWS, LANES), F32),
        ),
        grid=(tp // tr,),
        in_specs=[
            pl.BlockSpec((tr, d), lambda i: (i, 0)),
            pl.BlockSpec((1, d), lambda i: (0, 0)),
            pl.BlockSpec((ROUTER_ROWS, d), lambda i: (0, 0)),
            pl.BlockSpec((ROUTER_ROWS, 1), lambda i: (0, 0)),
        ],
        out_specs=(
            pl.BlockSpec((tr * SUBLANES, LANES), lambda i: (i, 0)),
            pl.BlockSpec((8, tr), lambda i: (0, i)),
            pl.BlockSpec((8, tr), lambda i: (0, i)),
            pl.BlockSpec((ROUTER_ROWS, LANES), lambda i: (0, 0)),
        ),
        scratch_shapes=[pltpu.VMEM((ROUTER_ROWS, 1), F32)],
        compiler_params=_params(("arbitrary",)),
        name="moe_router",
    )(h, gain.reshape(1, d), wr, br)


def _slots_kernel(cnt_ref, idx_ref, pos_ref, meta_ref):
    e1 = idx_ref[0:1, :]
    e2 = idx_ref[1:2, :]
    off1 = jnp.zeros_like(e1)
    off2 = jnp.zeros_like(e2)
    visit = lax.broadcasted_iota(jnp.int32, (1, meta_ref.shape[1]), 1)
    v_tile = jnp.zeros_like(visit)
    v_expert = jnp.zeros_like(visit)
    v_lo = jnp.zeros_like(visit)
    v_hi = jnp.zeros_like(visit)
    v_next = jnp.zeros_like(visit)
    v_start = jnp.zeros_like(visit)
    v_count = jnp.zeros_like(visit)
    next_live = [None] * N_EXPERTS
    nxt = jnp.int32(-1)
    for e in reversed(range(N_EXPERTS)):
        next_live[e] = jnp.where(nxt >= 0, nxt, e)
        nxt = jnp.where(cnt_ref[e] > 0, e, nxt)
    start = jnp.int32(0)
    v_base = jnp.int32(0)
    for e in range(N_EXPERTS):
        off1 = jnp.where(e1 == e, start, off1)
        off2 = jnp.where(e2 == e, start, off2)
        n = cnt_ref[e]
        end = start + n
        first_tile = start // TILE_M
        n_visits = jnp.where(n > 0, (jnp.maximum(end, 1) - 1) // TILE_M - first_tile + 1, 0)
        mine = jnp.logical_and(visit >= v_base, visit < v_base + n_visits)
        row0 = (first_tile + visit - v_base) * TILE_M
        v_tile = jnp.where(mine, first_tile + visit - v_base, v_tile)
        v_expert = jnp.where(mine, e, v_expert)
        v_next = jnp.where(mine, next_live[e], v_next)
        v_start = jnp.where(mine, v_base, v_start)
        v_count = jnp.where(mine, n_visits, v_count)
        v_lo = jnp.where(mine, jnp.maximum(start - row0, 0), v_lo)
        v_hi = jnp.where(mine, jnp.minimum(end - row0, TILE_M), v_hi)
        start = end
        v_base = v_base + n_visits
    r8 = lax.broadcasted_iota(jnp.int32, pos_ref.shape, 0)
    pos_ref[...] = jnp.where(r8 == 0, off1 + idx_ref[2:3, :], jnp.where(r8 == 1, off2 + idx_ref[3:4, :], 0))
    m8 = lax.broadcasted_iota(jnp.int32, meta_ref.shape, 0)
    rows = (v_tile, v_expert, v_lo, v_hi, v_base, v_next, v_start, v_count)
    meta = rows[-1]
    for r in reversed(range(len(rows) - 1)):
        meta = jnp.where(m8 == r, rows[r], meta)
    meta_ref[...] = meta


def moe_slots(counts, idx, tr=512):
    tp = idx.shape[1]
    nt_lanes = 2 * LANES
    return pl.pallas_call(
        _slots_kernel,
        out_shape=(jax.ShapeDtypeStruct((8, tp), jnp.int32), jax.ShapeDtypeStruct((8, nt_lanes), jnp.int32)),
        grid_spec=pltpu.PrefetchScalarGridSpec(
            num_scalar_prefetch=1,
            grid=(tp // tr,),
            in_specs=[pl.BlockSpec((8, tr), lambda i, c: (0, i))],
            out_specs=(
                pl.BlockSpec((8, tr), lambda i, c: (0, i)),
                pl.BlockSpec((8, nt_lanes), lambda i, c: (0, 0)),
            ),
        ),
        compiler_params=_params(("arbitrary",)),
        name="moe_slots",
    )(counts, idx)


def _invert_kernel(pos_ref, code_ref, *, tp):
    def per_token(t, carry):
        code_ref[pos_ref[t]] = 2 * t
        code_ref[pos_ref[tp + t]] = 2 * t + 1
        return carry

    lax.fori_loop(0, tp, per_token, 0, unroll=8)


def moe_invert(pos_flat):
    tp = pos_flat.shape[0] // 2
    return pl.pallas_call(
        functools.partial(_invert_kernel, tp=tp),
        out_shape=jax.ShapeDtypeStruct((2 * tp,), jnp.int32),
        grid_spec=pltpu.PrefetchScalarGridSpec(
            num_scalar_prefetch=1,
            grid=(1,),
            in_specs=[],
            out_specs=pl.BlockSpec(memory_space=pltpu.SMEM),
        ),
        compiler_params=_params(("arbitrary",)),
        name="moe_invert",
    )(pos_flat)


def _expert_kernel(vt_ref, ve_ref, vlo_ref, vhi_ref, nv_ref, vnext_ref, vstart_ref, vcount_ref, code_ref,
                   hn_ref, wg_hbm, wu_hbm, wd_hbm, y_ref,
                   xbuf, xcur, yacc, wgs, wus, wds, wgb, wub, wdb, gsem, wsem, *, n_tiles, layer):
    v = pl.program_id(0)
    nv = nv_ref[0]
    t = vt_ref[v]
    prev_v = jnp.maximum(v - 1, 0)
    first = jnp.logical_or(v == 0, vt_ref[prev_v] != t)
    new_expert = jnp.logical_or(v == 0, ve_ref[prev_v] != ve_ref[v])

    def weight_copies(e):
        w = layer * N_EXPERTS + e
        return (pltpu.make_async_copy(wg_hbm.at[w], wgs, wsem.at[0]),
                pltpu.make_async_copy(wu_hbm.at[w], wus, wsem.at[1]),
                pltpu.make_async_copy(wd_hbm.at[w], wds, wsem.at[2]))

    def row_copy(tile, r):
        tok = lax.shift_right_logical(code_ref[tile * TILE_M + r], 1)
        return pltpu.make_async_copy(hn_ref.at[_token_tile(tok)], xbuf.at[_token_tile(r)], gsem)

    def gather_wait():
        pltpu.make_async_copy(hn_ref.at[pl.ds(0, TILE_M * SUBLANES)], xbuf, gsem).wait()

    def compute(accumulate):
        x = xcur[...]
        a = jnp.dot(x, wgb[...], preferred_element_type=F32)
        u = jnp.dot(x, wub[...], preferred_element_type=F32)
        row = lax.broadcasted_iota(jnp.int32, (TILE_M, 1), 0)
        mine = jnp.logical_and(row >= vlo_ref[v], row < vhi_ref[v])
        hid = jnp.where(mine, a * _sigmoid(a) * u, 0.0).astype(BF16)
        y = jnp.dot(hid, wdb[...], preferred_element_type=F32)
        if accumulate:
            y = y + yacc[...]
        yacc[...] = y
        _store_token_tiles(y_ref, (), _pack_halves(y))

    @pl.when(v == 0)
    def _():
        def step(r, carry):
            row_copy(0, r).start()
            return carry
        lax.fori_loop(0, TILE_M, step, 0, unroll=8)
        for copy in weight_copies(ve_ref[0]):
            copy.start()

    @pl.when(v < nv)
    def _():
        @pl.when(new_expert)
        def _():
            for copy in weight_copies(ve_ref[v]):
                copy.wait()
            wgb[...] = wgs[...].astype(BF16)
            wub[...] = wus[...].astype(BF16)
            wdb[...] = wds[...].astype(BF16)

        @pl.when(first)
        def _():
            gather_wait()
            xcur[...] = _unpack_halves(_load_token_tiles(xbuf, (), 0, TILE_M)).astype(BF16)
            ahead = jnp.minimum(t + 1, n_tiles - 1)
            for r in range(TILE_M):
                row_copy(ahead, r).start()
            compute(False)

        @pl.when(jnp.logical_not(first))
        def _():
            compute(True)

        k = v - vstart_ref[v]
        last_of_expert = k == vcount_ref[v] - 1
        has_next = vnext_ref[v] != ve_ref[v]
        for c, copy in enumerate(weight_copies(vnext_ref[v])):
            @pl.when(jnp.logical_and(has_next, jnp.logical_or(k == c, jnp.logical_and(last_of_expert, k < c))))
            def _():
                copy.start()

        @pl.when(v == nv - 1)
        def _():
            gather_wait()


def moe_experts(meta, code, hn, w_gate, w_up, w_down, layer):
    tp = hn.shape[0] // SUBLANES
    d = 2 * SUBLANES * LANES
    f = w_gate.shape[-1]
    assert (2 * tp) % TILE_M == 0
    n_tiles = (2 * tp) // TILE_M
    max_visits = n_tiles + N_EXPERTS - 1

    def y_map(v, vt, ve, vlo, vhi, nv, *_):
        return (vt[jnp.minimum(v, nv[0] - 1)], 0)

    any_spec = pl.BlockSpec(memory_space=pl.ANY)
    return pl.pallas_call(
        functools.partial(_expert_kernel, n_tiles=n_tiles, layer=layer),
        out_shape=jax.ShapeDtypeStruct((2 * tp * SUBLANES, LANES), jnp.uint32),
        grid_spec=pltpu.PrefetchScalarGridSpec(
            num_scalar_prefetch=9,
            grid=(max_visits,),
            in_specs=[any_spec, any_spec, any_spec, any_spec],
            out_specs=pl.BlockSpec((TILE_M * SUBLANES, LANES), y_map),
            scratch_shapes=[
                pltpu.VMEM((TILE_M * SUBLANES, LANES), jnp.uint32), pltpu.VMEM((TILE_M, d), BF16),
                pltpu.VMEM((TILE_M, d), F32),
                pltpu.VMEM((d, f), F32), pltpu.VMEM((d, f), F32), pltpu.VMEM((f, d), F32),
                pltpu.VMEM((d, f), BF16), pltpu.VMEM((d, f), BF16), pltpu.VMEM((f, d), BF16),
                pltpu.SemaphoreType.DMA, pltpu.SemaphoreType.DMA((3,)),
            ],
        ),
        compiler_params=_params(("arbitrary",), vmem=56 * 1024 * 1024),
        name="moe_experts",
    )(meta[0, :max_visits], meta[1, :max_visits], meta[2, :max_visits], meta[3, :max_visits], meta[4, :1],
      meta[5, :max_visits], meta[6, :max_visits], meta[7, :max_visits], code, hn, w_gate, w_up, w_down)


def _combine_kernel(pos_ref, *refs, tc, tp, n_steps, parts, first_token, write_h):
    h_refs, w_refs = refs[:parts], refs[parts:2 * parts]
    g_ref, y_ref = refs[2 * parts:2 * parts + 2]
    out_refs, (ybuf, sem) = refs[2 * parts + 2:-2], refs[-2:]
    i = pl.program_id(0)
    slot = lax.rem(i, 2)

    def row_copy(step, buf, s, r):
        p = pos_ref[s * tp + first_token(step) + r]
        return pltpu.make_async_copy(y_ref.at[_token_tile(p)], ybuf.at[buf, _token_tile(s * tc + r)],
                                     sem.at[buf])

    @pl.when(i == 0)
    def _():
        def step(r, carry):
            row_copy(0, 0, 0, r).start()
            row_copy(0, 0, 1, r).start()
            return carry
        lax.fori_loop(0, tc, step, 0, unroll=8)

    pltpu.make_async_copy(y_ref.at[pl.ds(0, 2 * tc * SUBLANES)], ybuf.at[slot], sem.at[slot]).wait()

    @pl.when(i + 1 < n_steps)
    def _():
        for r in range(tc):
            row_copy(i + 1, 1 - slot, 0, r).start()
            row_copy(i + 1, 1 - slot, 1, r).start()

    rows = tc // parts
    for c in range(parts):
        w = w_refs[c][...]
        h = (h_refs[c][...] + w[:, 0:1] * _unpack_halves(_load_token_tiles(ybuf, (slot,), c * rows, rows))
             + w[:, 1:2] * _unpack_halves(_load_token_tiles(ybuf, (slot,), tc + c * rows, rows)))
        if write_h:
            out_refs[0][c * rows:(c + 1) * rows, :] = h
        out_refs[-1][c * rows:(c + 1) * rows, :] = _rms(h, g_ref[...]).astype(out_refs[-1].dtype)


def moe_combine(pos_flat, h, w_col, y, gain, hn_dtype, final_shape=None):
    tp, d = h.shape
    tc = 256
    if final_shape is None:
        parts = 1
        n_steps = tp // tc

        def first_token(i):
            return i * tc

        def piece(c, width):
            return pl.BlockSpec((tc, width), lambda i, p: (i, 0))

        out_shape = (jax.ShapeDtypeStruct((tp, d), F32), jax.ShapeDtypeStruct((tp, d), hn_dtype))
        out_specs = (pl.BlockSpec((tc, d), lambda i, p: (i, 0)), pl.BlockSpec((tc, d), lambda i, p: (i, 0)))
    else:
        bsz, seq, _ = final_shape
        parts = tc // CHUNK
        nc = tp // bsz // CHUNK
        per_seq = seq // tc
        n_steps = bsz * per_seq

        def first_token(i):
            return (i // per_seq) * (nc * CHUNK) + CHUNK + (i % per_seq) * tc

        def piece(c, width):
            return pl.BlockSpec((CHUNK, width),
                                lambda i, p: ((i // per_seq) * nc + 1 + (i % per_seq) * parts + c, 0))

        out_shape = (jax.ShapeDtypeStruct(final_shape, hn_dtype),)
        out_specs = (pl.BlockSpec((None, tc, d), lambda i, p: (i // per_seq, i % per_seq, 0)),)
    return pl.pallas_call(
        functools.partial(_combine_kernel, tc=tc, tp=tp, n_steps=n_steps, parts=parts,
                          first_token=first_token, write_h=final_shape is None),
        out_shape=out_shape,
        grid_spec=pltpu.PrefetchScalarGridSpec(
            num_scalar_prefetch=1,
            grid=(n_steps,),
            in_specs=[piece(c, d) for c in range(parts)] + [piece(c, 2) for c in range(parts)] + [
                pl.BlockSpec((1, d), lambda i, p: (0, 0)),
                pl.BlockSpec(memory_space=pl.ANY),
            ],
            out_specs=out_specs,
            scratch_shapes=[pltpu.VMEM((2, 2 * tc * SUBLANES, LANES), jnp.uint32),
                            pltpu.SemaphoreType.DMA((2,))],
        ),
        compiler_params=_params(("arbitrary",)),
        name="moe_combine",
    )(pos_flat, *([h] * parts), *([w_col] * parts), gain.reshape(1, d), y)


def hierarchical_moe(h, gain, w_rg, b_rg, w_re, b_re, w_gate, w_up, w_down, layer, next_gain, hn_dtype,
                     final_shape=None):
    tp, d = h.shape
    f = w_gate.shape[-1]
    hn, idx, w_rows, cnt = moe_router(h, gain, w_rg, b_rg, w_re, b_re)
    counts = cnt[EXPERT_ROW0:EXPERT_ROW0 + N_EXPERTS, 0].astype(jnp.int32)
    pos, meta = moe_slots(counts, idx)
    pos_flat = pos[:2].reshape(2 * tp)
    code = moe_invert(pos_flat)
    y = moe_experts(meta, code, hn, w_gate.reshape(-1, d, f), w_up.reshape(-1, d, f),
                    w_down.reshape(-1, f, d), layer)
    return moe_combine(pos_flat, h, w_rows[:2].T, y, next_gain, hn_dtype, final_shape)


def kernel(x, meta_tokens, norm_mixer, norm_ffn, norm_final, ret_w_in, ret_gn, ret_w_out,
           fox_w_in, fox_b_f, fox_w_out, moe_w_rg, moe_b_rg, moe_w_re, moe_b_re,
           moe_w_gate, moe_w_up, moe_w_down):
    bsz, seq, d = x.shape
    depth = norm_mixer.shape[0]
    nc = (seq + CHUNK) // CHUNK
    lp = nc * CHUNK
    h, hn = embed_norm(x, meta_tokens.astype(x.dtype), norm_mixer[0], BF16)
    for i in range(depth):
        j = i // 2
        if i % 2 == 0:
            proj = matmul(hn, ret_w_in, j, BF16, tm=1536)
            gated = retention_core(proj, ret_gn[j], bsz, nc)
            h = matmul(gated, ret_w_out, j, F32, residual=h, tn=512)
        else:
            qkv = matmul(hn, fox_w_in, j, BF16, n=3 * d, tm=1536, scale_cols=d,
                         scale=(d // FOX_HEADS) ** -0.5 * LOG2E)
            pq, pk = forget_gates(hn, fox_w_in[j, :, 3 * d:], fox_b_f[j], bsz, lp)
            o = fox_attention(qkv, pq, pk, bsz, lp)
            h = matmul(o, fox_w_out, j, F32, residual=h, tm=1536, tn=512)
        last = i == depth - 1
        outs = hierarchical_moe(h, norm_ffn[i], moe_w_rg[i], moe_b_rg[i], moe_w_re[i], moe_b_re[i],
                                moe_w_gate, moe_w_up, moe_w_down, i,
                                norm_final if last else norm_mixer[i + 1],
                                F32 if last else BF16, (bsz, seq, d) if last else None)
        if last:
            return outs[0]
        h, hn = outs
```

```python
import functools

import jax
import jax.numpy as jnp
from jax import lax
from jax.experimental import pallas as pl
from jax.experimental.pallas import tpu as pltpu

N_META = 16
CHUNK = 128
PAD = CHUNK - N_META
RMS_EPS = 1e-6
GN_EPS = 1e-6
NEG_INF = -1e30
RET_HEADS = 8
FOX_HEADS = 16
N_GROUPS = 4
EXPERTS_PER_GROUP = 8
N_EXPERTS = N_GROUPS * EXPERTS_PER_GROUP
ROPE_BASE = 10000.0

LANES = 128
ROUTER_ROWS = 48
EXPERT_ROW0 = N_GROUPS
TILE_M = 256
VMEM_BYTES = 64 * 1024 * 1024
VMEM_LIMIT = 3 * VMEM_BYTES // 4
VMEM_LIMIT_LARGE = 7 * VMEM_BYTES // 8

F32 = jnp.float32
BF16 = jnp.bfloat16


def _params(sem, vmem=VMEM_LIMIT):
    return pltpu.CompilerParams(dimension_semantics=sem, vmem_limit_bytes=vmem)


def _rms(h, g):
    return h * lax.rsqrt(jnp.mean(h * h, axis=-1, keepdims=True) + RMS_EPS) * g


def _sigmoid(x):
    return 1.0 / (1.0 + jnp.exp(-x))


def _pack_halves(x):
    half = x.shape[1] // 2
    return pltpu.pack_elementwise([x[:, :half], x[:, half:]], packed_dtype=BF16)


def _unpack_halves(p):
    lo = pltpu.unpack_elementwise(p, index=0, packed_dtype=BF16, unpacked_dtype=F32)
    hi = pltpu.unpack_elementwise(p, index=1, packed_dtype=BF16, unpacked_dtype=F32)
    return jnp.concatenate([lo, hi], axis=1)


SUBLANES = 8


def _token_tile(t):
    start = t * SUBLANES
    return pl.ds(start if isinstance(t, int) else pl.multiple_of(start, SUBLANES), SUBLANES)


def _store_token_tiles(ref, index, packed):
    m = packed.shape[0]
    for s in range(SUBLANES):
        ref[index + (pl.ds(s, m, stride=SUBLANES), slice(None))] = packed[:, s * LANES:(s + 1) * LANES]


def _load_token_tiles(ref, index, first_token, m):
    return jnp.concatenate(
        [ref[index + (pl.ds(first_token * SUBLANES + s, m, stride=SUBLANES), slice(None))]
         for s in range(SUBLANES)], axis=1)


EMBED_CHUNKS = 3


def _embed_norm_kernel(*refs):
    x_refs, (meta_ref, g_ref, h_ref, hn_ref) = refs[:EMBED_CHUNKS], refs[EMBED_CHUNKS:]
    i = pl.program_id(1)
    for c, x_ref in enumerate(x_refs):
        h_ref[c * CHUNK:(c + 1) * CHUNK, :] = x_ref[...]

    @pl.when(i == 0)
    def _():
        h_ref[:PAD, :] = jnp.zeros((PAD, h_ref.shape[1]), h_ref.dtype)
        h_ref[PAD:CHUNK, :] = meta_ref[...]

    hn_ref[...] = _rms(h_ref[...], g_ref[...]).astype(hn_ref.dtype)


def embed_norm(x, meta, gain, hn_dtype):
    bsz, seq, d = x.shape
    nc = (seq + CHUNK) // CHUNK
    assert nc % EMBED_CHUNKS == 0
    tp = bsz * nc * CHUNK
    steps = nc // EMBED_CHUNKS
    rows = EMBED_CHUNKS * CHUNK

    def x_spec(c):
        return pl.BlockSpec((None, CHUNK, d), lambda b, i: (b, jnp.maximum(EMBED_CHUNKS * i + c - 1, 0), 0))

    return pl.pallas_call(
        _embed_norm_kernel,
        out_shape=(jax.ShapeDtypeStruct((tp, d), F32), jax.ShapeDtypeStruct((tp, d), hn_dtype)),
        grid=(bsz, steps),
        in_specs=[x_spec(c) for c in range(EMBED_CHUNKS)] + [
            pl.BlockSpec((N_META, d), lambda b, i: (0, 0)),
            pl.BlockSpec((1, d), lambda b, i: (0, 0)),
        ],
        out_specs=(
            pl.BlockSpec((rows, d), lambda b, i: (b * steps + i, 0)),
            pl.BlockSpec((rows, d), lambda b, i: (b * steps + i, 0)),
        ),
        compiler_params=_params(("parallel", "parallel")),
        name="embed_norm",
    )(*([x] * EMBED_CHUNKS), meta, gain.reshape(1, d))


def _mm_kernel(*refs, has_res, scale_tiles, scale):
    if has_res:
        x_ref, w_ref, r_ref, o_ref, wb_ref = refs
    else:
        x_ref, w_ref, o_ref, wb_ref = refs

    @pl.when(pl.program_id(1) == 0)
    def _():
        wb_ref[...] = w_ref[...].astype(BF16)

    acc = jnp.dot(x_ref[...], wb_ref[...], preferred_element_type=F32)
    if scale_tiles:
        acc = acc * jnp.where(pl.program_id(0) < scale_tiles, scale, 1.0)
    if has_res:
        acc = acc + r_ref[...]
    o_ref[...] = acc.astype(o_ref.dtype)


def matmul(x, w, layer, out_dtype, n=None, residual=None, tm=512, tn=1024, scale_cols=0, scale=1.0):
    m, kdim = x.shape
    n = w.shape[2] if n is None else n
    assert m % tm == 0 and n % tn == 0 and scale_cols % tn == 0
    in_specs = [
        pl.BlockSpec((tm, kdim), lambda j, i: (i, 0)),
        pl.BlockSpec((None, kdim, tn), lambda j, i: (layer, 0, j)),
    ]
    args = [x, w]
    if residual is not None:
        in_specs.append(pl.BlockSpec((tm, tn), lambda j, i: (i, j)))
        args.append(residual)
    return pl.pallas_call(
        functools.partial(_mm_kernel, has_res=residual is not None, scale_tiles=scale_cols // tn, scale=scale),
        out_shape=jax.ShapeDtypeStruct((m, n), out_dtype),
        grid=(n // tn, m // tm),
        in_specs=in_specs,
        out_specs=pl.BlockSpec((tm, tn), lambda j, i: (i, j)),
        scratch_shapes=[pltpu.VMEM((kdim, tn), BF16)],
        compiler_params=_params(("parallel", "arbitrary")),
        name="matmul",
    )(*args)


def _retention_kernel(q_ref, k_ref, v_ref, g_ref, cos_ref, sin_ref, dmat_ref, xi_ref, zeta_ref,
                      gch_ref, gn_ref, o_ref, state_ref, *, nh, dk, dv):
    c = pl.program_id(1)

    @pl.when(c == 0)
    def _():
        state_ref[...] = jnp.zeros_like(state_ref)

    cos = cos_ref[...]
    sin = sin_ref[...]
    half = dk // 2

    def rot(u):
        u1, u2 = u[:, :half], u[:, half:]
        return jnp.concatenate([u1 * cos - u2 * sin, u1 * sin + u2 * cos], axis=1)

    row = lax.broadcasted_iota(jnp.int32, (CHUNK, 1), 0)
    valid = jnp.logical_or(row >= PAD, c > 0)
    for h in range(nh):
        q = rot(q_ref[:, h * dk:(h + 1) * dk].astype(F32))
        k = rot(k_ref[:, h * dk:(h + 1) * dk].astype(F32)) * (dk ** -0.5)
        k = jnp.where(valid, k, 0.0)
        v = v_ref[:, h * dv:(h + 1) * dv]
        vb = jnp.where(valid, v, jnp.zeros_like(v))
        qb = q.astype(BF16)
        kb = k.astype(BF16)

        scores = lax.dot_general(qb, kb, (((1,), (1,)), ((), ())), preferred_element_type=F32)
        scores = scores * dmat_ref[h]
        inner = jnp.dot(scores.astype(BF16), vb, preferred_element_type=F32)
        state = state_ref[h]
        cross = jnp.dot(qb, state.astype(BF16), preferred_element_type=F32) * xi_ref[h]
        y = inner + cross
        kz = (k * zeta_ref[h]).astype(BF16)
        state_ref[h] = gch_ref[h] * state + lax.dot_general(
            kz, vb, (((0,), (0,)), ((), ())), preferred_element_type=F32)

        mu = jnp.mean(y, axis=-1, keepdims=True)
        dlt = y - mu
        var = jnp.mean(dlt * dlt, axis=-1, keepdims=True)
        yn = dlt * lax.rsqrt(var + GN_EPS) * gn_ref[h]
        hg = 0.5 * g_ref[:, h * dv:(h + 1) * dv].astype(F32)
        o_ref[:, h * dv:(h + 1) * dv] = ((hg + hg * jnp.tanh(hg)) * yn).astype(o_ref.dtype)


def retention_core(proj, gn_gain, bsz, nc):
    tp = proj.shape[0]
    d = proj.shape[1] // 6
    nh = RET_HEADS
    dk = d // nh
    dv = 2 * d // nh
    lp = nc * CHUNK
    half = dk // 2
    pos = (jnp.arange(lp) - PAD).astype(F32)
    inv = ROPE_BASE ** (-jnp.arange(half, dtype=F32) / half)
    ang = pos[:, None] * inv[None, :]
    cos, sin = jnp.cos(ang), jnp.sin(ang)
    log_g = jnp.log1p(-jnp.exp2(-5.0 - jnp.arange(nh, dtype=F32)))
    idx = jnp.arange(CHUNK, dtype=F32)
    diff = idx[:, None] - idx[None, :]
    dmat = jnp.where(diff[None] >= 0, jnp.exp(jnp.maximum(diff, 0.0)[None] * log_g[:, None, None]), 0.0)
    xi = jnp.exp((idx + 1.0)[None, :] * log_g[:, None])[:, :, None]
    zeta = jnp.exp((CHUNK - 1.0 - idx)[None, :] * log_g[:, None])[:, :, None]
    g_chunk = jnp.exp(CHUNK * log_g)[:, None, None]
    return pl.pallas_call(
        functools.partial(_retention_kernel, nh=nh, dk=dk, dv=dv),
        out_shape=jax.ShapeDtypeStruct((tp, 2 * d), BF16),
        grid=(bsz, nc),
        in_specs=[
            pl.BlockSpec((CHUNK, d), lambda b, c: (b * nc + c, 0)),
            pl.BlockSpec((CHUNK, d), lambda b, c: (b * nc + c, 1)),
            pl.BlockSpec((CHUNK, 2 * d), lambda b, c: (b * nc + c, 1)),
            pl.BlockSpec((CHUNK, 2 * d), lambda b, c: (b * nc + c, 2)),
            pl.BlockSpec((CHUNK, half), lambda b, c: (c, 0)),
            pl.BlockSpec((CHUNK, half), lambda b, c: (c, 0)),
            pl.BlockSpec((nh, CHUNK, CHUNK), lambda b, c: (0, 0, 0)),
            pl.BlockSpec((nh, CHUNK, 1), lambda b, c: (0, 0, 0)),
            pl.BlockSpec((nh, CHUNK, 1), lambda b, c: (0, 0, 0)),
            pl.BlockSpec((nh, 1, 1), lambda b, c: (0, 0, 0)),
            pl.BlockSpec((nh, 1, dv), lambda b, c: (0, 0, 0)),
        ],
        out_specs=pl.BlockSpec((CHUNK, 2 * d), lambda b, c: (b * nc + c, 0)),
        scratch_shapes=[pltpu.VMEM((nh, dk, dv), F32)],
        compiler_params=_params(("parallel", "arbitrary")),
        name="retention_core",
    )(proj, proj, proj, proj, cos, sin, dmat, xi, zeta, g_chunk, gn_gain.reshape(nh, 1, dv))


LOG2E = 1.4426950408889634
MASK_BIG = 1e30
ONES_LANE = LANES - 1


def _bias_selectors(nh):
    h = jnp.arange(nh)
    selq = jnp.zeros((nh, 3 * LANES, LANES), F32)
    selk = jnp.zeros((nh, 3 * LANES, LANES), F32)
    for part in range(3):
        selq = selq.at[h, part * LANES + h, part].set(1.0)
        selk = selk.at[h, part * LANES + h, 3 + part].set(-1.0)
        selq = selq.at[h, ONES_LANE, 3 + part].set(1.0)
        selk = selk.at[h, ONES_LANE, part].set(1.0)
    return selq.astype(BF16), selk.astype(BF16)


def _forget_kernel(hn_ref, wf_ref, bf_ref, pq_ref, pk_ref, carry_ref, *, rows):
    i = pl.program_id(1)

    @pl.when(i == 0)
    def _():
        carry_ref[...] = jnp.zeros_like(carry_ref)

    z = jnp.dot(hn_ref[...], wf_ref[...], preferred_element_type=F32) + bf_ref[...]
    lf = jnp.minimum(z, 0.0) - jnp.log1p(jnp.exp(-jnp.abs(z)))
    row = lax.broadcasted_iota(jnp.int32, (rows, 1), 0)
    valid = i * rows + row >= PAD
    lf = jnp.where(valid, lf, 0.0)

    def split3(a):
        hi = a.astype(BF16)
        r1 = a - hi.astype(F32)
        mid = r1.astype(BF16)
        lo = (r1 - mid.astype(F32)).astype(BF16)
        return hi, mid, lo

    r_i = lax.broadcasted_iota(jnp.int32, (rows, rows), 0)
    c_i = lax.broadcasted_iota(jnp.int32, (rows, rows), 1)
    tri = (r_i >= c_i).astype(BF16)
    cs = sum(jnp.dot(tri, part, preferred_element_type=F32) for part in split3(lf))
    cs = cs + carry_ref[...]
    carry_ref[...] = cs[rows - 1:rows, :]

    hi, mid, lo = (p.astype(F32) for p in split3(cs * LOG2E))
    lane = lax.broadcasted_iota(jnp.int32, (rows, LANES), 1)
    ones = lane == ONES_LANE
    live = jnp.logical_and(valid, jnp.logical_not(ones))
    pq_ref[...] = jnp.concatenate(
        [jnp.where(ones, 1.0, hi), jnp.where(ones, 0.0, mid), jnp.where(ones, 0.0, lo)], axis=1).astype(BF16)
    pk_ref[...] = jnp.concatenate(
        [jnp.where(ones, 1.0, jnp.where(valid, hi, MASK_BIG)), jnp.where(live, mid, 0.0),
         jnp.where(live, lo, 0.0)], axis=1).astype(BF16)


def forget_gates(hn, w_f, b_f, bsz, lp, rows=384):
    tp, d = hn.shape
    nh = w_f.shape[1]
    steps = lp // rows
    wf = jnp.zeros((d, LANES), BF16).at[:, :nh].set(w_f.astype(BF16))
    bf = jnp.zeros((1, LANES), F32).at[0, :nh].set(b_f.astype(F32))
    return pl.pallas_call(
        functools.partial(_forget_kernel, rows=rows),
        out_shape=(jax.ShapeDtypeStruct((tp, 3 * LANES), BF16), jax.ShapeDtypeStruct((tp, 3 * LANES), BF16)),
        grid=(bsz, steps),
        in_specs=[
            pl.BlockSpec((rows, d), lambda b, i: (b * steps + i, 0)),
            pl.BlockSpec((d, LANES), lambda b, i: (0, 0)),
            pl.BlockSpec((1, LANES), lambda b, i: (0, 0)),
        ],
        out_specs=(
            pl.BlockSpec((rows, 3 * LANES), lambda b, i: (b * steps + i, 0)),
            pl.BlockSpec((rows, 3 * LANES), lambda b, i: (b * steps + i, 0)),
        ),
        scratch_shapes=[pltpu.VMEM((1, LANES), F32)],
        compiler_params=_params(("parallel", "arbitrary")),
        name="forget_gates",
    )(hn, wf, bf)


FOX_HEADS_PER_STEP = 4


FOX_STRIP = 32


def _fox_kernel(q_ref, pq_ref, k_ref, pk_ref, v_ref, selq_ref, selk_ref, o_ref,
                m_ref, l_ref, acc_ref, s_ref, p_ref, a_ref, ck_ref, *, tq, tk, lp, dh):
    i = pl.program_id(2)
    hs = FOX_HEADS_PER_STEP
    reps = tk // LANES

    @pl.when(i == 0)
    def _():
        for j in range(hs):
            ck_ref[:, j * LANES:(j + 1) * LANES] = jnp.dot(
                pk_ref[...], selk_ref[j], preferred_element_type=F32).astype(BF16)

    def sweep(nr):
        m_ref[:, :nr, :] = jnp.full((hs, nr, LANES), 10.0 * NEG_INF, F32)
        l_ref[:, :nr, :] = jnp.zeros((hs, nr, LANES), F32)
        acc_ref[:, :nr, :] = jnp.zeros((hs, nr, dh), F32)
        qa = [jnp.concatenate(
            [q_ref[:nr, j * dh:(j + 1) * dh],
             jnp.dot(pq_ref[:nr, :], selq_ref[j], preferred_element_type=F32).astype(BF16)], axis=1)
            for j in range(hs)]

        def scores(kb, lo):
            start = pl.multiple_of(kb * tk, tk)
            for j in range(hs):
                ka = jnp.concatenate([k_ref[pl.ds(start, tk), j * dh:(j + 1) * dh],
                                      ck_ref[pl.ds(start, tk), j * LANES:(j + 1) * LANES]], axis=1)
                s_ref[j, lo:nr, :] = lax.dot_general(qa[j][lo:], ka, (((1,), (1,)), ((), ())),
                                                     preferred_element_type=F32)

        def absorb(kb, diag, lo):
            start = pl.multiple_of(kb * tk, tk)
            for j in range(hs):
                for r0 in range(lo, nr, FOX_STRIP):
                    rows = slice(r0, r0 + FOX_STRIP)
                    s = s_ref[j, rows, :]
                    if diag is not None and r0 < diag * tk + tk - 1:
                        r_i = r0 + lax.broadcasted_iota(jnp.int32, (FOX_STRIP, tk), 0)
                        c_i = diag * tk + lax.broadcasted_iota(jnp.int32, (FOX_STRIP, tk), 1)
                        s = jnp.where(c_i <= r_i, s, NEG_INF)
                    m_old = m_ref[j, rows, :]
                    m_new = jnp.maximum(m_old, jnp.max(s, axis=1, keepdims=True))
                    alpha = jnp.exp2(m_old - m_new)
                    p = jnp.exp2(s - jnp.concatenate([m_new] * reps, axis=1))
                    l_ref[j, rows, :] = alpha * l_ref[j, rows, :] + jnp.sum(p, axis=1, keepdims=True)
                    m_ref[j, rows, :] = m_new
                    a_ref[j, rows, :] = alpha
                    p_ref[j, rows, :] = p.astype(BF16)
            for j in range(hs):
                acc_ref[j, lo:nr, :] = a_ref[j, lo:nr, :] * acc_ref[j, lo:nr, :] + jnp.dot(
                    p_ref[j, lo:nr, :], v_ref[pl.ds(start, tk), j * dh:(j + 1) * dh],
                    preferred_element_type=F32)

        def body(kb, carry):
            scores(kb, 0)
            absorb(kb, None, 0)
            return carry

        n_full = i * (tq // tk)
        lax.fori_loop(0, n_full, body, 0)
        for diag in range(nr // tk):
            scores(n_full + diag, diag * tk)
            absorb(n_full + diag, diag, diag * tk)
        for j in range(hs):
            o_ref[:nr, j * dh:(j + 1) * dh] = (acc_ref[j, :nr, :] / l_ref[j, :nr, :]).astype(o_ref.dtype)

    tail = lp % tq
    if tail == 0:
        sweep(tq)
    else:
        @pl.when(i < lp // tq)
        def _():
            sweep(tq)

        @pl.when(i == lp // tq)
        def _():
            sweep(tail)


def fox_attention(qkv, pq, pk, bsz, lp, tq=768, tk=384):
    tp = qkv.shape[0]
    d = qkv.shape[1] // 3
    nh = FOX_HEADS
    dh = d // nh
    assert dh == LANES and tq % tk == 0 and lp % tk == 0 and (lp % tq) % tk == 0
    nq = pl.cdiv(lp, tq)
    hs = FOX_HEADS_PER_STEP
    ng = nh // hs
    qkv3 = qkv.reshape(bsz, lp, 3 * d)
    pq3 = pq.reshape(bsz, lp, 3 * LANES)
    pk3 = pk.reshape(bsz, lp, 3 * LANES)
    selq, selk = _bias_selectors(nh)
    out = pl.pallas_call(
        functools.partial(_fox_kernel, tq=tq, tk=tk, lp=lp, dh=dh),
        out_shape=jax.ShapeDtypeStruct((bsz, lp, d), BF16),
        grid=(bsz, ng, nq),
        in_specs=[
            pl.BlockSpec((None, tq, hs * dh), lambda b, g, i: (b, i, g)),
            pl.BlockSpec((None, tq, 3 * LANES), lambda b, g, i: (b, i, 0)),
            pl.BlockSpec((None, lp, hs * dh), lambda b, g, i: (b, 0, ng + g)),
            pl.BlockSpec((None, lp, 3 * LANES), lambda b, g, i: (b, 0, 0)),
            pl.BlockSpec((None, lp, hs * dh), lambda b, g, i: (b, 0, 2 * ng + g)),
            pl.BlockSpec((hs, 3 * LANES, LANES), lambda b, g, i: (g, 0, 0)),
            pl.BlockSpec((hs, 3 * LANES, LANES), lambda b, g, i: (g, 0, 0)),
        ],
        out_specs=pl.BlockSpec((None, tq, hs * dh), lambda b, g, i: (b, i, g)),
        scratch_shapes=[pltpu.VMEM((hs, tq, LANES), F32), pltpu.VMEM((hs, tq, LANES), F32),
                        pltpu.VMEM((hs, tq, dh), F32), pltpu.VMEM((hs, tq, tk), F32),
                        pltpu.VMEM((hs, tq, tk), BF16), pltpu.VMEM((hs, tq, LANES), F32),
                        pltpu.VMEM((lp, hs * LANES), BF16)],
        compiler_params=_params(("parallel", "parallel", "arbitrary"), vmem=VMEM_LIMIT_LARGE),
        name="fox_attention",
    )(qkv3, pq3, qkv3, pk3, qkv3, selq, selk)
    return out.reshape(tp, d)


def _router_kernel(h_ref, g_ref, wr_ref, br_ref, hn_ref, idx_ref, w_ref, cnt_ref, carry_ref, *, tr):
    i = pl.program_id(0)

    @pl.when(i == 0)
    def _():
        carry_ref[...] = jnp.zeros_like(carry_ref)

    hn = _rms(h_ref[...], g_ref[...])
    _store_token_tiles(hn_ref, (), _pack_halves(hn))
    lg = lax.dot_general(wr_ref[...], hn.astype(BF16), (((1,), (1,)), ((), ())),
                         preferred_element_type=F32) + br_ref[...]
    row = lax.broadcasted_iota(jnp.int32, (ROUTER_ROWS, tr), 0)
    big = jnp.int32(1 << 20)
    is_g = row < N_GROUPS
    mg = jnp.max(jnp.where(is_g, lg, -jnp.inf), axis=0, keepdims=True)
    g_sel = jnp.min(jnp.where(jnp.logical_and(is_g, lg == mg), row, big), axis=0, keepdims=True)
    sg = jnp.sum(jnp.where(is_g, jnp.exp(lg - mg), 0.0), axis=0, keepdims=True)
    p_g = 1.0 / sg
    lo = EXPERT_ROW0 + EXPERTS_PER_GROUP * g_sel
    is_e = jnp.logical_and(row >= lo, row < lo + EXPERTS_PER_GROUP)
    me = jnp.max(jnp.where(is_e, lg, -jnp.inf), axis=0, keepdims=True)
    ee = jnp.where(is_e, jnp.exp(lg - me), 0.0)
    pe = ee / jnp.sum(ee, axis=0, keepdims=True)
    pe1 = jnp.where(is_e, pe, -1.0)
    m1 = jnp.max(pe1, axis=0, keepdims=True)
    i1 = jnp.min(jnp.where(pe1 == m1, row, big), axis=0, keepdims=True)
    pe2 = jnp.where(row == i1, -1.0, pe1)
    m2 = jnp.max(pe2, axis=0, keepdims=True)
    i2 = jnp.min(jnp.where(pe2 == m2, row, big), axis=0, keepdims=True)
    den = m1 + m2
    w1 = p_g * (m1 / den)
    w2 = p_g * (m2 / den)

    oh1 = row == i1
    oh2 = row == i2
    ohs = jnp.logical_or(oh1, oh2).astype(F32)
    r_i = lax.broadcasted_iota(jnp.int32, (tr, tr), 0)
    c_i = lax.broadcasted_iota(jnp.int32, (tr, tr), 1)
    tri = (r_i < c_i).astype(BF16)
    cnt = jnp.dot(ohs.astype(BF16), tri, preferred_element_type=F32) + carry_ref[...]
    rank1 = jnp.sum(jnp.where(oh1, cnt, 0.0), axis=0, keepdims=True)
    rank2 = jnp.sum(jnp.where(oh2, cnt, 0.0), axis=0, keepdims=True)
    carry_ref[...] += jnp.sum(ohs, axis=1, keepdims=True)

    r8 = lax.broadcasted_iota(jnp.int32, (8, tr), 0)
    e1 = i1 - EXPERT_ROW0
    e2 = i2 - EXPERT_ROW0
    idx_ref[...] = jnp.where(r8 == 0, e1, jnp.where(r8 == 1, e2, jnp.where(
        r8 == 2, rank1.astype(jnp.int32), jnp.where(r8 == 3, rank2.astype(jnp.int32), 0))))
    w_ref[...] = jnp.where(r8 == 0, w1, jnp.where(r8 == 1, w2, 0.0))
    cnt_ref[...] = jnp.broadcast_to(carry_ref[...], cnt_ref.shape)


def moe_router(h, gain, w_rg, b_rg, w_re, b_re, tr=512):
    tp, d = h.shape
    assert d // 2 == SUBLANES * LANES
    wr = jnp.zeros((ROUTER_ROWS, d), BF16)
    wr = wr.at[:N_GROUPS].set(w_rg.T.astype(BF16))
    wr = wr.at[EXPERT_ROW0:EXPERT_ROW0 + N_EXPERTS].set(w_re.reshape(d, N_EXPERTS).T.astype(BF16))
    br = jnp.zeros((ROUTER_ROWS, 1), F32)
    br = br.at[:N_GROUPS, 0].set(b_rg.astype(F32))
    br = br.at[EXPERT_ROW0:EXPERT_ROW0 + N_EXPERTS, 0].set(b_re.reshape(N_EXPERTS).astype(F32))
    return pl.pallas_call(
        functools.partial(_router_kernel, tr=tr),
        out_shape=(
            jax.ShapeDtypeStruct((tp * SUBLANES, LANES), jnp.uint32),
            jax.ShapeDtypeStruct((8, tp), jnp.int32),
            jax.ShapeDtypeStruct((8, tp), F32),
            jax.ShapeDtypeStruct((ROUTER_ROWS, LANES), F32),
        ),
        grid=(tp // tr,),
        in_specs=[
            pl.BlockSpec((tr, d), lambda i: (i, 0)),
            pl.BlockSpec((1, d), lambda i: (0, 0)),
            pl.BlockSpec((ROUTER_ROWS, d), lambda i: (0, 0)),
            pl.BlockSpec((ROUTER_ROWS, 1), lambda i: (0, 0)),
        ],
        out_specs=(
            pl.BlockSpec((tr * SUBLANES, LANES), lambda i: (i, 0)),
            pl.BlockSpec((8, tr), lambda i: (0, i)),
            pl.BlockSpec((8, tr), lambda i: (0, i)),
            pl.BlockSpec((ROUTER_ROWS, LANES), lambda i: (0, 0)),
        ),
        scratch_shapes=[pltpu.VMEM((ROUTER_ROWS, 1), F32)],
        compiler_params=_params(("arbitrary",)),
        name="moe_router",
    )(h, gain.reshape(1, d), wr, br)


def _slots_kernel(cnt_ref, idx_ref, pos_ref, meta_ref):
    e1 = idx_ref[0:1, :]
    e2 = idx_ref[1:2, :]
    off1 = jnp.zeros_like(e1)
    off2 = jnp.zeros_like(e2)
    visit = lax.broadcasted_iota(jnp.int32, (1, meta_ref.shape[1]), 1)
    v_tile = jnp.zeros_like(visit)
    v_expert = jnp.zeros_like(visit)
    v_lo = jnp.zeros_like(visit)
    v_hi = jnp.zeros_like(visit)
    v_next = jnp.zeros_like(visit)
    v_start = jnp.zeros_like(visit)
    v_count = jnp.zeros_like(visit)
    next_live = [None] * N_EXPERTS
    nxt = jnp.int32(-1)
    for e in reversed(range(N_EXPERTS)):
        next_live[e] = jnp.where(nxt >= 0, nxt, e)
        nxt = jnp.where(cnt_ref[e] > 0, e, nxt)
    start = jnp.int32(0)
    v_base = jnp.int32(0)
    for e in range(N_EXPERTS):
        off1 = jnp.where(e1 == e, start, off1)
        off2 = jnp.where(e2 == e, start, off2)
        n = cnt_ref[e]
        end = start + n
        first_tile = start // TILE_M
        n_visits = jnp.where(n > 0, (jnp.maximum(end, 1) - 1) // TILE_M - first_tile + 1, 0)
        mine = jnp.logical_and(visit >= v_base, visit < v_base + n_visits)
        row0 = (first_tile + visit - v_base) * TILE_M
        v_tile = jnp.where(mine, first_tile + visit - v_base, v_tile)
        v_expert = jnp.where(mine, e, v_expert)
        v_next = jnp.where(mine, next_live[e], v_next)
        v_start = jnp.where(mine, v_base, v_start)
        v_count = jnp.where(mine, n_visits, v_count)
        v_lo = jnp.where(mine, jnp.maximum(start - row0, 0), v_lo)
        v_hi = jnp.where(mine, jnp.minimum(end - row0, TILE_M), v_hi)
        start = end
        v_base = v_base + n_visits
    r8 = lax.broadcasted_iota(jnp.int32, pos_ref.shape, 0)
    pos_ref[...] = jnp.where(r8 == 0, off1 + idx_ref[2:3, :], jnp.where(r8 == 1, off2 + idx_ref[3:4, :], 0))
    m8 = lax.broadcasted_iota(jnp.int32, meta_ref.shape, 0)
    rows = (v_tile, v_expert, v_lo, v_hi, v_base, v_next, v_start, v_count)
    meta = rows[-1]
    for r in reversed(range(len(rows) - 1)):
        meta = jnp.where(m8 == r, rows[r], meta)
    meta_ref[...] = meta


def moe_slots(counts, idx, tr=512):
    tp = idx.shape[1]
    nt_lanes = 2 * LANES
    return pl.pallas_call(
        _slots_kernel,
        out_shape=(jax.ShapeDtypeStruct((8, tp), jnp.int32), jax.ShapeDtypeStruct((8, nt_lanes), jnp.int32)),
        grid_spec=pltpu.PrefetchScalarGridSpec(
            num_scalar_prefetch=1,
            grid=(tp // tr,),
            in_specs=[pl.BlockSpec((8, tr), lambda i, c: (0, i))],
            out_specs=(
                pl.BlockSpec((8, tr), lambda i, c: (0, i)),
                pl.BlockSpec((8, nt_lanes), lambda i, c: (0, 0)),
            ),
        ),
        compiler_params=_params(("arbitrary",)),
        name="moe_slots",
    )(counts, idx)


def _invert_kernel(pos_ref, code_ref, *, tp):
    def per_token(t, carry):
        code_ref[pos_ref[t]] = 2 * t
        code_ref[pos_ref[tp + t]] = 2 * t + 1
        return carry

    lax.fori_loop(0, tp, per_token, 0, unroll=8)


def moe_invert(pos_flat):
    tp = pos_flat.shape[0] // 2
    return pl.pallas_call(
        functools.partial(_invert_kernel, tp=tp),
        out_shape=jax.ShapeDtypeStruct((2 * tp,), jnp.int32),
        grid_spec=pltpu.PrefetchScalarGridSpec(
            num_scalar_prefetch=1,
            grid=(1,),
            in_specs=[],
            out_specs=pl.BlockSpec(memory_space=pltpu.SMEM),
        ),
        compiler_params=_params(("arbitrary",)),
        name="moe_invert",
    )(pos_flat)


def _expert_kernel(vt_ref, ve_ref, vlo_ref, vhi_ref, nv_ref, vnext_ref, vstart_ref, vcount_ref, code_ref,
                   hn_ref, wg_hbm, wu_hbm, wd_hbm, y_ref,
                   xbuf, xcur, yacc, wgs, wus, wds, wgb, wub, wdb, gsem, wsem, *, n_tiles, layer):
    v = pl.program_id(0)
    nv = nv_ref[0]
    t = vt_ref[v]
    prev_v = jnp.maximum(v - 1, 0)
    first = jnp.logical_or(v == 0, vt_ref[prev_v] != t)
    new_expert = jnp.logical_or(v == 0, ve_ref[prev_v] != ve_ref[v])

    def weight_copies(e):
        w = layer * N_EXPERTS + e
        return (pltpu.make_async_copy(wg_hbm.at[w], wgs, wsem.at[0]),
                pltpu.make_async_copy(wu_hbm.at[w], wus, wsem.at[1]),
                pltpu.make_async_copy(wd_hbm.at[w], wds, wsem.at[2]))

    def row_copy(tile, r):
        tok = lax.shift_right_logical(code_ref[tile * TILE_M + r], 1)
        return pltpu.make_async_copy(hn_ref.at[_token_tile(tok)], xbuf.at[_token_tile(r)], gsem)

    def gather_wait():
        pltpu.make_async_copy(hn_ref.at[pl.ds(0, TILE_M * SUBLANES)], xbuf, gsem).wait()

    def compute(accumulate):
        x = xcur[...]
        a = jnp.dot(x, wgb[...], preferred_element_type=F32)
        u = jnp.dot(x, wub[...], preferred_element_type=F32)
        row = lax.broadcasted_iota(jnp.int32, (TILE_M, 1), 0)
        mine = jnp.logical_and(row >= vlo_ref[v], row < vhi_ref[v])
        hid = jnp.where(mine, a * _sigmoid(a) * u, 0.0).astype(BF16)
        y = jnp.dot(hid, wdb[...], preferred_element_type=F32)
        if accumulate:
            y = y + yacc[...]
        yacc[...] = y
        _store_token_tiles(y_ref, (), _pack_halves(y))

    @pl.when(v == 0)
    def _():
        def step(r, carry):
            row_copy(0, r).start()
            return carry
        lax.fori_loop(0, TILE_M, step, 0, unroll=8)
        for copy in weight_copies(ve_ref[0]):
            copy.start()

    @pl.when(v < nv)
    def _():
        @pl.when(new_expert)
        def _():
            for copy in weight_copies(ve_ref[v]):
                copy.wait()
            wgb[...] = wgs[...].astype(BF16)
            wub[...] = wus[...].astype(BF16)
            wdb[...] = wds[...].astype(BF16)

        @pl.when(first)
        def _():
            gather_wait()
            xcur[...] = _unpack_halves(_load_token_tiles(xbuf, (), 0, TILE_M)).astype(BF16)
            ahead = jnp.minimum(t + 1, n_tiles - 1)
            for r in range(TILE_M):
                row_copy(ahead, r).start()
            compute(False)

        @pl.when(jnp.logical_not(first))
        def _():
            compute(True)

        k = v - vstart_ref[v]
        last_of_expert = k == vcount_ref[v] - 1
        has_next = vnext_ref[v] != ve_ref[v]
        for c, copy in enumerate(weight_copies(vnext_ref[v])):
            @pl.when(jnp.logical_and(has_next, jnp.logical_or(k == c, jnp.logical_and(last_of_expert, k < c))))
            def _():
                copy.start()

        @pl.when(v == nv - 1)
        def _():
            gather_wait()


def moe_experts(meta, code, hn, w_gate, w_up, w_down, layer):
    tp = hn.shape[0] // SUBLANES
    d = 2 * SUBLANES * LANES
    f = w_gate.shape[-1]
    assert (2 * tp) % TILE_M == 0
    n_tiles = (2 * tp) // TILE_M
    max_visits = n_tiles + N_EXPERTS - 1

    def y_map(v, vt, ve, vlo, vhi, nv, *_):
        return (vt[jnp.minimum(v, nv[0] - 1)], 0)

    any_spec = pl.BlockSpec(memory_space=pl.ANY)
    return pl.pallas_call(
        functools.partial(_expert_kernel, n_tiles=n_tiles, layer=layer),
        out_shape=jax.ShapeDtypeStruct((2 * tp * SUBLANES, LANES), jnp.uint32),
        grid_spec=pltpu.PrefetchScalarGridSpec(
            num_scalar_prefetch=9,
            grid=(max_visits,),
            in_specs=[any_spec, any_spec, any_spec, any_spec],
            out_specs=pl.BlockSpec((TILE_M * SUBLANES, LANES), y_map),
            scratch_shapes=[
                pltpu.VMEM((TILE_M * SUBLANES, LANES), jnp.uint32), pltpu.VMEM((TILE_M, d), BF16),
                pltpu.VMEM((TILE_M, d), F32),
                pltpu.VMEM((d, f), F32), pltpu.VMEM((d, f), F32), pltpu.VMEM((f, d), F32),
                pltpu.VMEM((d, f), BF16), pltpu.VMEM((d, f), BF16), pltpu.VMEM((f, d), BF16),
                pltpu.SemaphoreType.DMA, pltpu.SemaphoreType.DMA((3,)),
            ],
        ),
        compiler_params=_params(("arbitrary",), vmem=VMEM_LIMIT_LARGE),
        name="moe_experts",
    )(meta[0, :max_visits], meta[1, :max_visits], meta[2, :max_visits], meta[3, :max_visits], meta[4, :1],
      meta[5, :max_visits], meta[6, :max_visits], meta[7, :max_visits], code, hn, w_gate, w_up, w_down)


def _combine_kernel(pos_ref, *refs, tc, tp, n_steps, parts, first_token, write_h):
    h_refs, w_refs = refs[:parts], refs[parts:2 * parts]
    g_ref, y_ref = refs[2 * parts:2 * parts + 2]
    out_refs, (ybuf, sem) = refs[2 * parts + 2:-2], refs[-2:]
    i = pl.program_id(0)
    slot = lax.rem(i, 2)

    def row_copy(step, buf, s, r):
        p = pos_ref[s * tp + first_token(step) + r]
        return pltpu.make_async_copy(y_ref.at[_token_tile(p)], ybuf.at[buf, _token_tile(s * tc + r)],
                                     sem.at[buf])

    @pl.when(i == 0)
    def _():
        def step(r, carry):
            row_copy(0, 0, 0, r).start()
            row_copy(0, 0, 1, r).start()
            return carry
        lax.fori_loop(0, tc, step, 0, unroll=8)

    pltpu.make_async_copy(y_ref.at[pl.ds(0, 2 * tc * SUBLANES)], ybuf.at[slot], sem.at[slot]).wait()

    @pl.when(i + 1 < n_steps)
    def _():
        for r in range(tc):
            row_copy(i + 1, 1 - slot, 0, r).start()
            row_copy(i + 1, 1 - slot, 1, r).start()

    rows = tc // parts
    for c in range(parts):
        w = w_refs[c][...]
        h = (h_refs[c][...] + w[:, 0:1] * _unpack_halves(_load_token_tiles(ybuf, (slot,), c * rows, rows))
             + w[:, 1:2] * _unpack_halves(_load_token_tiles(ybuf, (slot,), tc + c * rows, rows)))
        if write_h:
            out_refs[0][c * rows:(c + 1) * rows, :] = h
        out_refs[-1][c * rows:(c + 1) * rows, :] = _rms(h, g_ref[...]).astype(out_refs[-1].dtype)


def moe_combine(pos_flat, h, w_col, y, gain, hn_dtype, final_shape=None):
    tp, d = h.shape
    tc = 256
    if final_shape is None:
        parts = 1
        n_steps = tp // tc

        def first_token(i):
            return i * tc

        def piece(c, width):
            return pl.BlockSpec((tc, width), lambda i, p: (i, 0))

        out_shape = (jax.ShapeDtypeStruct((tp, d), F32), jax.ShapeDtypeStruct((tp, d), hn_dtype))
        out_specs = (pl.BlockSpec((tc, d), lambda i, p: (i, 0)), pl.BlockSpec((tc, d), lambda i, p: (i, 0)))
    else:
        bsz, seq, _ = final_shape
        parts = tc // CHUNK
        nc = tp // bsz // CHUNK
        per_seq = seq // tc
        n_steps = bsz * per_seq

        def first_token(i):
            return (i // per_seq) * (nc * CHUNK) + CHUNK + (i % per_seq) * tc

        def piece(c, width):
            return pl.BlockSpec((CHUNK, width),
                                lambda i, p: ((i // per_seq) * nc + 1 + (i % per_seq) * parts + c, 0))

        out_shape = (jax.ShapeDtypeStruct(final_shape, hn_dtype),)
        out_specs = (pl.BlockSpec((None, tc, d), lambda i, p: (i // per_seq, i % per_seq, 0)),)
    return pl.pallas_call(
        functools.partial(_combine_kernel, tc=tc, tp=tp, n_steps=n_steps, parts=parts,
                          first_token=first_token, write_h=final_shape is None),
        out_shape=out_shape,
        grid_spec=pltpu.PrefetchScalarGridSpec(
            num_scalar_prefetch=1,
            grid=(n_steps,),
            in_specs=[piece(c, d) for c in range(parts)] + [piece(c, 2) for c in range(parts)] + [
                pl.BlockSpec((1, d), lambda i, p: (0, 0)),
                pl.BlockSpec(memory_space=pl.ANY),
            ],
            out_specs=out_specs,
            scratch_shapes=[pltpu.VMEM((2, 2 * tc * SUBLANES, LANES), jnp.uint32),
                            pltpu.SemaphoreType.DMA((2,))],
        ),
        compiler_params=_params(("arbitrary",)),
        name="moe_combine",
    )(pos_flat, *([h] * parts), *([w_col] * parts), gain.reshape(1, d), y)


def hierarchical_moe(h, gain, w_rg, b_rg, w_re, b_re, w_gate, w_up, w_down, layer, next_gain, hn_dtype,
                     final_shape=None):
    tp, d = h.shape
    f = w_gate.shape[-1]
    hn, idx, w_rows, cnt = moe_router(h, gain, w_rg, b_rg, w_re, b_re)
    counts = cnt[EXPERT_ROW0:EXPERT_ROW0 + N_EXPERTS, 0].astype(jnp.int32)
    pos, meta = moe_slots(counts, idx)
    pos_flat = pos[:2].reshape(2 * tp)
    code = moe_invert(pos_flat)
    y = moe_experts(meta, code, hn, w_gate.reshape(-1, d, f), w_up.reshape(-1, d, f),
                    w_down.reshape(-1, f, d), layer)
    return moe_combine(pos_flat, h, w_rows[:2].T, y, next_gain, hn_dtype, final_shape)


def kernel(x, meta_tokens, norm_mixer, norm_ffn, norm_final, ret_w_in, ret_gn, ret_w_out,
           fox_w_in, fox_b_f, fox_w_out, moe_w_rg, moe_b_rg, moe_w_re, moe_b_re,
           moe_w_gate, moe_w_up, moe_w_down):
    bsz, seq, d = x.shape
    depth = norm_mixer.shape[0]
    nc = (seq + CHUNK) // CHUNK
    lp = nc * CHUNK
    h, hn = embed_norm(x, meta_tokens.astype(x.dtype), norm_mixer[0], BF16)
    for i in range(depth):
        j = i // 2
        if i % 2 == 0:
            proj = matmul(hn, ret_w_in, j, BF16, tm=1536)
            gated = retention_core(proj, ret_gn[j], bsz, nc)
            h = matmul(gated, ret_w_out, j, F32, residual=h, tn=512)
        else:
            qkv = matmul(hn, fox_w_in, j, BF16, n=3 * d, tm=1536, scale_cols=d,
                         scale=(d // FOX_HEADS) ** -0.5 * LOG2E)
            pq, pk = forget_gates(hn, fox_w_in[j, :, 3 * d:], fox_b_f[j], bsz, lp)
            o = fox_attention(qkv, pq, pk, bsz, lp)
            h = matmul(o, fox_w_out, j, F32, residual=h, tm=1536, tn=512)
        last = i == depth - 1
        outs = hierarchical_moe(h, norm_ffn[i], moe_w_rg[i], moe_b_rg[i], moe_w_re[i], moe_b_re[i],
                                moe_w_gate, moe_w_up, moe_w_down, i,
                                norm_final if last else norm_mixer[i + 1],
                                F32 if last else BF16, (bsz, seq, d) if last else None)
        if last:
            return outs[0]
        h, hn = outs
```

```python
import functools

import jax
import jax.numpy as jnp
from jax import lax
from jax.experimental import pallas as pl
from jax.experimental.pallas import tpu as pltpu

N_META = 16
CHUNK = 128
PAD = CHUNK - N_META
RMS_EPS = 1e-6
GN_EPS = 1e-6
NEG_INF = -1e30
RET_HEADS = 8
FOX_HEADS = 16
N_GROUPS = 4
EXPERTS_PER_GROUP = 8
N_EXPERTS = N_GROUPS * EXPERTS_PER_GROUP
ROPE_BASE = 10000.0

LANES = 128
ROUTER_ROWS = 48
EXPERT_ROW0 = N_GROUPS
TILE_M = 256
VMEM_BYTES = 64 * 1024 * 1024
VMEM_LIMIT = 3 * VMEM_BYTES // 4
VMEM_LIMIT_LARGE = 7 * VMEM_BYTES // 8

F32 = jnp.float32
BF16 = jnp.bfloat16


def _params(sem, vmem=VMEM_LIMIT):
    return pltpu.CompilerParams(dimension_semantics=sem, vmem_limit_bytes=vmem)


def _rms(h, g):
    return h * lax.rsqrt(jnp.mean(h * h, axis=-1, keepdims=True) + RMS_EPS) * g


def _sigmoid(x):
    return 1.0 / (1.0 + jnp.exp(-x))


def _pack_halves(x):
    half = x.shape[1] // 2
    return pltpu.pack_elementwise([x[:, :half], x[:, half:]], packed_dtype=BF16)


def _unpack_halves(p):
    lo = pltpu.unpack_elementwise(p, index=0, packed_dtype=BF16, unpacked_dtype=F32)
    hi = pltpu.unpack_elementwise(p, index=1, packed_dtype=BF16, unpacked_dtype=F32)
    return jnp.concatenate([lo, hi], axis=1)


SUBLANES = 8


def _token_tile(t):
    start = t * SUBLANES
    return pl.ds(start if isinstance(t, int) else pl.multiple_of(start, SUBLANES), SUBLANES)


def _store_token_tiles(ref, index, packed):
    m = packed.shape[0]
    for s in range(SUBLANES):
        ref[index + (pl.ds(s, m, stride=SUBLANES), slice(None))] = packed[:, s * LANES:(s + 1) * LANES]


def _load_token_tiles(ref, index, first_token, m):
    return jnp.concatenate(
        [ref[index + (pl.ds(first_token * SUBLANES + s, m, stride=SUBLANES), slice(None))]
         for s in range(SUBLANES)], axis=1)


EMBED_CHUNKS = 3


def _embed_norm_kernel(*refs):
    x_refs, (meta_ref, g_ref, h_ref, hn_ref) = refs[:EMBED_CHUNKS], refs[EMBED_CHUNKS:]
    i = pl.program_id(1)
    for c, x_ref in enumerate(x_refs):
        h_ref[c * CHUNK:(c + 1) * CHUNK, :] = x_ref[...]

    @pl.when(i == 0)
    def _():
        h_ref[:PAD, :] = jnp.zeros((PAD, h_ref.shape[1]), h_ref.dtype)
        h_ref[PAD:CHUNK, :] = meta_ref[...]

    hn_ref[...] = _rms(h_ref[...], g_ref[...]).astype(hn_ref.dtype)


def embed_norm(x, meta, gain, hn_dtype):
    bsz, seq, d = x.shape
    nc = (seq + CHUNK) // CHUNK
    assert nc % EMBED_CHUNKS == 0
    tp = bsz * nc * CHUNK
    steps = nc // EMBED_CHUNKS
    rows = EMBED_CHUNKS * CHUNK

    def x_spec(c):
        return pl.BlockSpec((None, CHUNK, d), lambda b, i: (b, jnp.maximum(EMBED_CHUNKS * i + c - 1, 0), 0))

    return pl.pallas_call(
        _embed_norm_kernel,
        out_shape=(jax.ShapeDtypeStruct((tp, d), F32), jax.ShapeDtypeStruct((tp, d), hn_dtype)),
        grid=(bsz, steps),
        in_specs=[x_spec(c) for c in range(EMBED_CHUNKS)] + [
            pl.BlockSpec((N_META, d), lambda b, i: (0, 0)),
            pl.BlockSpec((1, d), lambda b, i: (0, 0)),
        ],
        out_specs=(
            pl.BlockSpec((rows, d), lambda b, i: (b * steps + i, 0)),
            pl.BlockSpec((rows, d), lambda b, i: (b * steps + i, 0)),
        ),
        compiler_params=_params(("parallel", "parallel")),
        name="embed_norm",
    )(*([x] * EMBED_CHUNKS), meta, gain.reshape(1, d))


def _mm_kernel(*refs, has_res, scale_tiles, scale):
    if has_res:
        x_ref, w_ref, r_ref, o_ref, wb_ref = refs
    else:
        x_ref, w_ref, o_ref, wb_ref = refs

    @pl.when(pl.program_id(1) == 0)
    def _():
        wb_ref[...] = w_ref[...].astype(BF16)

    acc = jnp.dot(x_ref[...], wb_ref[...], preferred_element_type=F32)
    if scale_tiles:
        acc = acc * jnp.where(pl.program_id(0) < scale_tiles, scale, 1.0)
    if has_res:
        acc = acc + r_ref[...]
    o_ref[...] = acc.astype(o_ref.dtype)


def matmul(x, w, layer, out_dtype, n=None, residual=None, tm=512, tn=1024, scale_cols=0, scale=1.0):
    m, kdim = x.shape
    n = w.shape[2] if n is None else n
    assert m % tm == 0 and n % tn == 0 and scale_cols % tn == 0
    in_specs = [
        pl.BlockSpec((tm, kdim), lambda j, i: (i, 0)),
        pl.BlockSpec((None, kdim, tn), lambda j, i: (layer, 0, j)),
    ]
    args = [x, w]
    if residual is not None:
        in_specs.append(pl.BlockSpec((tm, tn), lambda j, i: (i, j)))
        args.append(residual)
    return pl.pallas_call(
        functools.partial(_mm_kernel, has_res=residual is not None, scale_tiles=scale_cols // tn, scale=scale),
        out_shape=jax.ShapeDtypeStruct((m, n), out_dtype),
        grid=(n // tn, m // tm),
        in_specs=in_specs,
        out_specs=pl.BlockSpec((tm, tn), lambda j, i: (i, j)),
        scratch_shapes=[pltpu.VMEM((kdim, tn), BF16)],
        compiler_params=_params(("parallel", "arbitrary")),
        name="matmul",
    )(*args)


def _retention_kernel(q_ref, k_ref, v_ref, g_ref, cos_ref, sin_ref, dmat_ref, xi_ref, zeta_ref,
                      gch_ref, gn_ref, o_ref, state_ref, *, nh, dk, dv):
    c = pl.program_id(1)

    @pl.when(c == 0)
    def _():
        state_ref[...] = jnp.zeros_like(state_ref)

    cos = cos_ref[...]
    sin = sin_ref[...]
    half = dk // 2

    def rot(u):
        u1, u2 = u[:, :half], u[:, half:]
        return jnp.concatenate([u1 * cos - u2 * sin, u1 * sin + u2 * cos], axis=1)

    row = lax.broadcasted_iota(jnp.int32, (CHUNK, 1), 0)
    valid = jnp.logical_or(row >= PAD, c > 0)
    for h in range(nh):
        q = rot(q_ref[:, h * dk:(h + 1) * dk].astype(F32))
        k = rot(k_ref[:, h * dk:(h + 1) * dk].astype(F32)) * (dk ** -0.5)
        k = jnp.where(valid, k, 0.0)
        v = v_ref[:, h * dv:(h + 1) * dv]
        vb = jnp.where(valid, v, jnp.zeros_like(v))
        qb = q.astype(BF16)
        kb = k.astype(BF16)

        scores = lax.dot_general(qb, kb, (((1,), (1,)), ((), ())), preferred_element_type=F32)
        scores = scores * dmat_ref[h]
        inner = jnp.dot(scores.astype(BF16), vb, preferred_element_type=F32)
        state = state_ref[h]
        cross = jnp.dot(qb, state.astype(BF16), preferred_element_type=F32) * xi_ref[h]
        y = inner + cross
        kz = (k * zeta_ref[h]).astype(BF16)
        state_ref[h] = gch_ref[h] * state + lax.dot_general(
            kz, vb, (((0,), (0,)), ((), ())), preferred_element_type=F32)

        mu = jnp.mean(y, axis=-1, keepdims=True)
        dlt = y - mu
        var = jnp.mean(dlt * dlt, axis=-1, keepdims=True)
        yn = dlt * lax.rsqrt(var + GN_EPS) * gn_ref[h]
        hg = 0.5 * g_ref[:, h * dv:(h + 1) * dv].astype(F32)
        o_ref[:, h * dv:(h + 1) * dv] = ((hg + hg * jnp.tanh(hg)) * yn).astype(o_ref.dtype)


def retention_core(proj, gn_gain, bsz, nc):
    tp = proj.shape[0]
    d = proj.shape[1] // 6
    nh = RET_HEADS
    dk = d // nh
    dv = 2 * d // nh
    lp = nc * CHUNK
    half = dk // 2
    pos = (jnp.arange(lp) - PAD).astype(F32)
    inv = ROPE_BASE ** (-jnp.arange(half, dtype=F32) / half)
    ang = pos[:, None] * inv[None, :]
    cos, sin = jnp.cos(ang), jnp.sin(ang)
    log_g = jnp.log1p(-jnp.exp2(-5.0 - jnp.arange(nh, dtype=F32)))
    idx = jnp.arange(CHUNK, dtype=F32)
    diff = idx[:, None] - idx[None, :]
    dmat = jnp.where(diff[None] >= 0, jnp.exp(jnp.maximum(diff, 0.0)[None] * log_g[:, None, None]), 0.0)
    xi = jnp.exp((idx + 1.0)[None, :] * log_g[:, None])[:, :, None]
    zeta = jnp.exp((CHUNK - 1.0 - idx)[None, :] * log_g[:, None])[:, :, None]
    g_chunk = jnp.exp(CHUNK * log_g)[:, None, None]
    return pl.pallas_call(
        functools.partial(_retention_kernel, nh=nh, dk=dk, dv=dv),
        out_shape=jax.ShapeDtypeStruct((tp, 2 * d), BF16),
        grid=(bsz, nc),
        in_specs=[
            pl.BlockSpec((CHUNK, d), lambda b, c: (b * nc + c, 0)),
            pl.BlockSpec((CHUNK, d), lambda b, c: (b * nc + c, 1)),
            pl.BlockSpec((CHUNK, 2 * d), lambda b, c: (b * nc + c, 1)),
            pl.BlockSpec((CHUNK, 2 * d), lambda b, c: (b * nc + c, 2)),
            pl.BlockSpec((CHUNK, half), lambda b, c: (c, 0)),
            pl.BlockSpec((CHUNK, half), lambda b, c: (c, 0)),
            pl.BlockSpec((nh, CHUNK, CHUNK), lambda b, c: (0, 0, 0)),
            pl.BlockSpec((nh, CHUNK, 1), lambda b, c: (0, 0, 0)),
            pl.BlockSpec((nh, CHUNK, 1), lambda b, c: (0, 0, 0)),
            pl.BlockSpec((nh, 1, 1), lambda b, c: (0, 0, 0)),
            pl.BlockSpec((nh, 1, dv), lambda b, c: (0, 0, 0)),
        ],
        out_specs=pl.BlockSpec((CHUNK, 2 * d), lambda b, c: (b * nc + c, 0)),
        scratch_shapes=[pltpu.VMEM((nh, dk, dv), F32)],
        compiler_params=_params(("parallel", "arbitrary")),
        name="retention_core",
    )(proj, proj, proj, proj, cos, sin, dmat, xi, zeta, g_chunk, gn_gain.reshape(nh, 1, dv))


LOG2E = 1.4426950408889634
MASK_BIG = 1e30
ONES_LANE = LANES - 1


def _bias_selectors(nh):
    h = jnp.arange(nh)
    selq = jnp.zeros((nh, LANES, LANES), F32)
    selk = jnp.zeros((nh, LANES, LANES), F32)
    for part in range(3):
        selq = selq.at[h, part * nh + h, part].set(1.0)
        selk = selk.at[h, part * nh + h, 3 + part].set(-1.0)
        selq = selq.at[h, ONES_LANE, 3 + part].set(1.0)
        selk = selk.at[h, ONES_LANE, part].set(1.0)
    return selq.astype(BF16), selk.astype(BF16)


def _forget_kernel(hn_ref, wf_ref, bf_ref, pq_ref, pk_ref, carry_ref, *, rows, nh):
    i = pl.program_id(1)

    @pl.when(i == 0)
    def _():
        carry_ref[...] = jnp.zeros_like(carry_ref)

    z = jnp.dot(hn_ref[...], wf_ref[...], preferred_element_type=F32) + bf_ref[...]
    lf = jnp.minimum(z, 0.0) - jnp.log1p(jnp.exp(-jnp.abs(z)))
    row = lax.broadcasted_iota(jnp.int32, (rows, 1), 0)
    valid = i * rows + row >= PAD
    lf = jnp.where(valid, lf, 0.0)

    def split3(a):
        hi = a.astype(BF16)
        r1 = a - hi.astype(F32)
        mid = r1.astype(BF16)
        lo = (r1 - mid.astype(F32)).astype(BF16)
        return hi, mid, lo

    r_i = lax.broadcasted_iota(jnp.int32, (rows, rows), 0)
    c_i = lax.broadcasted_iota(jnp.int32, (rows, rows), 1)
    tri = (r_i >= c_i).astype(BF16)
    cs = sum(jnp.dot(tri, part, preferred_element_type=F32) for part in split3(lf))
    cs = cs + carry_ref[...]
    carry_ref[...] = cs[rows - 1:rows, :]

    hi, mid, lo = (p.astype(F32) for p in split3(cs * LOG2E))
    lane = lax.broadcasted_iota(jnp.int32, (rows, LANES), 1)

    def lay_out(parts):
        row = jnp.where(lane == ONES_LANE, 1.0, 0.0)
        for k, part in enumerate(parts):
            moved = part if k == 0 else pltpu.roll(part, k * nh, axis=1)
            row = jnp.where(jnp.logical_and(lane >= k * nh, lane < (k + 1) * nh), moved, row)
        return row.astype(BF16)

    pq_ref[...] = lay_out((hi, mid, lo))
    pk_ref[...] = lay_out((jnp.where(valid, hi, MASK_BIG), jnp.where(valid, mid, 0.0),
                           jnp.where(valid, lo, 0.0)))


def forget_gates(hn, w_f, b_f, bsz, lp, rows=384):
    tp, d = hn.shape
    nh = w_f.shape[1]
    assert 3 * nh <= ONES_LANE and lp % rows == 0
    steps = lp // rows
    wf = jnp.zeros((d, LANES), BF16).at[:, :nh].set(w_f.astype(BF16))
    bf = jnp.zeros((1, LANES), F32).at[0, :nh].set(b_f.astype(F32))
    return pl.pallas_call(
        functools.partial(_forget_kernel, rows=rows, nh=nh),
        out_shape=(jax.ShapeDtypeStruct((tp, LANES), BF16), jax.ShapeDtypeStruct((tp, LANES), BF16)),
        grid=(bsz, steps),
        in_specs=[
            pl.BlockSpec((rows, d), lambda b, i: (b * steps + i, 0)),
            pl.BlockSpec((d, LANES), lambda b, i: (0, 0)),
            pl.BlockSpec((1, LANES), lambda b, i: (0, 0)),
        ],
        out_specs=(
            pl.BlockSpec((rows, LANES), lambda b, i: (b * steps + i, 0)),
            pl.BlockSpec((rows, LANES), lambda b, i: (b * steps + i, 0)),
        ),
        scratch_shapes=[pltpu.VMEM((1, LANES), F32)],
        compiler_params=_params(("parallel", "arbitrary")),
        name="forget_gates",
    )(hn, wf, bf)


FOX_HEADS_PER_STEP = 4


FOX_STRIP = 32


def _fox_kernel(q_ref, pq_ref, k_ref, pk_ref, v_ref, selq_ref, selk_ref, o_ref,
                m_ref, l_ref, acc_ref, s_ref, p_ref, a_ref, ck_ref, *, tq, tk, lp, dh):
    i = pl.program_id(2)
    hs = FOX_HEADS_PER_STEP
    reps = tk // LANES

    @pl.when(i == 0)
    def _():
        for j in range(hs):
            ck_ref[:, j * LANES:(j + 1) * LANES] = jnp.dot(
                pk_ref[...], selk_ref[j], preferred_element_type=F32).astype(BF16)

    def sweep(nr):
        m_ref[:, :nr, :] = jnp.full((hs, nr, LANES), 10.0 * NEG_INF, F32)
        l_ref[:, :nr, :] = jnp.zeros((hs, nr, LANES), F32)
        acc_ref[:, :nr, :] = jnp.zeros((hs, nr, dh), F32)
        qa = [jnp.concatenate(
            [q_ref[:nr, j * dh:(j + 1) * dh],
             jnp.dot(pq_ref[:nr, :], selq_ref[j], preferred_element_type=F32).astype(BF16)], axis=1)
            for j in range(hs)]

        def scores(kb, lo):
            start = pl.multiple_of(kb * tk, tk)
            for j in range(hs):
                ka = jnp.concatenate([k_ref[pl.ds(start, tk), j * dh:(j + 1) * dh],
                                      ck_ref[pl.ds(start, tk), j * LANES:(j + 1) * LANES]], axis=1)
                s_ref[j, lo:nr, :] = lax.dot_general(qa[j][lo:], ka, (((1,), (1,)), ((), ())),
                                                     preferred_element_type=F32)

        def absorb(kb, diag, lo):
            start = pl.multiple_of(kb * tk, tk)
            for j in range(hs):
                for r0 in range(lo, nr, FOX_STRIP):
                    rows = slice(r0, r0 + FOX_STRIP)
                    s = s_ref[j, rows, :]
                    if diag is not None and r0 < diag * tk + tk - 1:
                        r_i = r0 + lax.broadcasted_iota(jnp.int32, (FOX_STRIP, tk), 0)
                        c_i = diag * tk + lax.broadcasted_iota(jnp.int32, (FOX_STRIP, tk), 1)
                        s = jnp.where(c_i <= r_i, s, NEG_INF)
                    m_old = m_ref[j, rows, :]
                    m_new = jnp.maximum(m_old, jnp.max(s, axis=1, keepdims=True))
                    alpha = jnp.exp2(m_old - m_new)
                    p = jnp.exp2(s - jnp.concatenate([m_new] * reps, axis=1))
                    l_ref[j, rows, :] = alpha * l_ref[j, rows, :] + jnp.sum(p, axis=1, keepdims=True)
                    m_ref[j, rows, :] = m_new
                    a_ref[j, rows, :] = alpha
                    p_ref[j, rows, :] = p.astype(BF16)
            for j in range(hs):
                acc_ref[j, lo:nr, :] = a_ref[j, lo:nr, :] * acc_ref[j, lo:nr, :] + jnp.dot(
                    p_ref[j, lo:nr, :], v_ref[pl.ds(start, tk), j * dh:(j + 1) * dh],
                    preferred_element_type=F32)

        def body(kb, carry):
            scores(kb, 0)
            absorb(kb, None, 0)
            return carry

        n_full = i * (tq // tk)
        lax.fori_loop(0, n_full, body, 0)
        for diag in range(nr // tk):
            scores(n_full + diag, diag * tk)
            absorb(n_full + diag, diag, diag * tk)
        for j in range(hs):
            o_ref[:nr, j * dh:(j + 1) * dh] = (acc_ref[j, :nr, :] / l_ref[j, :nr, :]).astype(o_ref.dtype)

    tail = lp % tq
    if tail == 0:
        sweep(tq)
    else:
        @pl.when(i < lp // tq)
        def _():
            sweep(tq)

        @pl.when(i == lp // tq)
        def _():
            sweep(tail)


def fox_attention(qkv, pq, pk, bsz, lp, tq=768, tk=384):
    tp = qkv.shape[0]
    d = qkv.shape[1] // 3
    nh = FOX_HEADS
    dh = d // nh
    assert dh == LANES and tq % tk == 0 and lp % tk == 0 and (lp % tq) % tk == 0
    nq = pl.cdiv(lp, tq)
    hs = FOX_HEADS_PER_STEP
    ng = nh // hs
    qkv3 = qkv.reshape(bsz, lp, 3 * d)
    pq3 = pq.reshape(bsz, lp, LANES)
    pk3 = pk.reshape(bsz, lp, LANES)
    selq, selk = _bias_selectors(nh)
    out = pl.pallas_call(
        functools.partial(_fox_kernel, tq=tq, tk=tk, lp=lp, dh=dh),
        out_shape=jax.ShapeDtypeStruct((bsz, lp, d), BF16),
        grid=(bsz, ng, nq),
        in_specs=[
            pl.BlockSpec((None, tq, hs * dh), lambda b, g, i: (b, i, g)),
            pl.BlockSpec((None, tq, LANES), lambda b, g, i: (b, i, 0)),
            pl.BlockSpec((None, lp, hs * dh), lambda b, g, i: (b, 0, ng + g)),
            pl.BlockSpec((None, lp, LANES), lambda b, g, i: (b, 0, 0)),
            pl.BlockSpec((None, lp, hs * dh), lambda b, g, i: (b, 0, 2 * ng + g)),
            pl.BlockSpec((hs, LANES, LANES), lambda b, g, i: (g, 0, 0)),
            pl.BlockSpec((hs, LANES, LANES), lambda b, g, i: (g, 0, 0)),
        ],
        out_specs=pl.BlockSpec((None, tq, hs * dh), lambda b, g, i: (b, i, g)),
        scratch_shapes=[pltpu.VMEM((hs, tq, LANES), F32), pltpu.VMEM((hs, tq, LANES), F32),
                        pltpu.VMEM((hs, tq, dh), F32), pltpu.VMEM((hs, tq, tk), F32),
                        pltpu.VMEM((hs, tq, tk), BF16), pltpu.VMEM((hs, tq, LANES), F32),
                        pltpu.VMEM((lp, hs * LANES), BF16)],
        compiler_params=_params(("parallel", "parallel", "arbitrary"), vmem=VMEM_LIMIT_LARGE),
        name="fox_attention",
    )(qkv3, pq3, qkv3, pk3, qkv3, selq, selk)
    return out.reshape(tp, d)


def _router_kernel(h_ref, g_ref, wr_ref, br_ref, hn_ref, idx_ref, w_ref, cnt_ref, carry_ref, *, tr):
    i = pl.program_id(0)

    @pl.when(i == 0)
    def _():
        carry_ref[...] = jnp.zeros_like(carry_ref)

    hn = _rms(h_ref[...], g_ref[...])
    _store_token_tiles(hn_ref, (), _pack_halves(hn))
    lg = lax.dot_general(wr_ref[...], hn.astype(BF16), (((1,), (1,)), ((), ())),
                         preferred_element_type=F32) + br_ref[...]
    row = lax.broadcasted_iota(jnp.int32, (ROUTER_ROWS, tr), 0)
    big = jnp.int32(1 << 20)
    is_g = row < N_GROUPS
    mg = jnp.max(jnp.where(is_g, lg, -jnp.inf), axis=0, keepdims=True)
    g_sel = jnp.min(jnp.where(jnp.logical_and(is_g, lg == mg), row, big), axis=0, keepdims=True)
    sg = jnp.sum(jnp.where(is_g, jnp.exp(lg - mg), 0.0), axis=0, keepdims=True)
    p_g = 1.0 / sg
    lo = EXPERT_ROW0 + EXPERTS_PER_GROUP * g_sel
    is_e = jnp.logical_and(row >= lo, row < lo + EXPERTS_PER_GROUP)
    me = jnp.max(jnp.where(is_e, lg, -jnp.inf), axis=0, keepdims=True)
    ee = jnp.where(is_e, jnp.exp(lg - me), 0.0)
    pe = ee / jnp.sum(ee, axis=0, keepdims=True)
    pe1 = jnp.where(is_e, pe, -1.0)
    m1 = jnp.max(pe1, axis=0, keepdims=True)
    i1 = jnp.min(jnp.where(pe1 == m1, row, big), axis=0, keepdims=True)
    pe2 = jnp.where(row == i1, -1.0, pe1)
    m2 = jnp.max(pe2, axis=0, keepdims=True)
    i2 = jnp.min(jnp.where(pe2 == m2, row, big), axis=0, keepdims=True)
    den = m1 + m2
    w1 = p_g * (m1 / den)
    w2 = p_g * (m2 / den)

    oh1 = row == i1
    oh2 = row == i2
    ohs = jnp.logical_or(oh1, oh2).astype(F32)
    r_i = lax.broadcasted_iota(jnp.int32, (tr, tr), 0)
    c_i = lax.broadcasted_iota(jnp.int32, (tr, tr), 1)
    tri = (r_i < c_i).astype(BF16)
    cnt = jnp.dot(ohs.astype(BF16), tri, preferred_element_type=F32) + carry_ref[...]
    rank1 = jnp.sum(jnp.where(oh1, cnt, 0.0), axis=0, keepdims=True)
    rank2 = jnp.sum(jnp.where(oh2, cnt, 0.0), axis=0, keepdims=True)
    carry_ref[...] += jnp.sum(ohs, axis=1, keepdims=True)

    r8 = lax.broadcasted_iota(jnp.int32, (8, tr), 0)
    e1 = i1 - EXPERT_ROW0
    e2 = i2 - EXPERT_ROW0
    idx_ref[...] = jnp.where(r8 == 0, e1, jnp.where(r8 == 1, e2, jnp.where(
        r8 == 2, rank1.astype(jnp.int32), jnp.where(r8 == 3, rank2.astype(jnp.int32), 0))))
    w_ref[...] = jnp.where(r8 == 0, w1, jnp.where(r8 == 1, w2, 0.0))
    cnt_ref[...] = jnp.broadcast_to(carry_ref[...], cnt_ref.shape)


def moe_router(h, gain, w_rg, b_rg, w_re, b_re, tr=512):
    tp, d = h.shape
    assert d // 2 == SUBLANES * LANES
    wr = jnp.zeros((ROUTER_ROWS, d), BF16)
    wr = wr.at[:N_GROUPS].set(w_rg.T.astype(BF16))
    wr = wr.at[EXPERT_ROW0:EXPERT_ROW0 + N_EXPERTS].set(w_re.reshape(d, N_EXPERTS).T.astype(BF16))
    br = jnp.zeros((ROUTER_ROWS, 1), F32)
    br = br.at[:N_GROUPS, 0].set(b_rg.astype(F32))
    br = br.at[EXPERT_ROW0:EXPERT_ROW0 + N_EXPERTS, 0].set(b_re.reshape(N_EXPERTS).astype(F32))
    return pl.pallas_call(
        functools.partial(_router_kernel, tr=tr),
        out_shape=(
            jax.ShapeDtypeStruct((tp * SUBLANES, LANES), jnp.uint32),
            jax.ShapeDtypeStruct((8, tp), jnp.int32),
            jax.ShapeDtypeStruct((8, tp), F32),
            jax.ShapeDtypeStruct((ROUTER_ROWS, LANES), F32),
        ),
        grid=(tp // tr,),
        in_specs=[
            pl.BlockSpec((tr, d), lambda i: (i, 0)),
            pl.BlockSpec((1, d), lambda i: (0, 0)),
            pl.BlockSpec((ROUTER_ROWS, d), lambda i: (0, 0)),
            pl.BlockSpec((ROUTER_ROWS, 1), lambda i: (0, 0)),
        ],
        out_specs=(
            pl.BlockSpec((tr * SUBLANES, LANES), lambda i: (i, 0)),
            pl.BlockSpec((8, tr), lambda i: (0, i)),
            pl.BlockSpec((8, tr), lambda i: (0, i)),
            pl.BlockSpec((ROUTER_ROWS, LANES), lambda i: (0, 0)),
        ),
        scratch_shapes=[pltpu.VMEM((ROUTER_ROWS, 1), F32)],
        compiler_params=_params(("arbitrary",)),
        name="moe_router",
    )(h, gain.reshape(1, d), wr, br)


def _slots_kernel(cnt_ref, idx_ref, pos_ref, meta_ref):
    e1 = idx_ref[0:1, :]
    e2 = idx_ref[1:2, :]
    off1 = jnp.zeros_like(e1)
    off2 = jnp.zeros_like(e2)
    visit = lax.broadcasted_iota(jnp.int32, (1, meta_ref.shape[1]), 1)
    v_tile = jnp.zeros_like(visit)
    v_expert = jnp.zeros_like(visit)
    v_lo = jnp.zeros_like(visit)
    v_hi = jnp.zeros_like(visit)
    v_next = jnp.zeros_like(visit)
    v_start = jnp.zeros_like(visit)
    v_count = jnp.zeros_like(visit)
    next_live = [None] * N_EXPERTS
    nxt = jnp.int32(-1)
    for e in reversed(range(N_EXPERTS)):
        next_live[e] = jnp.where(nxt >= 0, nxt, e)
        nxt = jnp.where(cnt_ref[e] > 0, e, nxt)
    start = jnp.int32(0)
    v_base = jnp.int32(0)
    for e in range(N_EXPERTS):
        off1 = jnp.where(e1 == e, start, off1)
        off2 = jnp.where(e2 == e, start, off2)
        n = cnt_ref[e]
        end = start + n
        first_tile = start // TILE_M
        n_visits = jnp.where(n > 0, (jnp.maximum(end, 1) - 1) // TILE_M - first_tile + 1, 0)
        mine = jnp.logical_and(visit >= v_base, visit < v_base + n_visits)
        row0 = (first_tile + visit - v_base) * TILE_M
        v_tile = jnp.where(mine, first_tile + visit - v_base, v_tile)
        v_expert = jnp.where(mine, e, v_expert)
        v_next = jnp.where(mine, next_live[e], v_next)
        v_start = jnp.where(mine, v_base, v_start)
        v_count = jnp.where(mine, n_visits, v_count)
        v_lo = jnp.where(mine, jnp.maximum(start - row0, 0), v_lo)
        v_hi = jnp.where(mine, jnp.minimum(end - row0, TILE_M), v_hi)
        start = end
        v_base = v_base + n_visits
    r8 = lax.broadcasted_iota(jnp.int32, pos_ref.shape, 0)
    pos_ref[...] = jnp.where(r8 == 0, off1 + idx_ref[2:3, :], jnp.where(r8 == 1, off2 + idx_ref[3:4, :], 0))
    m8 = lax.broadcasted_iota(jnp.int32, meta_ref.shape, 0)
    rows = (v_tile, v_expert, v_lo, v_hi, v_base, v_next, v_start, v_count)
    meta = rows[-1]
    for r in reversed(range(len(rows) - 1)):
        meta = jnp.where(m8 == r, rows[r], meta)
    meta_ref[...] = meta


def moe_slots(counts, idx, tr=512):
    tp = idx.shape[1]
    nt_lanes = 2 * LANES
    return pl.pallas_call(
        _slots_kernel,
        out_shape=(jax.ShapeDtypeStruct((8, tp), jnp.int32), jax.ShapeDtypeStruct((8, nt_lanes), jnp.int32)),
        grid_spec=pltpu.PrefetchScalarGridSpec(
            num_scalar_prefetch=1,
            grid=(tp // tr,),
            in_specs=[pl.BlockSpec((8, tr), lambda i, c: (0, i))],
            out_specs=(
                pl.BlockSpec((8, tr), lambda i, c: (0, i)),
                pl.BlockSpec((8, nt_lanes), lambda i, c: (0, 0)),
            ),
        ),
        compiler_params=_params(("arbitrary",)),
        name="moe_slots",
    )(counts, idx)


def _invert_kernel(pos_ref, code_ref, *, tp):
    def per_token(t, carry):
        code_ref[pos_ref[t]] = 2 * t
        code_ref[pos_ref[tp + t]] = 2 * t + 1
        return carry

    lax.fori_loop(0, tp, per_token, 0, unroll=8)


def moe_invert(pos_flat):
    tp = pos_flat.shape[0] // 2
    return pl.pallas_call(
        functools.partial(_invert_kernel, tp=tp),
        out_shape=jax.ShapeDtypeStruct((2 * tp,), jnp.int32),
        grid_spec=pltpu.PrefetchScalarGridSpec(
            num_scalar_prefetch=1,
            grid=(1,),
            in_specs=[],
            out_specs=pl.BlockSpec(memory_space=pltpu.SMEM),
        ),
        compiler_params=_params(("arbitrary",)),
        name="moe_invert",
    )(pos_flat)


def _expert_kernel(vt_ref, ve_ref, vlo_ref, vhi_ref, nv_ref, vnext_ref, vstart_ref, vcount_ref, code_ref,
                   hn_ref, wg_hbm, wu_hbm, wd_hbm, y_ref,
                   xbuf, xcur, yacc, wgs, wus, wds, wgb, wub, wdb, gsem, wsem, *, n_tiles, layer):
    v = pl.program_id(0)
    nv = nv_ref[0]
    t = vt_ref[v]
    prev_v = jnp.maximum(v - 1, 0)
    first = jnp.logical_or(v == 0, vt_ref[prev_v] != t)
    new_expert = jnp.logical_or(v == 0, ve_ref[prev_v] != ve_ref[v])

    def weight_copies(e):
        w = layer * N_EXPERTS + e
        return (pltpu.make_async_copy(wg_hbm.at[w], wgs, wsem.at[0]),
                pltpu.make_async_copy(wu_hbm.at[w], wus, wsem.at[1]),
                pltpu.make_async_copy(wd_hbm.at[w], wds, wsem.at[2]))

    def row_copy(tile, r):
        tok = lax.shift_right_logical(code_ref[tile * TILE_M + r], 1)
        return pltpu.make_async_copy(hn_ref.at[_token_tile(tok)], xbuf.at[_token_tile(r)], gsem)

    def gather_wait():
        pltpu.make_async_copy(hn_ref.at[pl.ds(0, TILE_M * SUBLANES)], xbuf, gsem).wait()

    def compute(accumulate):
        x = xcur[...]
        a = jnp.dot(x, wgb[...], preferred_element_type=F32)
        u = jnp.dot(x, wub[...], preferred_element_type=F32)
        row = lax.broadcasted_iota(jnp.int32, (TILE_M, 1), 0)
        mine = jnp.logical_and(row >= vlo_ref[v], row < vhi_ref[v])
        hid = jnp.where(mine, a * _sigmoid(a) * u, 0.0).astype(BF16)
        y = jnp.dot(hid, wdb[...], preferred_element_type=F32)
        if accumulate:
            y = y + yacc[...]
        yacc[...] = y
        _store_token_tiles(y_ref, (), _pack_halves(y))

    @pl.when(v == 0)
    def _():
        def step(r, carry):
            row_copy(0, r).start()
            return carry
        lax.fori_loop(0, TILE_M, step, 0, unroll=8)
        for copy in weight_copies(ve_ref[0]):
            copy.start()

    @pl.when(v < nv)
    def _():
        @pl.when(new_expert)
        def _():
            for copy in weight_copies(ve_ref[v]):
                copy.wait()
            wgb[...] = wgs[...].astype(BF16)
            wub[...] = wus[...].astype(BF16)
            wdb[...] = wds[...].astype(BF16)

        @pl.when(first)
        def _():
            gather_wait()
            xcur[...] = _unpack_halves(_load_token_tiles(xbuf, (), 0, TILE_M)).astype(BF16)
            ahead = jnp.minimum(t + 1, n_tiles - 1)
            for r in range(TILE_M):
                row_copy(ahead, r).start()
            compute(False)

        @pl.when(jnp.logical_not(first))
        def _():
            compute(True)

        k = v - vstart_ref[v]
        last_of_expert = k == vcount_ref[v] - 1
        has_next = vnext_ref[v] != ve_ref[v]
        for c, copy in enumerate(weight_copies(vnext_ref[v])):
            @pl.when(jnp.logical_and(has_next, jnp.logical_or(k == c, jnp.logical_and(last_of_expert, k < c))))
            def _():
                copy.start()

        @pl.when(v == nv - 1)
        def _():
            gather_wait()


def moe_experts(meta, code, hn, w_gate, w_up, w_down, layer):
    tp = hn.shape[0] // SUBLANES
    d = 2 * SUBLANES * LANES
    f = w_gate.shape[-1]
    assert (2 * tp) % TILE_M == 0
    n_tiles = (2 * tp) // TILE_M
    max_visits = n_tiles + N_EXPERTS - 1

    def y_map(v, vt, ve, vlo, vhi, nv, *_):
        return (vt[jnp.minimum(v, nv[0] - 1)], 0)

    any_spec = pl.BlockSpec(memory_space=pl.ANY)
    return pl.pallas_call(
        functools.partial(_expert_kernel, n_tiles=n_tiles, layer=layer),
        out_shape=jax.ShapeDtypeStruct((2 * tp * SUBLANES, LANES), jnp.uint32),
        grid_spec=pltpu.PrefetchScalarGridSpec(
            num_scalar_prefetch=9,
            grid=(max_visits,),
            in_specs=[any_spec, any_spec, any_spec, any_spec],
            out_specs=pl.BlockSpec((TILE_M * SUBLANES, LANES), y_map),
            scratch_shapes=[
                pltpu.VMEM((TILE_M * SUBLANES, LANES), jnp.uint32), pltpu.VMEM((TILE_M, d), BF16),
                pltpu.VMEM((TILE_M, d), F32),
                pltpu.VMEM((d, f), F32), pltpu.VMEM((d, f), F32), pltpu.VMEM((f, d), F32),
                pltpu.VMEM((d, f), BF16), pltpu.VMEM((d, f), BF16), pltpu.VMEM((f, d), BF16),
                pltpu.SemaphoreType.DMA, pltpu.SemaphoreType.DMA((3,)),
            ],
        ),
        compiler_params=_params(("arbitrary",), vmem=VMEM_LIMIT_LARGE),
        name="moe_experts",
    )(meta[0, :max_visits], meta[1, :max_visits], meta[2, :max_visits], meta[3, :max_visits], meta[4, :1],
      meta[5, :max_visits], meta[6, :max_visits], meta[7, :max_visits], code, hn, w_gate, w_up, w_down)


def _combine_kernel(pos_ref, *refs, tc, tp, n_steps, parts, first_token, write_h):
    h_refs, w_refs = refs[:parts], refs[parts:2 * parts]
    g_ref, y_ref = refs[2 * parts:2 * parts + 2]
    out_refs, (ybuf, sem) = refs[2 * parts + 2:-2], refs[-2:]
    i = pl.program_id(0)
    slot = lax.rem(i, 2)

    def row_copy(step, buf, s, r):
        p = pos_ref[s * tp + first_token(step) + r]
        return pltpu.make_async_copy(y_ref.at[_token_tile(p)], ybuf.at[buf, _token_tile(s * tc + r)],
                                     sem.at[buf])

    @pl.when(i == 0)
    def _():
        def step(r, carry):
            row_copy(0, 0, 0, r).start()
            row_copy(0, 0, 1, r).start()
            return carry
        lax.fori_loop(0, tc, step, 0, unroll=8)

    pltpu.make_async_copy(y_ref.at[pl.ds(0, 2 * tc * SUBLANES)], ybuf.at[slot], sem.at[slot]).wait()

    @pl.when(i + 1 < n_steps)
    def _():
        for r in range(tc):
            row_copy(i + 1, 1 - slot, 0, r).start()
            row_copy(i + 1, 1 - slot, 1, r).start()

    rows = tc // parts
    for c in range(parts):
        w = w_refs[c][...]
        h = (h_refs[c][...] + w[:, 0:1] * _unpack_halves(_load_token_tiles(ybuf, (slot,), c * rows, rows))
             + w[:, 1:2] * _unpack_halves(_load_token_tiles(ybuf, (slot,), tc + c * rows, rows)))
        if write_h:
            out_refs[0][c * rows:(c + 1) * rows, :] = h
        out_refs[-1][c * rows:(c + 1) * rows, :] = _rms(h, g_ref[...]).astype(out_refs[-1].dtype)


def moe_combine(pos_flat, h, w_col, y, gain, hn_dtype, final_shape=None):
    tp, d = h.shape
    tc = 256
    if final_shape is None:
        parts = 1
        n_steps = tp // tc

        def first_token(i):
            return i * tc

        def piece(c, width):
            return pl.BlockSpec((tc, width), lambda i, p: (i, 0))

        out_shape = (jax.ShapeDtypeStruct((tp, d), F32), jax.ShapeDtypeStruct((tp, d), hn_dtype))
        out_specs = (pl.BlockSpec((tc, d), lambda i, p: (i, 0)), pl.BlockSpec((tc, d), lambda i, p: (i, 0)))
    else:
        bsz, seq, _ = final_shape
        parts = tc // CHUNK
        nc = tp // bsz // CHUNK
        per_seq = seq // tc
        n_steps = bsz * per_seq

        def first_token(i):
            return (i // per_seq) * (nc * CHUNK) + CHUNK + (i % per_seq) * tc

        def piece(c, width):
            return pl.BlockSpec((CHUNK, width),
                                lambda i, p: ((i // per_seq) * nc + 1 + (i % per_seq) * parts + c, 0))

        out_shape = (jax.ShapeDtypeStruct(final_shape, hn_dtype),)
        out_specs = (pl.BlockSpec((None, tc, d), lambda i, p: (i // per_seq, i % per_seq, 0)),)
    return pl.pallas_call(
        functools.partial(_combine_kernel, tc=tc, tp=tp, n_steps=n_steps, parts=parts,
                          first_token=first_token, write_h=final_shape is None),
        out_shape=out_shape,
        grid_spec=pltpu.PrefetchScalarGridSpec(
            num_scalar_prefetch=1,
            grid=(n_steps,),
            in_specs=[piece(c, d) for c in range(parts)] + [piece(c, 2) for c in range(parts)] + [
                pl.BlockSpec((1, d), lambda i, p: (0, 0)),
                pl.BlockSpec(memory_space=pl.ANY),
            ],
            out_specs=out_specs,
            scratch_shapes=[pltpu.VMEM((2, 2 * tc * SUBLANES, LANES), jnp.uint32),
                            pltpu.SemaphoreType.DMA((2,))],
        ),
        compiler_params=_params(("arbitrary",)),
        name="moe_combine",
    )(pos_flat, *([h] * parts), *([w_col] * parts), gain.reshape(1, d), y)


def hierarchical_moe(h, gain, w_rg, b_rg, w_re, b_re, w_gate, w_up, w_down, layer, next_gain, hn_dtype,
                     final_shape=None):
    tp, d = h.shape
    f = w_gate.shape[-1]
    hn, idx, w_rows, cnt = moe_router(h, gain, w_rg, b_rg, w_re, b_re)
    counts = cnt[EXPERT_ROW0:EXPERT_ROW0 + N_EXPERTS, 0].astype(jnp.int32)
    pos, meta = moe_slots(counts, idx)
    pos_flat = pos[:2].reshape(2 * tp)
    code = moe_invert(pos_flat)
    y = moe_experts(meta, code, hn, w_gate.reshape(-1, d, f), w_up.reshape(-1, d, f),
                    w_down.reshape(-1, f, d), layer)
    return moe_combine(pos_flat, h, w_rows[:2].T, y, next_gain, hn_dtype, final_shape)


def kernel(x, meta_tokens, norm_mixer, norm_ffn, norm_final, ret_w_in, ret_gn, ret_w_out,
           fox_w_in, fox_b_f, fox_w_out, moe_w_rg, moe_b_rg, moe_w_re, moe_b_re,
           moe_w_gate, moe_w_up, moe_w_down):
    bsz, seq, d = x.shape
    depth = norm_mixer.shape[0]
    nc = (seq + CHUNK) // CHUNK
    lp = nc * CHUNK
    h, hn = embed_norm(x, meta_tokens.astype(x.dtype), norm_mixer[0], BF16)
    for i in range(depth):
        j = i // 2
        if i % 2 == 0:
            proj = matmul(hn, ret_w_in, j, BF16, tm=1536)
            gated = retention_core(proj, ret_gn[j], bsz, nc)
            h = matmul(gated, ret_w_out, j, F32, residual=h, tn=512)
        else:
            qkv = matmul(hn, fox_w_in, j, BF16, n=3 * d, tm=1536, scale_cols=d,
                         scale=(d // FOX_HEADS) ** -0.5 * LOG2E)
            pq, pk = forget_gates(hn, fox_w_in[j, :, 3 * d:], fox_b_f[j], bsz, lp)
            o = fox_attention(qkv, pq, pk, bsz, lp)
            h = matmul(o, fox_w_out, j, F32, residual=h, tm=1536, tn=512)
        last = i == depth - 1
        outs = hierarchical_moe(h, norm_ffn[i], moe_w_rg[i], moe_b_rg[i], moe_w_re[i], moe_b_re[i],
                                moe_w_gate, moe_w_up, moe_w_down, i,
                                norm_final if last else norm_mixer[i + 1],
                                F32 if last else BF16, (bsz, seq, d) if last else None)
        if last:
            return outs[0]
        h, hn = outs
```

```python
import functools

import jax
import jax.numpy as jnp
from jax import lax
from jax.experimental import pallas as pl
from jax.experimental.pallas import tpu as pltpu

N_META = 16
CHUNK = 128
PAD = CHUNK - N_META
RMS_EPS = 1e-6
GN_EPS = 1e-6
NEG_INF = -1e30
RET_HEADS = 8
FOX_HEADS = 16
N_GROUPS = 4
EXPERTS_PER_GROUP = 8
N_EXPERTS = N_GROUPS * EXPERTS_PER_GROUP
ROPE_BASE = 10000.0

LANES = 128
ROUTER_ROWS = 48
EXPERT_ROW0 = N_GROUPS
TILE_M = 256
VMEM_BYTES = 64 * 1024 * 1024
VMEM_LIMIT = 3 * VMEM_BYTES // 4
VMEM_LIMIT_LARGE = 7 * VMEM_BYTES // 8

F32 = jnp.float32
BF16 = jnp.bfloat16


def _params(sem, vmem=VMEM_LIMIT):
    return pltpu.CompilerParams(dimension_semantics=sem, vmem_limit_bytes=vmem)


def _rms(h, g):
    return h * lax.rsqrt(jnp.mean(h * h, axis=-1, keepdims=True) + RMS_EPS) * g


def _sigmoid(x):
    return 1.0 / (1.0 + jnp.exp(-x))


def _pack_halves(x):
    half = x.shape[1] // 2
    return pltpu.pack_elementwise([x[:, :half], x[:, half:]], packed_dtype=BF16)


def _unpack_halves(p):
    lo = pltpu.unpack_elementwise(p, index=0, packed_dtype=BF16, unpacked_dtype=F32)
    hi = pltpu.unpack_elementwise(p, index=1, packed_dtype=BF16, unpacked_dtype=F32)
    return jnp.concatenate([lo, hi], axis=1)


SUBLANES = 8


def _token_tile(t):
    start = t * SUBLANES
    return pl.ds(start if isinstance(t, int) else pl.multiple_of(start, SUBLANES), SUBLANES)


def _store_token_tiles(ref, index, packed):
    m = packed.shape[0]
    for s in range(SUBLANES):
        ref[index + (pl.ds(s, m, stride=SUBLANES), slice(None))] = packed[:, s * LANES:(s + 1) * LANES]


def _load_token_tiles(ref, index, first_token, m):
    return jnp.concatenate(
        [ref[index + (pl.ds(first_token * SUBLANES + s, m, stride=SUBLANES), slice(None))]
         for s in range(SUBLANES)], axis=1)


EMBED_CHUNKS = 3


def _embed_norm_kernel(*refs):
    x_refs, (meta_ref, g_ref, h_ref, hn_ref) = refs[:EMBED_CHUNKS], refs[EMBED_CHUNKS:]
    i = pl.program_id(1)
    for c, x_ref in enumerate(x_refs):
        h_ref[c * CHUNK:(c + 1) * CHUNK, :] = x_ref[...]

    @pl.when(i == 0)
    def _():
        h_ref[:PAD, :] = jnp.zeros((PAD, h_ref.shape[1]), h_ref.dtype)
        h_ref[PAD:CHUNK, :] = meta_ref[...]

    hn_ref[...] = _rms(h_ref[...], g_ref[...]).astype(hn_ref.dtype)


def embed_norm(x, meta, gain, hn_dtype):
    bsz, seq, d = x.shape
    nc = (seq + CHUNK) // CHUNK
    assert nc % EMBED_CHUNKS == 0
    tp = bsz * nc * CHUNK
    steps = nc // EMBED_CHUNKS
    rows = EMBED_CHUNKS * CHUNK

    def x_spec(c):
        return pl.BlockSpec((None, CHUNK, d), lambda b, i: (b, jnp.maximum(EMBED_CHUNKS * i + c - 1, 0), 0))

    return pl.pallas_call(
        _embed_norm_kernel,
        out_shape=(jax.ShapeDtypeStruct((tp, d), F32), jax.ShapeDtypeStruct((tp, d), hn_dtype)),
        grid=(bsz, steps),
        in_specs=[x_spec(c) for c in range(EMBED_CHUNKS)] + [
            pl.BlockSpec((N_META, d), lambda b, i: (0, 0)),
            pl.BlockSpec((1, d), lambda b, i: (0, 0)),
        ],
        out_specs=(
            pl.BlockSpec((rows, d), lambda b, i: (b * steps + i, 0)),
            pl.BlockSpec((rows, d), lambda b, i: (b * steps + i, 0)),
        ),
        compiler_params=_params(("parallel", "parallel")),
        name="embed_norm",
    )(*([x] * EMBED_CHUNKS), meta, gain.reshape(1, d))


def _mm_kernel(*refs, has_res, scale_tiles, scale):
    if has_res:
        x_ref, w_ref, r_ref, o_ref, wb_ref = refs
    else:
        x_ref, w_ref, o_ref, wb_ref = refs

    @pl.when(pl.program_id(1) == 0)
    def _():
        wb_ref[...] = w_ref[...].astype(BF16)

    acc = jnp.dot(x_ref[...], wb_ref[...], preferred_element_type=F32)
    if scale_tiles:
        acc = acc * jnp.where(pl.program_id(0) < scale_tiles, scale, 1.0)
    if has_res:
        acc = acc + r_ref[...]
    o_ref[...] = acc.astype(o_ref.dtype)


def matmul(x, w, layer, out_dtype, n=None, residual=None, tm=512, tn=1024, scale_cols=0, scale=1.0):
    m, kdim = x.shape
    n = w.shape[2] if n is None else n
    assert m % tm == 0 and n % tn == 0 and scale_cols % tn == 0
    in_specs = [
        pl.BlockSpec((tm, kdim), lambda j, i: (i, 0)),
        pl.BlockSpec((None, kdim, tn), lambda j, i: (layer, 0, j)),
    ]
    args = [x, w]
    if residual is not None:
        in_specs.append(pl.BlockSpec((tm, tn), lambda j, i: (i, j)))
        args.append(residual)
    return pl.pallas_call(
        functools.partial(_mm_kernel, has_res=residual is not None, scale_tiles=scale_cols // tn, scale=scale),
        out_shape=jax.ShapeDtypeStruct((m, n), out_dtype),
        grid=(n // tn, m // tm),
        in_specs=in_specs,
        out_specs=pl.BlockSpec((tm, tn), lambda j, i: (i, j)),
        scratch_shapes=[pltpu.VMEM((kdim, tn), BF16)],
        compiler_params=_params(("parallel", "arbitrary")),
        name="matmul",
    )(*args)


def _retention_kernel(q_ref, k_ref, v_ref, g_ref, cos_ref, sin_ref, dmat_ref, xi_ref, zeta_ref,
                      gch_ref, gn_ref, o_ref, state_ref, *, nh, dk, dv):
    c = pl.program_id(1)

    @pl.when(c == 0)
    def _():
        state_ref[...] = jnp.zeros_like(state_ref)

    cos = cos_ref[...]
    sin = sin_ref[...]
    half = dk // 2

    def rot(u):
        u1, u2 = u[:, :half], u[:, half:]
        return jnp.concatenate([u1 * cos - u2 * sin, u1 * sin + u2 * cos], axis=1)

    row = lax.broadcasted_iota(jnp.int32, (CHUNK, 1), 0)
    valid = jnp.logical_or(row >= PAD, c > 0)
    for h in range(nh):
        q = rot(q_ref[:, h * dk:(h + 1) * dk].astype(F32))
        k = rot(k_ref[:, h * dk:(h + 1) * dk].astype(F32)) * (dk ** -0.5)
        k = jnp.where(valid, k, 0.0)
        v = v_ref[:, h * dv:(h + 1) * dv]
        vb = jnp.where(valid, v, jnp.zeros_like(v))
        qb = q.astype(BF16)
        kb = k.astype(BF16)

        scores = lax.dot_general(qb, kb, (((1,), (1,)), ((), ())), preferred_element_type=F32)
        scores = scores * dmat_ref[h]
        inner = jnp.dot(scores.astype(BF16), vb, preferred_element_type=F32)
        state = state_ref[h]
        cross = jnp.dot(qb, state.astype(BF16), preferred_element_type=F32) * xi_ref[h]
        y = inner + cross
        kz = (k * zeta_ref[h]).astype(BF16)
        state_ref[h] = gch_ref[h] * state + lax.dot_general(
            kz, vb, (((0,), (0,)), ((), ())), preferred_element_type=F32)

        mu = jnp.mean(y, axis=-1, keepdims=True)
        dlt = y - mu
        var = jnp.mean(dlt * dlt, axis=-1, keepdims=True)
        yn = dlt * lax.rsqrt(var + GN_EPS) * gn_ref[h]
        hg = 0.5 * g_ref[:, h * dv:(h + 1) * dv].astype(F32)
        o_ref[:, h * dv:(h + 1) * dv] = ((hg + hg * jnp.tanh(hg)) * yn).astype(o_ref.dtype)


def retention_core(proj, gn_gain, bsz, nc):
    tp = proj.shape[0]
    d = proj.shape[1] // 6
    nh = RET_HEADS
    dk = d // nh
    dv = 2 * d // nh
    lp = nc * CHUNK
    half = dk // 2
    pos = (jnp.arange(lp) - PAD).astype(F32)
    inv = ROPE_BASE ** (-jnp.arange(half, dtype=F32) / half)
    ang = pos[:, None] * inv[None, :]
    cos, sin = jnp.cos(ang), jnp.sin(ang)
    log_g = jnp.log1p(-jnp.exp2(-5.0 - jnp.arange(nh, dtype=F32)))
    idx = jnp.arange(CHUNK, dtype=F32)
    diff = idx[:, None] - idx[None, :]
    dmat = jnp.where(diff[None] >= 0, jnp.exp(jnp.maximum(diff, 0.0)[None] * log_g[:, None, None]), 0.0)
    xi = jnp.exp((idx + 1.0)[None, :] * log_g[:, None])[:, :, None]
    zeta = jnp.exp((CHUNK - 1.0 - idx)[None, :] * log_g[:, None])[:, :, None]
    g_chunk = jnp.exp(CHUNK * log_g)[:, None, None]
    return pl.pallas_call(
        functools.partial(_retention_kernel, nh=nh, dk=dk, dv=dv),
        out_shape=jax.ShapeDtypeStruct((tp, 2 * d), BF16),
        grid=(bsz, nc),
        in_specs=[
            pl.BlockSpec((CHUNK, d), lambda b, c: (b * nc + c, 0)),
            pl.BlockSpec((CHUNK, d), lambda b, c: (b * nc + c, 1)),
            pl.BlockSpec((CHUNK, 2 * d), lambda b, c: (b * nc + c, 1)),
            pl.BlockSpec((CHUNK, 2 * d), lambda b, c: (b * nc + c, 2)),
            pl.BlockSpec((CHUNK, half), lambda b, c: (c, 0)),
            pl.BlockSpec((CHUNK, half), lambda b, c: (c, 0)),
            pl.BlockSpec((nh, CHUNK, CHUNK), lambda b, c: (0, 0, 0)),
            pl.BlockSpec((nh, CHUNK, 1), lambda b, c: (0, 0, 0)),
            pl.BlockSpec((nh, CHUNK, 1), lambda b, c: (0, 0, 0)),
            pl.BlockSpec((nh, 1, 1), lambda b, c: (0, 0, 0)),
            pl.BlockSpec((nh, 1, dv), lambda b, c: (0, 0, 0)),
        ],
        out_specs=pl.BlockSpec((CHUNK, 2 * d), lambda b, c: (b * nc + c, 0)),
        scratch_shapes=[pltpu.VMEM((nh, dk, dv), F32)],
        compiler_params=_params(("parallel", "arbitrary")),
        name="retention_core",
    )(proj, proj, proj, proj, cos, sin, dmat, xi, zeta, g_chunk, gn_gain.reshape(nh, 1, dv))


LOG2E = 1.4426950408889634
MASK_BIG = 1e30
ONES_LANE = LANES - 1


def _bias_selectors(nh):
    h = jnp.arange(nh)
    selq = jnp.zeros((nh, LANES, LANES), F32)
    selk = jnp.zeros((nh, LANES, LANES), F32)
    for part in range(3):
        selq = selq.at[h, part * nh + h, part].set(1.0)
        selk = selk.at[h, part * nh + h, 3 + part].set(-1.0)
        selq = selq.at[h, ONES_LANE, 3 + part].set(1.0)
        selk = selk.at[h, ONES_LANE, part].set(1.0)
    return selq.astype(BF16), selk.astype(BF16)


def _forget_kernel(hn_ref, wf_ref, bf_ref, pq_ref, pk_ref, carry_ref, *, rows, nh):
    i = pl.program_id(1)

    @pl.when(i == 0)
    def _():
        carry_ref[...] = jnp.zeros_like(carry_ref)

    z = jnp.dot(hn_ref[...], wf_ref[...], preferred_element_type=F32) + bf_ref[...]
    lf = jnp.minimum(z, 0.0) - jnp.log1p(jnp.exp(-jnp.abs(z)))
    row = lax.broadcasted_iota(jnp.int32, (rows, 1), 0)
    valid = i * rows + row >= PAD
    lf = jnp.where(valid, lf, 0.0)

    def split3(a):
        hi = a.astype(BF16)
        r1 = a - hi.astype(F32)
        mid = r1.astype(BF16)
        lo = (r1 - mid.astype(F32)).astype(BF16)
        return hi, mid, lo

    r_i = lax.broadcasted_iota(jnp.int32, (rows, rows), 0)
    c_i = lax.broadcasted_iota(jnp.int32, (rows, rows), 1)
    tri = (r_i >= c_i).astype(BF16)
    cs = sum(jnp.dot(tri, part, preferred_element_type=F32) for part in split3(lf))
    cs = cs + carry_ref[...]
    carry_ref[...] = cs[rows - 1:rows, :]

    hi, mid, lo = (p.astype(F32) for p in split3(cs * LOG2E))
    lane = lax.broadcasted_iota(jnp.int32, (rows, LANES), 1)

    def lay_out(parts):
        row = jnp.where(lane == ONES_LANE, 1.0, 0.0)
        for k, part in enumerate(parts):
            moved = part if k == 0 else pltpu.roll(part, k * nh, axis=1)
            row = jnp.where(jnp.logical_and(lane >= k * nh, lane < (k + 1) * nh), moved, row)
        return row.astype(BF16)

    pq_ref[...] = lay_out((hi, mid, lo))
    pk_ref[...] = lay_out((jnp.where(valid, hi, MASK_BIG), jnp.where(valid, mid, 0.0),
                           jnp.where(valid, lo, 0.0)))


def forget_gates(hn, w_f, b_f, bsz, lp, rows=384):
    tp, d = hn.shape
    nh = w_f.shape[1]
    assert 3 * nh <= ONES_LANE and lp % rows == 0
    steps = lp // rows
    wf = jnp.zeros((d, LANES), BF16).at[:, :nh].set(w_f.astype(BF16))
    bf = jnp.zeros((1, LANES), F32).at[0, :nh].set(b_f.astype(F32))
    return pl.pallas_call(
        functools.partial(_forget_kernel, rows=rows, nh=nh),
        out_shape=(jax.ShapeDtypeStruct((tp, LANES), BF16), jax.ShapeDtypeStruct((tp, LANES), BF16)),
        grid=(bsz, steps),
        in_specs=[
            pl.BlockSpec((rows, d), lambda b, i: (b * steps + i, 0)),
            pl.BlockSpec((d, LANES), lambda b, i: (0, 0)),
            pl.BlockSpec((1, LANES), lambda b, i: (0, 0)),
        ],
        out_specs=(
            pl.BlockSpec((rows, LANES), lambda b, i: (b * steps + i, 0)),
            pl.BlockSpec((rows, LANES), lambda b, i: (b * steps + i, 0)),
        ),
        scratch_shapes=[pltpu.VMEM((1, LANES), F32)],
        compiler_params=_params(("parallel", "arbitrary")),
        name="forget_gates",
    )(hn, wf, bf)


FOX_HEADS_PER_STEP = 4


FOX_STRIP = 32


def _fox_kernel(q_ref, pq_ref, k_ref, pk_ref, v_ref, selq_ref, selk_ref, o_ref,
                m_ref, l_ref, acc_ref, s_ref, p_ref, a_ref, ck_ref, *, tq, tk, lp, dh):
    i = pl.program_id(2)
    hs = FOX_HEADS_PER_STEP
    reps = tk // LANES

    @pl.when(i == 0)
    def _():
        for j in range(hs):
            ck_ref[:, j * LANES:(j + 1) * LANES] = jnp.dot(
                pk_ref[...], selk_ref[j], preferred_element_type=F32).astype(BF16)

    def sweep(nr):
        m_ref[:, :nr, :] = jnp.full((hs, nr, LANES), 10.0 * NEG_INF, F32)
        l_ref[:, :nr, :] = jnp.zeros((hs, nr, LANES), F32)
        acc_ref[:, :nr, :] = jnp.zeros((hs, nr, dh), F32)
        qa = [jnp.concatenate(
            [q_ref[:nr, j * dh:(j + 1) * dh],
             jnp.dot(pq_ref[:nr, :], selq_ref[j], preferred_element_type=F32).astype(BF16)], axis=1)
            for j in range(hs)]

        def scores(kb, lo):
            start = pl.multiple_of(kb * tk, tk)
            for j in range(hs):
                ka = jnp.concatenate([k_ref[pl.ds(start, tk), j * dh:(j + 1) * dh],
                                      ck_ref[pl.ds(start, tk), j * LANES:(j + 1) * LANES]], axis=1)
                s_ref[j, lo:nr, :] = lax.dot_general(qa[j][lo:], ka, (((1,), (1,)), ((), ())),
                                                     preferred_element_type=F32)

        def absorb(kb, diag, lo):
            start = pl.multiple_of(kb * tk, tk)
            for j in range(hs):
                for r0 in range(lo, nr, FOX_STRIP):
                    rows = slice(r0, r0 + FOX_STRIP)
                    s = s_ref[j, rows, :]
                    if diag is not None and r0 < diag * tk + tk - 1:
                        r_i = r0 + lax.broadcasted_iota(jnp.int32, (FOX_STRIP, tk), 0)
                        c_i = diag * tk + lax.broadcasted_iota(jnp.int32, (FOX_STRIP, tk), 1)
                        s = jnp.where(c_i <= r_i, s, NEG_INF)
                    m_old = m_ref[j, rows, :]
                    m_new = jnp.maximum(m_old, jnp.max(s, axis=1, keepdims=True))
                    alpha = jnp.exp2(m_old - m_new)
                    p = jnp.exp2(s - jnp.concatenate([m_new] * reps, axis=1))
                    l_ref[j, rows, :] = alpha * l_ref[j, rows, :] + jnp.sum(p, axis=1, keepdims=True)
                    m_ref[j, rows, :] = m_new
                    a_ref[j, rows, :] = alpha
                    p_ref[j, rows, :] = p.astype(BF16)
            for j in range(hs):
                acc_ref[j, lo:nr, :] = a_ref[j, lo:nr, :] * acc_ref[j, lo:nr, :] + jnp.dot(
                    p_ref[j, lo:nr, :], v_ref[pl.ds(start, tk), j * dh:(j + 1) * dh],
                    preferred_element_type=F32)

        def body(kb, carry):
            scores(kb, 0)
            absorb(kb, None, 0)
            return carry

        n_full = i * (tq // tk)
        lax.fori_loop(0, n_full, body, 0)
        for diag in range(nr // tk):
            scores(n_full + diag, diag * tk)
            absorb(n_full + diag, diag, diag * tk)
        for j in range(hs):
            o_ref[:nr, j * dh:(j + 1) * dh] = (acc_ref[j, :nr, :] / l_ref[j, :nr, :]).astype(o_ref.dtype)

    tail = lp % tq
    if tail == 0:
        sweep(tq)
    else:
        @pl.when(i < lp // tq)
        def _():
            sweep(tq)

        @pl.when(i == lp // tq)
        def _():
            sweep(tail)


def fox_attention(qkv, pq, pk, bsz, lp, tq=768, tk=384):
    tp = qkv.shape[0]
    d = qkv.shape[1] // 3
    nh = FOX_HEADS
    dh = d // nh
    assert dh == LANES and tq % tk == 0 and lp % tk == 0 and (lp % tq) % tk == 0
    nq = pl.cdiv(lp, tq)
    hs = FOX_HEADS_PER_STEP
    ng = nh // hs
    qkv3 = qkv.reshape(bsz, lp, 3 * d)
    pq3 = pq.reshape(bsz, lp, LANES)
    pk3 = pk.reshape(bsz, lp, LANES)
    selq, selk = _bias_selectors(nh)
    out = pl.pallas_call(
        functools.partial(_fox_kernel, tq=tq, tk=tk, lp=lp, dh=dh),
        out_shape=jax.ShapeDtypeStruct((bsz, lp, d), BF16),
        grid=(bsz, ng, nq),
        in_specs=[
            pl.BlockSpec((None, tq, hs * dh), lambda b, g, i: (b, i, g)),
            pl.BlockSpec((None, tq, LANES), lambda b, g, i: (b, i, 0)),
            pl.BlockSpec((None, lp, hs * dh), lambda b, g, i: (b, 0, ng + g)),
            pl.BlockSpec((None, lp, LANES), lambda b, g, i: (b, 0, 0)),
            pl.BlockSpec((None, lp, hs * dh), lambda b, g, i: (b, 0, 2 * ng + g)),
            pl.BlockSpec((hs, LANES, LANES), lambda b, g, i: (g, 0, 0)),
            pl.BlockSpec((hs, LANES, LANES), lambda b, g, i: (g, 0, 0)),
        ],
        out_specs=pl.BlockSpec((None, tq, hs * dh), lambda b, g, i: (b, i, g)),
        scratch_shapes=[pltpu.VMEM((hs, tq, LANES), F32), pltpu.VMEM((hs, tq, LANES), F32),
                        pltpu.VMEM((hs, tq, dh), F32), pltpu.VMEM((hs, tq, tk), F32),
                        pltpu.VMEM((hs, tq, tk), BF16), pltpu.VMEM((hs, tq, LANES), F32),
                        pltpu.VMEM((lp, hs * LANES), BF16)],
        compiler_params=_params(("parallel", "parallel", "arbitrary"), vmem=VMEM_LIMIT_LARGE),
        name="fox_attention",
    )(qkv3, pq3, qkv3, pk3, qkv3, selq, selk)
    return out.reshape(tp, d)


def _router_kernel(h_ref, g_ref, wr_ref, br_ref, hn_ref, idx_ref, w_ref, cnt_ref, carry_ref, *, tr):
    i = pl.program_id(0)

    @pl.when(i == 0)
    def _():
        carry_ref[...] = jnp.zeros_like(carry_ref)

    hn = _rms(h_ref[...], g_ref[...])
    _store_token_tiles(hn_ref, (), _pack_halves(hn))
    lg = lax.dot_general(wr_ref[...], hn.astype(BF16), (((1,), (1,)), ((), ())),
                         preferred_element_type=F32) + br_ref[...]
    row = lax.broadcasted_iota(jnp.int32, (ROUTER_ROWS, tr), 0)
    big = jnp.int32(1 << 20)
    is_g = row < N_GROUPS
    mg = jnp.max(jnp.where(is_g, lg, -jnp.inf), axis=0, keepdims=True)
    g_sel = jnp.min(jnp.where(jnp.logical_and(is_g, lg == mg), row, big), axis=0, keepdims=True)
    sg = jnp.sum(jnp.where(is_g, jnp.exp(lg - mg), 0.0), axis=0, keepdims=True)
    p_g = 1.0 / sg
    lo = EXPERT_ROW0 + EXPERTS_PER_GROUP * g_sel
    is_e = jnp.logical_and(row >= lo, row < lo + EXPERTS_PER_GROUP)
    me = jnp.max(jnp.where(is_e, lg, -jnp.inf), axis=0, keepdims=True)
    ee = jnp.where(is_e, jnp.exp(lg - me), 0.0)
    pe = ee / jnp.sum(ee, axis=0, keepdims=True)
    pe1 = jnp.where(is_e, pe, -1.0)
    m1 = jnp.max(pe1, axis=0, keepdims=True)
    i1 = jnp.min(jnp.where(pe1 == m1, row, big), axis=0, keepdims=True)
    pe2 = jnp.where(row == i1, -1.0, pe1)
    m2 = jnp.max(pe2, axis=0, keepdims=True)
    i2 = jnp.min(jnp.where(pe2 == m2, row, big), axis=0, keepdims=True)
    den = m1 + m2
    w1 = p_g * (m1 / den)
    w2 = p_g * (m2 / den)

    oh1 = row == i1
    oh2 = row == i2
    ohs = jnp.logical_or(oh1, oh2).astype(F32)
    r_i = lax.broadcasted_iota(jnp.int32, (tr, tr), 0)
    c_i = lax.broadcasted_iota(jnp.int32, (tr, tr), 1)
    tri = (r_i < c_i).astype(BF16)
    cnt = jnp.dot(ohs.astype(BF16), tri, preferred_element_type=F32) + carry_ref[...]
    rank1 = jnp.sum(jnp.where(oh1, cnt, 0.0), axis=0, keepdims=True)
    rank2 = jnp.sum(jnp.where(oh2, cnt, 0.0), axis=0, keepdims=True)
    carry_ref[...] += jnp.sum(ohs, axis=1, keepdims=True)

    r8 = lax.broadcasted_iota(jnp.int32, (8, tr), 0)
    e1 = i1 - EXPERT_ROW0
    e2 = i2 - EXPERT_ROW0
    idx_ref[...] = jnp.where(r8 == 0, e1, jnp.where(r8 == 1, e2, jnp.where(
        r8 == 2, rank1.astype(jnp.int32), jnp.where(r8 == 3, rank2.astype(jnp.int32), 0))))
    w_ref[...] = jnp.where(r8 == 0, w1, jnp.where(r8 == 1, w2, 0.0))
    cnt_ref[...] = jnp.broadcast_to(carry_ref[...], cnt_ref.shape)


def moe_router(h, gain, w_rg, b_rg, w_re, b_re, tr=512):
    tp, d = h.shape
    assert d // 2 == SUBLANES * LANES
    wr = jnp.zeros((ROUTER_ROWS, d), BF16)
    wr = wr.at[:N_GROUPS].set(w_rg.T.astype(BF16))
    wr = wr.at[EXPERT_ROW0:EXPERT_ROW0 + N_EXPERTS].set(w_re.reshape(d, N_EXPERTS).T.astype(BF16))
    br = jnp.zeros((ROUTER_ROWS, 1), F32)
    br = br.at[:N_GROUPS, 0].set(b_rg.astype(F32))
    br = br.at[EXPERT_ROW0:EXPERT_ROW0 + N_EXPERTS, 0].set(b_re.reshape(N_EXPERTS).astype(F32))
    return pl.pallas_call(
        functools.partial(_router_kernel, tr=tr),
        out_shape=(
            jax.ShapeDtypeStruct((tp * SUBLANES, LANES), jnp.uint32),
            jax.ShapeDtypeStruct((8, tp), jnp.int32),
            jax.ShapeDtypeStruct((8, tp), F32),
            jax.ShapeDtypeStruct((ROUTER_ROWS, LANES), F32),
        ),
        grid=(tp // tr,),
        in_specs=[
            pl.BlockSpec((tr, d), lambda i: (i, 0)),
            pl.BlockSpec((1, d), lambda i: (0, 0)),
            pl.BlockSpec((ROUTER_ROWS, d), lambda i: (0, 0)),
            pl.BlockSpec((ROUTER_ROWS, 1), lambda i: (0, 0)),
        ],
        out_specs=(
            pl.BlockSpec((tr * SUBLANES, LANES), lambda i: (i, 0)),
            pl.BlockSpec((8, tr), lambda i: (0, i)),
            pl.BlockSpec((8, tr), lambda i: (0, i)),
            pl.BlockSpec((ROUTER_ROWS, LANES), lambda i: (0, 0)),
        ),
        scratch_shapes=[pltpu.VMEM((ROUTER_ROWS, 1), F32)],
        compiler_params=_params(("arbitrary",)),
        name="moe_router",
    )(h, gain.reshape(1, d), wr, br)


def _slots_kernel(cnt_ref, idx_ref, pos_ref, meta_ref):
    e1 = idx_ref[0:1, :]
    e2 = idx_ref[1:2, :]
    off1 = jnp.zeros_like(e1)
    off2 = jnp.zeros_like(e2)
    visit = lax.broadcasted_iota(jnp.int32, (1, meta_ref.shape[1]), 1)
    v_tile = jnp.zeros_like(visit)
    v_expert = jnp.zeros_like(visit)
    v_lo = jnp.zeros_like(visit)
    v_hi = jnp.zeros_like(visit)
    v_next = jnp.zeros_like(visit)
    v_start = jnp.zeros_like(visit)
    v_count = jnp.zeros_like(visit)
    next_live = [None] * N_EXPERTS
    nxt = jnp.int32(-1)
    for e in reversed(range(N_EXPERTS)):
        next_live[e] = jnp.where(nxt >= 0, nxt, e)
        nxt = jnp.where(cnt_ref[e] > 0, e, nxt)
    start = jnp.int32(0)
    v_base = jnp.int32(0)
    for e in range(N_EXPERTS):
        off1 = jnp.where(e1 == e, start, off1)
        off2 = jnp.where(e2 == e, start, off2)
        n = cnt_ref[e]
        end = start + n
        first_tile = start // TILE_M
        n_visits = jnp.where(n > 0, (jnp.maximum(end, 1) - 1) // TILE_M - first_tile + 1, 0)
        mine = jnp.logical_and(visit >= v_base, visit < v_base + n_visits)
        row0 = (first_tile + visit - v_base) * TILE_M
        v_tile = jnp.where(mine, first_tile + visit - v_base, v_tile)
        v_expert = jnp.where(mine, e, v_expert)
        v_next = jnp.where(mine, next_live[e], v_next)
        v_start = jnp.where(mine, v_base, v_start)
        v_count = jnp.where(mine, n_visits, v_count)
        v_lo = jnp.where(mine, jnp.maximum(start - row0, 0), v_lo)
        v_hi = jnp.where(mine, jnp.minimum(end - row0, TILE_M), v_hi)
        start = end
        v_base = v_base + n_visits
    r8 = lax.broadcasted_iota(jnp.int32, pos_ref.shape, 0)
    pos_ref[...] = jnp.where(r8 == 0, off1 + idx_ref[2:3, :], jnp.where(r8 == 1, off2 + idx_ref[3:4, :], 0))
    m8 = lax.broadcasted_iota(jnp.int32, meta_ref.shape, 0)
    rows = (v_tile, v_expert, v_lo, v_hi, v_base, v_next, v_start, v_count)
    meta = rows[-1]
    for r in reversed(range(len(rows) - 1)):
        meta = jnp.where(m8 == r, rows[r], meta)
    meta_ref[...] = meta


def moe_slots(counts, idx, tr=512):
    tp = idx.shape[1]
    nt_lanes = 2 * LANES
    return pl.pallas_call(
        _slots_kernel,
        out_shape=(jax.ShapeDtypeStruct((8, tp), jnp.int32), jax.ShapeDtypeStruct((8, nt_lanes), jnp.int32)),
        grid_spec=pltpu.PrefetchScalarGridSpec(
            num_scalar_prefetch=1,
            grid=(tp // tr,),
            in_specs=[pl.BlockSpec((8, tr), lambda i, c: (0, i))],
            out_specs=(
                pl.BlockSpec((8, tr), lambda i, c: (0, i)),
                pl.BlockSpec((8, nt_lanes), lambda i, c: (0, 0)),
            ),
        ),
        compiler_params=_params(("arbitrary",)),
        name="moe_slots",
    )(counts, idx)


def _invert_kernel(pos_ref, code_ref, *, tp):
    def per_token(t, carry):
        code_ref[pos_ref[t]] = 2 * t
        code_ref[pos_ref[tp + t]] = 2 * t + 1
        return carry

    lax.fori_loop(0, tp, per_token, 0, unroll=8)


def moe_invert(pos_flat):
    tp = pos_flat.shape[0] // 2
    return pl.pallas_call(
        functools.partial(_invert_kernel, tp=tp),
        out_shape=jax.ShapeDtypeStruct((2 * tp,), jnp.int32),
        grid_spec=pltpu.PrefetchScalarGridSpec(
            num_scalar_prefetch=1,
            grid=(1,),
            in_specs=[],
            out_specs=pl.BlockSpec(memory_space=pltpu.SMEM),
        ),
        compiler_params=_params(("arbitrary",)),
        name="moe_invert",
    )(pos_flat)


def _expert_kernel(vt_ref, ve_ref, vlo_ref, vhi_ref, nv_ref, vnext_ref, vstart_ref, vcount_ref, code_ref,
                   hn_ref, wg_hbm, wu_hbm, wd_hbm, y_ref,
                   xbuf, xcur, yacc, wgs, wus, wds, wgb, wub, wdb, gsem, wsem, *, n_tiles, layer):
    v = pl.program_id(0)
    nv = nv_ref[0]
    t = vt_ref[v]
    prev_v = jnp.maximum(v - 1, 0)
    first = jnp.logical_or(v == 0, vt_ref[prev_v] != t)
    new_expert = jnp.logical_or(v == 0, ve_ref[prev_v] != ve_ref[v])

    def weight_copies(e):
        w = layer * N_EXPERTS + e
        return (pltpu.make_async_copy(wg_hbm.at[w], wgs, wsem.at[0]),
                pltpu.make_async_copy(wu_hbm.at[w], wus, wsem.at[1]),
                pltpu.make_async_copy(wd_hbm.at[w], wds, wsem.at[2]))

    def row_copy(tile, r):
        tok = lax.shift_right_logical(code_ref[tile * TILE_M + r], 1)
        return pltpu.make_async_copy(hn_ref.at[_token_tile(tok)], xbuf.at[_token_tile(r)], gsem)

    def gather_wait():
        pltpu.make_async_copy(hn_ref.at[pl.ds(0, TILE_M * SUBLANES)], xbuf, gsem).wait()

    def compute(accumulate):
        x = xcur[...]
        a = jnp.dot(x, wgb[...], preferred_element_type=F32)
        u = jnp.dot(x, wub[...], preferred_element_type=F32)
        row = lax.broadcasted_iota(jnp.int32, (TILE_M, 1), 0)
        mine = jnp.logical_and(row >= vlo_ref[v], row < vhi_ref[v])
        hid = jnp.where(mine, a * _sigmoid(a) * u, 0.0).astype(BF16)
        y = jnp.dot(hid, wdb[...], preferred_element_type=F32)
        if accumulate:
            y = y + yacc[...]
        yacc[...] = y
        _store_token_tiles(y_ref, (), _pack_halves(y))

    @pl.when(v == 0)
    def _():
        def step(r, carry):
            row_copy(0, r).start()
            return carry
        lax.fori_loop(0, TILE_M, step, 0, unroll=8)
        for copy in weight_copies(ve_ref[0]):
            copy.start()

    @pl.when(v < nv)
    def _():
        @pl.when(new_expert)
        def _():
            for copy in weight_copies(ve_ref[v]):
                copy.wait()
            wgb[...] = wgs[...].astype(BF16)
            wub[...] = wus[...].astype(BF16)
            wdb[...] = wds[...].astype(BF16)

        k = v - vstart_ref[v]
        last_of_expert = k == vcount_ref[v] - 1
        has_next = vnext_ref[v] != ve_ref[v]
        for c, copy in enumerate(weight_copies(vnext_ref[v])):
            @pl.when(jnp.logical_and(has_next, jnp.logical_or(k == c, jnp.logical_and(last_of_expert, k < c))))
            def _():
                copy.start()

        @pl.when(first)
        def _():
            gather_wait()
            xcur[...] = _unpack_halves(_load_token_tiles(xbuf, (), 0, TILE_M)).astype(BF16)
            ahead = jnp.minimum(t + 1, n_tiles - 1)
            for r in range(TILE_M):
                row_copy(ahead, r).start()
            compute(False)

        @pl.when(jnp.logical_not(first))
        def _():
            compute(True)

        @pl.when(v == nv - 1)
        def _():
            gather_wait()


def moe_experts(meta, code, hn, w_gate, w_up, w_down, layer):
    tp = hn.shape[0] // SUBLANES
    d = 2 * SUBLANES * LANES
    f = w_gate.shape[-1]
    assert (2 * tp) % TILE_M == 0
    n_tiles = (2 * tp) // TILE_M
    max_visits = n_tiles + N_EXPERTS - 1

    def y_map(v, vt, ve, vlo, vhi, nv, *_):
        return (vt[jnp.minimum(v, nv[0] - 1)], 0)

    any_spec = pl.BlockSpec(memory_space=pl.ANY)
    return pl.pallas_call(
        functools.partial(_expert_kernel, n_tiles=n_tiles, layer=layer),
        out_shape=jax.ShapeDtypeStruct((2 * tp * SUBLANES, LANES), jnp.uint32),
        grid_spec=pltpu.PrefetchScalarGridSpec(
            num_scalar_prefetch=9,
            grid=(max_visits,),
            in_specs=[any_spec, any_spec, any_spec, any_spec],
            out_specs=pl.BlockSpec((TILE_M * SUBLANES, LANES), y_map),
            scratch_shapes=[
                pltpu.VMEM((TILE_M * SUBLANES, LANES), jnp.uint32), pltpu.VMEM((TILE_M, d), BF16),
                pltpu.VMEM((TILE_M, d), F32),
                pltpu.VMEM((d, f), F32), pltpu.VMEM((d, f), F32), pltpu.VMEM((f, d), F32),
                pltpu.VMEM((d, f), BF16), pltpu.VMEM((d, f), BF16), pltpu.VMEM((f, d), BF16),
                pltpu.SemaphoreType.DMA, pltpu.SemaphoreType.DMA((3,)),
            ],
        ),
        compiler_params=_params(("arbitrary",), vmem=VMEM_LIMIT_LARGE),
        name="moe_experts",
    )(meta[0, :max_visits], meta[1, :max_visits], meta[2, :max_visits], meta[3, :max_visits], meta[4, :1],
      meta[5, :max_visits], meta[6, :max_visits], meta[7, :max_visits], code, hn, w_gate, w_up, w_down)


def _combine_kernel(pos_ref, *refs, tc, tp, n_steps, parts, first_token, write_h):
    h_refs, w_refs = refs[:parts], refs[parts:2 * parts]
    g_ref, y_ref = refs[2 * parts:2 * parts + 2]
    out_refs, (ybuf, sem) = refs[2 * parts + 2:-2], refs[-2:]
    i = pl.program_id(0)
    slot = lax.rem(i, 2)

    def row_copy(step, buf, s, r):
        p = pos_ref[s * tp + first_token(step) + r]
        return pltpu.make_async_copy(y_ref.at[_token_tile(p)], ybuf.at[buf, _token_tile(s * tc + r)],
                                     sem.at[buf])

    @pl.when(i == 0)
    def _():
        def step(r, carry):
            row_copy(0, 0, 0, r).start()
            row_copy(0, 0, 1, r).start()
            return carry
        lax.fori_loop(0, tc, step, 0, unroll=8)

    pltpu.make_async_copy(y_ref.at[pl.ds(0, 2 * tc * SUBLANES)], ybuf.at[slot], sem.at[slot]).wait()

    @pl.when(i + 1 < n_steps)
    def _():
        for r in range(tc):
            row_copy(i + 1, 1 - slot, 0, r).start()
            row_copy(i + 1, 1 - slot, 1, r).start()

    rows = tc // parts
    for c in range(parts):
        w = w_refs[c][...]
        h = (h_refs[c][...] + w[:, 0:1] * _unpack_halves(_load_token_tiles(ybuf, (slot,), c * rows, rows))
             + w[:, 1:2] * _unpack_halves(_load_token_tiles(ybuf, (slot,), tc + c * rows, rows)))
        if write_h:
            out_refs[0][c * rows:(c + 1) * rows, :] = h
        out_refs[-1][c * rows:(c + 1) * rows, :] = _rms(h, g_ref[...]).astype(out_refs[-1].dtype)


def moe_combine(pos_flat, h, w_col, y, gain, hn_dtype, final_shape=None):
    tp, d = h.shape
    tc = 256
    if final_shape is None:
        parts = 1
        n_steps = tp // tc

        def first_token(i):
            return i * tc

        def piece(c, width):
            return pl.BlockSpec((tc, width), lambda i, p: (i, 0))

        out_shape = (jax.ShapeDtypeStruct((tp, d), F32), jax.ShapeDtypeStruct((tp, d), hn_dtype))
        out_specs = (pl.BlockSpec((tc, d), lambda i, p: (i, 0)), pl.BlockSpec((tc, d), lambda i, p: (i, 0)))
    else:
        bsz, seq, _ = final_shape
        parts = tc // CHUNK
        nc = tp // bsz // CHUNK
        per_seq = seq // tc
        n_steps = bsz * per_seq

        def first_token(i):
            return (i // per_seq) * (nc * CHUNK) + CHUNK + (i % per_seq) * tc

        def piece(c, width):
            return pl.BlockSpec((CHUNK, width),
                                lambda i, p: ((i // per_seq) * nc + 1 + (i % per_seq) * parts + c, 0))

        out_shape = (jax.ShapeDtypeStruct(final_shape, hn_dtype),)
        out_specs = (pl.BlockSpec((None, tc, d), lambda i, p: (i // per_seq, i % per_seq, 0)),)
    return pl.pallas_call(
        functools.partial(_combine_kernel, tc=tc, tp=tp, n_steps=n_steps, parts=parts,
                          first_token=first_token, write_h=final_shape is None),
        out_shape=out_shape,
        grid_spec=pltpu.PrefetchScalarGridSpec(
            num_scalar_prefetch=1,
            grid=(n_steps,),
            in_specs=[piece(c, d) for c in range(parts)] + [piece(c, 2) for c in range(parts)] + [
                pl.BlockSpec((1, d), lambda i, p: (0, 0)),
                pl.BlockSpec(memory_space=pl.ANY),
            ],
            out_specs=out_specs,
            scratch_shapes=[pltpu.VMEM((2, 2 * tc * SUBLANES, LANES), jnp.uint32),
                            pltpu.SemaphoreType.DMA((2,))],
        ),
        compiler_params=_params(("arbitrary",)),
        name="moe_combine",
    )(pos_flat, *([h] * parts), *([w_col] * parts), gain.reshape(1, d), y)


def hierarchical_moe(h, gain, w_rg, b_rg, w_re, b_re, w_gate, w_up, w_down, layer, next_gain, hn_dtype,
                     final_shape=None):
    tp, d = h.shape
    f = w_gate.shape[-1]
    hn, idx, w_rows, cnt = moe_router(h, gain, w_rg, b_rg, w_re, b_re)
    counts = cnt[EXPERT_ROW0:EXPERT_ROW0 + N_EXPERTS, 0].astype(jnp.int32)
    pos, meta = moe_slots(counts, idx)
    pos_flat = pos[:2].reshape(2 * tp)
    code = moe_invert(pos_flat)
    y = moe_experts(meta, code, hn, w_gate.reshape(-1, d, f), w_up.reshape(-1, d, f),
                    w_down.reshape(-1, f, d), layer)
    return moe_combine(pos_flat, h, w_rows[:2].T, y, next_gain, hn_dtype, final_shape)


def kernel(x, meta_tokens, norm_mixer, norm_ffn, norm_final, ret_w_in, ret_gn, ret_w_out,
           fox_w_in, fox_b_f, fox_w_out, moe_w_rg, moe_b_rg, moe_w_re, moe_b_re,
           moe_w_gate, moe_w_up, moe_w_down):
    bsz, seq, d = x.shape
    depth = norm_mixer.shape[0]
    nc = (seq + CHUNK) // CHUNK
    lp = nc * CHUNK
    h, hn = embed_norm(x, meta_tokens.astype(x.dtype), norm_mixer[0], BF16)
    for i in range(depth):
        j = i // 2
        if i % 2 == 0:
            proj = matmul(hn, ret_w_in, j, BF16, tm=1536)
            gated = retention_core(proj, ret_gn[j], bsz, nc)
            h = matmul(gated, ret_w_out, j, F32, residual=h, tn=512)
        else:
            qkv = matmul(hn, fox_w_in, j, BF16, n=3 * d, tm=1536, scale_cols=d,
                         scale=(d // FOX_HEADS) ** -0.5 * LOG2E)
            pq, pk = forget_gates(hn, fox_w_in[j, :, 3 * d:], fox_b_f[j], bsz, lp)
            o = fox_attention(qkv, pq, pk, bsz, lp)
            h = matmul(o, fox_w_out, j, F32, residual=h, tm=1536, tn=512)
        last = i == depth - 1
        outs = hierarchical_moe(h, norm_ffn[i], moe_w_rg[i], moe_b_rg[i], moe_w_re[i], moe_b_re[i],
                                moe_w_gate, moe_w_up, moe_w_down, i,
                                norm_final if last else norm_mixer[i + 1],
                                F32 if last else BF16, (bsz, seq, d) if last else None)
        if last:
            return outs[0]
        h, hn = outs
```

```python
import functools

import jax
import jax.numpy as jnp
from jax import lax
from jax.experimental import pallas as pl
from jax.experimental.pallas import tpu as pltpu

N_META = 16
CHUNK = 128
PAD = CHUNK - N_META
RMS_EPS = 1e-6
GN_EPS = 1e-6
NEG_INF = -1e30
RET_HEADS = 8
FOX_HEADS = 16
N_GROUPS = 4
EXPERTS_PER_GROUP = 8
N_EXPERTS = N_GROUPS * EXPERTS_PER_GROUP
ROPE_BASE = 10000.0

LANES = 128
ROUTER_ROWS = 48
EXPERT_ROW0 = N_GROUPS
TILE_M = 256
VMEM_BYTES = 64 * 1024 * 1024
VMEM_LIMIT = 3 * VMEM_BYTES // 4
VMEM_LIMIT_LARGE = 7 * VMEM_BYTES // 8

F32 = jnp.float32
BF16 = jnp.bfloat16


def _params(sem, vmem=VMEM_LIMIT):
    return pltpu.CompilerParams(dimension_semantics=sem, vmem_limit_bytes=vmem)


def _rms(h, g):
    return h * lax.rsqrt(jnp.mean(h * h, axis=-1, keepdims=True) + RMS_EPS) * g


def _sigmoid(x):
    return 1.0 / (1.0 + jnp.exp(-x))


def _pack_halves(x):
    half = x.shape[1] // 2
    return pltpu.pack_elementwise([x[:, :half], x[:, half:]], packed_dtype=BF16)


def _unpack_halves(p):
    lo = pltpu.unpack_elementwise(p, index=0, packed_dtype=BF16, unpacked_dtype=F32)
    hi = pltpu.unpack_elementwise(p, index=1, packed_dtype=BF16, unpacked_dtype=F32)
    return jnp.concatenate([lo, hi], axis=1)


SUBLANES = 8


def _token_tile(t):
    start = t * SUBLANES
    return pl.ds(start if isinstance(t, int) else pl.multiple_of(start, SUBLANES), SUBLANES)


def _store_token_tiles(ref, index, packed):
    m = packed.shape[0]
    for s in range(SUBLANES):
        ref[index + (pl.ds(s, m, stride=SUBLANES), slice(None))] = packed[:, s * LANES:(s + 1) * LANES]


def _load_token_tiles(ref, index, first_token, m):
    return jnp.concatenate(
        [ref[index + (pl.ds(first_token * SUBLANES + s, m, stride=SUBLANES), slice(None))]
         for s in range(SUBLANES)], axis=1)


EMBED_CHUNKS = 3


def _embed_norm_kernel(*refs):
    x_refs, (meta_ref, g_ref, h_ref, hn_ref) = refs[:EMBED_CHUNKS], refs[EMBED_CHUNKS:]
    i = pl.program_id(1)
    for c, x_ref in enumerate(x_refs):
        h_ref[c * CHUNK:(c + 1) * CHUNK, :] = x_ref[...]

    @pl.when(i == 0)
    def _():
        h_ref[:PAD, :] = jnp.zeros((PAD, h_ref.shape[1]), h_ref.dtype)
        h_ref[PAD:CHUNK, :] = meta_ref[...]

    hn_ref[...] = _rms(h_ref[...], g_ref[...]).astype(hn_ref.dtype)


def embed_norm(x, meta, gain, hn_dtype):
    bsz, seq, d = x.shape
    nc = (seq + CHUNK) // CHUNK
    assert nc % EMBED_CHUNKS == 0
    tp = bsz * nc * CHUNK
    steps = nc // EMBED_CHUNKS
    rows = EMBED_CHUNKS * CHUNK

    def x_spec(c):
        return pl.BlockSpec((None, CHUNK, d), lambda b, i: (b, jnp.maximum(EMBED_CHUNKS * i + c - 1, 0), 0))

    return pl.pallas_call(
        _embed_norm_kernel,
        out_shape=(jax.ShapeDtypeStruct((tp, d), F32), jax.ShapeDtypeStruct((tp, d), hn_dtype)),
        grid=(bsz, steps),
        in_specs=[x_spec(c) for c in range(EMBED_CHUNKS)] + [
            pl.BlockSpec((N_META, d), lambda b, i: (0, 0)),
            pl.BlockSpec((1, d), lambda b, i: (0, 0)),
        ],
        out_specs=(
            pl.BlockSpec((rows, d), lambda b, i: (b * steps + i, 0)),
            pl.BlockSpec((rows, d), lambda b, i: (b * steps + i, 0)),
        ),
        compiler_params=_params(("parallel", "parallel")),
        name="embed_norm",
    )(*([x] * EMBED_CHUNKS), meta, gain.reshape(1, d))


def _mm_kernel(*refs, has_res, scale_tiles, scale):
    if has_res:
        x_ref, w_ref, r_ref, o_ref, wb_ref = refs
    else:
        x_ref, w_ref, o_ref, wb_ref = refs

    @pl.when(pl.program_id(1) == 0)
    def _():
        wb_ref[...] = w_ref[...].astype(BF16)

    acc = jnp.dot(x_ref[...], wb_ref[...], preferred_element_type=F32)
    if scale_tiles:
        acc = acc * jnp.where(pl.program_id(0) < scale_tiles, scale, 1.0)
    if has_res:
        acc = acc + r_ref[...]
    o_ref[...] = acc.astype(o_ref.dtype)


def matmul(x, w, layer, out_dtype, n=None, residual=None, tm=512, tn=1024, scale_cols=0, scale=1.0,
           single_buffer_w=False):
    m, kdim = x.shape
    n = w.shape[2] if n is None else n
    assert m % tm == 0 and n % tn == 0 and scale_cols % tn == 0
    w_buffers = 1 if single_buffer_w else 2
    in_specs = [
        pl.BlockSpec((tm, kdim), lambda j, i: (i, 0)),
        pl.BlockSpec((None, kdim, tn), lambda j, i: (layer, 0, j), pipeline_mode=pl.Buffered(w_buffers)),
    ]
    args = [x, w]
    vmem_bytes = (2 * tm * kdim * x.dtype.itemsize + w_buffers * kdim * tn * w.dtype.itemsize
                  + kdim * tn * 2 + 2 * tm * tn * jnp.dtype(out_dtype).itemsize)
    if residual is not None:
        in_specs.append(pl.BlockSpec((tm, tn), lambda j, i: (i, j)))
        args.append(residual)
        vmem_bytes += 2 * tm * tn * residual.dtype.itemsize
    return pl.pallas_call(
        functools.partial(_mm_kernel, has_res=residual is not None, scale_tiles=scale_cols // tn, scale=scale),
        out_shape=jax.ShapeDtypeStruct((m, n), out_dtype),
        grid=(n // tn, m // tm),
        in_specs=in_specs,
        out_specs=pl.BlockSpec((tm, tn), lambda j, i: (i, j)),
        scratch_shapes=[pltpu.VMEM((kdim, tn), BF16)],
        compiler_params=_params(("parallel", "arbitrary"),
                                vmem=VMEM_LIMIT if vmem_bytes < VMEM_LIMIT * 7 // 8 else VMEM_LIMIT_LARGE),
        name="matmul",
    )(*args)


def _retention_kernel(q_ref, k_ref, v_ref, g_ref, cos_ref, sin_ref, dmat_ref, xi_ref, zeta_ref,
                      gch_ref, gn_ref, o_ref, state_ref, *, nh, dk, dv):
    c = pl.program_id(1)

    @pl.when(c == 0)
    def _():
        state_ref[...] = jnp.zeros_like(state_ref)

    cos = cos_ref[...]
    sin = sin_ref[...]
    half = dk // 2

    def rot(u):
        u1, u2 = u[:, :half], u[:, half:]
        return jnp.concatenate([u1 * cos - u2 * sin, u1 * sin + u2 * cos], axis=1)

    row = lax.broadcasted_iota(jnp.int32, (CHUNK, 1), 0)
    valid = jnp.logical_or(row >= PAD, c > 0)
    for h in range(nh):
        q = rot(q_ref[:, h * dk:(h + 1) * dk].astype(F32))
        k = rot(k_ref[:, h * dk:(h + 1) * dk].astype(F32)) * (dk ** -0.5)
        k = jnp.where(valid, k, 0.0)
        v = v_ref[:, h * dv:(h + 1) * dv]
        vb = jnp.where(valid, v, jnp.zeros_like(v))
        qb = q.astype(BF16)
        kb = k.astype(BF16)

        scores = lax.dot_general(qb, kb, (((1,), (1,)), ((), ())), preferred_element_type=F32)
        scores = scores * dmat_ref[h]
        inner = jnp.dot(scores.astype(BF16), vb, preferred_element_type=F32)
        state = state_ref[h]
        cross = jnp.dot(qb, state.astype(BF16), preferred_element_type=F32) * xi_ref[h]
        y = inner + cross
        kz = (k * zeta_ref[h]).astype(BF16)
        state_ref[h] = gch_ref[h] * state + lax.dot_general(
            kz, vb, (((0,), (0,)), ((), ())), preferred_element_type=F32)

        mu = jnp.mean(y, axis=-1, keepdims=True)
        dlt = y - mu
        var = jnp.mean(dlt * dlt, axis=-1, keepdims=True)
        yn = dlt * lax.rsqrt(var + GN_EPS) * gn_ref[h]
        hg = 0.5 * g_ref[:, h * dv:(h + 1) * dv].astype(F32)
        o_ref[:, h * dv:(h + 1) * dv] = ((hg + hg * jnp.tanh(hg)) * yn).astype(o_ref.dtype)


def retention_core(proj, gn_gain, bsz, nc):
    tp = proj.shape[0]
    d = proj.shape[1] // 6
    nh = RET_HEADS
    dk = d // nh
    dv = 2 * d // nh
    lp = nc * CHUNK
    half = dk // 2
    pos = (jnp.arange(lp) - PAD).astype(F32)
    inv = ROPE_BASE ** (-jnp.arange(half, dtype=F32) / half)
    ang = pos[:, None] * inv[None, :]
    cos, sin = jnp.cos(ang), jnp.sin(ang)
    log_g = jnp.log1p(-jnp.exp2(-5.0 - jnp.arange(nh, dtype=F32)))
    idx = jnp.arange(CHUNK, dtype=F32)
    diff = idx[:, None] - idx[None, :]
    dmat = jnp.where(diff[None] >= 0, jnp.exp(jnp.maximum(diff, 0.0)[None] * log_g[:, None, None]), 0.0)
    xi = jnp.exp((idx + 1.0)[None, :] * log_g[:, None])[:, :, None]
    zeta = jnp.exp((CHUNK - 1.0 - idx)[None, :] * log_g[:, None])[:, :, None]
    g_chunk = jnp.exp(CHUNK * log_g)[:, None, None]
    return pl.pallas_call(
        functools.partial(_retention_kernel, nh=nh, dk=dk, dv=dv),
        out_shape=jax.ShapeDtypeStruct((tp, 2 * d), BF16),
        grid=(bsz, nc),
        in_specs=[
            pl.BlockSpec((CHUNK, d), lambda b, c: (b * nc + c, 0)),
            pl.BlockSpec((CHUNK, d), lambda b, c: (b * nc + c, 1)),
            pl.BlockSpec((CHUNK, 2 * d), lambda b, c: (b * nc + c, 1)),
            pl.BlockSpec((CHUNK, 2 * d), lambda b, c: (b * nc + c, 2)),
            pl.BlockSpec((CHUNK, half), lambda b, c: (c, 0)),
            pl.BlockSpec((CHUNK, half), lambda b, c: (c, 0)),
            pl.BlockSpec((nh, CHUNK, CHUNK), lambda b, c: (0, 0, 0)),
            pl.BlockSpec((nh, CHUNK, 1), lambda b, c: (0, 0, 0)),
            pl.BlockSpec((nh, CHUNK, 1), lambda b, c: (0, 0, 0)),
            pl.BlockSpec((nh, 1, 1), lambda b, c: (0, 0, 0)),
            pl.BlockSpec((nh, 1, dv), lambda b, c: (0, 0, 0)),
        ],
        out_specs=pl.BlockSpec((CHUNK, 2 * d), lambda b, c: (b * nc + c, 0)),
        scratch_shapes=[pltpu.VMEM((nh, dk, dv), F32)],
        compiler_params=_params(("parallel", "arbitrary")),
        name="retention_core",
    )(proj, proj, proj, proj, cos, sin, dmat, xi, zeta, g_chunk, gn_gain.reshape(nh, 1, dv))


LOG2E = 1.4426950408889634
MASK_BIG = 1e30
ONES_LANE = LANES - 1


def _bias_selectors(nh):
    h = jnp.arange(nh)
    selq = jnp.zeros((nh, LANES, LANES), F32)
    selk = jnp.zeros((nh, LANES, LANES), F32)
    for part in range(3):
        selq = selq.at[h, part * nh + h, part].set(1.0)
        selk = selk.at[h, part * nh + h, 3 + part].set(-1.0)
        selq = selq.at[h, ONES_LANE, 3 + part].set(1.0)
        selk = selk.at[h, ONES_LANE, part].set(1.0)
    return selq.astype(BF16), selk.astype(BF16)


def _forget_kernel(hn_ref, wf_ref, bf_ref, pq_ref, pk_ref, carry_ref, *, rows, nh):
    i = pl.program_id(1)

    @pl.when(i == 0)
    def _():
        carry_ref[...] = jnp.zeros_like(carry_ref)

    z = jnp.dot(hn_ref[...], wf_ref[...], preferred_element_type=F32) + bf_ref[...]
    lf = jnp.minimum(z, 0.0) - jnp.log1p(jnp.exp(-jnp.abs(z)))
    row = lax.broadcasted_iota(jnp.int32, (rows, 1), 0)
    valid = i * rows + row >= PAD
    lf = jnp.where(valid, lf, 0.0)

    def split3(a):
        hi = a.astype(BF16)
        r1 = a - hi.astype(F32)
        mid = r1.astype(BF16)
        lo = (r1 - mid.astype(F32)).astype(BF16)
        return hi, mid, lo

    r_i = lax.broadcasted_iota(jnp.int32, (rows, rows), 0)
    c_i = lax.broadcasted_iota(jnp.int32, (rows, rows), 1)
    tri = (r_i >= c_i).astype(BF16)
    cs = sum(jnp.dot(tri, part, preferred_element_type=F32) for part in split3(lf))
    cs = cs + carry_ref[...]
    carry_ref[...] = cs[rows - 1:rows, :]

    hi, mid, lo = (p.astype(F32) for p in split3(cs * LOG2E))
    lane = lax.broadcasted_iota(jnp.int32, (rows, LANES), 1)

    def lay_out(parts):
        row = jnp.where(lane == ONES_LANE, 1.0, 0.0)
        for k, part in enumerate(parts):
            moved = part if k == 0 else pltpu.roll(part, k * nh, axis=1)
            row = jnp.where(jnp.logical_and(lane >= k * nh, lane < (k + 1) * nh), moved, row)
        return row.astype(BF16)

    pq_ref[...] = lay_out((hi, mid, lo))
    pk_ref[...] = lay_out((jnp.where(valid, hi, MASK_BIG), jnp.where(valid, mid, 0.0),
                           jnp.where(valid, lo, 0.0)))


def forget_gates(hn, w_f, b_f, bsz, lp, rows=384):
    tp, d = hn.shape
    nh = w_f.shape[1]
    assert 3 * nh <= ONES_LANE and lp % rows == 0
    steps = lp // rows
    wf = jnp.zeros((d, LANES), BF16).at[:, :nh].set(w_f.astype(BF16))
    bf = jnp.zeros((1, LANES), F32).at[0, :nh].set(b_f.astype(F32))
    return pl.pallas_call(
        functools.partial(_forget_kernel, rows=rows, nh=nh),
        out_shape=(jax.ShapeDtypeStruct((tp, LANES), BF16), jax.ShapeDtypeStruct((tp, LANES), BF16)),
        grid=(bsz, steps),
        in_specs=[
            pl.BlockSpec((rows, d), lambda b, i: (b * steps + i, 0)),
            pl.BlockSpec((d, LANES), lambda b, i: (0, 0)),
            pl.BlockSpec((1, LANES), lambda b, i: (0, 0)),
        ],
        out_specs=(
            pl.BlockSpec((rows, LANES), lambda b, i: (b * steps + i, 0)),
            pl.BlockSpec((rows, LANES), lambda b, i: (b * steps + i, 0)),
        ),
        scratch_shapes=[pltpu.VMEM((1, LANES), F32)],
        compiler_params=_params(("parallel", "arbitrary")),
        name="forget_gates",
    )(hn, wf, bf)


FOX_HEADS_PER_STEP = 4


FOX_STRIP = 32


def _fox_kernel(q_ref, pq_ref, k_ref, pk_ref, v_ref, selq_ref, selk_ref, o_ref,
                m_ref, l_ref, acc_ref, s_ref, p_ref, a_ref, ck_ref, *, tq, tk, lp, dh):
    i = pl.program_id(2)
    hs = FOX_HEADS_PER_STEP
    reps = tk // LANES

    @pl.when(i == 0)
    def _():
        for j in range(hs):
            ck_ref[:, j * LANES:(j + 1) * LANES] = jnp.dot(
                pk_ref[...], selk_ref[j], preferred_element_type=F32).astype(BF16)

    def sweep(nr):
        m_ref[:, :nr, :] = jnp.full((hs, nr, LANES), 10.0 * NEG_INF, F32)
        l_ref[:, :nr, :] = jnp.zeros((hs, nr, LANES), F32)
        acc_ref[:, :nr, :] = jnp.zeros((hs, nr, dh), F32)
        qa = [jnp.concatenate(
            [q_ref[:nr, j * dh:(j + 1) * dh],
             jnp.dot(pq_ref[:nr, :], selq_ref[j], preferred_element_type=F32).astype(BF16)], axis=1)
            for j in range(hs)]

        def scores(kb, lo):
            start = pl.multiple_of(kb * tk, tk)
            for j in range(hs):
                ka = jnp.concatenate([k_ref[pl.ds(start, tk), j * dh:(j + 1) * dh],
                                      ck_ref[pl.ds(start, tk), j * LANES:(j + 1) * LANES]], axis=1)
                s_ref[j, lo:nr, :] = lax.dot_general(qa[j][lo:], ka, (((1,), (1,)), ((), ())),
                                                     preferred_element_type=F32)

        def absorb(kb, diag, lo):
            start = pl.multiple_of(kb * tk, tk)
            for j in range(hs):
                for r0 in range(lo, nr, FOX_STRIP):
                    rows = slice(r0, r0 + FOX_STRIP)
                    s = s_ref[j, rows, :]
                    if diag is not None and r0 < diag * tk + tk - 1:
                        r_i = r0 + lax.broadcasted_iota(jnp.int32, (FOX_STRIP, tk), 0)
                        c_i = diag * tk + lax.broadcasted_iota(jnp.int32, (FOX_STRIP, tk), 1)
                        s = jnp.where(c_i <= r_i, s, NEG_INF)
                    m_old = m_ref[j, rows, :]
                    m_new = jnp.maximum(m_old, jnp.max(s, axis=1, keepdims=True))
                    alpha = jnp.exp2(m_old - m_new)
                    p = jnp.exp2(s - jnp.concatenate([m_new] * reps, axis=1))
                    l_ref[j, rows, :] = alpha * l_ref[j, rows, :] + jnp.sum(p, axis=1, keepdims=True)
                    m_ref[j, rows, :] = m_new
                    a_ref[j, rows, :] = alpha
                    p_ref[j, rows, :] = p.astype(BF16)
            for j in range(hs):
                acc_ref[j, lo:nr, :] = a_ref[j, lo:nr, :] * acc_ref[j, lo:nr, :] + jnp.dot(
                    p_ref[j, lo:nr, :], v_ref[pl.ds(start, tk), j * dh:(j + 1) * dh],
                    preferred_element_type=F32)

        def body(kb, carry):
            scores(kb, 0)
            absorb(kb, None, 0)
            return carry

        n_full = i * (tq // tk)
        lax.fori_loop(0, n_full, body, 0)
        for diag in range(nr // tk):
            scores(n_full + diag, diag * tk)
            absorb(n_full + diag, diag, diag * tk)
        for j in range(hs):
            o_ref[:nr, j * dh:(j + 1) * dh] = (acc_ref[j, :nr, :] / l_ref[j, :nr, :]).astype(o_ref.dtype)

    tail = lp % tq
    if tail == 0:
        sweep(tq)
    else:
        @pl.when(i < lp // tq)
        def _():
            sweep(tq)

        @pl.when(i == lp // tq)
        def _():
            sweep(tail)


def fox_attention(qkv, pq, pk, bsz, lp, tq=768, tk=384):
    tp = qkv.shape[0]
    d = qkv.shape[1] // 3
    nh = FOX_HEADS
    dh = d // nh
    assert dh == LANES and tq % tk == 0 and lp % tk == 0 and (lp % tq) % tk == 0
    nq = pl.cdiv(lp, tq)
    hs = FOX_HEADS_PER_STEP
    ng = nh // hs
    qkv3 = qkv.reshape(bsz, lp, 3 * d)
    pq3 = pq.reshape(bsz, lp, LANES)
    pk3 = pk.reshape(bsz, lp, LANES)
    selq, selk = _bias_selectors(nh)
    out = pl.pallas_call(
        functools.partial(_fox_kernel, tq=tq, tk=tk, lp=lp, dh=dh),
        out_shape=jax.ShapeDtypeStruct((bsz, lp, d), BF16),
        grid=(bsz, ng, nq),
        in_specs=[
            pl.BlockSpec((None, tq, hs * dh), lambda b, g, i: (b, i, g)),
            pl.BlockSpec((None, tq, LANES), lambda b, g, i: (b, i, 0)),
            pl.BlockSpec((None, lp, hs * dh), lambda b, g, i: (b, 0, ng + g)),
            pl.BlockSpec((None, lp, LANES), lambda b, g, i: (b, 0, 0)),
            pl.BlockSpec((None, lp, hs * dh), lambda b, g, i: (b, 0, 2 * ng + g)),
            pl.BlockSpec((hs, LANES, LANES), lambda b, g, i: (g, 0, 0)),
            pl.BlockSpec((hs, LANES, LANES), lambda b, g, i: (g, 0, 0)),
        ],
        out_specs=pl.BlockSpec((None, tq, hs * dh), lambda b, g, i: (b, i, g)),
        scratch_shapes=[pltpu.VMEM((hs, tq, LANES), F32), pltpu.VMEM((hs, tq, LANES), F32),
                        pltpu.VMEM((hs, tq, dh), F32), pltpu.VMEM((hs, tq, tk), F32),
                        pltpu.VMEM((hs, tq, tk), BF16), pltpu.VMEM((hs, tq, LANES), F32),
                        pltpu.VMEM((lp, hs * LANES), BF16)],
        compiler_params=_params(("parallel", "parallel", "arbitrary"), vmem=VMEM_LIMIT_LARGE),
        name="fox_attention",
    )(qkv3, pq3, qkv3, pk3, qkv3, selq, selk)
    return out.reshape(tp, d)


def _router_kernel(h_ref, g_ref, wr_ref, br_ref, hn_ref, idx_ref, w_ref, cnt_ref, carry_ref, *, tr):
    i = pl.program_id(0)

    @pl.when(i == 0)
    def _():
        carry_ref[...] = jnp.zeros_like(carry_ref)

    hn = _rms(h_ref[...], g_ref[...])
    _store_token_tiles(hn_ref, (), _pack_halves(hn))
    lg = lax.dot_general(wr_ref[...], hn.astype(BF16), (((1,), (1,)), ((), ())),
                         preferred_element_type=F32) + br_ref[...]
    row = lax.broadcasted_iota(jnp.int32, (ROUTER_ROWS, tr), 0)
    big = jnp.int32(1 << 20)
    is_g = row < N_GROUPS
    mg = jnp.max(jnp.where(is_g, lg, -jnp.inf), axis=0, keepdims=True)
    g_sel = jnp.min(jnp.where(jnp.logical_and(is_g, lg == mg), row, big), axis=0, keepdims=True)
    sg = jnp.sum(jnp.where(is_g, jnp.exp(lg - mg), 0.0), axis=0, keepdims=True)
    p_g = 1.0 / sg
    lo = EXPERT_ROW0 + EXPERTS_PER_GROUP * g_sel
    is_e = jnp.logical_and(row >= lo, row < lo + EXPERTS_PER_GROUP)
    me = jnp.max(jnp.where(is_e, lg, -jnp.inf), axis=0, keepdims=True)
    ee = jnp.where(is_e, jnp.exp(lg - me), 0.0)
    pe = ee / jnp.sum(ee, axis=0, keepdims=True)
    pe1 = jnp.where(is_e, pe, -1.0)
    m1 = jnp.max(pe1, axis=0, keepdims=True)
    i1 = jnp.min(jnp.where(pe1 == m1, row, big), axis=0, keepdims=True)
    pe2 = jnp.where(row == i1, -1.0, pe1)
    m2 = jnp.max(pe2, axis=0, keepdims=True)
    i2 = jnp.min(jnp.where(pe2 == m2, row, big), axis=0, keepdims=True)
    den = m1 + m2
    w1 = p_g * (m1 / den)
    w2 = p_g * (m2 / den)

    oh1 = row == i1
    oh2 = row == i2
    ohs = jnp.logical_or(oh1, oh2).astype(F32)
    r_i = lax.broadcasted_iota(jnp.int32, (tr, tr), 0)
    c_i = lax.broadcasted_iota(jnp.int32, (tr, tr), 1)
    tri = (r_i < c_i).astype(BF16)
    cnt = jnp.dot(ohs.astype(BF16), tri, preferred_element_type=F32) + carry_ref[...]
    rank1 = jnp.sum(jnp.where(oh1, cnt, 0.0), axis=0, keepdims=True)
    rank2 = jnp.sum(jnp.where(oh2, cnt, 0.0), axis=0, keepdims=True)
    carry_ref[...] += jnp.sum(ohs, axis=1, keepdims=True)

    r8 = lax.broadcasted_iota(jnp.int32, (8, tr), 0)
    e1 = i1 - EXPERT_ROW0
    e2 = i2 - EXPERT_ROW0
    idx_ref[...] = jnp.where(r8 == 0, e1, jnp.where(r8 == 1, e2, jnp.where(
        r8 == 2, rank1.astype(jnp.int32), jnp.where(r8 == 3, rank2.astype(jnp.int32), 0))))
    w_ref[...] = jnp.where(r8 == 0, w1, jnp.where(r8 == 1, w2, 0.0))
    cnt_ref[...] = jnp.broadcast_to(carry_ref[...], cnt_ref.shape)


def moe_router(h, gain, w_rg, b_rg, w_re, b_re, tr=512):
    tp, d = h.shape
    assert d // 2 == SUBLANES * LANES
    wr = jnp.zeros((ROUTER_ROWS, d), BF16)
    wr = wr.at[:N_GROUPS].set(w_rg.T.astype(BF16))
    wr = wr.at[EXPERT_ROW0:EXPERT_ROW0 + N_EXPERTS].set(w_re.reshape(d, N_EXPERTS).T.astype(BF16))
    br = jnp.zeros((ROUTER_ROWS, 1), F32)
    br = br.at[:N_GROUPS, 0].set(b_rg.astype(F32))
    br = br.at[EXPERT_ROW0:EXPERT_ROW0 + N_EXPERTS, 0].set(b_re.reshape(N_EXPERTS).astype(F32))
    return pl.pallas_call(
        functools.partial(_router_kernel, tr=tr),
        out_shape=(
            jax.ShapeDtypeStruct((tp * SUBLANES, LANES), jnp.uint32),
            jax.ShapeDtypeStruct((8, tp), jnp.int32),
            jax.ShapeDtypeStruct((8, tp), F32),
            jax.ShapeDtypeStruct((ROUTER_ROWS, LANES), F32),
        ),
        grid=(tp // tr,),
        in_specs=[
            pl.BlockSpec((tr, d), lambda i: (i, 0)),
            pl.BlockSpec((1, d), lambda i: (0, 0)),
            pl.BlockSpec((ROUTER_ROWS, d), lambda i: (0, 0)),
            pl.BlockSpec((ROUTER_ROWS, 1), lambda i: (0, 0)),
        ],
        out_specs=(
            pl.BlockSpec((tr * SUBLANES, LANES), lambda i: (i, 0)),
            pl.BlockSpec((8, tr), lambda i: (0, i)),
            pl.BlockSpec((8, tr), lambda i: (0, i)),
            pl.BlockSpec((ROUTER_ROWS, LANES), lambda i: (0, 0)),
        ),
        scratch_shapes=[pltpu.VMEM((ROUTER_ROWS, 1), F32)],
        compiler_params=_params(("arbitrary",)),
        name="moe_router",
    )(h, gain.reshape(1, d), wr, br)


def _slots_kernel(cnt_ref, idx_ref, pos_ref, meta_ref):
    e1 = idx_ref[0:1, :]
    e2 = idx_ref[1:2, :]
    off1 = jnp.zeros_like(e1)
    off2 = jnp.zeros_like(e2)
    visit = lax.broadcasted_iota(jnp.int32, (1, meta_ref.shape[1]), 1)
    v_tile = jnp.zeros_like(visit)
    v_expert = jnp.zeros_like(visit)
    v_lo = jnp.zeros_like(visit)
    v_hi = jnp.zeros_like(visit)
    v_next = jnp.zeros_like(visit)
    v_start = jnp.zeros_like(visit)
    v_count = jnp.zeros_like(visit)
    next_live = [None] * N_EXPERTS
    nxt = jnp.int32(-1)
    for e in reversed(range(N_EXPERTS)):
        next_live[e] = jnp.where(nxt >= 0, nxt, e)
        nxt = jnp.where(cnt_ref[e] > 0, e, nxt)
    start = jnp.int32(0)
    v_base = jnp.int32(0)
    for e in range(N_EXPERTS):
        off1 = jnp.where(e1 == e, start, off1)
        off2 = jnp.where(e2 == e, start, off2)
        n = cnt_ref[e]
        end = start + n
        first_tile = start // TILE_M
        n_visits = jnp.where(n > 0, (jnp.maximum(end, 1) - 1) // TILE_M - first_tile + 1, 0)
        mine = jnp.logical_and(visit >= v_base, visit < v_base + n_visits)
        row0 = (first_tile + visit - v_base) * TILE_M
        v_tile = jnp.where(mine, first_tile + visit - v_base, v_tile)
        v_expert = jnp.where(mine, e, v_expert)
        v_next = jnp.where(mine, next_live[e], v_next)
        v_start = jnp.where(mine, v_base, v_start)
        v_count = jnp.where(mine, n_visits, v_count)
        v_lo = jnp.where(mine, jnp.maximum(start - row0, 0), v_lo)
        v_hi = jnp.where(mine, jnp.minimum(end - row0, TILE_M), v_hi)
        start = end
        v_base = v_base + n_visits
    r8 = lax.broadcasted_iota(jnp.int32, pos_ref.shape, 0)
    pos_ref[...] = jnp.where(r8 == 0, off1 + idx_ref[2:3, :], jnp.where(r8 == 1, off2 + idx_ref[3:4, :], 0))
    m8 = lax.broadcasted_iota(jnp.int32, meta_ref.shape, 0)
    rows = (v_tile, v_expert, v_lo, v_hi, v_base, v_next, v_start, v_count)
    meta = rows[-1]
    for r in reversed(range(len(rows) - 1)):
        meta = jnp.where(m8 == r, rows[r], meta)
    meta_ref[...] = meta


def moe_slots(counts, idx, tr=512):
    tp = idx.shape[1]
    nt_lanes = 2 * LANES
    return pl.pallas_call(
        _slots_kernel,
        out_shape=(jax.ShapeDtypeStruct((8, tp), jnp.int32), jax.ShapeDtypeStruct((8, nt_lanes), jnp.int32)),
        grid_spec=pltpu.PrefetchScalarGridSpec(
            num_scalar_prefetch=1,
            grid=(tp // tr,),
            in_specs=[pl.BlockSpec((8, tr), lambda i, c: (0, i))],
            out_specs=(
                pl.BlockSpec((8, tr), lambda i, c: (0, i)),
                pl.BlockSpec((8, nt_lanes), lambda i, c: (0, 0)),
            ),
        ),
        compiler_params=_params(("arbitrary",)),
        name="moe_slots",
    )(counts, idx)


def _invert_kernel(pos_ref, code_ref, *, tp):
    def per_token(t, carry):
        code_ref[pos_ref[t]] = 2 * t
        code_ref[pos_ref[tp + t]] = 2 * t + 1
        return carry

    lax.fori_loop(0, tp, per_token, 0, unroll=8)


def moe_invert(pos_flat):
    tp = pos_flat.shape[0] // 2
    return pl.pallas_call(
        functools.partial(_invert_kernel, tp=tp),
        out_shape=jax.ShapeDtypeStruct((2 * tp,), jnp.int32),
        grid_spec=pltpu.PrefetchScalarGridSpec(
            num_scalar_prefetch=1,
            grid=(1,),
            in_specs=[],
            out_specs=pl.BlockSpec(memory_space=pltpu.SMEM),
        ),
        compiler_params=_params(("arbitrary",)),
        name="moe_invert",
    )(pos_flat)


def _expert_kernel(vt_ref, ve_ref, vlo_ref, vhi_ref, nv_ref, vnext_ref, vstart_ref, vcount_ref, code_ref,
                   hn_ref, wg_hbm, wu_hbm, wd_hbm, y_ref,
                   xbuf, xcur, yacc, wgs, wus, wds, wgb, wub, wdb, gsem, wsem, *, n_tiles, layer):
    v = pl.program_id(0)
    nv = nv_ref[0]
    t = vt_ref[v]
    prev_v = jnp.maximum(v - 1, 0)
    first = jnp.logical_or(v == 0, vt_ref[prev_v] != t)
    new_expert = jnp.logical_or(v == 0, ve_ref[prev_v] != ve_ref[v])

    def weight_copies(e):
        w = layer * N_EXPERTS + e
        return (pltpu.make_async_copy(wg_hbm.at[w], wgs, wsem.at[0]),
                pltpu.make_async_copy(wu_hbm.at[w], wus, wsem.at[1]),
                pltpu.make_async_copy(wd_hbm.at[w], wds, wsem.at[2]))

    def row_copy(tile, r):
        tok = lax.shift_right_logical(code_ref[tile * TILE_M + r], 1)
        return pltpu.make_async_copy(hn_ref.at[_token_tile(tok)], xbuf.at[_token_tile(r)], gsem)

    def gather_wait():
        pltpu.make_async_copy(hn_ref.at[pl.ds(0, TILE_M * SUBLANES)], xbuf, gsem).wait()

    def compute(accumulate):
        x = xcur[...]
        a = jnp.dot(x, wgb[...], preferred_element_type=F32)
        u = jnp.dot(x, wub[...], preferred_element_type=F32)
        row = lax.broadcasted_iota(jnp.int32, (TILE_M, 1), 0)
        mine = jnp.logical_and(row >= vlo_ref[v], row < vhi_ref[v])
        hid = jnp.where(mine, a * _sigmoid(a) * u, 0.0).astype(BF16)
        y = jnp.dot(hid, wdb[...], preferred_element_type=F32)
        if accumulate:
            y = y + yacc[...]
        yacc[...] = y
        _store_token_tiles(y_ref, (), _pack_halves(y))

    @pl.when(v == 0)
    def _():
        def step(r, carry):
            row_copy(0, r).start()
            return carry
        lax.fori_loop(0, TILE_M, step, 0, unroll=8)
        for copy in weight_copies(ve_ref[0]):
            copy.start()

    @pl.when(v < nv)
    def _():
        @pl.when(new_expert)
        def _():
            for copy in weight_copies(ve_ref[v]):
                copy.wait()
            wgb[...] = wgs[...].astype(BF16)
            wub[...] = wus[...].astype(BF16)
            wdb[...] = wds[...].astype(BF16)

        @pl.when(first)
        def _():
            gather_wait()
            xcur[...] = _unpack_halves(_load_token_tiles(xbuf, (), 0, TILE_M)).astype(BF16)
            ahead = jnp.minimum(t + 1, n_tiles - 1)
            for r in range(TILE_M):
                row_copy(ahead, r).start()
            compute(False)

        @pl.when(jnp.logical_not(first))
        def _():
            compute(True)

        k = v - vstart_ref[v]
        last_of_expert = k == vcount_ref[v] - 1
        has_next = vnext_ref[v] != ve_ref[v]
        for c, copy in enumerate(weight_copies(vnext_ref[v])):
            @pl.when(jnp.logical_and(has_next, jnp.logical_or(k == c, jnp.logical_and(last_of_expert, k < c))))
            def _():
                copy.start()

        @pl.when(v == nv - 1)
        def _():
            gather_wait()


def moe_experts(meta, code, hn, w_gate, w_up, w_down, layer):
    tp = hn.shape[0] // SUBLANES
    d = 2 * SUBLANES * LANES
    f = w_gate.shape[-1]
    assert (2 * tp) % TILE_M == 0
    n_tiles = (2 * tp) // TILE_M
    max_visits = n_tiles + N_EXPERTS - 1

    def y_map(v, vt, ve, vlo, vhi, nv, *_):
        return (vt[jnp.minimum(v, nv[0] - 1)], 0)

    any_spec = pl.BlockSpec(memory_space=pl.ANY)
    return pl.pallas_call(
        functools.partial(_expert_kernel, n_tiles=n_tiles, layer=layer),
        out_shape=jax.ShapeDtypeStruct((2 * tp * SUBLANES, LANES), jnp.uint32),
        grid_spec=pltpu.PrefetchScalarGridSpec(
            num_scalar_prefetch=9,
            grid=(max_visits,),
            in_specs=[any_spec, any_spec, any_spec, any_spec],
            out_specs=pl.BlockSpec((TILE_M * SUBLANES, LANES), y_map),
            scratch_shapes=[
                pltpu.VMEM((TILE_M * SUBLANES, LANES), jnp.uint32), pltpu.VMEM((TILE_M, d), BF16),
                pltpu.VMEM((TILE_M, d), F32),
                pltpu.VMEM((d, f), F32), pltpu.VMEM((d, f), F32), pltpu.VMEM((f, d), F32),
                pltpu.VMEM((d, f), BF16), pltpu.VMEM((d, f), BF16), pltpu.VMEM((f, d), BF16),
                pltpu.SemaphoreType.DMA, pltpu.SemaphoreType.DMA((3,)),
            ],
        ),
        compiler_params=_params(("arbitrary",), vmem=VMEM_LIMIT_LARGE),
        name="moe_experts",
    )(meta[0, :max_visits], meta[1, :max_visits], meta[2, :max_visits], meta[3, :max_visits], meta[4, :1],
      meta[5, :max_visits], meta[6, :max_visits], meta[7, :max_visits], code, hn, w_gate, w_up, w_down)


def _combine_kernel(pos_ref, *refs, tc, tp, n_steps, parts, first_token, write_h):
    h_refs, w_refs = refs[:parts], refs[parts:2 * parts]
    g_ref, y_ref = refs[2 * parts:2 * parts + 2]
    out_refs, (ybuf, sem) = refs[2 * parts + 2:-2], refs[-2:]
    i = pl.program_id(0)
    slot = lax.rem(i, 2)

    def row_copy(step, buf, s, r):
        p = pos_ref[s * tp + first_token(step) + r]
        return pltpu.make_async_copy(y_ref.at[_token_tile(p)], ybuf.at[buf, _token_tile(s * tc + r)],
                                     sem.at[buf])

    @pl.when(i == 0)
    def _():
        def step(r, carry):
            row_copy(0, 0, 0, r).start()
            row_copy(0, 0, 1, r).start()
            return carry
        lax.fori_loop(0, tc, step, 0, unroll=8)

    pltpu.make_async_copy(y_ref.at[pl.ds(0, 2 * tc * SUBLANES)], ybuf.at[slot], sem.at[slot]).wait()

    @pl.when(i + 1 < n_steps)
    def _():
        for r in range(tc):
            row_copy(i + 1, 1 - slot, 0, r).start()
            row_copy(i + 1, 1 - slot, 1, r).start()

    rows = tc // parts
    for c in range(parts):
        w = w_refs[c][...]
        h = (h_refs[c][...] + w[:, 0:1] * _unpack_halves(_load_token_tiles(ybuf, (slot,), c * rows, rows))
             + w[:, 1:2] * _unpack_halves(_load_token_tiles(ybuf, (slot,), tc + c * rows, rows)))
        if write_h:
            out_refs[0][c * rows:(c + 1) * rows, :] = h
        out_refs[-1][c * rows:(c + 1) * rows, :] = _rms(h, g_ref[...]).astype(out_refs[-1].dtype)


def moe_combine(pos_flat, h, w_col, y, gain, hn_dtype, final_shape=None):
    tp, d = h.shape
    tc = 256
    if final_shape is None:
        parts = 1
        n_steps = tp // tc

        def first_token(i):
            return i * tc

        def piece(c, width):
            return pl.BlockSpec((tc, width), lambda i, p: (i, 0))

        out_shape = (jax.ShapeDtypeStruct((tp, d), F32), jax.ShapeDtypeStruct((tp, d), hn_dtype))
        out_specs = (pl.BlockSpec((tc, d), lambda i, p: (i, 0)), pl.BlockSpec((tc, d), lambda i, p: (i, 0)))
    else:
        bsz, seq, _ = final_shape
        parts = tc // CHUNK
        nc = tp // bsz // CHUNK
        per_seq = seq // tc
        n_steps = bsz * per_seq

        def first_token(i):
            return (i // per_seq) * (nc * CHUNK) + CHUNK + (i % per_seq) * tc

        def piece(c, width):
            return pl.BlockSpec((CHUNK, width),
                                lambda i, p: ((i // per_seq) * nc + 1 + (i % per_seq) * parts + c, 0))

        out_shape = (jax.ShapeDtypeStruct(final_shape, hn_dtype),)
        out_specs = (pl.BlockSpec((None, tc, d), lambda i, p: (i // per_seq, i % per_seq, 0)),)
    return pl.pallas_call(
        functools.partial(_combine_kernel, tc=tc, tp=tp, n_steps=n_steps, parts=parts,
                          first_token=first_token, write_h=final_shape is None),
        out_shape=out_shape,
        grid_spec=pltpu.PrefetchScalarGridSpec(
            num_scalar_prefetch=1,
            grid=(n_steps,),
            in_specs=[piece(c, d) for c in range(parts)] + [piece(c, 2) for c in range(parts)] + [
                pl.BlockSpec((1, d), lambda i, p: (0, 0)),
                pl.BlockSpec(memory_space=pl.ANY),
            ],
            out_specs=out_specs,
            scratch_shapes=[pltpu.VMEM((2, 2 * tc * SUBLANES, LANES), jnp.uint32),
                            pltpu.SemaphoreType.DMA((2,))],
        ),
        compiler_params=_params(("arbitrary",)),
        name="moe_combine",
    )(pos_flat, *([h] * parts), *([w_col] * parts), gain.reshape(1, d), y)


def hierarchical_moe(h, gain, w_rg, b_rg, w_re, b_re, w_gate, w_up, w_down, layer, next_gain, hn_dtype,
                     final_shape=None):
    tp, d = h.shape
    f = w_gate.shape[-1]
    hn, idx, w_rows, cnt = moe_router(h, gain, w_rg, b_rg, w_re, b_re)
    counts = cnt[EXPERT_ROW0:EXPERT_ROW0 + N_EXPERTS, 0].astype(jnp.int32)
    pos, meta = moe_slots(counts, idx)
    pos_flat = pos[:2].reshape(2 * tp)
    code = moe_invert(pos_flat)
    y = moe_experts(meta, code, hn, w_gate.reshape(-1, d, f), w_up.reshape(-1, d, f),
                    w_down.reshape(-1, f, d), layer)
    return moe_combine(pos_flat, h, w_rows[:2].T, y, next_gain, hn_dtype, final_shape)


def kernel(x, meta_tokens, norm_mixer, norm_ffn, norm_final, ret_w_in, ret_gn, ret_w_out,
           fox_w_in, fox_b_f, fox_w_out, moe_w_rg, moe_b_rg, moe_w_re, moe_b_re,
           moe_w_gate, moe_w_up, moe_w_down):
    bsz, seq, d = x.shape
    depth = norm_mixer.shape[0]
    nc = (seq + CHUNK) // CHUNK
    lp = nc * CHUNK
    h, hn = embed_norm(x, meta_tokens.astype(x.dtype), norm_mixer[0], BF16)
    for i in range(depth):
        j = i // 2
        if i % 2 == 0:
            proj = matmul(hn, ret_w_in, j, BF16, tm=1536)
            gated = retention_core(proj, ret_gn[j], bsz, nc)
            h = matmul(gated, ret_w_out, j, F32, residual=h, tm=1536, tn=512, single_buffer_w=True)
        else:
            qkv = matmul(hn, fox_w_in, j, BF16, n=3 * d, tm=1536, scale_cols=d,
                         scale=(d // FOX_HEADS) ** -0.5 * LOG2E)
            pq, pk = forget_gates(hn, fox_w_in[j, :, 3 * d:], fox_b_f[j], bsz, lp)
            o = fox_attention(qkv, pq, pk, bsz, lp)
            h = matmul(o, fox_w_out, j, F32, residual=h, tm=1536, tn=512)
        last = i == depth - 1
        outs = hierarchical_moe(h, norm_ffn[i], moe_w_rg[i], moe_b_rg[i], moe_w_re[i], moe_b_re[i],
                                moe_w_gate, moe_w_up, moe_w_down, i,
                                norm_final if last else norm_mixer[i + 1],
                                F32 if last else BF16, (bsz, seq, d) if last else None)
        if last:
            return outs[0]
        h, hn = outs
```

```python
import functools

import jax
import jax.numpy as jnp
from jax import lax
from jax.experimental import pallas as pl
from jax.experimental.pallas import tpu as pltpu

N_META = 16
CHUNK = 128
PAD = CHUNK - N_META
RMS_EPS = 1e-6
GN_EPS = 1e-6
NEG_INF = -1e30
RET_HEADS = 8
FOX_HEADS = 16
N_GROUPS = 4
EXPERTS_PER_GROUP = 8
N_EXPERTS = N_GROUPS * EXPERTS_PER_GROUP
ROPE_BASE = 10000.0

LANES = 128
ROUTER_ROWS = 48
EXPERT_ROW0 = N_GROUPS
TILE_M = 256
VMEM_BYTES = 64 * 1024 * 1024
VMEM_LIMIT = 3 * VMEM_BYTES // 4
VMEM_LIMIT_LARGE = 7 * VMEM_BYTES // 8

F32 = jnp.float32
BF16 = jnp.bfloat16


def _params(sem, vmem=VMEM_LIMIT):
    return pltpu.CompilerParams(dimension_semantics=sem, vmem_limit_bytes=vmem)


def _rms(h, g):
    return h * lax.rsqrt(jnp.mean(h * h, axis=-1, keepdims=True) + RMS_EPS) * g


def _sigmoid(x):
    return 1.0 / (1.0 + jnp.exp(-x))


def _pack_halves(x):
    half = x.shape[1] // 2
    return pltpu.pack_elementwise([x[:, :half], x[:, half:]], packed_dtype=BF16)


def _unpack_halves(p):
    lo = pltpu.unpack_elementwise(p, index=0, packed_dtype=BF16, unpacked_dtype=F32)
    hi = pltpu.unpack_elementwise(p, index=1, packed_dtype=BF16, unpacked_dtype=F32)
    return jnp.concatenate([lo, hi], axis=1)


SUBLANES = 8


def _token_tile(t):
    start = t * SUBLANES
    return pl.ds(start if isinstance(t, int) else pl.multiple_of(start, SUBLANES), SUBLANES)


def _store_token_tiles(ref, index, packed):
    m = packed.shape[0]
    for s in range(SUBLANES):
        ref[index + (pl.ds(s, m, stride=SUBLANES), slice(None))] = packed[:, s * LANES:(s + 1) * LANES]


def _load_token_tiles(ref, index, first_token, m):
    return jnp.concatenate(
        [ref[index + (pl.ds(first_token * SUBLANES + s, m, stride=SUBLANES), slice(None))]
         for s in range(SUBLANES)], axis=1)


EMBED_CHUNKS = 3


def _embed_norm_kernel(*refs):
    x_refs, (meta_ref, g_ref, h_ref, hn_ref) = refs[:EMBED_CHUNKS], refs[EMBED_CHUNKS:]
    i = pl.program_id(1)
    for c, x_ref in enumerate(x_refs):
        h_ref[c * CHUNK:(c + 1) * CHUNK, :] = x_ref[...]

    @pl.when(i == 0)
    def _():
        h_ref[:PAD, :] = jnp.zeros((PAD, h_ref.shape[1]), h_ref.dtype)
        h_ref[PAD:CHUNK, :] = meta_ref[...]

    hn_ref[...] = _rms(h_ref[...], g_ref[...]).astype(hn_ref.dtype)


def embed_norm(x, meta, gain, hn_dtype):
    bsz, seq, d = x.shape
    nc = (seq + CHUNK) // CHUNK
    assert nc % EMBED_CHUNKS == 0
    tp = bsz * nc * CHUNK
    steps = nc // EMBED_CHUNKS
    rows = EMBED_CHUNKS * CHUNK

    def x_spec(c):
        return pl.BlockSpec((None, CHUNK, d), lambda b, i: (b, jnp.maximum(EMBED_CHUNKS * i + c - 1, 0), 0))

    return pl.pallas_call(
        _embed_norm_kernel,
        out_shape=(jax.ShapeDtypeStruct((tp, d), F32), jax.ShapeDtypeStruct((tp, d), hn_dtype)),
        grid=(bsz, steps),
        in_specs=[x_spec(c) for c in range(EMBED_CHUNKS)] + [
            pl.BlockSpec((N_META, d), lambda b, i: (0, 0)),
            pl.BlockSpec((1, d), lambda b, i: (0, 0)),
        ],
        out_specs=(
            pl.BlockSpec((rows, d), lambda b, i: (b * steps + i, 0)),
            pl.BlockSpec((rows, d), lambda b, i: (b * steps + i, 0)),
        ),
        compiler_params=_params(("parallel", "parallel")),
        name="embed_norm",
    )(*([x] * EMBED_CHUNKS), meta, gain.reshape(1, d))


def _mm_kernel(*refs, has_res, scale_tiles, scale):
    if has_res:
        x_ref, w_ref, r_ref, o_ref, wb_ref = refs
    else:
        x_ref, w_ref, o_ref, wb_ref = refs

    @pl.when(pl.program_id(1) == 0)
    def _():
        wb_ref[...] = w_ref[...].astype(BF16)

    acc = jnp.dot(x_ref[...], wb_ref[...], preferred_element_type=F32)
    if scale_tiles:
        acc = acc * jnp.where(pl.program_id(0) < scale_tiles, scale, 1.0)
    if has_res:
        acc = acc + r_ref[...]
    o_ref[...] = acc.astype(o_ref.dtype)


def matmul(x, w, layer, out_dtype, n=None, residual=None, tm=512, tn=1024, scale_cols=0, scale=1.0,
           single_buffer_w=False):
    m, kdim = x.shape
    n = w.shape[2] if n is None else n
    assert m % tm == 0 and n % tn == 0 and scale_cols % tn == 0
    w_buffers = 1 if single_buffer_w else 2
    in_specs = [
        pl.BlockSpec((tm, kdim), lambda j, i: (i, 0)),
        pl.BlockSpec((None, kdim, tn), lambda j, i: (layer, 0, j), pipeline_mode=pl.Buffered(w_buffers)),
    ]
    args = [x, w]
    vmem_bytes = (2 * tm * kdim * x.dtype.itemsize + w_buffers * kdim * tn * w.dtype.itemsize
                  + kdim * tn * 2 + 2 * tm * tn * jnp.dtype(out_dtype).itemsize)
    if residual is not None:
        in_specs.append(pl.BlockSpec((tm, tn), lambda j, i: (i, j)))
        args.append(residual)
        vmem_bytes += 2 * tm * tn * residual.dtype.itemsize
    return pl.pallas_call(
        functools.partial(_mm_kernel, has_res=residual is not None, scale_tiles=scale_cols // tn, scale=scale),
        out_shape=jax.ShapeDtypeStruct((m, n), out_dtype),
        grid=(n // tn, m // tm),
        in_specs=in_specs,
        out_specs=pl.BlockSpec((tm, tn), lambda j, i: (i, j)),
        scratch_shapes=[pltpu.VMEM((kdim, tn), BF16)],
        compiler_params=_params(("parallel", "arbitrary"),
                                vmem=VMEM_LIMIT if vmem_bytes < VMEM_LIMIT * 7 // 8 else VMEM_LIMIT_LARGE),
        name="matmul",
    )(*args)


def _retention_kernel(q_ref, k_ref, v_ref, g_ref, cos_ref, sin_ref, dmat_ref, xi_ref, zeta_ref,
                      gch_ref, gn_ref, o_ref, state_ref, *, nh, dk, dv):
    c = pl.program_id(1)

    @pl.when(c == 0)
    def _():
        state_ref[...] = jnp.zeros_like(state_ref)

    cos = cos_ref[...]
    sin = sin_ref[...]
    half = dk // 2

    def rot(u):
        u1, u2 = u[:, :half], u[:, half:]
        return jnp.concatenate([u1 * cos - u2 * sin, u1 * sin + u2 * cos], axis=1)

    row = lax.broadcasted_iota(jnp.int32, (CHUNK, 1), 0)
    valid = jnp.logical_or(row >= PAD, c > 0)
    for h in range(nh):
        q = rot(q_ref[:, h * dk:(h + 1) * dk].astype(F32))
        k = rot(k_ref[:, h * dk:(h + 1) * dk].astype(F32)) * (dk ** -0.5)
        k = jnp.where(valid, k, 0.0)
        v = v_ref[:, h * dv:(h + 1) * dv]
        vb = jnp.where(valid, v, jnp.zeros_like(v))
        qb = q.astype(BF16)
        kb = k.astype(BF16)

        scores = lax.dot_general(qb, kb, (((1,), (1,)), ((), ())), preferred_element_type=F32)
        scores = scores * dmat_ref[h]
        inner = jnp.dot(scores.astype(BF16), vb, preferred_element_type=F32)
        state = state_ref[h]
        cross = jnp.dot(qb, state.astype(BF16), preferred_element_type=F32) * xi_ref[h]
        y = inner + cross
        kz = (k * zeta_ref[h]).astype(BF16)
        state_ref[h] = gch_ref[h] * state + lax.dot_general(
            kz, vb, (((0,), (0,)), ((), ())), preferred_element_type=F32)

        mu = jnp.mean(y, axis=-1, keepdims=True)
        dlt = y - mu
        var = jnp.mean(dlt * dlt, axis=-1, keepdims=True)
        yn = dlt * lax.rsqrt(var + GN_EPS) * gn_ref[h]
        hg = 0.5 * g_ref[:, h * dv:(h + 1) * dv].astype(F32)
        o_ref[:, h * dv:(h + 1) * dv] = ((hg + hg * jnp.tanh(hg)) * yn).astype(o_ref.dtype)


def retention_core(proj, gn_gain, bsz, nc):
    tp = proj.shape[0]
    d = proj.shape[1] // 6
    nh = RET_HEADS
    dk = d // nh
    dv = 2 * d // nh
    lp = nc * CHUNK
    half = dk // 2
    pos = (jnp.arange(lp) - PAD).astype(F32)
    inv = ROPE_BASE ** (-jnp.arange(half, dtype=F32) / half)
    ang = pos[:, None] * inv[None, :]
    cos, sin = jnp.cos(ang), jnp.sin(ang)
    log_g = jnp.log1p(-jnp.exp2(-5.0 - jnp.arange(nh, dtype=F32)))
    idx = jnp.arange(CHUNK, dtype=F32)
    diff = idx[:, None] - idx[None, :]
    dmat = jnp.where(diff[None] >= 0, jnp.exp(jnp.maximum(diff, 0.0)[None] * log_g[:, None, None]), 0.0)
    xi = jnp.exp((idx + 1.0)[None, :] * log_g[:, None])[:, :, None]
    zeta = jnp.exp((CHUNK - 1.0 - idx)[None, :] * log_g[:, None])[:, :, None]
    g_chunk = jnp.exp(CHUNK * log_g)[:, None, None]
    return pl.pallas_call(
        functools.partial(_retention_kernel, nh=nh, dk=dk, dv=dv),
        out_shape=jax.ShapeDtypeStruct((tp, 2 * d), BF16),
        grid=(bsz, nc),
        in_specs=[
            pl.BlockSpec((CHUNK, d), lambda b, c: (b * nc + c, 0)),
            pl.BlockSpec((CHUNK, d), lambda b, c: (b * nc + c, 1)),
            pl.BlockSpec((CHUNK, 2 * d), lambda b, c: (b * nc + c, 1)),
            pl.BlockSpec((CHUNK, 2 * d), lambda b, c: (b * nc + c, 2)),
            pl.BlockSpec((CHUNK, half), lambda b, c: (c, 0)),
            pl.BlockSpec((CHUNK, half), lambda b, c: (c, 0)),
            pl.BlockSpec((nh, CHUNK, CHUNK), lambda b, c: (0, 0, 0)),
            pl.BlockSpec((nh, CHUNK, 1), lambda b, c: (0, 0, 0)),
            pl.BlockSpec((nh, CHUNK, 1), lambda b, c: (0, 0, 0)),
            pl.BlockSpec((nh, 1, 1), lambda b, c: (0, 0, 0)),
            pl.BlockSpec((nh, 1, dv), lambda b, c: (0, 0, 0)),
        ],
        out_specs=pl.BlockSpec((CHUNK, 2 * d), lambda b, c: (b * nc + c, 0)),
        scratch_shapes=[pltpu.VMEM((nh, dk, dv), F32)],
        compiler_params=_params(("parallel", "arbitrary")),
        name="retention_core",
    )(proj, proj, proj, proj, cos, sin, dmat, xi, zeta, g_chunk, gn_gain.reshape(nh, 1, dv))


LOG2E = 1.4426950408889634
MASK_BIG = 1e30
ONES_LANE = LANES - 1


def _bias_selectors(nh):
    h = jnp.arange(nh)
    selq = jnp.zeros((nh, LANES, LANES), F32)
    selk = jnp.zeros((nh, LANES, LANES), F32)
    for part in range(3):
        selq = selq.at[h, part * nh + h, part].set(1.0)
        selk = selk.at[h, part * nh + h, 3 + part].set(-1.0)
        selq = selq.at[h, ONES_LANE, 3 + part].set(1.0)
        selk = selk.at[h, ONES_LANE, part].set(1.0)
    return selq.astype(BF16), selk.astype(BF16)


def _forget_kernel(hn_ref, wf_ref, bf_ref, pq_ref, pk_ref, carry_ref, *, rows, nh):
    i = pl.program_id(1)

    @pl.when(i == 0)
    def _():
        carry_ref[...] = jnp.zeros_like(carry_ref)

    z = jnp.dot(hn_ref[...], wf_ref[...], preferred_element_type=F32) + bf_ref[...]
    lf = jnp.minimum(z, 0.0) - jnp.log1p(jnp.exp(-jnp.abs(z)))
    row = lax.broadcasted_iota(jnp.int32, (rows, 1), 0)
    valid = i * rows + row >= PAD
    lf = jnp.where(valid, lf, 0.0)

    def split3(a):
        hi = a.astype(BF16)
        r1 = a - hi.astype(F32)
        mid = r1.astype(BF16)
        lo = (r1 - mid.astype(F32)).astype(BF16)
        return hi, mid, lo

    r_i = lax.broadcasted_iota(jnp.int32, (rows, rows), 0)
    c_i = lax.broadcasted_iota(jnp.int32, (rows, rows), 1)
    tri = (r_i >= c_i).astype(BF16)
    cs = sum(jnp.dot(tri, part, preferred_element_type=F32) for part in split3(lf))
    cs = cs + carry_ref[...]
    carry_ref[...] = cs[rows - 1:rows, :]

    hi, mid, lo = (p.astype(F32) for p in split3(cs * LOG2E))
    lane = lax.broadcasted_iota(jnp.int32, (rows, LANES), 1)

    def lay_out(parts):
        row = jnp.where(lane == ONES_LANE, 1.0, 0.0)
        for k, part in enumerate(parts):
            moved = part if k == 0 else pltpu.roll(part, k * nh, axis=1)
            row = jnp.where(jnp.logical_and(lane >= k * nh, lane < (k + 1) * nh), moved, row)
        return row.astype(BF16)

    pq_ref[...] = lay_out((hi, mid, lo))
    pk_ref[...] = lay_out((jnp.where(valid, hi, MASK_BIG), jnp.where(valid, mid, 0.0),
                           jnp.where(valid, lo, 0.0)))


def forget_gates(hn, w_f, b_f, bsz, lp, rows=384):
    tp, d = hn.shape
    nh = w_f.shape[1]
    assert 3 * nh <= ONES_LANE and lp % rows == 0
    steps = lp // rows
    wf = jnp.zeros((d, LANES), BF16).at[:, :nh].set(w_f.astype(BF16))
    bf = jnp.zeros((1, LANES), F32).at[0, :nh].set(b_f.astype(F32))
    return pl.pallas_call(
        functools.partial(_forget_kernel, rows=rows, nh=nh),
        out_shape=(jax.ShapeDtypeStruct((tp, LANES), BF16), jax.ShapeDtypeStruct((tp, LANES), BF16)),
        grid=(bsz, steps),
        in_specs=[
            pl.BlockSpec((rows, d), lambda b, i: (b * steps + i, 0)),
            pl.BlockSpec((d, LANES), lambda b, i: (0, 0)),
            pl.BlockSpec((1, LANES), lambda b, i: (0, 0)),
        ],
        out_specs=(
            pl.BlockSpec((rows, LANES), lambda b, i: (b * steps + i, 0)),
            pl.BlockSpec((rows, LANES), lambda b, i: (b * steps + i, 0)),
        ),
        scratch_shapes=[pltpu.VMEM((1, LANES), F32)],
        compiler_params=_params(("parallel", "arbitrary")),
        name="forget_gates",
    )(hn, wf, bf)


FOX_HEADS_PER_STEP = 4


FOX_STRIP = 32


def _fox_kernel(q_ref, pq_ref, k_ref, pk_ref, v_ref, selq_ref, selk_ref, o_ref,
                m_ref, l_ref, acc_ref, s_ref, p_ref, a_ref, ck_ref, *, tq, tk, lp, dh):
    i = pl.program_id(2)
    hs = FOX_HEADS_PER_STEP
    reps = tk // LANES

    @pl.when(i == 0)
    def _():
        for j in range(hs):
            ck_ref[:, j * LANES:(j + 1) * LANES] = jnp.dot(
                pk_ref[...], selk_ref[j], preferred_element_type=F32).astype(BF16)

    def sweep(nr):
        m_ref[:, :nr, :] = jnp.full((hs, nr, LANES), 10.0 * NEG_INF, F32)
        l_ref[:, :nr, :] = jnp.zeros((hs, nr, LANES), F32)
        acc_ref[:, :nr, :] = jnp.zeros((hs, nr, dh), F32)
        qa = [jnp.concatenate(
            [q_ref[:nr, j * dh:(j + 1) * dh],
             jnp.dot(pq_ref[:nr, :], selq_ref[j], preferred_element_type=F32).astype(BF16)], axis=1)
            for j in range(hs)]

        def scores(kb, lo):
            start = pl.multiple_of(kb * tk, tk)
            for j in range(hs):
                ka = jnp.concatenate([k_ref[pl.ds(start, tk), j * dh:(j + 1) * dh],
                                      ck_ref[pl.ds(start, tk), j * LANES:(j + 1) * LANES]], axis=1)
                s_ref[j, lo:nr, :] = lax.dot_general(qa[j][lo:], ka, (((1,), (1,)), ((), ())),
                                                     preferred_element_type=F32)

        def absorb(kb, diag, lo):
            start = pl.multiple_of(kb * tk, tk)
            for j in range(hs):
                for r0 in range(lo, nr, FOX_STRIP):
                    rows = slice(r0, r0 + FOX_STRIP)
                    s = s_ref[j, rows, :]
                    if diag is not None and r0 < diag * tk + tk - 1:
                        r_i = r0 + lax.broadcasted_iota(jnp.int32, (FOX_STRIP, tk), 0)
                        c_i = diag * tk + lax.broadcasted_iota(jnp.int32, (FOX_STRIP, tk), 1)
                        s = jnp.where(c_i <= r_i, s, NEG_INF)
                    m_old = m_ref[j, rows, :]
                    m_new = jnp.maximum(m_old, jnp.max(s, axis=1, keepdims=True))
                    alpha = jnp.exp2(m_old - m_new)
                    p = jnp.exp2(s - jnp.concatenate([m_new] * reps, axis=1))
                    l_ref[j, rows, :] = alpha * l_ref[j, rows, :] + jnp.sum(p, axis=1, keepdims=True)
                    m_ref[j, rows, :] = m_new
                    a_ref[j, rows, :] = alpha
                    p_ref[j, rows, :] = p.astype(BF16)
            for j in range(hs):
                acc_ref[j, lo:nr, :] = a_ref[j, lo:nr, :] * acc_ref[j, lo:nr, :] + jnp.dot(
                    p_ref[j, lo:nr, :], v_ref[pl.ds(start, tk), j * dh:(j + 1) * dh],
                    preferred_element_type=F32)

        def body(kb, carry):
            scores(kb, 0)
            absorb(kb, None, 0)
            return carry

        n_full = i * (tq // tk)
        lax.fori_loop(0, n_full, body, 0)
        for diag in range(nr // tk):
            scores(n_full + diag, diag * tk)
            absorb(n_full + diag, diag, diag * tk)
        for j in range(hs):
            o_ref[:nr, j * dh:(j + 1) * dh] = (acc_ref[j, :nr, :] / l_ref[j, :nr, :]).astype(o_ref.dtype)

    tail = lp % tq
    if tail == 0:
        sweep(tq)
    else:
        @pl.when(i < lp // tq)
        def _():
            sweep(tq)

        @pl.when(i == lp // tq)
        def _():
            sweep(tail)


def fox_attention(qkv, pq, pk, bsz, lp, tq=768, tk=384):
    tp = qkv.shape[0]
    d = qkv.shape[1] // 3
    nh = FOX_HEADS
    dh = d // nh
    assert dh == LANES and tq % tk == 0 and lp % tk == 0 and (lp % tq) % tk == 0
    nq = pl.cdiv(lp, tq)
    hs = FOX_HEADS_PER_STEP
    ng = nh // hs
    qkv3 = qkv.reshape(bsz, lp, 3 * d)
    pq3 = pq.reshape(bsz, lp, LANES)
    pk3 = pk.reshape(bsz, lp, LANES)
    selq, selk = _bias_selectors(nh)
    out = pl.pallas_call(
        functools.partial(_fox_kernel, tq=tq, tk=tk, lp=lp, dh=dh),
        out_shape=jax.ShapeDtypeStruct((bsz, lp, d), BF16),
        grid=(bsz, ng, nq),
        in_specs=[
            pl.BlockSpec((None, tq, hs * dh), lambda b, g, i: (b, i, g)),
            pl.BlockSpec((None, tq, LANES), lambda b, g, i: (b, i, 0)),
            pl.BlockSpec((None, lp, hs * dh), lambda b, g, i: (b, 0, ng + g)),
            pl.BlockSpec((None, lp, LANES), lambda b, g, i: (b, 0, 0)),
            pl.BlockSpec((None, lp, hs * dh), lambda b, g, i: (b, 0, 2 * ng + g)),
            pl.BlockSpec((hs, LANES, LANES), lambda b, g, i: (g, 0, 0)),
            pl.BlockSpec((hs, LANES, LANES), lambda b, g, i: (g, 0, 0)),
        ],
        out_specs=pl.BlockSpec((None, tq, hs * dh), lambda b, g, i: (b, i, g)),
        scratch_shapes=[pltpu.VMEM((hs, tq, LANES), F32), pltpu.VMEM((hs, tq, LANES), F32),
                        pltpu.VMEM((hs, tq, dh), F32), pltpu.VMEM((hs, tq, tk), F32),
                        pltpu.VMEM((hs, tq, tk), BF16), pltpu.VMEM((hs, tq, LANES), F32),
                        pltpu.VMEM((lp, hs * LANES), BF16)],
        compiler_params=_params(("parallel", "parallel", "arbitrary"), vmem=VMEM_LIMIT_LARGE),
        name="fox_attention",
    )(qkv3, pq3, qkv3, pk3, qkv3, selq, selk)
    return out.reshape(tp, d)


def _router_kernel(h_ref, g_ref, wr_ref, br_ref, hn_ref, idx_ref, w_ref, cnt_ref, carry_ref, *, tr):
    i = pl.program_id(0)

    @pl.when(i == 0)
    def _():
        carry_ref[...] = jnp.zeros_like(carry_ref)

    hn = _rms(h_ref[...], g_ref[...])
    _store_token_tiles(hn_ref, (), _pack_halves(hn))
    lg = lax.dot_general(wr_ref[...], hn.astype(BF16), (((1,), (1,)), ((), ())),
                         preferred_element_type=F32) + br_ref[...]
    row = lax.broadcasted_iota(jnp.int32, (ROUTER_ROWS, tr), 0)
    big = jnp.int32(1 << 20)
    is_g = row < N_GROUPS
    mg = jnp.max(jnp.where(is_g, lg, -jnp.inf), axis=0, keepdims=True)
    g_sel = jnp.min(jnp.where(jnp.logical_and(is_g, lg == mg), row, big), axis=0, keepdims=True)
    sg = jnp.sum(jnp.where(is_g, jnp.exp(lg - mg), 0.0), axis=0, keepdims=True)
    p_g = 1.0 / sg
    lo = EXPERT_ROW0 + EXPERTS_PER_GROUP * g_sel
    is_e = jnp.logical_and(row >= lo, row < lo + EXPERTS_PER_GROUP)
    me = jnp.max(jnp.where(is_e, lg, -jnp.inf), axis=0, keepdims=True)
    ee = jnp.where(is_e, jnp.exp(lg - me), 0.0)
    pe = ee / jnp.sum(ee, axis=0, keepdims=True)
    pe1 = jnp.where(is_e, pe, -1.0)
    m1 = jnp.max(pe1, axis=0, keepdims=True)
    i1 = jnp.min(jnp.where(pe1 == m1, row, big), axis=0, keepdims=True)
    pe2 = jnp.where(row == i1, -1.0, pe1)
    m2 = jnp.max(pe2, axis=0, keepdims=True)
    i2 = jnp.min(jnp.where(pe2 == m2, row, big), axis=0, keepdims=True)
    den = m1 + m2
    w1 = p_g * (m1 / den)
    w2 = p_g * (m2 / den)

    oh1 = row == i1
    oh2 = row == i2
    ohs = jnp.logical_or(oh1, oh2).astype(F32)
    r_i = lax.broadcasted_iota(jnp.int32, (tr, tr), 0)
    c_i = lax.broadcasted_iota(jnp.int32, (tr, tr), 1)
    tri = (r_i < c_i).astype(BF16)
    cnt = jnp.dot(ohs.astype(BF16), tri, preferred_element_type=F32) + carry_ref[...]
    rank1 = jnp.sum(jnp.where(oh1, cnt, 0.0), axis=0, keepdims=True)
    rank2 = jnp.sum(jnp.where(oh2, cnt, 0.0), axis=0, keepdims=True)
    carry_ref[...] += jnp.sum(ohs, axis=1, keepdims=True)

    r8 = lax.broadcasted_iota(jnp.int32, (8, tr), 0)
    e1 = i1 - EXPERT_ROW0
    e2 = i2 - EXPERT_ROW0
    idx_ref[...] = jnp.where(r8 == 0, e1, jnp.where(r8 == 1, e2, jnp.where(
        r8 == 2, rank1.astype(jnp.int32), jnp.where(r8 == 3, rank2.astype(jnp.int32), 0))))
    w_ref[...] = jnp.where(r8 == 0, w1, jnp.where(r8 == 1, w2, 0.0))
    cnt_ref[...] = jnp.broadcast_to(carry_ref[...], cnt_ref.shape)


def moe_router(h, gain, w_rg, b_rg, w_re, b_re, tr=512):
    tp, d = h.shape
    assert d // 2 == SUBLANES * LANES
    wr = jnp.zeros((ROUTER_ROWS, d), BF16)
    wr = wr.at[:N_GROUPS].set(w_rg.T.astype(BF16))
    wr = wr.at[EXPERT_ROW0:EXPERT_ROW0 + N_EXPERTS].set(w_re.reshape(d, N_EXPERTS).T.astype(BF16))
    br = jnp.zeros((ROUTER_ROWS, 1), F32)
    br = br.at[:N_GROUPS, 0].set(b_rg.astype(F32))
    br = br.at[EXPERT_ROW0:EXPERT_ROW0 + N_EXPERTS, 0].set(b_re.reshape(N_EXPERTS).astype(F32))
    return pl.pallas_call(
        functools.partial(_router_kernel, tr=tr),
        out_shape=(
            jax.ShapeDtypeStruct((tp * SUBLANES, LANES), jnp.uint32),
            jax.ShapeDtypeStruct((8, tp), jnp.int32),
            jax.ShapeDtypeStruct((8, tp), F32),
            jax.ShapeDtypeStruct((ROUTER_ROWS, LANES), F32),
        ),
        grid=(tp // tr,),
        in_specs=[
            pl.BlockSpec((tr, d), lambda i: (i, 0)),
            pl.BlockSpec((1, d), lambda i: (0, 0)),
            pl.BlockSpec((ROUTER_ROWS, d), lambda i: (0, 0)),
            pl.BlockSpec((ROUTER_ROWS, 1), lambda i: (0, 0)),
        ],
        out_specs=(
            pl.BlockSpec((tr * SUBLANES, LANES), lambda i: (i, 0)),
            pl.BlockSpec((8, tr), lambda i: (0, i)),
            pl.BlockSpec((8, tr), lambda i: (0, i)),
            pl.BlockSpec((ROUTER_ROWS, LANES), lambda i: (0, 0)),
        ),
        scratch_shapes=[pltpu.VMEM((ROUTER_ROWS, 1), F32)],
        compiler_params=_params(("arbitrary",)),
        name="moe_router",
    )(h, gain.reshape(1, d), wr, br)


def _slots_kernel(cnt_ref, idx_ref, pos_ref, meta_ref):
    e1 = idx_ref[0:1, :]
    e2 = idx_ref[1:2, :]
    off1 = jnp.zeros_like(e1)
    off2 = jnp.zeros_like(e2)
    visit = lax.broadcasted_iota(jnp.int32, (1, meta_ref.shape[1]), 1)
    v_tile = jnp.zeros_like(visit)
    v_expert = jnp.zeros_like(visit)
    v_lo = jnp.zeros_like(visit)
    v_hi = jnp.zeros_like(visit)
    v_next = jnp.zeros_like(visit)
    v_start = jnp.zeros_like(visit)
    v_count = jnp.zeros_like(visit)
    next_live = [None] * N_EXPERTS
    nxt = jnp.int32(-1)
    for e in reversed(range(N_EXPERTS)):
        next_live[e] = jnp.where(nxt >= 0, nxt, e)
        nxt = jnp.where(cnt_ref[e] > 0, e, nxt)
    start = jnp.int32(0)
    v_base = jnp.int32(0)
    for e in range(N_EXPERTS):
        off1 = jnp.where(e1 == e, start, off1)
        off2 = jnp.where(e2 == e, start, off2)
        n = cnt_ref[e]
        end = start + n
        first_tile = start // TILE_M
        n_visits = jnp.where(n > 0, (jnp.maximum(end, 1) - 1) // TILE_M - first_tile + 1, 0)
        mine = jnp.logical_and(visit >= v_base, visit < v_base + n_visits)
        row0 = (first_tile + visit - v_base) * TILE_M
        v_tile = jnp.where(mine, first_tile + visit - v_base, v_tile)
        v_expert = jnp.where(mine, e, v_expert)
        v_next = jnp.where(mine, next_live[e], v_next)
        v_start = jnp.where(mine, v_base, v_start)
        v_count = jnp.where(mine, n_visits, v_count)
        v_lo = jnp.where(mine, jnp.maximum(start - row0, 0), v_lo)
        v_hi = jnp.where(mine, jnp.minimum(end - row0, TILE_M), v_hi)
        start = end
        v_base = v_base + n_visits
    r8 = lax.broadcasted_iota(jnp.int32, pos_ref.shape, 0)
    pos_ref[...] = jnp.where(r8 == 0, off1 + idx_ref[2:3, :], jnp.where(r8 == 1, off2 + idx_ref[3:4, :], 0))
    m8 = lax.broadcasted_iota(jnp.int32, meta_ref.shape, 0)
    rows = (v_tile, v_expert, v_lo, v_hi, v_base, v_next, v_start, v_count)
    meta = rows[-1]
    for r in reversed(range(len(rows) - 1)):
        meta = jnp.where(m8 == r, rows[r], meta)
    meta_ref[...] = meta


def moe_slots(counts, idx, tr=512):
    tp = idx.shape[1]
    nt_lanes = 2 * LANES
    return pl.pallas_call(
        _slots_kernel,
        out_shape=(jax.ShapeDtypeStruct((8, tp), jnp.int32), jax.ShapeDtypeStruct((8, nt_lanes), jnp.int32)),
        grid_spec=pltpu.PrefetchScalarGridSpec(
            num_scalar_prefetch=1,
            grid=(tp // tr,),
            in_specs=[pl.BlockSpec((8, tr), lambda i, c: (0, i))],
            out_specs=(
                pl.BlockSpec((8, tr), lambda i, c: (0, i)),
                pl.BlockSpec((8, nt_lanes), lambda i, c: (0, 0)),
            ),
        ),
        compiler_params=_params(("arbitrary",)),
        name="moe_slots",
    )(counts, idx)


def _invert_kernel(pos_ref, code_ref, *, tp):
    def per_token(t, carry):
        code_ref[pos_ref[t]] = 2 * t
        code_ref[pos_ref[tp + t]] = 2 * t + 1
        return carry

    lax.fori_loop(0, tp, per_token, 0, unroll=8)


def moe_invert(pos_flat):
    tp = pos_flat.shape[0] // 2
    return pl.pallas_call(
        functools.partial(_invert_kernel, tp=tp),
        out_shape=jax.ShapeDtypeStruct((2 * tp,), jnp.int32),
        grid_spec=pltpu.PrefetchScalarGridSpec(
            num_scalar_prefetch=1,
            grid=(1,),
            in_specs=[],
            out_specs=pl.BlockSpec(memory_space=pltpu.SMEM),
        ),
        compiler_params=_params(("arbitrary",)),
        name="moe_invert",
    )(pos_flat)


def _expert_kernel(vt_ref, ve_ref, vlo_ref, vhi_ref, nv_ref, vnext_ref, vstart_ref, vcount_ref, code_ref,
                   hn_ref, wg_hbm, wu_hbm, wd_hbm, y_ref,
                   xbuf, xcur, yacc, wgs, wus, wds, wgb, wub, wdb, gsem, wsem, *, n_tiles, layer):
    v = pl.program_id(0)
    nv = nv_ref[0]
    t = vt_ref[v]
    prev_v = jnp.maximum(v - 1, 0)
    first = jnp.logical_or(v == 0, vt_ref[prev_v] != t)
    new_expert = jnp.logical_or(v == 0, ve_ref[prev_v] != ve_ref[v])

    def weight_copies(e):
        w = layer * N_EXPERTS + e
        return (pltpu.make_async_copy(wg_hbm.at[w], wgs, wsem.at[0]),
                pltpu.make_async_copy(wu_hbm.at[w], wus, wsem.at[1]),
                pltpu.make_async_copy(wd_hbm.at[w], wds, wsem.at[2]))

    def row_copy(tile, r):
        tok = lax.shift_right_logical(code_ref[tile * TILE_M + r], 1)
        return pltpu.make_async_copy(hn_ref.at[_token_tile(tok)], xbuf.at[_token_tile(r)], gsem)

    def gather_wait():
        pltpu.make_async_copy(hn_ref.at[pl.ds(0, TILE_M * SUBLANES)], xbuf, gsem).wait()

    def compute(accumulate):
        x = xcur[...]
        a = jnp.dot(x, wgb[...], preferred_element_type=F32)
        u = jnp.dot(x, wub[...], preferred_element_type=F32)
        row = lax.broadcasted_iota(jnp.int32, (TILE_M, 1), 0)
        mine = jnp.logical_and(row >= vlo_ref[v], row < vhi_ref[v])
        hid = jnp.where(mine, a * _sigmoid(a) * u, 0.0).astype(BF16)
        y = jnp.dot(hid, wdb[...], preferred_element_type=F32)
        if accumulate:
            y = y + yacc[...]
        yacc[...] = y
        _store_token_tiles(y_ref, (), _pack_halves(y))

    @pl.when(v == 0)
    def _():
        def step(r, carry):
            row_copy(0, r).start()
            return carry
        lax.fori_loop(0, TILE_M, step, 0, unroll=8)
        for copy in weight_copies(ve_ref[0]):
            copy.start()

    @pl.when(v < nv)
    def _():
        @pl.when(new_expert)
        def _():
            for copy in weight_copies(ve_ref[v]):
                copy.wait()
            wgb[...] = wgs[...].astype(BF16)
            wub[...] = wus[...].astype(BF16)
            wdb[...] = wds[...].astype(BF16)

        @pl.when(first)
        def _():
            gather_wait()
            xcur[...] = _unpack_halves(_load_token_tiles(xbuf, (), 0, TILE_M)).astype(BF16)
            ahead = jnp.minimum(t + 1, n_tiles - 1)
            for r in range(TILE_M):
                row_copy(ahead, r).start()
            compute(False)

        @pl.when(jnp.logical_not(first))
        def _():
            compute(True)

        k = v - vstart_ref[v]
        last_of_expert = k == vcount_ref[v] - 1
        has_next = vnext_ref[v] != ve_ref[v]
        for c, copy in enumerate(weight_copies(vnext_ref[v])):
            @pl.when(jnp.logical_and(has_next, jnp.logical_or(k == c, jnp.logical_and(last_of_expert, k < c))))
            def _():
                copy.start()

        @pl.when(v == nv - 1)
        def _():
            gather_wait()


def moe_experts(meta, code, hn, w_gate, w_up, w_down, layer):
    tp = hn.shape[0] // SUBLANES
    d = 2 * SUBLANES * LANES
    f = w_gate.shape[-1]
    assert (2 * tp) % TILE_M == 0
    n_tiles = (2 * tp) // TILE_M
    max_visits = n_tiles + N_EXPERTS - 1

    def y_map(v, vt, ve, vlo, vhi, nv, *_):
        return (vt[jnp.minimum(v, nv[0] - 1)], 0)

    any_spec = pl.BlockSpec(memory_space=pl.ANY)
    return pl.pallas_call(
        functools.partial(_expert_kernel, n_tiles=n_tiles, layer=layer),
        out_shape=jax.ShapeDtypeStruct((2 * tp * SUBLANES, LANES), jnp.uint32),
        grid_spec=pltpu.PrefetchScalarGridSpec(
            num_scalar_prefetch=9,
            grid=(max_visits,),
            in_specs=[any_spec, any_spec, any_spec, any_spec],
            out_specs=pl.BlockSpec((TILE_M * SUBLANES, LANES), y_map),
            scratch_shapes=[
                pltpu.VMEM((TILE_M * SUBLANES, LANES), jnp.uint32), pltpu.VMEM((TILE_M, d), BF16),
                pltpu.VMEM((TILE_M, d), F32),
                pltpu.VMEM((d, f), F32), pltpu.VMEM((d, f), F32), pltpu.VMEM((f, d), F32),
                pltpu.VMEM((d, f), BF16), pltpu.VMEM((d, f), BF16), pltpu.VMEM((f, d), BF16),
                pltpu.SemaphoreType.DMA, pltpu.SemaphoreType.DMA((3,)),
            ],
        ),
        compiler_params=_params(("arbitrary",), vmem=VMEM_LIMIT_LARGE),
        name="moe_experts",
    )(meta[0, :max_visits], meta[1, :max_visits], meta[2, :max_visits], meta[3, :max_visits], meta[4, :1],
      meta[5, :max_visits], meta[6, :max_visits], meta[7, :max_visits], code, hn, w_gate, w_up, w_down)


def _combine_kernel(pos_ref, *refs, tc, tp, n_steps, parts, first_token, write_h):
    h_refs, w_refs = refs[:parts], refs[parts:2 * parts]
    g_ref, y_ref = refs[2 * parts:2 * parts + 2]
    out_refs, (ybuf, sem) = refs[2 * parts + 2:-2], refs[-2:]
    i = pl.program_id(0)
    slot = lax.rem(i, 2)

    def row_copy(step, buf, s, r):
        p = pos_ref[s * tp + first_token(step) + r]
        return pltpu.make_async_copy(y_ref.at[_token_tile(p)], ybuf.at[buf, _token_tile(s * tc + r)],
                                     sem.at[buf])

    @pl.when(i == 0)
    def _():
        def step(r, carry):
            row_copy(0, 0, 0, r).start()
            row_copy(0, 0, 1, r).start()
            return carry
        lax.fori_loop(0, tc, step, 0, unroll=8)

    pltpu.make_async_copy(y_ref.at[pl.ds(0, 2 * tc * SUBLANES)], ybuf.at[slot], sem.at[slot]).wait()

    @pl.when(i + 1 < n_steps)
    def _():
        for r in range(tc):
            row_copy(i + 1, 1 - slot, 0, r).start()
            row_copy(i + 1, 1 - slot, 1, r).start()

    rows = tc // parts
    for c in range(parts):
        w = w_refs[c][...]
        h = (h_refs[c][...] + w[:, 0:1] * _unpack_halves(_load_token_tiles(ybuf, (slot,), c * rows, rows))
             + w[:, 1:2] * _unpack_halves(_load_token_tiles(ybuf, (slot,), tc + c * rows, rows)))
        if write_h:
            out_refs[0][c * rows:(c + 1) * rows, :] = h
        out_refs[-1][c * rows:(c + 1) * rows, :] = _rms(h, g_ref[...]).astype(out_refs[-1].dtype)


def moe_combine(pos_flat, h, w_col, y, gain, hn_dtype, final_shape=None):
    tp, d = h.shape
    tc = 256
    if final_shape is None:
        parts = 1
        n_steps = tp // tc

        def first_token(i):
            return i * tc

        def piece(c, width):
            return pl.BlockSpec((tc, width), lambda i, p: (i, 0))

        out_shape = (jax.ShapeDtypeStruct((tp, d), F32), jax.ShapeDtypeStruct((tp, d), hn_dtype))
        out_specs = (pl.BlockSpec((tc, d), lambda i, p: (i, 0)), pl.BlockSpec((tc, d), lambda i, p: (i, 0)))
    else:
        bsz, seq, _ = final_shape
        parts = tc // CHUNK
        nc = tp // bsz // CHUNK
        per_seq = seq // tc
        n_steps = bsz * per_seq

        def first_token(i):
            return (i // per_seq) * (nc * CHUNK) + CHUNK + (i % per_seq) * tc

        def piece(c, width):
            return pl.BlockSpec((CHUNK, width),
                                lambda i, p: ((i // per_seq) * nc + 1 + (i % per_seq) * parts + c, 0))

        out_shape = (jax.ShapeDtypeStruct(final_shape, hn_dtype),)
        out_specs = (pl.BlockSpec((None, tc, d), lambda i, p: (i // per_seq, i % per_seq, 0)),)
    return pl.pallas_call(
        functools.partial(_combine_kernel, tc=tc, tp=tp, n_steps=n_steps, parts=parts,
                          first_token=first_token, write_h=final_shape is None),
        out_shape=out_shape,
        grid_spec=pltpu.PrefetchScalarGridSpec(
            num_scalar_prefetch=1,
            grid=(n_steps,),
            in_specs=[piece(c, d) for c in range(parts)] + [piece(c, 2) for c in range(parts)] + [
                pl.BlockSpec((1, d), lambda i, p: (0, 0)),
                pl.BlockSpec(memory_space=pl.ANY),
            ],
            out_specs=out_specs,
            scratch_shapes=[pltpu.VMEM((2, 2 * tc * SUBLANES, LANES), jnp.uint32),
                            pltpu.SemaphoreType.DMA((2,))],
        ),
        compiler_params=_params(("arbitrary",)),
        name="moe_combine",
    )(pos_flat, *([h] * parts), *([w_col] * parts), gain.reshape(1, d), y)


def hierarchical_moe(h, gain, w_rg, b_rg, w_re, b_re, w_gate, w_up, w_down, layer, next_gain, hn_dtype,
                     final_shape=None):
    tp, d = h.shape
    f = w_gate.shape[-1]
    hn, idx, w_rows, cnt = moe_router(h, gain, w_rg, b_rg, w_re, b_re)
    counts = cnt[EXPERT_ROW0:EXPERT_ROW0 + N_EXPERTS, 0].astype(jnp.int32)
    pos, meta = moe_slots(counts, idx)
    pos_flat = pos[:2].reshape(2 * tp)
    code = moe_invert(pos_flat)
    y = moe_experts(meta, code, hn, w_gate.reshape(-1, d, f), w_up.reshape(-1, d, f),
                    w_down.reshape(-1, f, d), layer)
    return moe_combine(pos_flat, h, w_rows[:2].T, y, next_gain, hn_dtype, final_shape)


def kernel(x, meta_tokens, norm_mixer, norm_ffn, norm_final, ret_w_in, ret_gn, ret_w_out,
           fox_w_in, fox_b_f, fox_w_out, moe_w_rg, moe_b_rg, moe_w_re, moe_b_re,
           moe_w_gate, moe_w_up, moe_w_down):
    bsz, seq, d = x.shape
    depth = norm_mixer.shape[0]
    nc = (seq + CHUNK) // CHUNK
    lp = nc * CHUNK
    h, hn = embed_norm(x, meta_tokens.astype(x.dtype), norm_mixer[0], BF16)
    for i in range(depth):
        j = i // 2
        if i % 2 == 0:
            proj = matmul(hn, ret_w_in, j, BF16, tm=1536)
            gated = retention_core(proj, ret_gn[j], bsz, nc)
            h = matmul(gated, ret_w_out, j, F32, residual=h, tm=1536, tn=512, single_buffer_w=True)
        else:
            qkv = matmul(hn, fox_w_in, j, BF16, n=3 * d, tm=1536, scale_cols=d,
                         scale=(d // FOX_HEADS) ** -0.5 * LOG2E)
            pq, pk = forget_gates(hn, fox_w_in[j, :, 3 * d:], fox_b_f[j], bsz, lp)
            o = fox_attention(qkv, pq, pk, bsz, lp)
            h = matmul(o, fox_w_out, j, F32, residual=h, tm=768, tn=1024)
        last = i == depth - 1
        outs = hierarchical_moe(h, norm_ffn[i], moe_w_rg[i], moe_b_rg[i], moe_w_re[i], moe_b_re[i],
                                moe_w_gate, moe_w_up, moe_w_down, i,
                                norm_final if last else norm_mixer[i + 1],
                                F32 if last else BF16, (bsz, seq, d) if last else None)
        if last:
            return outs[0]
        h, hn = outs
```

```python
import functools

import jax
import jax.numpy as jnp
from jax import lax
from jax.experimental import pallas as pl
from jax.experimental.pallas import tpu as pltpu

N_META = 16
CHUNK = 128
PAD = CHUNK - N_META
RMS_EPS = 1e-6
GN_EPS = 1e-6
NEG_INF = -1e30
RET_HEADS = 8
FOX_HEADS = 16
N_GROUPS = 4
EXPERTS_PER_GROUP = 8
N_EXPERTS = N_GROUPS * EXPERTS_PER_GROUP
ROPE_BASE = 10000.0

LANES = 128
ROUTER_ROWS = 48
EXPERT_ROW0 = N_GROUPS
TILE_M = 256
VMEM_BYTES = 64 * 1024 * 1024
VMEM_LIMIT = 3 * VMEM_BYTES // 4
VMEM_LIMIT_LARGE = 7 * VMEM_BYTES // 8

F32 = jnp.float32
BF16 = jnp.bfloat16


def _params(sem, vmem=VMEM_LIMIT):
    return pltpu.CompilerParams(dimension_semantics=sem, vmem_limit_bytes=vmem)


def _rms(h, g):
    return h * lax.rsqrt(jnp.mean(h * h, axis=-1, keepdims=True) + RMS_EPS) * g


def _sigmoid(x):
    return 1.0 / (1.0 + jnp.exp(-x))


def _pack_halves(x):
    half = x.shape[1] // 2
    return pltpu.pack_elementwise([x[:, :half], x[:, half:]], packed_dtype=BF16)


def _unpack_halves(p):
    lo = pltpu.unpack_elementwise(p, index=0, packed_dtype=BF16, unpacked_dtype=F32)
    hi = pltpu.unpack_elementwise(p, index=1, packed_dtype=BF16, unpacked_dtype=F32)
    return jnp.concatenate([lo, hi], axis=1)


SUBLANES = 8


def _token_tile(t):
    start = t * SUBLANES
    return pl.ds(start if isinstance(t, int) else pl.multiple_of(start, SUBLANES), SUBLANES)


def _store_token_tiles(ref, index, packed):
    m = packed.shape[0]
    for s in range(SUBLANES):
        ref[index + (pl.ds(s, m, stride=SUBLANES), slice(None))] = packed[:, s * LANES:(s + 1) * LANES]


def _load_token_tiles(ref, index, first_token, m):
    return jnp.concatenate(
        [ref[index + (pl.ds(first_token * SUBLANES + s, m, stride=SUBLANES), slice(None))]
         for s in range(SUBLANES)], axis=1)


EMBED_CHUNKS = 3


def _embed_norm_kernel(*refs):
    x_refs, (meta_ref, g_ref, h_ref, hn_ref) = refs[:EMBED_CHUNKS], refs[EMBED_CHUNKS:]
    i = pl.program_id(1)
    for c, x_ref in enumerate(x_refs):
        h_ref[c * CHUNK:(c + 1) * CHUNK, :] = x_ref[...]

    @pl.when(i == 0)
    def _():
        h_ref[:PAD, :] = jnp.zeros((PAD, h_ref.shape[1]), h_ref.dtype)
        h_ref[PAD:CHUNK, :] = meta_ref[...]

    hn_ref[...] = _rms(h_ref[...], g_ref[...]).astype(hn_ref.dtype)


def embed_norm(x, meta, gain, hn_dtype):
    bsz, seq, d = x.shape
    nc = (seq + CHUNK) // CHUNK
    assert nc % EMBED_CHUNKS == 0
    tp = bsz * nc * CHUNK
    steps = nc // EMBED_CHUNKS
    rows = EMBED_CHUNKS * CHUNK

    def x_spec(c):
        return pl.BlockSpec((None, CHUNK, d), lambda b, i: (b, jnp.maximum(EMBED_CHUNKS * i + c - 1, 0), 0))

    return pl.pallas_call(
        _embed_norm_kernel,
        out_shape=(jax.ShapeDtypeStruct((tp, d), F32), jax.ShapeDtypeStruct((tp, d), hn_dtype)),
        grid=(bsz, steps),
        in_specs=[x_spec(c) for c in range(EMBED_CHUNKS)] + [
            pl.BlockSpec((N_META, d), lambda b, i: (0, 0)),
            pl.BlockSpec((1, d), lambda b, i: (0, 0)),
        ],
        out_specs=(
            pl.BlockSpec((rows, d), lambda b, i: (b * steps + i, 0)),
            pl.BlockSpec((rows, d), lambda b, i: (b * steps + i, 0)),
        ),
        compiler_params=_params(("parallel", "parallel")),
        name="embed_norm",
    )(*([x] * EMBED_CHUNKS), meta, gain.reshape(1, d))


def _mm_kernel(*refs, has_res, scale_tiles, scale):
    if has_res:
        x_ref, w_ref, r_ref, o_ref, wb_ref = refs
    else:
        x_ref, w_ref, o_ref, wb_ref = refs

    @pl.when(pl.program_id(1) == 0)
    def _():
        wb_ref[...] = w_ref[...].astype(BF16)

    acc = jnp.dot(x_ref[...], wb_ref[...], preferred_element_type=F32)
    if scale_tiles:
        acc = acc * jnp.where(pl.program_id(0) < scale_tiles, scale, 1.0)
    if has_res:
        acc = acc + r_ref[...]
    o_ref[...] = acc.astype(o_ref.dtype)


def matmul(x, w, layer, out_dtype, n=None, residual=None, tm=512, tn=1024, scale_cols=0, scale=1.0,
           single_buffer_w=False):
    m, kdim = x.shape
    n = w.shape[2] if n is None else n
    assert m % tm == 0 and n % tn == 0 and scale_cols % tn == 0
    w_buffers = 1 if single_buffer_w else 2
    in_specs = [
        pl.BlockSpec((tm, kdim), lambda j, i: (i, 0)),
        pl.BlockSpec((None, kdim, tn), lambda j, i: (layer, 0, j), pipeline_mode=pl.Buffered(w_buffers)),
    ]
    args = [x, w]
    vmem_bytes = (2 * tm * kdim * x.dtype.itemsize + w_buffers * kdim * tn * w.dtype.itemsize
                  + kdim * tn * 2 + 2 * tm * tn * jnp.dtype(out_dtype).itemsize)
    if residual is not None:
        in_specs.append(pl.BlockSpec((tm, tn), lambda j, i: (i, j)))
        args.append(residual)
        vmem_bytes += 2 * tm * tn * residual.dtype.itemsize
    return pl.pallas_call(
        functools.partial(_mm_kernel, has_res=residual is not None, scale_tiles=scale_cols // tn, scale=scale),
        out_shape=jax.ShapeDtypeStruct((m, n), out_dtype),
        grid=(n // tn, m // tm),
        in_specs=in_specs,
        out_specs=pl.BlockSpec((tm, tn), lambda j, i: (i, j)),
        scratch_shapes=[pltpu.VMEM((kdim, tn), BF16)],
        compiler_params=_params(("parallel", "arbitrary"),
                                vmem=VMEM_LIMIT if vmem_bytes < VMEM_LIMIT * 7 // 8 else VMEM_LIMIT_LARGE),
        name="matmul",
    )(*args)


def _retention_kernel(q_ref, k_ref, v_ref, g_ref, cos_ref, sin_ref, dmat_ref, xi_ref, zeta_ref,
                      gch_ref, gn_ref, o_ref, state_ref, *, nh, dk, dv):
    c = pl.program_id(1)

    @pl.when(c == 0)
    def _():
        state_ref[...] = jnp.zeros_like(state_ref)

    cos = cos_ref[...]
    sin = sin_ref[...]
    half = dk // 2

    def rot(u):
        u1, u2 = u[:, :half], u[:, half:]
        return jnp.concatenate([u1 * cos - u2 * sin, u1 * sin + u2 * cos], axis=1)

    row = lax.broadcasted_iota(jnp.int32, (CHUNK, 1), 0)
    valid = jnp.logical_or(row >= PAD, c > 0)
    for h in range(nh):
        q = rot(q_ref[:, h * dk:(h + 1) * dk].astype(F32))
        k = rot(k_ref[:, h * dk:(h + 1) * dk].astype(F32)) * (dk ** -0.5)
        k = jnp.where(valid, k, 0.0)
        v = v_ref[:, h * dv:(h + 1) * dv]
        vb = jnp.where(valid, v, jnp.zeros_like(v))
        qb = q.astype(BF16)
        kb = k.astype(BF16)

        scores = lax.dot_general(qb, kb, (((1,), (1,)), ((), ())), preferred_element_type=F32)
        scores = scores * dmat_ref[h]
        inner = jnp.dot(scores.astype(BF16), vb, preferred_element_type=F32)
        state = state_ref[h]
        cross = jnp.dot(qb, state.astype(BF16), preferred_element_type=F32) * xi_ref[h]
        y = inner + cross
        kz = (k * zeta_ref[h]).astype(BF16)
        state_ref[h] = gch_ref[h] * state + lax.dot_general(
            kz, vb, (((0,), (0,)), ((), ())), preferred_element_type=F32)

        mu = jnp.mean(y, axis=-1, keepdims=True)
        dlt = y - mu
        var = jnp.mean(dlt * dlt, axis=-1, keepdims=True)
        yn = dlt * lax.rsqrt(var + GN_EPS) * gn_ref[h]
        hg = 0.5 * g_ref[:, h * dv:(h + 1) * dv].astype(F32)
        o_ref[:, h * dv:(h + 1) * dv] = ((hg + hg * jnp.tanh(hg)) * yn).astype(o_ref.dtype)


def retention_core(proj, gn_gain, bsz, nc):
    tp = proj.shape[0]
    d = proj.shape[1] // 6
    nh = RET_HEADS
    dk = d // nh
    dv = 2 * d // nh
    lp = nc * CHUNK
    half = dk // 2
    pos = (jnp.arange(lp) - PAD).astype(F32)
    inv = ROPE_BASE ** (-jnp.arange(half, dtype=F32) / half)
    ang = pos[:, None] * inv[None, :]
    cos, sin = jnp.cos(ang), jnp.sin(ang)
    log_g = jnp.log1p(-jnp.exp2(-5.0 - jnp.arange(nh, dtype=F32)))
    idx = jnp.arange(CHUNK, dtype=F32)
    diff = idx[:, None] - idx[None, :]
    dmat = jnp.where(diff[None] >= 0, jnp.exp(jnp.maximum(diff, 0.0)[None] * log_g[:, None, None]), 0.0)
    xi = jnp.exp((idx + 1.0)[None, :] * log_g[:, None])[:, :, None]
    zeta = jnp.exp((CHUNK - 1.0 - idx)[None, :] * log_g[:, None])[:, :, None]
    g_chunk = jnp.exp(CHUNK * log_g)[:, None, None]
    return pl.pallas_call(
        functools.partial(_retention_kernel, nh=nh, dk=dk, dv=dv),
        out_shape=jax.ShapeDtypeStruct((tp, 2 * d), BF16),
        grid=(bsz, nc),
        in_specs=[
            pl.BlockSpec((CHUNK, d), lambda b, c: (b * nc + c, 0)),
            pl.BlockSpec((CHUNK, d), lambda b, c: (b * nc + c, 1)),
            pl.BlockSpec((CHUNK, 2 * d), lambda b, c: (b * nc + c, 1)),
            pl.BlockSpec((CHUNK, 2 * d), lambda b, c: (b * nc + c, 2)),
            pl.BlockSpec((CHUNK, half), lambda b, c: (c, 0)),
            pl.BlockSpec((CHUNK, half), lambda b, c: (c, 0)),
            pl.BlockSpec((nh, CHUNK, CHUNK), lambda b, c: (0, 0, 0)),
            pl.BlockSpec((nh, CHUNK, 1), lambda b, c: (0, 0, 0)),
            pl.BlockSpec((nh, CHUNK, 1), lambda b, c: (0, 0, 0)),
            pl.BlockSpec((nh, 1, 1), lambda b, c: (0, 0, 0)),
            pl.BlockSpec((nh, 1, dv), lambda b, c: (0, 0, 0)),
        ],
        out_specs=pl.BlockSpec((CHUNK, 2 * d), lambda b, c: (b * nc + c, 0)),
        scratch_shapes=[pltpu.VMEM((nh, dk, dv), F32)],
        compiler_params=_params(("parallel", "arbitrary")),
        name="retention_core",
    )(proj, proj, proj, proj, cos, sin, dmat, xi, zeta, g_chunk, gn_gain.reshape(nh, 1, dv))


LOG2E = 1.4426950408889634
MASK_BIG = 1e30
ONES_LANE = LANES - 1


def _bias_selectors(nh):
    h = jnp.arange(nh)
    selq = jnp.zeros((nh, LANES, LANES), F32)
    selk = jnp.zeros((nh, LANES, LANES), F32)
    for part in range(3):
        selq = selq.at[h, part * nh + h, part].set(1.0)
        selk = selk.at[h, part * nh + h, 3 + part].set(-1.0)
        selq = selq.at[h, ONES_LANE, 3 + part].set(1.0)
        selk = selk.at[h, ONES_LANE, part].set(1.0)
    return selq.astype(BF16), selk.astype(BF16)


def _forget_kernel(hn_ref, wf_ref, bf_ref, pq_ref, pk_ref, carry_ref, *, rows, nh):
    i = pl.program_id(1)

    @pl.when(i == 0)
    def _():
        carry_ref[...] = jnp.zeros_like(carry_ref)

    z = jnp.dot(hn_ref[...], wf_ref[...], preferred_element_type=F32) + bf_ref[...]
    lf = jnp.minimum(z, 0.0) - jnp.log1p(jnp.exp(-jnp.abs(z)))
    row = lax.broadcasted_iota(jnp.int32, (rows, 1), 0)
    valid = i * rows + row >= PAD
    lf = jnp.where(valid, lf, 0.0)

    def split3(a):
        hi = a.astype(BF16)
        r1 = a - hi.astype(F32)
        mid = r1.astype(BF16)
        lo = (r1 - mid.astype(F32)).astype(BF16)
        return hi, mid, lo

    r_i = lax.broadcasted_iota(jnp.int32, (rows, rows), 0)
    c_i = lax.broadcasted_iota(jnp.int32, (rows, rows), 1)
    tri = (r_i >= c_i).astype(BF16)
    cs = sum(jnp.dot(tri, part, preferred_element_type=F32) for part in split3(lf))
    cs = cs + carry_ref[...]
    carry_ref[...] = cs[rows - 1:rows, :]

    hi, mid, lo = (p.astype(F32) for p in split3(cs * LOG2E))
    lane = lax.broadcasted_iota(jnp.int32, (rows, LANES), 1)

    def lay_out(parts):
        row = jnp.where(lane == ONES_LANE, 1.0, 0.0)
        for k, part in enumerate(parts):
            moved = part if k == 0 else pltpu.roll(part, k * nh, axis=1)
            row = jnp.where(jnp.logical_and(lane >= k * nh, lane < (k + 1) * nh), moved, row)
        return row.astype(BF16)

    pq_ref[...] = lay_out((hi, mid, lo))
    pk_ref[...] = lay_out((jnp.where(valid, hi, MASK_BIG), jnp.where(valid, mid, 0.0),
                           jnp.where(valid, lo, 0.0)))


def forget_gates(hn, w_f, b_f, bsz, lp, rows=384):
    tp, d = hn.shape
    nh = w_f.shape[1]
    assert 3 * nh <= ONES_LANE and lp % rows == 0
    steps = lp // rows
    wf = jnp.zeros((d, LANES), BF16).at[:, :nh].set(w_f.astype(BF16))
    bf = jnp.zeros((1, LANES), F32).at[0, :nh].set(b_f.astype(F32))
    return pl.pallas_call(
        functools.partial(_forget_kernel, rows=rows, nh=nh),
        out_shape=(jax.ShapeDtypeStruct((tp, LANES), BF16), jax.ShapeDtypeStruct((tp, LANES), BF16)),
        grid=(bsz, steps),
        in_specs=[
            pl.BlockSpec((rows, d), lambda b, i: (b * steps + i, 0)),
            pl.BlockSpec((d, LANES), lambda b, i: (0, 0)),
            pl.BlockSpec((1, LANES), lambda b, i: (0, 0)),
        ],
        out_specs=(
            pl.BlockSpec((rows, LANES), lambda b, i: (b * steps + i, 0)),
            pl.BlockSpec((rows, LANES), lambda b, i: (b * steps + i, 0)),
        ),
        scratch_shapes=[pltpu.VMEM((1, LANES), F32)],
        compiler_params=_params(("parallel", "arbitrary")),
        name="forget_gates",
    )(hn, wf, bf)


FOX_HEADS_PER_STEP = 4


FOX_STRIP = 32


def _fox_kernel(q_ref, pq_ref, k_ref, pk_ref, v_ref, selq_ref, selk_ref, o_ref,
                m_ref, l_ref, acc_ref, s_ref, p_ref, a_ref, ck_ref, *, tq, tk, lp, dh):
    i = pl.program_id(2)
    hs = FOX_HEADS_PER_STEP
    reps = tk // LANES

    @pl.when(i == 0)
    def _():
        for j in range(hs):
            ck_ref[:, j * LANES:(j + 1) * LANES] = jnp.dot(
                pk_ref[...], selk_ref[j], preferred_element_type=F32).astype(BF16)

    def sweep(nr):
        m_ref[:, :nr, :] = jnp.full((hs, nr, LANES), 10.0 * NEG_INF, F32)
        l_ref[:, :nr, :] = jnp.zeros((hs, nr, LANES), F32)
        acc_ref[:, :nr, :] = jnp.zeros((hs, nr, dh), F32)
        qa = [jnp.concatenate(
            [q_ref[:nr, j * dh:(j + 1) * dh],
             jnp.dot(pq_ref[:nr, :], selq_ref[j], preferred_element_type=F32).astype(BF16)], axis=1)
            for j in range(hs)]

        def scores(kb, lo):
            start = pl.multiple_of(kb * tk, tk)
            for j in range(hs):
                ka = jnp.concatenate([k_ref[pl.ds(start, tk), j * dh:(j + 1) * dh],
                                      ck_ref[pl.ds(start, tk), j * LANES:(j + 1) * LANES]], axis=1)
                s_ref[j, lo:nr, :] = lax.dot_general(qa[j][lo:], ka, (((1,), (1,)), ((), ())),
                                                     preferred_element_type=F32)

        def absorb(kb, diag, lo):
            start = pl.multiple_of(kb * tk, tk)
            for j in range(hs):
                for r0 in range(lo, nr, FOX_STRIP):
                    rows = slice(r0, r0 + FOX_STRIP)
                    s = s_ref[j, rows, :]
                    if diag is not None and r0 < diag * tk + tk - 1:
                        r_i = r0 + lax.broadcasted_iota(jnp.int32, (FOX_STRIP, tk), 0)
                        c_i = diag * tk + lax.broadcasted_iota(jnp.int32, (FOX_STRIP, tk), 1)
                        s = jnp.where(c_i <= r_i, s, NEG_INF)
                    m_old = m_ref[j, rows, :]
                    m_new = jnp.maximum(m_old, jnp.max(s, axis=1, keepdims=True))
                    alpha = jnp.exp2(m_old - m_new)
                    p = jnp.exp2(s - jnp.concatenate([m_new] * reps, axis=1))
                    l_ref[j, rows, :] = alpha * l_ref[j, rows, :] + jnp.sum(p, axis=1, keepdims=True)
                    m_ref[j, rows, :] = m_new
                    a_ref[j, rows, :] = alpha
                    p_ref[j, rows, :] = p.astype(BF16)
            for j in range(hs):
                acc_ref[j, lo:nr, :] = a_ref[j, lo:nr, :] * acc_ref[j, lo:nr, :] + jnp.dot(
                    p_ref[j, lo:nr, :], v_ref[pl.ds(start, tk), j * dh:(j + 1) * dh],
                    preferred_element_type=F32)

        def body(kb, carry):
            scores(kb, 0)
            absorb(kb, None, 0)
            return carry

        n_full = i * (tq // tk)
        lax.fori_loop(0, n_full, body, 0)
        for diag in range(nr // tk):
            scores(n_full + diag, diag * tk)
            absorb(n_full + diag, diag, diag * tk)
        for j in range(hs):
            o_ref[:nr, j * dh:(j + 1) * dh] = (acc_ref[j, :nr, :] / l_ref[j, :nr, :]).astype(o_ref.dtype)

    tail = lp % tq
    if tail == 0:
        sweep(tq)
    else:
        @pl.when(i < lp // tq)
        def _():
            sweep(tq)

        @pl.when(i == lp // tq)
        def _():
            sweep(tail)


def fox_attention(qkv, pq, pk, bsz, lp, tq=768, tk=384):
    tp = qkv.shape[0]
    d = qkv.shape[1] // 3
    nh = FOX_HEADS
    dh = d // nh
    assert dh == LANES and tq % tk == 0 and lp % tk == 0 and (lp % tq) % tk == 0
    nq = pl.cdiv(lp, tq)
    hs = FOX_HEADS_PER_STEP
    ng = nh // hs
    qkv3 = qkv.reshape(bsz, lp, 3 * d)
    pq3 = pq.reshape(bsz, lp, LANES)
    pk3 = pk.reshape(bsz, lp, LANES)
    selq, selk = _bias_selectors(nh)
    out = pl.pallas_call(
        functools.partial(_fox_kernel, tq=tq, tk=tk, lp=lp, dh=dh),
        out_shape=jax.ShapeDtypeStruct((bsz, lp, d), BF16),
        grid=(bsz, ng, nq),
        in_specs=[
            pl.BlockSpec((None, tq, hs * dh), lambda b, g, i: (b, i, g)),
            pl.BlockSpec((None, tq, LANES), lambda b, g, i: (b, i, 0)),
            pl.BlockSpec((None, lp, hs * dh), lambda b, g, i: (b, 0, ng + g)),
            pl.BlockSpec((None, lp, LANES), lambda b, g, i: (b, 0, 0)),
            pl.BlockSpec((None, lp, hs * dh), lambda b, g, i: (b, 0, 2 * ng + g)),
            pl.BlockSpec((hs, LANES, LANES), lambda b, g, i: (g, 0, 0)),
            pl.BlockSpec((hs, LANES, LANES), lambda b, g, i: (g, 0, 0)),
        ],
        out_specs=pl.BlockSpec((None, tq, hs * dh), lambda b, g, i: (b, i, g)),
        scratch_shapes=[pltpu.VMEM((hs, tq, LANES), F32), pltpu.VMEM((hs, tq, LANES), F32),
                        pltpu.VMEM((hs, tq, dh), F32), pltpu.VMEM((hs, tq, tk), F32),
                        pltpu.VMEM((hs, tq, tk), BF16), pltpu.VMEM((hs, tq, LANES), F32),
                        pltpu.VMEM((lp, hs * LANES), BF16)],
        compiler_params=_params(("parallel", "parallel", "arbitrary"), vmem=VMEM_LIMIT_LARGE),
        name="fox_attention",
    )(qkv3, pq3, qkv3, pk3, qkv3, selq, selk)
    return out.reshape(tp, d)


def _router_kernel(h_ref, g_ref, wr_ref, br_ref, hn_ref, idx_ref, w_ref, cnt_ref, carry_ref, *, tr):
    i = pl.program_id(0)

    @pl.when(i == 0)
    def _():
        carry_ref[...] = jnp.zeros_like(carry_ref)

    hn = _rms(h_ref[...], g_ref[...])
    _store_token_tiles(hn_ref, (), _pack_halves(hn))
    lg = lax.dot_general(wr_ref[...], hn.astype(BF16), (((1,), (1,)), ((), ())),
                         preferred_element_type=F32) + br_ref[...]
    row = lax.broadcasted_iota(jnp.int32, (ROUTER_ROWS, tr), 0)
    big = jnp.int32(1 << 20)
    is_g = row < N_GROUPS
    mg = jnp.max(jnp.where(is_g, lg, -jnp.inf), axis=0, keepdims=True)
    g_sel = jnp.min(jnp.where(jnp.logical_and(is_g, lg == mg), row, big), axis=0, keepdims=True)
    sg = jnp.sum(jnp.where(is_g, jnp.exp(lg - mg), 0.0), axis=0, keepdims=True)
    p_g = 1.0 / sg
    lo = EXPERT_ROW0 + EXPERTS_PER_GROUP * g_sel
    is_e = jnp.logical_and(row >= lo, row < lo + EXPERTS_PER_GROUP)
    me = jnp.max(jnp.where(is_e, lg, -jnp.inf), axis=0, keepdims=True)
    ee = jnp.where(is_e, jnp.exp(lg - me), 0.0)
    pe = ee / jnp.sum(ee, axis=0, keepdims=True)
    pe1 = jnp.where(is_e, pe, -1.0)
    m1 = jnp.max(pe1, axis=0, keepdims=True)
    i1 = jnp.min(jnp.where(pe1 == m1, row, big), axis=0, keepdims=True)
    pe2 = jnp.where(row == i1, -1.0, pe1)
    m2 = jnp.max(pe2, axis=0, keepdims=True)
    i2 = jnp.min(jnp.where(pe2 == m2, row, big), axis=0, keepdims=True)
    den = m1 + m2
    w1 = p_g * (m1 / den)
    w2 = p_g * (m2 / den)

    oh1 = row == i1
    oh2 = row == i2
    ohs = jnp.logical_or(oh1, oh2).astype(F32)
    r_i = lax.broadcasted_iota(jnp.int32, (tr, tr), 0)
    c_i = lax.broadcasted_iota(jnp.int32, (tr, tr), 1)
    tri = (r_i < c_i).astype(BF16)
    cnt = jnp.dot(ohs.astype(BF16), tri, preferred_element_type=F32) + carry_ref[...]
    rank1 = jnp.sum(jnp.where(oh1, cnt, 0.0), axis=0, keepdims=True)
    rank2 = jnp.sum(jnp.where(oh2, cnt, 0.0), axis=0, keepdims=True)
    carry_ref[...] += jnp.sum(ohs, axis=1, keepdims=True)

    r8 = lax.broadcasted_iota(jnp.int32, (8, tr), 0)
    e1 = i1 - EXPERT_ROW0
    e2 = i2 - EXPERT_ROW0
    idx_ref[...] = jnp.where(r8 == 0, e1, jnp.where(r8 == 1, e2, jnp.where(
        r8 == 2, rank1.astype(jnp.int32), jnp.where(r8 == 3, rank2.astype(jnp.int32), 0))))
    w_ref[...] = jnp.where(r8 == 0, w1, jnp.where(r8 == 1, w2, 0.0))
    cnt_ref[...] = jnp.broadcast_to(carry_ref[...], cnt_ref.shape)


def moe_router(h, gain, w_rg, b_rg, w_re, b_re, tr=512):
    tp, d = h.shape
    assert d // 2 == SUBLANES * LANES
    wr = jnp.zeros((ROUTER_ROWS, d), BF16)
    wr = wr.at[:N_GROUPS].set(w_rg.T.astype(BF16))
    wr = wr.at[EXPERT_ROW0:EXPERT_ROW0 + N_EXPERTS].set(w_re.reshape(d, N_EXPERTS).T.astype(BF16))
    br = jnp.zeros((ROUTER_ROWS, 1), F32)
    br = br.at[:N_GROUPS, 0].set(b_rg.astype(F32))
    br = br.at[EXPERT_ROW0:EXPERT_ROW0 + N_EXPERTS, 0].set(b_re.reshape(N_EXPERTS).astype(F32))
    return pl.pallas_call(
        functools.partial(_router_kernel, tr=tr),
        out_shape=(
            jax.ShapeDtypeStruct((tp * SUBLANES, LANES), jnp.uint32),
            jax.ShapeDtypeStruct((8, tp), jnp.int32),
            jax.ShapeDtypeStruct((8, tp), F32),
            jax.ShapeDtypeStruct((ROUTER_ROWS, LANES), F32),
        ),
        grid=(tp // tr,),
        in_specs=[
            pl.BlockSpec((tr, d), lambda i: (i, 0)),
            pl.BlockSpec((1, d), lambda i: (0, 0)),
            pl.BlockSpec((ROUTER_ROWS, d), lambda i: (0, 0)),
            pl.BlockSpec((ROUTER_ROWS, 1), lambda i: (0, 0)),
        ],
        out_specs=(
            pl.BlockSpec((tr * SUBLANES, LANES), lambda i: (i, 0)),
            pl.BlockSpec((8, tr), lambda i: (0, i)),
            pl.BlockSpec((8, tr), lambda i: (0, i)),
            pl.BlockSpec((ROUTER_ROWS, LANES), lambda i: (0, 0)),
        ),
        scratch_shapes=[pltpu.VMEM((ROUTER_ROWS, 1), F32)],
        compiler_params=_params(("arbitrary",)),
        name="moe_router",
    )(h, gain.reshape(1, d), wr, br)


def _slots_kernel(cnt_ref, idx_ref, pos_ref, meta_ref):
    e1 = idx_ref[0:1, :]
    e2 = idx_ref[1:2, :]
    off1 = jnp.zeros_like(e1)
    off2 = jnp.zeros_like(e2)
    visit = lax.broadcasted_iota(jnp.int32, (1, meta_ref.shape[1]), 1)
    v_tile = jnp.zeros_like(visit)
    v_expert = jnp.zeros_like(visit)
    v_lo = jnp.zeros_like(visit)
    v_hi = jnp.zeros_like(visit)
    v_next = jnp.zeros_like(visit)
    v_start = jnp.zeros_like(visit)
    v_count = jnp.zeros_like(visit)
    next_live = [None] * N_EXPERTS
    nxt = jnp.int32(-1)
    for e in reversed(range(N_EXPERTS)):
        next_live[e] = jnp.where(nxt >= 0, nxt, e)
        nxt = jnp.where(cnt_ref[e] > 0, e, nxt)
    start = jnp.int32(0)
    v_base = jnp.int32(0)
    for e in range(N_EXPERTS):
        off1 = jnp.where(e1 == e, start, off1)
        off2 = jnp.where(e2 == e, start, off2)
        n = cnt_ref[e]
        end = start + n
        first_tile = start // TILE_M
        n_visits = jnp.where(n > 0, (jnp.maximum(end, 1) - 1) // TILE_M - first_tile + 1, 0)
        mine = jnp.logical_and(visit >= v_base, visit < v_base + n_visits)
        row0 = (first_tile + visit - v_base) * TILE_M
        v_tile = jnp.where(mine, first_tile + visit - v_base, v_tile)
        v_expert = jnp.where(mine, e, v_expert)
        v_next = jnp.where(mine, next_live[e], v_next)
        v_start = jnp.where(mine, v_base, v_start)
        v_count = jnp.where(mine, n_visits, v_count)
        v_lo = jnp.where(mine, jnp.maximum(start - row0, 0), v_lo)
        v_hi = jnp.where(mine, jnp.minimum(end - row0, TILE_M), v_hi)
        start = end
        v_base = v_base + n_visits
    r8 = lax.broadcasted_iota(jnp.int32, pos_ref.shape, 0)
    pos_ref[...] = jnp.where(r8 == 0, off1 + idx_ref[2:3, :], jnp.where(r8 == 1, off2 + idx_ref[3:4, :], 0))
    m8 = lax.broadcasted_iota(jnp.int32, meta_ref.shape, 0)
    rows = (v_tile, v_expert, v_lo, v_hi, v_base, v_next, v_start, v_count)
    meta = rows[-1]
    for r in reversed(range(len(rows) - 1)):
        meta = jnp.where(m8 == r, rows[r], meta)
    meta_ref[...] = meta


def moe_slots(counts, idx, tr=512):
    tp = idx.shape[1]
    nt_lanes = 2 * LANES
    return pl.pallas_call(
        _slots_kernel,
        out_shape=(jax.ShapeDtypeStruct((8, tp), jnp.int32), jax.ShapeDtypeStruct((8, nt_lanes), jnp.int32)),
        grid_spec=pltpu.PrefetchScalarGridSpec(
            num_scalar_prefetch=1,
            grid=(tp // tr,),
            in_specs=[pl.BlockSpec((8, tr), lambda i, c: (0, i))],
            out_specs=(
                pl.BlockSpec((8, tr), lambda i, c: (0, i)),
                pl.BlockSpec((8, nt_lanes), lambda i, c: (0, 0)),
            ),
        ),
        compiler_params=_params(("arbitrary",)),
        name="moe_slots",
    )(counts, idx)


def _invert_kernel(pos_ref, code_ref, *, tp):
    def per_token(t, carry):
        code_ref[pos_ref[t]] = 2 * t
        code_ref[pos_ref[tp + t]] = 2 * t + 1
        return carry

    lax.fori_loop(0, tp, per_token, 0, unroll=8)


def moe_invert(pos_flat):
    tp = pos_flat.shape[0] // 2
    return pl.pallas_call(
        functools.partial(_invert_kernel, tp=tp),
        out_shape=jax.ShapeDtypeStruct((2 * tp,), jnp.int32),
        grid_spec=pltpu.PrefetchScalarGridSpec(
            num_scalar_prefetch=1,
            grid=(1,),
            in_specs=[],
            out_specs=pl.BlockSpec(memory_space=pltpu.SMEM),
        ),
        compiler_params=_params(("arbitrary",)),
        name="moe_invert",
    )(pos_flat)


def _expert_kernel(vt_ref, ve_ref, vlo_ref, vhi_ref, nv_ref, vnext_ref, vstart_ref, vcount_ref, code_ref,
                   hn_ref, wg_hbm, wu_hbm, wd_hbm, y_ref,
                   xbuf, xcur, yacc, wgs, wus, wds, wgb, wub, wdb, gsem, wsem, *, n_tiles, layer):
    v = pl.program_id(0)
    nv = nv_ref[0]
    t = vt_ref[v]
    prev_v = jnp.maximum(v - 1, 0)
    first = jnp.logical_or(v == 0, vt_ref[prev_v] != t)
    new_expert = jnp.logical_or(v == 0, ve_ref[prev_v] != ve_ref[v])

    def weight_copies(e):
        w = layer * N_EXPERTS + e
        return (pltpu.make_async_copy(wg_hbm.at[w], wgs, wsem.at[0]),
                pltpu.make_async_copy(wu_hbm.at[w], wus, wsem.at[1]),
                pltpu.make_async_copy(wd_hbm.at[w], wds, wsem.at[2]))

    def row_copy(tile, r):
        tok = lax.shift_right_logical(code_ref[tile * TILE_M + r], 1)
        return pltpu.make_async_copy(hn_ref.at[_token_tile(tok)], xbuf.at[_token_tile(r)], gsem)

    def gather_wait():
        pltpu.make_async_copy(hn_ref.at[pl.ds(0, TILE_M * SUBLANES)], xbuf, gsem).wait()

    def compute(accumulate):
        x = xcur[...]
        a = jnp.dot(x, wgb[...], preferred_element_type=F32)
        u = jnp.dot(x, wub[...], preferred_element_type=F32)
        row = lax.broadcasted_iota(jnp.int32, (TILE_M, 1), 0)
        mine = jnp.logical_and(row >= vlo_ref[v], row < vhi_ref[v])
        hid = jnp.where(mine, a * _sigmoid(a) * u, 0.0).astype(BF16)
        y = jnp.dot(hid, wdb[...], preferred_element_type=F32)
        if accumulate:
            y = y + yacc[...]
        yacc[...] = y
        _store_token_tiles(y_ref, (), _pack_halves(y))

    @pl.when(v == 0)
    def _():
        def step(r, carry):
            row_copy(0, r).start()
            return carry
        lax.fori_loop(0, TILE_M, step, 0, unroll=8)
        for copy in weight_copies(ve_ref[0]):
            copy.start()

    @pl.when(v < nv)
    def _():
        @pl.when(new_expert)
        def _():
            for copy in weight_copies(ve_ref[v]):
                copy.wait()
            wgb[...] = wgs[...].astype(BF16)
            wub[...] = wus[...].astype(BF16)
            wdb[...] = wds[...].astype(BF16)

        @pl.when(first)
        def _():
            gather_wait()
            xcur[...] = _unpack_halves(_load_token_tiles(xbuf, (), 0, TILE_M)).astype(BF16)
            ahead = jnp.minimum(t + 1, n_tiles - 1)
            for r in range(TILE_M):
                row_copy(ahead, r).start()
            compute(False)

        @pl.when(jnp.logical_not(first))
        def _():
            compute(True)

        k = v - vstart_ref[v]
        last_of_expert = k == vcount_ref[v] - 1
        has_next = vnext_ref[v] != ve_ref[v]
        for c, copy in enumerate(weight_copies(vnext_ref[v])):
            @pl.when(jnp.logical_and(has_next, jnp.logical_or(k == c, jnp.logical_and(last_of_expert, k < c))))
            def _():
                copy.start()

        @pl.when(v == nv - 1)
        def _():
            gather_wait()


def moe_experts(meta, code, hn, w_gate, w_up, w_down, layer):
    tp = hn.shape[0] // SUBLANES
    d = 2 * SUBLANES * LANES
    f = w_gate.shape[-1]
    assert (2 * tp) % TILE_M == 0
    n_tiles = (2 * tp) // TILE_M
    max_visits = n_tiles + N_EXPERTS - 1

    def y_map(v, vt, ve, vlo, vhi, nv, *_):
        return (vt[jnp.minimum(v, nv[0] - 1)], 0)

    any_spec = pl.BlockSpec(memory_space=pl.ANY)
    return pl.pallas_call(
        functools.partial(_expert_kernel, n_tiles=n_tiles, layer=layer),
        out_shape=jax.ShapeDtypeStruct((2 * tp * SUBLANES, LANES), jnp.uint32),
        grid_spec=pltpu.PrefetchScalarGridSpec(
            num_scalar_prefetch=9,
            grid=(max_visits,),
            in_specs=[any_spec, any_spec, any_spec, any_spec],
            out_specs=pl.BlockSpec((TILE_M * SUBLANES, LANES), y_map),
            scratch_shapes=[
                pltpu.VMEM((TILE_M * SUBLANES, LANES), jnp.uint32), pltpu.VMEM((TILE_M, d), BF16),
                pltpu.VMEM((TILE_M, d), F32),
                pltpu.VMEM((d, f), F32), pltpu.VMEM((d, f), F32), pltpu.VMEM((f, d), F32),
                pltpu.VMEM((d, f), BF16), pltpu.VMEM((d, f), BF16), pltpu.VMEM((f, d), BF16),
                pltpu.SemaphoreType.DMA, pltpu.SemaphoreType.DMA((3,)),
            ],
        ),
        compiler_params=_params(("arbitrary",), vmem=VMEM_LIMIT_LARGE),
        name="moe_experts",
    )(meta[0, :max_visits], meta[1, :max_visits], meta[2, :max_visits], meta[3, :max_visits], meta[4, :1],
      meta[5, :max_visits], meta[6, :max_visits], meta[7, :max_visits], code, hn, w_gate, w_up, w_down)


def _combine_kernel(pos_ref, *refs, tc, tp, n_steps, parts, first_token, write_h):
    h_refs, w_refs = refs[:parts], refs[parts:2 * parts]
    g_ref, y_ref = refs[2 * parts:2 * parts + 2]
    out_refs, (ybuf, sem) = refs[2 * parts + 2:-2], refs[-2:]
    i = pl.program_id(0)
    slot = lax.rem(i, 2)

    def row_copy(step, buf, s, r):
        p = pos_ref[s * tp + first_token(step) + r]
        return pltpu.make_async_copy(y_ref.at[_token_tile(p)], ybuf.at[buf, _token_tile(s * tc + r)],
                                     sem.at[buf])

    @pl.when(i == 0)
    def _():
        def step(r, carry):
            row_copy(0, 0, 0, r).start()
            row_copy(0, 0, 1, r).start()
            return carry
        lax.fori_loop(0, tc, step, 0, unroll=8)

    pltpu.make_async_copy(y_ref.at[pl.ds(0, 2 * tc * SUBLANES)], ybuf.at[slot], sem.at[slot]).wait()

    @pl.when(i + 1 < n_steps)
    def _():
        for r in range(tc):
            row_copy(i + 1, 1 - slot, 0, r).start()
            row_copy(i + 1, 1 - slot, 1, r).start()

    rows = tc // parts
    for c in range(parts):
        w = w_refs[c][...]
        h = (h_refs[c][...] + w[:, 0:1] * _unpack_halves(_load_token_tiles(ybuf, (slot,), c * rows, rows))
             + w[:, 1:2] * _unpack_halves(_load_token_tiles(ybuf, (slot,), tc + c * rows, rows)))
        if write_h:
            out_refs[0][c * rows:(c + 1) * rows, :] = h
        out_refs[-1][c * rows:(c + 1) * rows, :] = _rms(h, g_ref[...]).astype(out_refs[-1].dtype)


def moe_combine(pos_flat, h, w_col, y, gain, hn_dtype, final_shape=None):
    tp, d = h.shape
    tc = 256
    if final_shape is None:
        parts = 1
        n_steps = tp // tc

        def first_token(i):
            return i * tc

        def piece(c, width):
            return pl.BlockSpec((tc, width), lambda i, p: (i, 0))

        out_shape = (jax.ShapeDtypeStruct((tp, d), F32), jax.ShapeDtypeStruct((tp, d), hn_dtype))
        out_specs = (pl.BlockSpec((tc, d), lambda i, p: (i, 0)), pl.BlockSpec((tc, d), lambda i, p: (i, 0)))
    else:
        bsz, seq, _ = final_shape
        parts = tc // CHUNK
        nc = tp // bsz // CHUNK
        per_seq = seq // tc
        n_steps = bsz * per_seq

        def first_token(i):
            return (i // per_seq) * (nc * CHUNK) + CHUNK + (i % per_seq) * tc

        def piece(c, width):
            return pl.BlockSpec((CHUNK, width),
                                lambda i, p: ((i // per_seq) * nc + 1 + (i % per_seq) * parts + c, 0))

        out_shape = (jax.ShapeDtypeStruct(final_shape, hn_dtype),)
        out_specs = (pl.BlockSpec((None, tc, d), lambda i, p: (i // per_seq, i % per_seq, 0)),)
    return pl.pallas_call(
        functools.partial(_combine_kernel, tc=tc, tp=tp, n_steps=n_steps, parts=parts,
                          first_token=first_token, write_h=final_shape is None),
        out_shape=out_shape,
        grid_spec=pltpu.PrefetchScalarGridSpec(
            num_scalar_prefetch=1,
            grid=(n_steps,),
            in_specs=[piece(c, d) for c in range(parts)] + [piece(c, 2) for c in range(parts)] + [
                pl.BlockSpec((1, d), lambda i, p: (0, 0)),
                pl.BlockSpec(memory_space=pl.ANY),
            ],
            out_specs=out_specs,
            scratch_shapes=[pltpu.VMEM((2, 2 * tc * SUBLANES, LANES), jnp.uint32),
                            pltpu.SemaphoreType.DMA((2,))],
        ),
        compiler_params=_params(("arbitrary",)),
        name="moe_combine",
    )(pos_flat, *([h] * parts), *([w_col] * parts), gain.reshape(1, d), y)


def hierarchical_moe(h, gain, w_rg, b_rg, w_re, b_re, w_gate, w_up, w_down, layer, next_gain, hn_dtype,
                     final_shape=None):
    tp, d = h.shape
    f = w_gate.shape[-1]
    hn, idx, w_rows, cnt = moe_router(h, gain, w_rg, b_rg, w_re, b_re)
    counts = cnt[EXPERT_ROW0:EXPERT_ROW0 + N_EXPERTS, 0].astype(jnp.int32)
    pos, meta = moe_slots(counts, idx)
    pos_flat = pos[:2].reshape(2 * tp)
    code = moe_invert(pos_flat)
    y = moe_experts(meta, code, hn, w_gate.reshape(-1, d, f), w_up.reshape(-1, d, f),
                    w_down.reshape(-1, f, d), layer)
    return moe_combine(pos_flat, h, w_rows[:2].T, y, next_gain, hn_dtype, final_shape)


def kernel(x, meta_tokens, norm_mixer, norm_ffn, norm_final, ret_w_in, ret_gn, ret_w_out,
           fox_w_in, fox_b_f, fox_w_out, moe_w_rg, moe_b_rg, moe_w_re, moe_b_re,
           moe_w_gate, moe_w_up, moe_w_down):
    bsz, seq, d = x.shape
    depth = norm_mixer.shape[0]
    nc = (seq + CHUNK) // CHUNK
    lp = nc * CHUNK
    h, hn = embed_norm(x, meta_tokens.astype(x.dtype), norm_mixer[0], BF16)
    for i in range(depth):
        j = i // 2
        if i % 2 == 0:
            proj = matmul(hn, ret_w_in, j, BF16, tm=1536)
            gated = retention_core(proj, ret_gn[j], bsz, nc)
            h = matmul(gated, ret_w_out, j, F32, residual=h, tm=768, tn=1024, single_buffer_w=True)
        else:
            qkv = matmul(hn, fox_w_in, j, BF16, n=3 * d, tm=1536, scale_cols=d,
                         scale=(d // FOX_HEADS) ** -0.5 * LOG2E)
            pq, pk = forget_gates(hn, fox_w_in[j, :, 3 * d:], fox_b_f[j], bsz, lp)
            o = fox_attention(qkv, pq, pk, bsz, lp)
            h = matmul(o, fox_w_out, j, F32, residual=h, tm=768, tn=1024)
        last = i == depth - 1
        outs = hierarchical_moe(h, norm_ffn[i], moe_w_rg[i], moe_b_rg[i], moe_w_re[i], moe_b_re[i],
                                moe_w_gate, moe_w_up, moe_w_down, i,
                                norm_final if last else norm_mixer[i + 1],
                                F32 if last else BF16, (bsz, seq, d) if last else None)
        if last:
            return outs[0]
        h, hn = outs
```
